```python
import math
import jax, jax.numpy as jnp
from jax import lax
import numpy as np

D_MODEL = 1024
BATCH = 4
SEQ = 8192
DEPTH = 1

D_MIX = D_MODEL
D_LRU = D_MIX // 2
LRU_HEADS = 8
LRU_HEAD_DIM = D_LRU // LRU_HEADS
CONV_WIDTH = 4
LRU_C = 8.0
D_FOURIER = D_MIX - D_LRU
FOURIER_GROUPS = 4
FOURIER_GROUP_DIM = D_FOURIER // FOURIER_GROUPS
D_IN_PROJ = 2 * D_LRU + D_FOURIER
N_EXPERTS = 16
CAPACITY_FACTOR = 2
D_FF_EXPERT = 2 * D_MODEL
EPS = 1e-6

kernel_name = "hybrid_rglru_fourier_ecmoe_encoder"


def rmsnorm(x, g):
    xf = x.astype(jnp.float32)
    y = xf * lax.rsqrt(jnp.mean(xf * xf, axis=-1, keepdims=True) + EPS)
    return (y * g.astype(jnp.float32)).astype(x.dtype)


def centred_depthwise_conv(u, w, b):
    s = u.shape[1]
    left = CONV_WIDTH // 2
    right = CONV_WIDTH - 1 - left
    up = jnp.pad(u, ((0, 0), (left, right), (0, 0)))
    out = b
    for k in range(CONV_WIDTH):
        out = out + w[k] * up[:, k:k + s, :]
    return out


def _linear_recurrence_combine(left, right):
    a1, b1 = left
    a2, b2 = right
    return a1 * a2, a2 * b1 + b2


def rglru(u, w_a, b_a, w_x, b_x, lam, reverse):
    bsz, s, _ = u.shape
    uf = u.astype(jnp.float32)
    uh = uf.reshape(bsz, s, LRU_HEADS, LRU_HEAD_DIM)
    r = jax.nn.sigmoid(jnp.einsum('bshi,hij->bshj', uh, w_a.astype(jnp.float32)).reshape(bsz, s, D_LRU) + b_a)
    i = jax.nn.sigmoid(jnp.einsum('bshi,hij->bshj', uh, w_x.astype(jnp.float32)).reshape(bsz, s, D_LRU) + b_x)
    log_a = LRU_C * r * jax.nn.log_sigmoid(lam.astype(jnp.float32))
    a = jnp.exp(log_a)
    mult = jnp.sqrt(-jnp.expm1(2.0 * log_a))
    bterm = mult * (i * uf)
    _, h = lax.associative_scan(_linear_recurrence_combine, (a, bterm), axis=1, reverse=reverse)
    return h


def fourier_mix(u):
    bsz, s, _ = u.shape
    f = u.astype(jnp.float32).reshape(bsz, s, FOURIER_GROUPS, FOURIER_GROUP_DIM)
    y = jnp.fft.fft2(f, axes=(1, 3), norm='ortho').real
    return y.reshape(bsz, s, D_FOURIER)


def expert_choice_moe(h, w_router, w_gate, w_up, w_down):
    bsz, s, d = h.shape
    cap = CAPACITY_FACTOR * s // N_EXPERTS
    logits = jnp.einsum('bsd,de->bse', h.astype(jnp.float32), w_router.astype(jnp.float32))
    aff = jax.nn.softmax(logits, axis=-1)
    scores = jnp.transpose(aff, (0, 2, 1))
    gate, idx = lax.top_k(scores, cap)
    bidx = jnp.arange(bsz)[:, None, None]
    xin = h[bidx, idx]
    g = jnp.einsum('becd,edf->becf', xin, w_gate)
    u = jnp.einsum('becd,edf->becf', xin, w_up)
    eo = jnp.einsum('becf,efd->becd', jax.nn.silu(g) * u, w_down)
    contrib = gate.astype(eo.dtype)[..., None] * eo
    out = jnp.zeros((bsz, s, d), dtype=eo.dtype).at[bidx, idx].add(contrib)
    return out


def setup_inputs(seed: int = 0) -> dict:
    key = jax.random.key(seed)
    ks = jax.random.split(key, 24)
    f32 = jnp.float32
    nrm = lambda k, shape, scale: jax.random.normal(k, shape, f32) * scale

    def lam_init(k):
        u = jax.random.uniform(k, (D_LRU,), f32, 0.9, 0.999)
        return jnp.log(u) - jnp.log1p(-u)

    return {
        "x": nrm(ks[0], (BATCH, SEQ, D_MODEL), 1.0),
        "norm1_g": 1.0 + nrm(ks[1], (D_MODEL,), 0.02),
        "w_in": nrm(ks[2], (D_MODEL, D_IN_PROJ), D_MODEL ** -0.5),
        "conv_w": nrm(ks[3], (CONV_WIDTH, D_LRU), CONV_WIDTH ** -0.5),
        "conv_b": nrm(ks[4], (D_LRU,), 0.01),
        "lru_wa_f": nrm(ks[5], (LRU_HEADS, LRU_HEAD_DIM, LRU_HEAD_DIM), LRU_HEAD_DIM ** -0.5),
        "lru_ba_f": nrm(ks[6], (D_LRU,), 0.01),
        "lru_wx_f": nrm(ks[7], (LRU_HEADS, LRU_HEAD_DIM, LRU_HEAD_DIM), LRU_HEAD_DIM ** -0.5),
        "lru_bx_f": nrm(ks[8], (D_LRU,), 0.01),
        "lru_lam_f": lam_init(ks[9]),
        "lru_wa_b": nrm(ks[10], (LRU_HEADS, LRU_HEAD_DIM, LRU_HEAD_DIM), LRU_HEAD_DIM ** -0.5),
        "lru_ba_b": nrm(ks[11], (D_LRU,), 0.01),
        "lru_wx_b": nrm(ks[12], (LRU_HEADS, LRU_HEAD_DIM, LRU_HEAD_DIM), LRU_HEAD_DIM ** -0.5),
        "lru_bx_b": nrm(ks[13], (D_LRU,), 0.01),
        "lru_lam_b": lam_init(ks[14]),
        "w_out": nrm(ks[15], (D_MIX, D_MODEL), D_MIX ** -0.5),
        "norm2_g": 1.0 + nrm(ks[16], (D_MODEL,), 0.02),
        "w_router": nrm(ks[17], (D_MODEL, N_EXPERTS), D_MODEL ** -0.5),
        "w_gate": nrm(ks[18], (N_EXPERTS, D_MODEL, D_FF_EXPERT), D_MODEL ** -0.5),
        "w_up": nrm(ks[19], (N_EXPERTS, D_MODEL, D_FF_EXPERT), D_MODEL ** -0.5),
        "w_down": nrm(ks[20], (N_EXPERTS, D_FF_EXPERT, D_MODEL), D_FF_EXPERT ** -0.5),
        "normf_g": 1.0 + nrm(ks[21], (D_MODEL,), 0.02),
    }


def reference(x, norm1_g, w_in, conv_w, conv_b,
              lru_wa_f, lru_ba_f, lru_wx_f, lru_bx_f, lru_lam_f,
              lru_wa_b, lru_ba_b, lru_wx_b, lru_bx_b, lru_lam_b,
              w_out, norm2_g, w_router, w_gate, w_up, w_down, normf_g):
    for _ in range(DEPTH):
        h = rmsnorm(x, norm1_g)
        p = jnp.einsum('bsd,dn->bsn', h, w_in)
        lru_x = p[..., :D_LRU]
        lru_gate = p[..., D_LRU:2 * D_LRU]
        four_in = p[..., 2 * D_LRU:]
        c = centred_depthwise_conv(lru_x, conv_w, conv_b)
        h_f = rglru(c, lru_wa_f, lru_ba_f, lru_wx_f, lru_bx_f, lru_lam_f, reverse=False)
        h_b = rglru(c, lru_wa_b, lru_ba_b, lru_wx_b, lru_bx_b, lru_lam_b, reverse=True)
        y_lru = (jax.nn.gelu(lru_gate.astype(jnp.float32)) * (h_f + h_b)).astype(x.dtype)
        y_four = fourier_mix(four_in).astype(x.dtype)
        mixed = jnp.concatenate([y_lru, y_four], axis=-1)
        x = x + jnp.einsum('bsm,md->bsd', mixed, w_out)
        h2 = rmsnorm(x, norm2_g)
        x = x + expert_choice_moe(h2, w_router, w_gate, w_up, w_down).astype(x.dtype)
    return rmsnorm(x, normf_g)
```

```python
import functools
import math

import jax
import jax.numpy as jnp
import numpy as np
from jax import lax
from jax.experimental import pallas as pl
from jax.experimental.pallas import tpu as pltpu

F32 = jnp.float32
BF16 = jnp.bfloat16

EPS = 1e-6
LRU_C = 8.0
LRU_HEADS = 8
N_EXPERTS = 16
CAPACITY_FACTOR = 2
FOURIER_GROUPS = 4

SUBLANES = 8
LANES = 128
MXU_DIM = 256
VMEM_LIMIT_BYTES = 56 * 1024 * 1024

ROW_TILE = 512
LRU_CHUNK = 512
N_SEGMENTS = SUBLANES
SLOT_BLOCK = MXU_DIM
TOKEN_TILE = MXU_DIM
FOURIER_N2 = LANES


def _cparams(semantics):
    return pltpu.CompilerParams(dimension_semantics=semantics,
                                vmem_limit_bytes=VMEM_LIMIT_BYTES)


def _rms_scale(x):
    return x * lax.rsqrt(jnp.mean(x * x, axis=-1, keepdims=True) + EPS)


def _inproj_kernel(x_ref, g_ref, w_ref, lx_ref, lg_ref, fo_ref):
    h = _rms_scale(x_ref[...]) * g_ref[...]
    p = jnp.dot(h.astype(BF16), w_ref[...], preferred_element_type=F32)
    d = lx_ref.shape[-1]
    lx_ref[...] = p[:, :d]
    lg_ref[...] = p[:, d:2 * d]
    fo_ref[...] = p[:, 2 * d:]


def _inproj(x2, g, w_bf, d_lru, d_four):
    m, d = x2.shape
    n = w_bf.shape[1]
    return pl.pallas_call(
        _inproj_kernel,
        grid=(m // ROW_TILE,),
        in_specs=[
            pl.BlockSpec((ROW_TILE, d), lambda i: (i, 0)),
            pl.BlockSpec((1, d), lambda i: (0, 0)),
            pl.BlockSpec((d, n), lambda i: (0, 0)),
        ],
        out_specs=[
            pl.BlockSpec((ROW_TILE, d_lru), lambda i: (i, 0)),
            pl.BlockSpec((ROW_TILE, d_lru), lambda i: (i, 0)),
            pl.BlockSpec((ROW_TILE, d_four), lambda i: (i, 0)),
        ],
        out_shape=[
            jax.ShapeDtypeStruct((m, d_lru), F32),
            jax.ShapeDtypeStruct((m, d_lru), F32),
            jax.ShapeDtypeStruct((m, d_four), F32),
        ],
        compiler_params=_cparams(("parallel",)),
        name="inproj",
    )(x2, g, w_bf)


def _shift_rows(x, shift):
    n = x.shape[0]
    rows = lax.broadcasted_iota(jnp.int32, x.shape, 0)
    rolled = pltpu.roll(x, shift % n, axis=0)
    keep = (rows >= shift) if shift > 0 else (rows < n + shift)
    return jnp.where(keep, rolled, 0.0)


def _lru_kernel(lx_ref, lg_ref, cw_ref, cb_ref, w_ref, b_ref, lam_ref, y_ref,
                xpad, af, bf, ab, bb):
    s = lx_ref.shape[1]
    c_blk = lx_ref.shape[2]
    seg = s // N_SEGMENTS
    n_chunks = s // LRU_CHUNK
    pad = SUBLANES

    zeros_pad = jnp.zeros((pad, c_blk), F32)
    xpad[pl.ds(0, pad), :] = zeros_pad
    xpad[pl.ds(pad + s, pad), :] = zeros_pad

    def copy_chunk(i, carry):
        t0 = pl.multiple_of(i * LRU_CHUNK, LRU_CHUNK)
        xpad[pl.ds(pad + t0, LRU_CHUNK), :] = lx_ref[0, pl.ds(t0, LRU_CHUNK), :]
        return carry

    lax.fori_loop(0, n_chunks, copy_chunk, 0)

    cw = cw_ref[...]
    cb = cb_ref[...]
    bias = b_ref[0]
    log_sig = jax.nn.log_sigmoid(lam_ref[0])
    w_cat = w_ref[0]

    def gates_chunk(i, carry):
        t0 = pl.multiple_of(i * LRU_CHUNK, LRU_CHUNK)
        win = xpad[pl.ds(t0, LRU_CHUNK + 2 * pad), :]
        c = cb
        for k in range(4):
            c = c + cw[k:k + 1, :] * win[pad - 2 + k:pad - 2 + k + LRU_CHUNK, :]
        z = jnp.dot(c.astype(BF16), w_cat, preferred_element_type=F32) + bias
        for d, (a_scr, b_scr) in enumerate(((af, bf), (ab, bb))):
            r = jax.nn.sigmoid(z[:, (2 * d) * c_blk:(2 * d + 1) * c_blk])
            ig = jax.nn.sigmoid(z[:, (2 * d + 1) * c_blk:(2 * d + 2) * c_blk])
            log_a = LRU_C * r * log_sig[:, d * c_blk:(d + 1) * c_blk]
            a = jnp.exp(log_a)
            th = jnp.tanh(log_a)
            mult = jnp.sqrt(-2.0 * th / (1.0 - th))
            a_scr[pl.ds(t0, LRU_CHUNK), :] = a
            b_scr[pl.ds(t0, LRU_CHUNK), :] = mult * (ig * c)
        return carry

    lax.fori_loop(0, n_chunks, gates_chunk, 0)

    def scan_step(a_scr, b_scr, o, carry):
        h, p = carry
        a = a_scr[pl.ds(o, N_SEGMENTS, stride=seg), :]
        b = b_scr[pl.ds(o, N_SEGMENTS, stride=seg), :]
        h = a * h + b
        p = a * p
        b_scr[pl.ds(o, N_SEGMENTS, stride=seg), :] = h
        a_scr[pl.ds(o, N_SEGMENTS, stride=seg), :] = p
        return h, p

    init = (jnp.zeros((N_SEGMENTS, c_blk), F32), jnp.ones((N_SEGMENTS, c_blk), F32))
    hf_end, pf_end = lax.fori_loop(0, seg, functools.partial(scan_step, af, bf), init)
    hb_end, pb_end = lax.fori_loop(
        0, seg, lambda i, c: scan_step(ab, bb, seg - 1 - i, c), init)

    cf = jnp.zeros((N_SEGMENTS, c_blk), F32)
    cbk = jnp.zeros((N_SEGMENTS, c_blk), F32)
    for _ in range(N_SEGMENTS - 1):
        cf = _shift_rows(hf_end + pf_end * cf, 1)
        cbk = _shift_rows(hb_end + pb_end * cbk, -1)

    out_chunk = min(LRU_CHUNK, seg)
    per_seg = seg // out_chunk
    for j in range(N_SEGMENTS):
        cf_j = cf[j:j + 1, :]
        cb_j = cbk[j:j + 1, :]

        def out_step(i, carry, j=j, cf_j=cf_j, cb_j=cb_j):
            t0 = pl.multiple_of(j * seg + i * out_chunk, out_chunk)
            rows = pl.ds(t0, out_chunk)
            hsum = (bf[rows, :] + af[rows, :] * cf_j) + (bb[rows, :] + ab[rows, :] * cb_j)
            y_ref[0, rows, :] = (jax.nn.gelu(lg_ref[0, rows, :]) * hsum).astype(y_ref.dtype)
            return carry

        lax.fori_loop(0, per_seg, out_step, 0)


def _lru(lx, lg, conv_w, conv_b, w_cat, b_cat, lam_cat):
    bsz, s, d_lru = lx.shape
    c_blk = LANES
    n_blk = d_lru // c_blk
    scr = pltpu.VMEM((s, c_blk), F32)
    return pl.pallas_call(
        _lru_kernel,
        grid=(bsz, n_blk),
        in_specs=[
            pl.BlockSpec((1, s, c_blk), lambda b, c: (b, 0, c)),
            pl.BlockSpec((1, s, c_blk), lambda b, c: (b, 0, c)),
            pl.BlockSpec((4, c_blk), lambda b, c: (0, c)),
            pl.BlockSpec((1, c_blk), lambda b, c: (0, c)),
            pl.BlockSpec((1, c_blk, 4 * c_blk), lambda b, c: (c, 0, 0)),
            pl.BlockSpec((1, 1, 4 * c_blk), lambda b, c: (c, 0, 0)),
            pl.BlockSpec((1, 1, 2 * c_blk), lambda b, c: (c, 0, 0)),
        ],
        out_specs=pl.BlockSpec((1, s, c_blk), lambda b, c: (b, 0, c)),
        out_shape=jax.ShapeDtypeStruct((bsz, s, d_lru), BF16),
        scratch_shapes=[pltpu.VMEM((s + 2 * SUBLANES, c_blk), F32), scr, scr, scr, scr],
        compiler_params=_cparams(("parallel", "parallel")),
        name="lru",
    )(lx, lg, conv_w, conv_b, w_cat, b_cat, lam_cat)


def _fourier_kernel(f_ref, ka_ref, kb_ref, cc_ref, sc_ref, twc_ref, tws_ref, y_ref,
                    ar_scr, ai_scr, *, scale):
    n1 = f_ref.shape[1]
    n_u = f_ref.shape[2]
    cols = f_ref.shape[4]
    r = SUBLANES * n1
    n2 = n_u * SUBLANES

    def stage_a(u, carry):
        xu = f_ref[0, :, u].reshape(r, cols)
        a = jnp.dot(ka_ref[...], xu.astype(BF16), preferred_element_type=F32)
        a_re, a_im = a[:r], a[r:]
        c = twc_ref[u]
        sn = tws_ref[u]
        ar_scr[u] = a_re * c - a_im * sn
        ai_scr[u] = a_re * sn + a_im * c
        return carry

    lax.fori_loop(0, n_u, stage_a, 0)

    def stage_b(k1, carry):
        rows = pl.ds(pl.multiple_of(k1 * SUBLANES, SUBLANES), SUBLANES)
        a_re = ar_scr[:, rows, :].reshape(n2, cols)
        a_im = ai_scr[:, rows, :].reshape(n2, cols)
        st = jnp.concatenate([a_re, a_im], axis=0).astype(BF16)
        x = jnp.dot(kb_ref[...], st, preferred_element_type=F32)
        y = (jnp.dot(x[:n2].astype(BF16), cc_ref[...], preferred_element_type=F32)
             + jnp.dot(x[n2:].astype(BF16), sc_ref[...], preferred_element_type=F32))
        y_ref[0, pl.ds(k1, n2, stride=n1), :] = y * scale
        return carry

    lax.fori_loop(0, n1, stage_b, 0)


def _dft_tables(s, c_grp):
    n2 = FOURIER_N2
    n1 = s // n2
    n_u = n2 // SUBLANES

    def cos_sin(num, den):
        ang = (2.0 * np.pi / den) * (num % den).astype(np.float64)
        return np.cos(ang), np.sin(ang)

    def const(a, dtype=F32):
        return jnp.asarray(a.astype(np.float32)).astype(dtype)

    i1 = np.arange(n1, dtype=np.int64)
    c1, s1 = cos_sin(i1[:, None] * i1[None, :], n1)
    eye8 = np.eye(SUBLANES)
    ka = const(np.concatenate([np.kron(c1, eye8), np.kron(-s1, eye8)], axis=0), BF16)

    i2 = np.arange(n2, dtype=np.int64)
    c2, s2 = cos_sin(i2[:, None] * i2[None, :], n2)
    kb = const(np.block([[c2, s2], [-s2, c2]]), BF16)

    ic = np.arange(c_grp, dtype=np.int64)
    cc, sc = cos_sin(ic[:, None] * ic[None, :], c_grp)

    u = np.arange(n_u, dtype=np.int64)[:, None, None]
    k1 = np.arange(n1, dtype=np.int64)[None, :, None]
    v = np.arange(SUBLANES, dtype=np.int64)[None, None, :]
    tc, ts = cos_sin(k1 * (SUBLANES * u + v), s)
    shape = (n_u, n1 * SUBLANES, c_grp)
    twc = jnp.broadcast_to(const(tc.reshape(n_u, -1, 1)), shape)
    tws = jnp.broadcast_to(const(-ts.reshape(n_u, -1, 1)), shape)
    return ka, kb, const(cc, BF16), const(sc, BF16), twc, tws


def _fourier(four, n_groups):
    bsz, s, d_four = four.shape
    c_grp = d_four // n_groups
    n2 = FOURIER_N2
    n1 = s // n2
    n_u = n2 // SUBLANES
    r = SUBLANES * n1
    ka, kb, cc, sc, twc, tws = _dft_tables(s, c_grp)
    f5 = four.reshape(bsz, n1, n_u, SUBLANES, d_four)
    scale = 1.0 / math.sqrt(s * c_grp)
    const2 = lambda b, g: (0, 0)
    const3 = lambda b, g: (0, 0, 0)
    return pl.pallas_call(
        functools.partial(_fourier_kernel, scale=scale),
        grid=(bsz, n_groups),
        in_specs=[
            pl.BlockSpec((1, n1, n_u, SUBLANES, c_grp), lambda b, g: (b, 0, 0, 0, g)),
            pl.BlockSpec(ka.shape, const2),
            pl.BlockSpec(kb.shape, const2),
            pl.BlockSpec(cc.shape, const2),
            pl.BlockSpec(sc.shape, const2),
            pl.BlockSpec(twc.shape, const3),
            pl.BlockSpec(tws.shape, const3),
        ],
        out_specs=pl.BlockSpec((1, s, c_grp), lambda b, g: (b, 0, g)),
        out_shape=jax.ShapeDtypeStruct((bsz, s, d_four), F32),
        scratch_shapes=[pltpu.VMEM((n_u, r, c_grp), F32), pltpu.VMEM((n_u, r, c_grp), F32)],
        compiler_params=_cparams(("parallel", "parallel")),
        name="fourier",
    )(f5, ka, kb, cc, sc, twc, tws)


def _outproj_kernel(x_ref, yl_ref, yf_ref, wo1_ref, wo2_ref, g2_ref, wr_ref,
                    x1_ref, h2_ref, aff_ref):
    x1 = (x_ref[0]
          + jnp.dot(yl_ref[0], wo1_ref[...], preferred_element_type=F32)
          + jnp.dot(yf_ref[0].astype(BF16), wo2_ref[...], preferred_element_type=F32))
    x1_ref[0] = x1
    h2 = _rms_scale(x1) * g2_ref[...]
    h2_ref[0] = h2.astype(BF16)
    logits = lax.dot_general(wr_ref[...], h2, (((1,), (1,)), ((), ())),
                             precision=lax.Precision.HIGHEST,
                             preferred_element_type=F32)
    ex = jnp.exp(logits - jnp.max(logits, axis=0, keepdims=True))
    aff_ref[0] = ex / jnp.sum(ex, axis=0, keepdims=True)


def _outproj(x, y_lru, y_four, wo1, wo2, g2, wr_t):
    bsz, s, d = x.shape
    d_lru = y_lru.shape[-1]
    d_four = y_four.shape[-1]
    n_e = wr_t.shape[0]
    tile = lambda w: pl.BlockSpec((1, ROW_TILE, w), lambda b, i: (b, i, 0))
    const = lambda shape: pl.BlockSpec(shape, lambda b, i: (0, 0))
    return pl.pallas_call(
        _outproj_kernel,
        grid=(bsz, s // ROW_TILE),
        in_specs=[tile(d), tile(d_lru), tile(d_four), const(wo1.shape), const(wo2.shape),
                  const(g2.shape), const(wr_t.shape)],
        out_specs=[tile(d), tile(d),
                   pl.BlockSpec((1, n_e, ROW_TILE), lambda b, i: (b, 0, i))],
        out_shape=[
            jax.ShapeDtypeStruct((bsz, s, d), F32),
            jax.ShapeDtypeStruct((bsz, s, d), BF16),
            jax.ShapeDtypeStruct((bsz, n_e, s), F32),
        ],
        compiler_params=_cparams(("parallel", "parallel")),
        name="outproj",
    )(x, y_lru, y_four, wo1, wo2, g2, wr_t)


def _select_kernel(aff_ref, tri_ref, pos_ref, starts_ref, *, cap):
    v = aff_ref[0]
    n_e, s = v.shape
    cap_f = float(cap)

    def midpoint(lo, hi):
        mid = 0.5 * (lo + hi)
        return mid, (mid > lo) & (mid < hi)

    def cond(carry):
        _, active = midpoint(*carry)
        return jnp.max(active.astype(F32)) > 0.0

    def body(carry):
        lo, hi = carry
        mid, active = midpoint(lo, hi)
        cnt = jnp.sum((v >= mid).astype(F32), axis=1, keepdims=True)
        enough = cnt >= cap_f
        return (jnp.where(active & enough, mid, lo), jnp.where(active & (~enough), mid, hi))

    lo0 = jnp.zeros((n_e, 1), F32)
    hi0 = jnp.full((n_e, 1), 2.0, F32)
    thr, _ = lax.while_loop(cond, body, (lo0, hi0))

    above = v > thr
    tie = v == thr
    need = cap_f - jnp.sum(above.astype(F32), axis=1, keepdims=True)

    n_tiles = s // TOKEN_TILE
    tri = tri_ref[...]
    lane = lax.broadcasted_iota(jnp.int32, (n_e, LANES), 1)

    def prefix(mask_f, want_starts):
        run = jnp.zeros((n_e, 1), F32)
        starts = jnp.zeros((n_e, LANES), F32)
        pieces = []
        for t in range(n_tiles):
            m = mask_f[:, t * TOKEN_TILE:(t + 1) * TOKEN_TILE]
            incl = jnp.dot(m.astype(BF16), tri, preferred_element_type=F32)
            pieces.append(run + incl - m)
            if want_starts:
                starts = jnp.where(lane == t, run, starts)
            run = run + incl[:, TOKEN_TILE - 1:TOKEN_TILE]
        if want_starts:
            starts = jnp.where(lane == n_tiles, run, starts)
        return jnp.concatenate(pieces, axis=1), starts

    tie_rank, _ = prefix(tie.astype(F32), False)
    sel = above | (tie & (tie_rank < need))
    pos, starts = prefix(sel.astype(F32), True)
    pos_ref[0] = jnp.where(sel, pos, -1.0).astype(jnp.int32)
    starts_ref[0] = starts.astype(jnp.int32)


def _select(aff_t, cap):
    bsz, n_e, s = aff_t.shape
    idx = jnp.arange(TOKEN_TILE, dtype=jnp.int32)
    tri = (idx[:, None] <= idx[None, :]).astype(BF16)
    return pl.pallas_call(
        functools.partial(_select_kernel, cap=cap),
        grid=(bsz,),
        in_specs=[
            pl.BlockSpec((1, n_e, s), lambda b: (b, 0, 0)),
            pl.BlockSpec(tri.shape, lambda b: (0, 0)),
        ],
        out_specs=[
            pl.BlockSpec((1, n_e, s), lambda b: (b, 0, 0)),
            pl.BlockSpec((1, n_e, LANES), lambda b: (b, 0, 0)),
        ],
        out_shape=[
            jax.ShapeDtypeStruct((bsz, n_e, s), jnp.int32),
            jax.ShapeDtypeStruct((bsz, n_e, LANES), jnp.int32),
        ],
        compiler_params=_cparams(("parallel",)),
        name="select",
    )(aff_t, tri)


def _moe_kernel(starts_ref, pos_ref, h2_ref, wg_ref, wu_ref, wd_ref, eo_ref, x_scr,
                *, n_experts):
    b = pl.program_id(0)
    e = pl.program_id(1)
    k = pl.program_id(2)
    n_k = pl.num_programs(2)
    n_blocks = x_scr.shape[0]

    @pl.when(k == 0)
    def _():
        x_scr[...] = jnp.zeros_like(x_scr)

    base = (b * n_experts + e) * LANES
    s0 = starts_ref[base + k]
    s1 = starts_ref[base + k + 1]
    posrow = pos_ref[0, 0, 0]
    slot = lax.broadcasted_iota(jnp.int32, (SLOT_BLOCK, TOKEN_TILE), 0)
    for m in range(n_blocks):
        @pl.when((s0 < SLOT_BLOCK * (m + 1)) & (s1 > SLOT_BLOCK * m))
        def _(m=m):
            onehot = (posrow == slot + SLOT_BLOCK * m).astype(BF16)
            x_scr[m] += jnp.dot(onehot, h2_ref[0], preferred_element_type=F32)

    @pl.when(k == n_k - 1)
    def _():
        for m in range(n_blocks):
            xm = x_scr[m].astype(BF16)
            g = jnp.dot(xm, wg_ref[0], preferred_element_type=F32)
            u = jnp.dot(xm, wu_ref[0], preferred_element_type=F32)
            act = (jax.nn.silu(g) * u).astype(BF16)
            eo = jnp.dot(act, wd_ref[0], preferred_element_type=F32)
            eo_ref[0, 0, pl.ds(m * SLOT_BLOCK, SLOT_BLOCK), :] = eo.astype(eo_ref.dtype)


def _moe(starts_flat, pos5, h2, wg, wu, wd, cap):
    bsz, s, d = h2.shape
    n_e, _, d_ff = wg.shape
    n_k = s // TOKEN_TILE
    grid_spec = pltpu.PrefetchScalarGridSpec(
        num_scalar_prefetch=1,
        grid=(bsz, n_e, n_k),
        in_specs=[
            pl.BlockSpec((1, 1, 1, 1, TOKEN_TILE), lambda b, e, k, st: (b, e, k, 0, 0)),
            pl.BlockSpec((1, TOKEN_TILE, d), lambda b, e, k, st: (b, k, 0)),
            pl.BlockSpec((1, d, d_ff), lambda b, e, k, st: (e, 0, 0)),
            pl.BlockSpec((1, d, d_ff), lambda b, e, k, st: (e, 0, 0)),
            pl.BlockSpec((1, d_ff, d), lambda b, e, k, st: (e, 0, 0)),
        ],
        out_specs=pl.BlockSpec((1, 1, cap, d), lambda b, e, k, st: (b, e, 0, 0)),
        scratch_shapes=[pltpu.VMEM((cap // SLOT_BLOCK, SLOT_BLOCK, d), F32)],
    )
    return pl.pallas_call(
        functools.partial(_moe_kernel, n_experts=n_e),
        grid_spec=grid_spec,
        out_shape=jax.ShapeDtypeStruct((bsz, n_e, cap, d), BF16),
        compiler_params=_cparams(("parallel", "parallel", "arbitrary")),
        name="moe",
    )(starts_flat, pos5, h2, wg, wu, wd)


def _combine_kernel(starts_ref, x1_ref, pos_ref, gate_ref, eoa_ref, eob_ref, gf_ref,
                    out_ref, acc, *, n_experts):
    b = pl.program_id(0)
    k = pl.program_id(1)
    e = pl.program_id(2)

    @pl.when(e == 0)
    def _():
        acc[...] = x1_ref[0]

    base = (b * n_experts + e) * LANES
    s0 = starts_ref[base + k]
    s1 = starts_ref[base + k + 1]
    lane_e = lax.broadcasted_iota(jnp.int32, pos_ref.shape[1:], 1)
    pcol = jnp.sum(jnp.where(lane_e == e, pos_ref[0], 0), axis=1, keepdims=True)
    gcol = jnp.sum(jnp.where(lane_e == e, gate_ref[0], 0.0), axis=1, keepdims=True)
    slot = lax.broadcasted_iota(jnp.int32, (TOKEN_TILE, SLOT_BLOCK), 1)
    m0 = s0 // SLOT_BLOCK

    @pl.when(s1 > s0)
    def _():
        onehot = (pcol - m0 * SLOT_BLOCK == slot).astype(BF16)
        acc[...] += gcol * jnp.dot(onehot, eoa_ref[0, 0], preferred_element_type=F32)

    @pl.when(s1 > (m0 + 1) * SLOT_BLOCK)
    def _():
        onehot = (pcol - (m0 + 1) * SLOT_BLOCK == slot).astype(BF16)
        acc[...] += gcol * jnp.dot(onehot, eob_ref[0, 0], preferred_element_type=F32)

    @pl.when(e == n_experts - 1)
    def _():
        out_ref[0] = _rms_scale(acc[...]) * gf_ref[...]


def _combine(starts_flat, x1, pos_c, gate_c, eo, gf):
    bsz, s, d = x1.shape
    n_e = pos_c.shape[-1]
    cap = eo.shape[2]
    n_k = s // TOKEN_TILE
    last_block = cap // SLOT_BLOCK - 1

    def block_a(b, k, e, st):
        m0 = st[(b * n_e + e) * LANES + k] // SLOT_BLOCK
        return (b, e, jnp.minimum(m0, last_block), 0)

    def block_b(b, k, e, st):
        m0 = st[(b * n_e + e) * LANES + k] // SLOT_BLOCK
        return (b, e, jnp.minimum(m0 + 1, last_block), 0)

    grid_spec = pltpu.PrefetchScalarGridSpec(
        num_scalar_prefetch=1,
        grid=(bsz, n_k, n_e),
        in_specs=[
            pl.BlockSpec((1, TOKEN_TILE, d), lambda b, k, e, st: (b, k, 0)),
            pl.BlockSpec((1, TOKEN_TILE, n_e), lambda b, k, e, st: (b, k, 0)),
            pl.BlockSpec((1, TOKEN_TILE, n_e), lambda b, k, e, st: (b, k, 0)),
            pl.BlockSpec((1, 1, SLOT_BLOCK, d), block_a),
            pl.BlockSpec((1, 1, SLOT_BLOCK, d), block_b),
            pl.BlockSpec((1, d), lambda b, k, e, st: (0, 0)),
        ],
        out_specs=pl.BlockSpec((1, TOKEN_TILE, d), lambda b, k, e, st: (b, k, 0)),
        scratch_shapes=[pltpu.VMEM((TOKEN_TILE, d), F32)],
    )
    return pl.pallas_call(
        functools.partial(_combine_kernel, n_experts=n_e),
        grid_spec=grid_spec,
        out_shape=jax.ShapeDtypeStruct((bsz, s, d), F32),
        compiler_params=_cparams(("parallel", "parallel", "arbitrary")),
        name="combine",
    )(starts_flat, x1, pos_c, gate_c, eo, eo, gf)


def _block_diag(w):
    h, hd, _ = w.shape
    eye = jnp.eye(h, dtype=w.dtype)
    return (eye[:, None, :, None] * w[:, :, None, :]).reshape(h * hd, h * hd)


def _lru_params(wa_f, wx_f, wa_b, wx_b, ba_f, bx_f, ba_b, bx_b, lam_f, lam_b):
    d_lru = ba_f.shape[0]
    n_blk = d_lru // LANES
    mats = [_block_diag(w) for w in (wa_f, wx_f, wa_b, wx_b)]
    w_cat = jnp.stack([
        jnp.concatenate([m[c * LANES:(c + 1) * LANES, c * LANES:(c + 1) * LANES] for m in mats], axis=1)
        for c in range(n_blk)]).astype(BF16)
    b_cat = jnp.stack([
        jnp.concatenate([v[c * LANES:(c + 1) * LANES] for v in (ba_f, bx_f, ba_b, bx_b)])
        for c in range(n_blk)])[:, None, :]
    lam_cat = jnp.stack([
        jnp.concatenate([v[c * LANES:(c + 1) * LANES] for v in (lam_f, lam_b)])
        for c in range(n_blk)])[:, None, :]
    return w_cat, b_cat, lam_cat


def kernel(x, norm1_g, w_in, conv_w, conv_b, lru_wa_f, lru_ba_f, lru_wx_f, lru_bx_f, lru_lam_f,
           lru_wa_b, lru_ba_b, lru_wx_b, lru_bx_b, lru_lam_b, w_out, norm2_g, w_router,
           w_gate, w_up, w_down, normf_g):
    bsz, s, d = x.shape
    d_lru = conv_b.shape[0]
    d_four = w_in.shape[1] - 2 * d_lru
    n_e = w_router.shape[1]
    cap = CAPACITY_FACTOR * s // n_e
    assert s % LRU_CHUNK == 0 and (s // N_SEGMENTS) % min(LRU_CHUNK, s // N_SEGMENTS) == 0
    assert s % FOURIER_N2 == 0 and cap % SLOT_BLOCK == 0 and s // TOKEN_TILE < LANES

    lx, lg, fo = _inproj(x.reshape(bsz * s, d), norm1_g[None, :], w_in.astype(BF16), d_lru, d_four)
    lx = lx.reshape(bsz, s, d_lru)
    lg = lg.reshape(bsz, s, d_lru)
    fo = fo.reshape(bsz, s, d_four)

    w_cat, b_cat, lam_cat = _lru_params(lru_wa_f, lru_wx_f, lru_wa_b, lru_wx_b,
                                        lru_ba_f, lru_bx_f, lru_ba_b, lru_bx_b,
                                        lru_lam_f, lru_lam_b)
    y_lru = _lru(lx, lg, conv_w, conv_b[None, :], w_cat, b_cat, lam_cat)
    y_four = _fourier(fo, FOURIER_GROUPS)

    w_out_bf = w_out.astype(BF16)
    x1, h2, aff_t = _outproj(x, y_lru, y_four, w_out_bf[:d_lru], w_out_bf[d_lru:],
                             norm2_g[None, :], w_router.T)

    pos_r, starts = _select(aff_t, cap)
    starts_flat = starts.reshape(-1)
    n_k = s // TOKEN_TILE
    eo = _moe(starts_flat, pos_r.reshape(bsz, n_e, n_k, 1, TOKEN_TILE), h2,
              w_gate.astype(BF16), w_up.astype(BF16), w_down.astype(BF16), cap)

    pos_c = jnp.transpose(pos_r, (0, 2, 1))
    gate_c = jnp.transpose(aff_t, (0, 2, 1))
    return _combine(starts_flat, x1, pos_c, gate_c, eo, normf_g[None, :])
```

```python
import functools
import math

import jax
import jax.numpy as jnp
import numpy as np
from jax import lax
from jax.experimental import pallas as pl
from jax.experimental.pallas import tpu as pltpu

F32 = jnp.float32
BF16 = jnp.bfloat16

EPS = 1e-6
LRU_C = 8.0
LRU_HEADS = 8
N_EXPERTS = 16
CAPACITY_FACTOR = 2
FOURIER_GROUPS = 4

SUBLANES = 8
LANES = 128
MXU_DIM = 256
VMEM_LIMIT_BYTES = 56 * 1024 * 1024

ROW_TILE = 512
LRU_CHUNK = 512
N_SEGMENTS = SUBLANES
SLOT_BLOCK = MXU_DIM
TOKEN_TILE = MXU_DIM
FOURIER_N2 = LANES
FF_CHUNK = 1024
WIN_ALIGN = 16
WIN_ROWS = SLOT_BLOCK + WIN_ALIGN


def _cparams(semantics):
    return pltpu.CompilerParams(dimension_semantics=semantics,
                                vmem_limit_bytes=VMEM_LIMIT_BYTES)


def _rms_scale(x):
    return x * lax.rsqrt(jnp.mean(x * x, axis=-1, keepdims=True) + EPS)


def _inproj_kernel(x_ref, g_ref, w_ref, lx_ref, lg_ref, fo_ref):
    h = _rms_scale(x_ref[...]) * g_ref[...]
    p = jnp.dot(h.astype(BF16), w_ref[...], preferred_element_type=F32)
    d = lx_ref.shape[-1]
    lx_ref[...] = p[:, :d]
    lg_ref[...] = p[:, d:2 * d]
    fo_ref[...] = p[:, 2 * d:]


def _inproj(x2, g, w_bf, d_lru, d_four):
    m, d = x2.shape
    n = w_bf.shape[1]
    return pl.pallas_call(
        _inproj_kernel,
        grid=(m // ROW_TILE,),
        in_specs=[
            pl.BlockSpec((ROW_TILE, d), lambda i: (i, 0)),
            pl.BlockSpec((1, d), lambda i: (0, 0)),
            pl.BlockSpec((d, n), lambda i: (0, 0)),
        ],
        out_specs=[
            pl.BlockSpec((ROW_TILE, d_lru), lambda i: (i, 0)),
            pl.BlockSpec((ROW_TILE, d_lru), lambda i: (i, 0)),
            pl.BlockSpec((ROW_TILE, d_four), lambda i: (i, 0)),
        ],
        out_shape=[
            jax.ShapeDtypeStruct((m, d_lru), F32),
            jax.ShapeDtypeStruct((m, d_lru), F32),
            jax.ShapeDtypeStruct((m, d_four), F32),
        ],
        compiler_params=_cparams(("parallel",)),
        name="inproj",
    )(x2, g, w_bf)


def _shift_rows(x, shift):
    n = x.shape[0]
    rows = lax.broadcasted_iota(jnp.int32, x.shape, 0)
    rolled = pltpu.roll(x, shift % n, axis=0)
    keep = (rows >= shift) if shift > 0 else (rows < n + shift)
    return jnp.where(keep, rolled, 0.0)


def _lru_kernel(lx_ref, lg_ref, cw_ref, cb_ref, w_ref, b_ref, lam_ref, y_ref,
                xpad, af, bf, ab, bb):
    s = lx_ref.shape[1]
    c_blk = lx_ref.shape[2]
    seg = s // N_SEGMENTS
    n_chunks = s // LRU_CHUNK
    pad = SUBLANES

    zeros_pad = jnp.zeros((pad, c_blk), F32)
    xpad[pl.ds(0, pad), :] = zeros_pad
    xpad[pl.ds(pad + s, pad), :] = zeros_pad

    def copy_chunk(i, carry):
        t0 = pl.multiple_of(i * LRU_CHUNK, LRU_CHUNK)
        xpad[pl.ds(pad + t0, LRU_CHUNK), :] = lx_ref[0, pl.ds(t0, LRU_CHUNK), :]
        return carry

    lax.fori_loop(0, n_chunks, copy_chunk, 0)

    cw = cw_ref[...]
    conv_width = cw.shape[0]
    cb = cb_ref[...]
    bias = b_ref[0]
    log_sig = jax.nn.log_sigmoid(lam_ref[0])
    w_cat = w_ref[0]

    def gates_chunk(i, carry):
        t0 = pl.multiple_of(i * LRU_CHUNK, LRU_CHUNK)
        win = xpad[pl.ds(t0, LRU_CHUNK + 2 * pad), :]
        c = cb
        left = conv_width // 2
        for k in range(conv_width):
            c = c + cw[k:k + 1, :] * win[pad - left + k:pad - left + k + LRU_CHUNK, :]
        z = jnp.dot(c.astype(BF16), w_cat, preferred_element_type=F32) + bias
        for d, (a_scr, b_scr) in enumerate(((af, bf), (ab, bb))):
            r = jax.nn.sigmoid(z[:, (2 * d) * c_blk:(2 * d + 1) * c_blk])
            ig = jax.nn.sigmoid(z[:, (2 * d + 1) * c_blk:(2 * d + 2) * c_blk])
            log_a = LRU_C * r * log_sig[:, d * c_blk:(d + 1) * c_blk]
            a = jnp.exp(log_a)
            th = jnp.tanh(log_a)
            mult = jnp.sqrt(-2.0 * th / (1.0 - th))
            a_scr[pl.ds(t0, LRU_CHUNK), :] = a
            b_scr[pl.ds(t0, LRU_CHUNK), :] = mult * (ig * c)
        return carry

    lax.fori_loop(0, n_chunks, gates_chunk, 0)

    def scan_step(a_scr, b_scr, o, carry):
        h, p = carry
        a = a_scr[pl.ds(o, N_SEGMENTS, stride=seg), :]
        b = b_scr[pl.ds(o, N_SEGMENTS, stride=seg), :]
        h = a * h + b
        p = a * p
        b_scr[pl.ds(o, N_SEGMENTS, stride=seg), :] = h
        a_scr[pl.ds(o, N_SEGMENTS, stride=seg), :] = p
        return h, p

    init = (jnp.zeros((N_SEGMENTS, c_blk), F32), jnp.ones((N_SEGMENTS, c_blk), F32))
    hf_end, pf_end = lax.fori_loop(0, seg, functools.partial(scan_step, af, bf), init)
    hb_end, pb_end = lax.fori_loop(
        0, seg, lambda i, c: scan_step(ab, bb, seg - 1 - i, c), init)

    cf = jnp.zeros((N_SEGMENTS, c_blk), F32)
    cbk = jnp.zeros((N_SEGMENTS, c_blk), F32)
    for _ in range(N_SEGMENTS - 1):
        cf = _shift_rows(hf_end + pf_end * cf, 1)
        cbk = _shift_rows(hb_end + pb_end * cbk, -1)

    out_chunk = min(LRU_CHUNK, seg)
    per_seg = seg // out_chunk
    for j in range(N_SEGMENTS):
        cf_j = cf[j:j + 1, :]
        cb_j = cbk[j:j + 1, :]

        def out_step(i, carry, j=j, cf_j=cf_j, cb_j=cb_j):
            t0 = pl.multiple_of(j * seg + i * out_chunk, out_chunk)
            rows = pl.ds(t0, out_chunk)
            hsum = (bf[rows, :] + af[rows, :] * cf_j) + (bb[rows, :] + ab[rows, :] * cb_j)
            y_ref[0, rows, :] = (jax.nn.gelu(lg_ref[0, rows, :]) * hsum).astype(y_ref.dtype)
            return carry

        lax.fori_loop(0, per_seg, out_step, 0)


def _lru(lx, lg, conv_w, conv_b, w_cat, b_cat, lam_cat):
    bsz, s, d_lru = lx.shape
    c_blk = LANES
    n_blk = d_lru // c_blk
    scr = pltpu.VMEM((s, c_blk), F32)
    return pl.pallas_call(
        _lru_kernel,
        grid=(bsz, n_blk),
        in_specs=[
            pl.BlockSpec((1, s, c_blk), lambda b, c: (b, 0, c)),
            pl.BlockSpec((1, s, c_blk), lambda b, c: (b, 0, c)),
            pl.BlockSpec((4, c_blk), lambda b, c: (0, c)),
            pl.BlockSpec((1, c_blk), lambda b, c: (0, c)),
            pl.BlockSpec((1, c_blk, 4 * c_blk), lambda b, c: (c, 0, 0)),
            pl.BlockSpec((1, 1, 4 * c_blk), lambda b, c: (c, 0, 0)),
            pl.BlockSpec((1, 1, 2 * c_blk), lambda b, c: (c, 0, 0)),
        ],
        out_specs=pl.BlockSpec((1, s, c_blk), lambda b, c: (b, 0, c)),
        out_shape=jax.ShapeDtypeStruct((bsz, s, d_lru), BF16),
        scratch_shapes=[pltpu.VMEM((s + 2 * SUBLANES, c_blk), F32), scr, scr, scr, scr],
        compiler_params=_cparams(("parallel", "parallel")),
        name="lru",
    )(lx, lg, conv_w, conv_b, w_cat, b_cat, lam_cat)


def _fourier_kernel(f_ref, ka_ref, kb_ref, cc_ref, sc_ref, twc_ref, tws_ref, y_ref,
                    ar_scr, ai_scr, *, scale):
    n1 = f_ref.shape[1]
    n_u = f_ref.shape[2]
    cols = f_ref.shape[4]
    r = SUBLANES * n1
    n2 = n_u * SUBLANES

    def stage_a(u, carry):
        xu = f_ref[0, :, u].reshape(r, cols)
        a = jnp.dot(ka_ref[...], xu.astype(BF16), preferred_element_type=F32)
        a_re, a_im = a[:r], a[r:]
        c = twc_ref[u]
        sn = tws_ref[u]
        ar_scr[u] = a_re * c - a_im * sn
        ai_scr[u] = a_re * sn + a_im * c
        return carry

    lax.fori_loop(0, n_u, stage_a, 0)

    def stage_b(k1, carry):
        rows = pl.ds(pl.multiple_of(k1 * SUBLANES, SUBLANES), SUBLANES)
        a_re = ar_scr[:, rows, :].reshape(n2, cols)
        a_im = ai_scr[:, rows, :].reshape(n2, cols)
        st = jnp.concatenate([a_re, a_im], axis=0).astype(BF16)
        x = jnp.dot(kb_ref[...], st, preferred_element_type=F32)
        y = (jnp.dot(x[:n2].astype(BF16), cc_ref[...], preferred_element_type=F32)
             + jnp.dot(x[n2:].astype(BF16), sc_ref[...], preferred_element_type=F32))
        y_ref[0, pl.ds(k1, n2, stride=n1), :] = y * scale
        return carry

    lax.fori_loop(0, n1, stage_b, 0)


def _dft_tables(s, c_grp):
    n2 = FOURIER_N2
    n1 = s // n2
    n_u = n2 // SUBLANES

    def cos_sin(num, den):
        ang = (2.0 * np.pi / den) * (num % den).astype(np.float64)
        return np.cos(ang), np.sin(ang)

    def const(a, dtype=F32):
        return jnp.asarray(a.astype(np.float32)).astype(dtype)

    i1 = np.arange(n1, dtype=np.int64)
    c1, s1 = cos_sin(i1[:, None] * i1[None, :], n1)
    eye8 = np.eye(SUBLANES)
    ka = const(np.concatenate([np.kron(c1, eye8), np.kron(-s1, eye8)], axis=0), BF16)

    i2 = np.arange(n2, dtype=np.int64)
    c2, s2 = cos_sin(i2[:, None] * i2[None, :], n2)
    kb = const(np.block([[c2, s2], [-s2, c2]]), BF16)

    ic = np.arange(c_grp, dtype=np.int64)
    cc, sc = cos_sin(ic[:, None] * ic[None, :], c_grp)

    u = np.arange(n_u, dtype=np.int64)[:, None, None]
    k1 = np.arange(n1, dtype=np.int64)[None, :, None]
    v = np.arange(SUBLANES, dtype=np.int64)[None, None, :]
    tc, ts = cos_sin(k1 * (SUBLANES * u + v), s)
    shape = (n_u, n1 * SUBLANES, c_grp)
    twc = jnp.broadcast_to(const(tc.reshape(n_u, -1, 1)), shape)
    tws = jnp.broadcast_to(const(-ts.reshape(n_u, -1, 1)), shape)
    return ka, kb, const(cc, BF16), const(sc, BF16), twc, tws


def _fourier(four, n_groups):
    bsz, s, d_four = four.shape
    c_grp = d_four // n_groups
    n2 = FOURIER_N2
    n1 = s // n2
    n_u = n2 // SUBLANES
    r = SUBLANES * n1
    ka, kb, cc, sc, twc, tws = _dft_tables(s, c_grp)
    f5 = four.reshape(bsz, n1, n_u, SUBLANES, d_four)
    scale = 1.0 / math.sqrt(s * c_grp)
    const2 = lambda b, g: (0, 0)
    const3 = lambda b, g: (0, 0, 0)
    return pl.pallas_call(
        functools.partial(_fourier_kernel, scale=scale),
        grid=(bsz, n_groups),
        in_specs=[
            pl.BlockSpec((1, n1, n_u, SUBLANES, c_grp), lambda b, g: (b, 0, 0, 0, g)),
            pl.BlockSpec(ka.shape, const2),
            pl.BlockSpec(kb.shape, const2),
            pl.BlockSpec(cc.shape, const2),
            pl.BlockSpec(sc.shape, const2),
            pl.BlockSpec(twc.shape, const3),
            pl.BlockSpec(tws.shape, const3),
        ],
        out_specs=pl.BlockSpec((1, s, c_grp), lambda b, g: (b, 0, g)),
        out_shape=jax.ShapeDtypeStruct((bsz, s, d_four), F32),
        scratch_shapes=[pltpu.VMEM((n_u, r, c_grp), F32), pltpu.VMEM((n_u, r, c_grp), F32)],
        compiler_params=_cparams(("parallel", "parallel")),
        name="fourier",
    )(f5, ka, kb, cc, sc, twc, tws)


def _outproj_kernel(x_ref, yl_ref, yf_ref, wo1_ref, wo2_ref, g2_ref, wr_ref,
                    x1_ref, h2_ref, aff_ref):
    x1 = (x_ref[0]
          + jnp.dot(yl_ref[0], wo1_ref[...], preferred_element_type=F32)
          + jnp.dot(yf_ref[0].astype(BF16), wo2_ref[...], preferred_element_type=F32))
    x1_ref[0] = x1
    h2 = _rms_scale(x1) * g2_ref[...]
    h2_ref[0] = h2.astype(BF16)
    logits = lax.dot_general(wr_ref[...], h2, (((1,), (1,)), ((), ())),
                             precision=lax.Precision.HIGHEST,
                             preferred_element_type=F32)
    ex = jnp.exp(logits - jnp.max(logits, axis=0, keepdims=True))
    aff_ref[0] = ex / jnp.sum(ex, axis=0, keepdims=True)


def _outproj(x, y_lru, y_four, wo1, wo2, g2, wr_t):
    bsz, s, d = x.shape
    d_lru = y_lru.shape[-1]
    d_four = y_four.shape[-1]
    n_e = wr_t.shape[0]
    tile = lambda w: pl.BlockSpec((1, ROW_TILE, w), lambda b, i: (b, i, 0))
    const = lambda shape: pl.BlockSpec(shape, lambda b, i: (0, 0))
    return pl.pallas_call(
        _outproj_kernel,
        grid=(bsz, s // ROW_TILE),
        in_specs=[tile(d), tile(d_lru), tile(d_four), const(wo1.shape), const(wo2.shape),
                  const(g2.shape), const(wr_t.shape)],
        out_specs=[tile(d), tile(d),
                   pl.BlockSpec((1, n_e, ROW_TILE), lambda b, i: (b, 0, i))],
        out_shape=[
            jax.ShapeDtypeStruct((bsz, s, d), F32),
            jax.ShapeDtypeStruct((bsz, s, d), BF16),
            jax.ShapeDtypeStruct((bsz, n_e, s), F32),
        ],
        compiler_params=_cparams(("parallel", "parallel")),
        name="outproj",
    )(x, y_lru, y_four, wo1, wo2, g2, wr_t)


def _select_kernel(aff_ref, tri_ref, pos_ref, starts_ref, *, cap):
    v = aff_ref[0]
    n_e, s = v.shape
    cap_f = float(cap)

    def midpoint(lo, hi):
        mid = 0.5 * (lo + hi)
        return mid, (mid > lo) & (mid < hi)

    def cond(carry):
        _, active = midpoint(*carry)
        return jnp.max(active.astype(F32)) > 0.0

    def body(carry):
        lo, hi = carry
        mid, active = midpoint(lo, hi)
        cnt = jnp.sum((v >= mid).astype(F32), axis=1, keepdims=True)
        enough = cnt >= cap_f
        return (jnp.where(active & enough, mid, lo), jnp.where(active & (~enough), mid, hi))

    lo0 = jnp.zeros((n_e, 1), F32)
    hi0 = jnp.full((n_e, 1), 2.0, F32)
    thr, _ = lax.while_loop(cond, body, (lo0, hi0))

    above = v > thr
    tie = v == thr
    need = cap_f - jnp.sum(above.astype(F32), axis=1, keepdims=True)

    n_tiles = s // TOKEN_TILE
    tri = tri_ref[...]
    lane = lax.broadcasted_iota(jnp.int32, (n_e, LANES), 1)

    def prefix(mask_f, want_starts):
        run = jnp.zeros((n_e, 1), F32)
        starts = jnp.zeros((n_e, LANES), F32)
        pieces = []
        for t in range(n_tiles):
            m = mask_f[:, t * TOKEN_TILE:(t + 1) * TOKEN_TILE]
            incl = jnp.dot(m.astype(BF16), tri, preferred_element_type=F32)
            pieces.append(run + incl - m)
            if want_starts:
                starts = jnp.where(lane == t, run, starts)
            run = run + incl[:, TOKEN_TILE - 1:TOKEN_TILE]
        if want_starts:
            starts = jnp.where(lane == n_tiles, run, starts)
        return jnp.concatenate(pieces, axis=1), starts

    tie_rank, _ = prefix(tie.astype(F32), False)
    sel = above | (tie & (tie_rank < need))
    pos, starts = prefix(sel.astype(F32), True)
    pos_ref[0] = jnp.where(sel, pos, -1.0).astype(jnp.int32)
    starts_ref[0] = starts.astype(jnp.int32)


def _select(aff_t, cap):
    bsz, n_e, s = aff_t.shape
    idx = jnp.arange(TOKEN_TILE, dtype=jnp.int32)
    tri = (idx[:, None] <= idx[None, :]).astype(BF16)
    return pl.pallas_call(
        functools.partial(_select_kernel, cap=cap),
        grid=(bsz,),
        in_specs=[
            pl.BlockSpec((1, n_e, s), lambda b: (b, 0, 0)),
            pl.BlockSpec(tri.shape, lambda b: (0, 0)),
        ],
        out_specs=[
            pl.BlockSpec((1, n_e, s), lambda b: (b, 0, 0)),
            pl.BlockSpec((1, n_e, LANES), lambda b: (b, 0, 0)),
        ],
        out_shape=[
            jax.ShapeDtypeStruct((bsz, n_e, s), jnp.int32),
            jax.ShapeDtypeStruct((bsz, n_e, LANES), jnp.int32),
        ],
        compiler_params=_cparams(("parallel",)),
        name="select",
    )(aff_t, tri)


def _moe_kernel(starts_ref, pos_ref, aff_ref, h2_ref, wg_ref, wu_ref, wd_ref, eo_ref,
                x_scr, g_scr, acc_scr, *, n_experts):
    b = pl.program_id(0)
    e = pl.program_id(1)
    f = pl.program_id(2)
    n_blocks = x_scr.shape[0]
    n_k = pos_ref.shape[2]

    @pl.when(f == 0)
    def _gather():
        x_scr[...] = jnp.zeros_like(x_scr)
        g_scr[...] = jnp.zeros_like(g_scr)
        acc_scr[...] = jnp.zeros_like(acc_scr)
        base = (b * n_experts + e) * LANES
        slot = lax.broadcasted_iota(jnp.int32, (SLOT_BLOCK, TOKEN_TILE), 0)

        def tile(k, carry):
            s0 = starts_ref[base + k]
            s1 = starts_ref[base + k + 1]
            posrow = pos_ref[0, 0, k]
            gaterow = aff_ref[0, 0, k]
            rows = pl.ds(pl.multiple_of(k * TOKEN_TILE, TOKEN_TILE), TOKEN_TILE)
            m0 = s0 // SLOT_BLOCK

            def pair(m):
                hit = posrow == slot + m * SLOT_BLOCK
                onehot = jnp.where(hit, 1.0, 0.0).astype(BF16)
                x_scr[m] += jnp.dot(onehot, h2_ref[0, rows, :], preferred_element_type=F32)
                g_scr[m] += jnp.sum(jnp.where(hit, gaterow, 0.0), axis=1, keepdims=True)

            @pl.when(s1 > s0)
            def _():
                pair(m0)

            @pl.when(s1 > (m0 + 1) * SLOT_BLOCK)
            def _():
                pair(m0 + 1)

            return carry

        lax.fori_loop(0, n_k, tile, 0)

    for m in range(n_blocks):
        xm = x_scr[m].astype(BF16)
        g = jnp.dot(xm, wg_ref[0], preferred_element_type=F32)
        u = jnp.dot(xm, wu_ref[0], preferred_element_type=F32)
        act = (jax.nn.silu(g) * u).astype(BF16)
        acc_scr[m] += jnp.dot(act, wd_ref[0], preferred_element_type=F32)

    @pl.when(f == pl.num_programs(2) - 1)
    def _():
        for m in range(n_blocks):
            rows = pl.ds(m * SLOT_BLOCK, SLOT_BLOCK)
            eo_ref[0, 0, rows, :] = (g_scr[m] * acc_scr[m]).astype(eo_ref.dtype)


def _moe(starts_flat, pos5, aff5, h2, wg, wu, wd, cap):
    bsz, s, d = h2.shape
    n_e, _, d_ff = wg.shape
    n_k = s // TOKEN_TILE
    n_blocks = cap // SLOT_BLOCK
    row5 = pl.BlockSpec((1, 1, n_k, 1, TOKEN_TILE), lambda b, e, f, st: (b, e, 0, 0, 0))
    grid_spec = pltpu.PrefetchScalarGridSpec(
        num_scalar_prefetch=1,
        grid=(bsz, n_e, d_ff // FF_CHUNK),
        in_specs=[
            row5,
            row5,
            pl.BlockSpec((1, s, d), lambda b, e, f, st: (b, 0, 0), pipeline_mode=pl.Buffered(1)),
            pl.BlockSpec((1, d, FF_CHUNK), lambda b, e, f, st: (e, 0, f)),
            pl.BlockSpec((1, d, FF_CHUNK), lambda b, e, f, st: (e, 0, f)),
            pl.BlockSpec((1, FF_CHUNK, d), lambda b, e, f, st: (e, f, 0)),
        ],
        out_specs=pl.BlockSpec((1, 1, cap, d), lambda b, e, f, st: (b, e, 0, 0)),
        scratch_shapes=[pltpu.VMEM((n_blocks, SLOT_BLOCK, d), F32),
                        pltpu.VMEM((n_blocks, SLOT_BLOCK, 1), F32),
                        pltpu.VMEM((n_blocks, SLOT_BLOCK, d), F32)],
    )
    return pl.pallas_call(
        functools.partial(_moe_kernel, n_experts=n_e),
        grid_spec=grid_spec,
        out_shape=jax.ShapeDtypeStruct((bsz, n_e, cap, d), BF16),
        compiler_params=_cparams(("parallel", "parallel", "arbitrary")),
        name="moe",
    )(starts_flat, pos5, aff5, h2, wg, wu, wd)


def _window_start(s0, cap):
    return jnp.minimum(s0 // WIN_ALIGN, (cap - WIN_ROWS) // WIN_ALIGN) * WIN_ALIGN


def _combine_kernel(starts_ref, x1_ref, pos_ref, gf_ref, *rest, n_experts, cap):
    win_refs = rest[:n_experts]
    out_ref, wcat_scr, acc_scr = rest[n_experts:]
    b = pl.program_id(0)
    k = pl.program_id(1)
    pos = pos_ref[0]
    lane = lax.broadcasted_iota(jnp.int32, (TOKEN_TILE, SLOT_BLOCK), 1)

    onehots = []
    rels = []
    for e in range(n_experts):
        base = (b * n_experts + e) * LANES
        ws = _window_start(starts_ref[base + k], cap)
        rel = pos[:, e:e + 1] - ws
        rels.append((rel, starts_ref[base + k + 1] - ws))
        onehots.append(jnp.where(rel == lane, 1.0, 0.0).astype(BF16))
        wcat_scr[pl.ds(e * SLOT_BLOCK, SLOT_BLOCK), :] = win_refs[e][pl.ds(0, SLOT_BLOCK), :]
    onehot = jnp.concatenate(onehots, axis=1)
    acc_scr[...] = x1_ref[0] + jnp.dot(onehot, wcat_scr[...], preferred_element_type=F32)

    tail_lane = lax.broadcasted_iota(jnp.int32, (TOKEN_TILE, WIN_ALIGN), 1)
    for e in range(n_experts):
        rel, end = rels[e]

        @pl.when(end > SLOT_BLOCK)
        def _(e=e, rel=rel):
            tail = jnp.where(rel - SLOT_BLOCK == tail_lane, 1.0, 0.0).astype(BF16)
            acc_scr[...] += jnp.dot(tail, win_refs[e][pl.ds(SLOT_BLOCK, WIN_ALIGN), :],
                                    preferred_element_type=F32)

    out_ref[0] = _rms_scale(acc_scr[...]) * gf_ref[...]


def _combine(starts_flat, x1, pos_c, eo, gf):
    bsz, s, d = x1.shape
    n_e = pos_c.shape[-1]
    cap = eo.shape[2]
    n_k = s // TOKEN_TILE

    def window_spec(e):
        def index_map(b, k, st):
            return (b, e, _window_start(st[(b * n_e + e) * LANES + k], cap), 0)
        return pl.BlockSpec((pl.Squeezed(), pl.Squeezed(), pl.Element(WIN_ROWS), pl.Element(d)),
                            index_map)

    grid_spec = pltpu.PrefetchScalarGridSpec(
        num_scalar_prefetch=1,
        grid=(bsz, n_k),
        in_specs=[
            pl.BlockSpec((1, TOKEN_TILE, d), lambda b, k, st: (b, k, 0)),
            pl.BlockSpec((1, TOKEN_TILE, n_e), lambda b, k, st: (b, k, 0)),
            pl.BlockSpec((1, d), lambda b, k, st: (0, 0)),
        ] + [window_spec(e) for e in range(n_e)],
        out_specs=pl.BlockSpec((1, TOKEN_TILE, d), lambda b, k, st: (b, k, 0)),
        scratch_shapes=[pltpu.VMEM((n_e * SLOT_BLOCK, d), BF16),
                        pltpu.VMEM((TOKEN_TILE, d), F32)],
    )
    return pl.pallas_call(
        functools.partial(_combine_kernel, n_experts=n_e, cap=cap),
        grid_spec=grid_spec,
        out_shape=jax.ShapeDtypeStruct((bsz, s, d), F32),
        compiler_params=_cparams(("parallel", "parallel")),
        name="combine",
    )(starts_flat, x1, pos_c, gf, *([eo] * n_e))


def _block_diag(w):
    h, hd, _ = w.shape
    eye = jnp.eye(h, dtype=w.dtype)
    return (eye[:, None, :, None] * w[:, :, None, :]).reshape(h * hd, h * hd)


def _lru_params(wa_f, wx_f, wa_b, wx_b, ba_f, bx_f, ba_b, bx_b, lam_f, lam_b):
    d_lru = ba_f.shape[0]
    n_blk = d_lru // LANES
    mats = [_block_diag(w) for w in (wa_f, wx_f, wa_b, wx_b)]
    w_cat = jnp.stack([
        jnp.concatenate([m[c * LANES:(c + 1) * LANES, c * LANES:(c + 1) * LANES] for m in mats], axis=1)
        for c in range(n_blk)]).astype(BF16)
    b_cat = jnp.stack([
        jnp.concatenate([v[c * LANES:(c + 1) * LANES] for v in (ba_f, bx_f, ba_b, bx_b)])
        for c in range(n_blk)])[:, None, :]
    lam_cat = jnp.stack([
        jnp.concatenate([v[c * LANES:(c + 1) * LANES] for v in (lam_f, lam_b)])
        for c in range(n_blk)])[:, None, :]
    return w_cat, b_cat, lam_cat


def kernel(x, norm1_g, w_in, conv_w, conv_b, lru_wa_f, lru_ba_f, lru_wx_f, lru_bx_f, lru_lam_f,
           lru_wa_b, lru_ba_b, lru_wx_b, lru_bx_b, lru_lam_b, w_out, norm2_g, w_router,
           w_gate, w_up, w_down, normf_g):
    bsz, s, d = x.shape
    d_lru = conv_b.shape[0]
    d_four = w_in.shape[1] - 2 * d_lru
    n_e = w_router.shape[1]
    cap = CAPACITY_FACTOR * s // n_e
    assert s % LRU_CHUNK == 0 and (s // N_SEGMENTS) % min(LRU_CHUNK, s // N_SEGMENTS) == 0
    assert s % FOURIER_N2 == 0 and cap % SLOT_BLOCK == 0 and s // TOKEN_TILE < LANES
    assert cap >= WIN_ROWS and w_gate.shape[2] % FF_CHUNK == 0

    lx, lg, fo = _inproj(x.reshape(bsz * s, d), norm1_g[None, :], w_in.astype(BF16), d_lru, d_four)
    lx = lx.reshape(bsz, s, d_lru)
    lg = lg.reshape(bsz, s, d_lru)
    fo = fo.reshape(bsz, s, d_four)

    w_cat, b_cat, lam_cat = _lru_params(lru_wa_f, lru_wx_f, lru_wa_b, lru_wx_b,
                                        lru_ba_f, lru_bx_f, lru_ba_b, lru_bx_b,
                                        lru_lam_f, lru_lam_b)
    y_lru = _lru(lx, lg, conv_w, conv_b[None, :], w_cat, b_cat, lam_cat)
    y_four = _fourier(fo, FOURIER_GROUPS)

    w_out_bf = w_out.astype(BF16)
    x1, h2, aff_t = _outproj(x, y_lru, y_four, w_out_bf[:d_lru], w_out_bf[d_lru:],
                             norm2_g[None, :], w_router.T)

    pos_r, starts = _select(aff_t, cap)
    starts_flat = starts.reshape(-1)
    n_k = s // TOKEN_TILE
    row5 = (bsz, n_e, n_k, 1, TOKEN_TILE)
    eo = _moe(starts_flat, pos_r.reshape(row5), aff_t.reshape(row5), h2,
              w_gate.astype(BF16), w_up.astype(BF16), w_down.astype(BF16), cap)

    pos_c = jnp.transpose(pos_r, (0, 2, 1))
    return _combine(starts_flat, x1, pos_c, eo, normf_g[None, :])
```

```python
import functools
import math

import jax
import jax.numpy as jnp
import numpy as np
from jax import lax
from jax.experimental import pallas as pl
from jax.experimental.pallas import tpu as pltpu

F32 = jnp.float32
BF16 = jnp.bfloat16

EPS = 1e-6
LRU_C = 8.0
LRU_HEADS = 8
N_EXPERTS = 16
CAPACITY_FACTOR = 2
FOURIER_GROUPS = 4

SUBLANES = 8
LANES = 128
MXU_DIM = 256
VMEM_LIMIT_BYTES = 56 * 1024 * 1024

ROW_TILE = 512
LRU_CHUNK = 512
SCAN_GROUPS = 4
SLOT_BLOCK = MXU_DIM
TOKEN_TILE = MXU_DIM
SEG_PAD = SUBLANES
FOURIER_N2 = LANES
FOURIER_K1_BATCH = 8
FF_CHUNK = 512
WIN_ALIGN = 16
WIN_ROWS = SLOT_BLOCK + WIN_ALIGN


def _cparams(semantics):
    return pltpu.CompilerParams(dimension_semantics=semantics,
                                vmem_limit_bytes=VMEM_LIMIT_BYTES)


def _rms_scale(x):
    return x * lax.rsqrt(jnp.mean(x * x, axis=-1, keepdims=True) + EPS)


def _inproj_kernel(x_ref, g_ref, w_ref, lx_ref, lg_ref, fo_ref):
    h = _rms_scale(x_ref[...]) * g_ref[...]
    p = jnp.dot(h.astype(BF16), w_ref[...], preferred_element_type=F32)
    d = lx_ref.shape[-1]
    lx_ref[...] = p[:, :d]
    lg_ref[...] = p[:, d:2 * d]
    fo_ref[...] = p[:, 2 * d:]


def _inproj(x2, g, w_bf, d_lru, d_four):
    m, d = x2.shape
    n = w_bf.shape[1]
    return pl.pallas_call(
        _inproj_kernel,
        grid=(m // ROW_TILE,),
        in_specs=[
            pl.BlockSpec((ROW_TILE, d), lambda i: (i, 0)),
            pl.BlockSpec((1, d), lambda i: (0, 0)),
            pl.BlockSpec((d, n), lambda i: (0, 0)),
        ],
        out_specs=[
            pl.BlockSpec((ROW_TILE, d_lru), lambda i: (i, 0)),
            pl.BlockSpec((ROW_TILE, d_lru), lambda i: (i, 0)),
            pl.BlockSpec((ROW_TILE, d_four), lambda i: (i, 0)),
        ],
        out_shape=[
            jax.ShapeDtypeStruct((m, d_lru), F32),
            jax.ShapeDtypeStruct((m, d_lru), F32),
            jax.ShapeDtypeStruct((m, d_four), F32),
        ],
        compiler_params=_cparams(("parallel",)),
        name="inproj",
    )(x2, g, w_bf)


def _shift_rows(x, shift):
    n = x.shape[0]
    rows = lax.broadcasted_iota(jnp.int32, x.shape, 0)
    rolled = pltpu.roll(x, shift % n, axis=0)
    keep = (rows >= shift) if shift > 0 else (rows < n + shift)
    return jnp.where(keep, rolled, 0.0)


def _lru_kernel(lx_ref, lg_ref, cw_ref, cb_ref, w_ref, b_ref, lam_ref, y_ref,
                xpad, af, bf, ab, bb):
    s = lx_ref.shape[1]
    c_blk = lx_ref.shape[2]
    n_seg = SCAN_GROUPS * SUBLANES
    seg = s // n_seg
    seg_stride = seg + SEG_PAD
    piece = min(seg, LRU_CHUNK)
    n_chunks = s // LRU_CHUNK
    pad = SUBLANES

    zeros_pad = jnp.zeros((pad, c_blk), F32)
    xpad[pl.ds(0, pad), :] = zeros_pad
    xpad[pl.ds(pad + s, pad), :] = zeros_pad

    def copy_chunk(i, carry):
        t0 = pl.multiple_of(i * LRU_CHUNK, LRU_CHUNK)
        xpad[pl.ds(pad + t0, LRU_CHUNK), :] = lx_ref[0, pl.ds(t0, LRU_CHUNK), :]
        return carry

    lax.fori_loop(0, n_chunks, copy_chunk, 0)

    cw = cw_ref[...]
    conv_width = cw.shape[0]
    cb = cb_ref[...]
    bias = b_ref[0]
    log_sig = jax.nn.log_sigmoid(lam_ref[0])
    w_cat = w_ref[0]

    def gates_chunk(i, carry):
        t0 = pl.multiple_of(i * LRU_CHUNK, LRU_CHUNK)
        win = xpad[pl.ds(t0, LRU_CHUNK + 2 * pad), :]
        c = cb
        left = conv_width // 2
        for k in range(conv_width):
            c = c + cw[k:k + 1, :] * win[pad - left + k:pad - left + k + LRU_CHUNK, :]
        z = jnp.dot(c.astype(BF16), w_cat, preferred_element_type=F32) + bias
        for d, (a_scr, b_scr) in enumerate(((af, bf), (ab, bb))):
            r = jax.nn.sigmoid(z[:, (2 * d) * c_blk:(2 * d + 1) * c_blk])
            ig = jax.nn.sigmoid(z[:, (2 * d + 1) * c_blk:(2 * d + 2) * c_blk])
            log_a = LRU_C * r * log_sig[:, d * c_blk:(d + 1) * c_blk]
            a = jnp.exp(log_a)
            th = jnp.tanh(log_a)
            mult = jnp.sqrt(-2.0 * th / (1.0 - th))
            bt = mult * (ig * c)
            for p in range(LRU_CHUNK // piece):
                t = t0 + p * piece
                dst = pl.ds(pl.multiple_of((t // seg) * seg_stride + t % seg, SUBLANES), piece)
                a_scr[dst, :] = a[p * piece:(p + 1) * piece]
                b_scr[dst, :] = bt[p * piece:(p + 1) * piece]
        return carry

    lax.fori_loop(0, n_chunks, gates_chunk, 0)

    def scan_step(i, carry):
        out = []
        for (a_scr, b_scr, o), (hs, ps) in zip(((af, bf, i), (ab, bb, seg - 1 - i)), carry):
            new_h, new_p = [], []
            for q in range(SCAN_GROUPS):
                rows = pl.ds(q * SUBLANES * seg_stride + o, SUBLANES, stride=seg_stride)
                a = a_scr[rows, :]
                h = a * hs[q] + b_scr[rows, :]
                p = a * ps[q]
                b_scr[rows, :] = h
                a_scr[rows, :] = p
                new_h.append(h)
                new_p.append(p)
            out.append((tuple(new_h), tuple(new_p)))
        return tuple(out)

    zero = jnp.zeros((SUBLANES, c_blk), F32)
    one = jnp.ones((SUBLANES, c_blk), F32)
    init = ((zero,) * SCAN_GROUPS, (one,) * SCAN_GROUPS)
    (hf_end, pf_end), (hb_end, pb_end) = lax.fori_loop(0, seg, scan_step, (init, init))

    row = lax.broadcasted_iota(jnp.int32, (SUBLANES, c_blk), 0)

    def entering(h_end, p_end, carry_in, first_row, shift):
        c = zero
        for _ in range(SUBLANES):
            c = jnp.where(row == first_row, carry_in, _shift_rows(h_end + p_end * c, shift))
        return c

    cf = [None] * SCAN_GROUPS
    cbk = [None] * SCAN_GROUPS
    carry_f = jnp.zeros((1, c_blk), F32)
    carry_b = jnp.zeros((1, c_blk), F32)
    for q in range(SCAN_GROUPS):
        cf[q] = entering(hf_end[q], pf_end[q], carry_f, 0, 1)
        carry_f = (hf_end[q] + pf_end[q] * cf[q])[SUBLANES - 1:SUBLANES, :]
        qb = SCAN_GROUPS - 1 - q
        cbk[qb] = entering(hb_end[qb], pb_end[qb], carry_b, SUBLANES - 1, -1)
        carry_b = (hb_end[qb] + pb_end[qb] * cbk[qb])[0:1, :]

    for j in range(n_seg):
        q, i = divmod(j, SUBLANES)
        src = pl.ds(j * seg_stride, seg)
        rows = pl.ds(j * seg, seg)
        hsum = ((bf[src, :] + af[src, :] * cf[q][i:i + 1, :])
                + (bb[src, :] + ab[src, :] * cbk[q][i:i + 1, :]))
        y_ref[0, rows, :] = (jax.nn.gelu(lg_ref[0, rows, :]) * hsum).astype(y_ref.dtype)


def _lru(lx, lg, conv_w, conv_b, w_cat, b_cat, lam_cat):
    bsz, s, d_lru = lx.shape
    c_blk = LANES
    n_blk = d_lru // c_blk
    scr = pltpu.VMEM((s + SCAN_GROUPS * SUBLANES * SEG_PAD, c_blk), F32)
    return pl.pallas_call(
        _lru_kernel,
        grid=(bsz, n_blk),
        in_specs=[
            pl.BlockSpec((1, s, c_blk), lambda b, c: (b, 0, c)),
            pl.BlockSpec((1, s, c_blk), lambda b, c: (b, 0, c)),
            pl.BlockSpec((4, c_blk), lambda b, c: (0, c)),
            pl.BlockSpec((1, c_blk), lambda b, c: (0, c)),
            pl.BlockSpec((1, c_blk, 4 * c_blk), lambda b, c: (c, 0, 0)),
            pl.BlockSpec((1, 1, 4 * c_blk), lambda b, c: (c, 0, 0)),
            pl.BlockSpec((1, 1, 2 * c_blk), lambda b, c: (c, 0, 0)),
        ],
        out_specs=pl.BlockSpec((1, s, c_blk), lambda b, c: (b, 0, c)),
        out_shape=jax.ShapeDtypeStruct((bsz, s, d_lru), BF16),
        scratch_shapes=[pltpu.VMEM((s + 2 * SUBLANES, c_blk), F32), scr, scr, scr, scr],
        compiler_params=_cparams(("parallel", "parallel")),
        name="lru",
    )(lx, lg, conv_w, conv_b, w_cat, b_cat, lam_cat)


def _fourier_kernel(f_ref, ka_ref, kb_ref, cc_ref, sc_ref, twc_ref, tws_ref, y_ref,
                    ar_scr, ai_scr, *, scale):
    n1 = f_ref.shape[1]
    n_u = f_ref.shape[2]
    cols = f_ref.shape[4]
    r = SUBLANES * n1
    n2 = n_u * SUBLANES

    def stage_a(u, carry):
        xu = f_ref[0, :, u].reshape(r, cols)
        a = jnp.dot(ka_ref[...], xu.astype(BF16), preferred_element_type=F32)
        a_re, a_im = a[:r], a[r:]
        c = twc_ref[u]
        sn = tws_ref[u]
        ar_scr[u] = a_re * c - a_im * sn
        ai_scr[u] = a_re * sn + a_im * c
        return carry

    lax.fori_loop(0, n_u, stage_a, 0)

    nb = FOURIER_K1_BATCH

    def stage_b(i, carry):
        rows = pl.ds(pl.multiple_of(i * (nb * SUBLANES), nb * SUBLANES), nb * SUBLANES)
        a_re = ar_scr[:, rows, :]
        a_im = ai_scr[:, rows, :]

        def rows_s2(a, j):
            return a[:, j * SUBLANES:(j + 1) * SUBLANES, :].reshape(n2, cols)

        st = jnp.concatenate(
            [jnp.concatenate([rows_s2(a_re, j), rows_s2(a_im, j)], axis=0) for j in range(nb)],
            axis=1).astype(BF16)
        x = jnp.dot(kb_ref[...], st, preferred_element_type=F32)
        x_re = jnp.concatenate([x[:n2, j * cols:(j + 1) * cols] for j in range(nb)], axis=0)
        x_im = jnp.concatenate([x[n2:, j * cols:(j + 1) * cols] for j in range(nb)], axis=0)
        y = (jnp.dot(x_re.astype(BF16), cc_ref[...], preferred_element_type=F32)
             + jnp.dot(x_im.astype(BF16), sc_ref[...], preferred_element_type=F32)) * scale
        for j in range(nb):
            y_ref[0, pl.ds(i * nb + j, n2, stride=n1), :] = y[j * n2:(j + 1) * n2]
        return carry

    lax.fori_loop(0, n1 // nb, stage_b, 0)


def _dft_tables(s, c_grp):
    n2 = FOURIER_N2
    n1 = s // n2
    n_u = n2 // SUBLANES

    def cos_sin(num, den):
        ang = (2.0 * np.pi / den) * (num % den).astype(np.float64)
        return np.cos(ang), np.sin(ang)

    def const(a, dtype=F32):
        return jnp.asarray(a.astype(np.float32)).astype(dtype)

    i1 = np.arange(n1, dtype=np.int64)
    c1, s1 = cos_sin(i1[:, None] * i1[None, :], n1)
    eye8 = np.eye(SUBLANES)
    ka = const(np.concatenate([np.kron(c1, eye8), np.kron(-s1, eye8)], axis=0), BF16)

    i2 = np.arange(n2, dtype=np.int64)
    c2, s2 = cos_sin(i2[:, None] * i2[None, :], n2)
    kb = const(np.block([[c2, s2], [-s2, c2]]), BF16)

    ic = np.arange(c_grp, dtype=np.int64)
    cc, sc = cos_sin(ic[:, None] * ic[None, :], c_grp)

    u = np.arange(n_u, dtype=np.int64)[:, None, None]
    k1 = np.arange(n1, dtype=np.int64)[None, :, None]
    v = np.arange(SUBLANES, dtype=np.int64)[None, None, :]
    tc, ts = cos_sin(k1 * (SUBLANES * u + v), s)
    shape = (n_u, n1 * SUBLANES, c_grp)
    twc = jnp.broadcast_to(const(tc.reshape(n_u, -1, 1)), shape)
    tws = jnp.broadcast_to(const(-ts.reshape(n_u, -1, 1)), shape)
    return ka, kb, const(cc, BF16), const(sc, BF16), twc, tws


def _fourier(four, n_groups):
    bsz, s, d_four = four.shape
    c_grp = d_four // n_groups
    n2 = FOURIER_N2
    n1 = s // n2
    n_u = n2 // SUBLANES
    r = SUBLANES * n1
    ka, kb, cc, sc, twc, tws = _dft_tables(s, c_grp)
    f5 = four.reshape(bsz, n1, n_u, SUBLANES, d_four)
    scale = 1.0 / math.sqrt(s * c_grp)
    const2 = lambda b, g: (0, 0)
    const3 = lambda b, g: (0, 0, 0)
    return pl.pallas_call(
        functools.partial(_fourier_kernel, scale=scale),
        grid=(bsz, n_groups),
        in_specs=[
            pl.BlockSpec((1, n1, n_u, SUBLANES, c_grp), lambda b, g: (b, 0, 0, 0, g)),
            pl.BlockSpec(ka.shape, const2),
            pl.BlockSpec(kb.shape, const2),
            pl.BlockSpec(cc.shape, const2),
            pl.BlockSpec(sc.shape, const2),
            pl.BlockSpec(twc.shape, const3),
            pl.BlockSpec(tws.shape, const3),
        ],
        out_specs=pl.BlockSpec((1, s, c_grp), lambda b, g: (b, 0, g)),
        out_shape=jax.ShapeDtypeStruct((bsz, s, d_four), F32),
        scratch_shapes=[pltpu.VMEM((n_u, r, c_grp), F32), pltpu.VMEM((n_u, r, c_grp), F32)],
        compiler_params=_cparams(("parallel", "parallel")),
        name="fourier",
    )(f5, ka, kb, cc, sc, twc, tws)


def _outproj_kernel(x_ref, yl_ref, yf_ref, wo1_ref, wo2_ref, g2_ref, wr_ref,
                    x1_ref, h2_ref, aff_ref):
    x1 = (x_ref[0]
          + jnp.dot(yl_ref[0], wo1_ref[...], preferred_element_type=F32)
          + jnp.dot(yf_ref[0].astype(BF16), wo2_ref[...], preferred_element_type=F32))
    x1_ref[0] = x1
    h2 = _rms_scale(x1) * g2_ref[...]
    h2_ref[0] = h2.astype(BF16)
    logits = lax.dot_general(wr_ref[...], h2, (((1,), (1,)), ((), ())),
                             precision=lax.Precision.HIGHEST,
                             preferred_element_type=F32)
    ex = jnp.exp(logits - jnp.max(logits, axis=0, keepdims=True))
    aff_ref[0] = ex / jnp.sum(ex, axis=0, keepdims=True)


def _outproj(x, y_lru, y_four, wo1, wo2, g2, wr_t):
    bsz, s, d = x.shape
    d_lru = y_lru.shape[-1]
    d_four = y_four.shape[-1]
    n_e = wr_t.shape[0]
    tile = lambda w: pl.BlockSpec((1, ROW_TILE, w), lambda b, i: (b, i, 0))
    const = lambda shape: pl.BlockSpec(shape, lambda b, i: (0, 0))
    return pl.pallas_call(
        _outproj_kernel,
        grid=(bsz, s // ROW_TILE),
        in_specs=[tile(d), tile(d_lru), tile(d_four), const(wo1.shape), const(wo2.shape),
                  const(g2.shape), const(wr_t.shape)],
        out_specs=[tile(d), tile(d),
                   pl.BlockSpec((1, n_e, ROW_TILE), lambda b, i: (b, 0, i))],
        out_shape=[
            jax.ShapeDtypeStruct((bsz, s, d), F32),
            jax.ShapeDtypeStruct((bsz, s, d), BF16),
            jax.ShapeDtypeStruct((bsz, n_e, s), F32),
        ],
        compiler_params=_cparams(("parallel", "parallel")),
        name="outproj",
    )(x, y_lru, y_four, wo1, wo2, g2, wr_t)


def _select_kernel(aff_ref, tri_ref, pos_ref, starts_ref, *, cap):
    v = aff_ref[0]
    n_e, s = v.shape
    cap_f = float(cap)

    def midpoint(lo, hi):
        mid = 0.5 * (lo + hi)
        return mid, (mid > lo) & (mid < hi)

    def cond(carry):
        _, active = midpoint(*carry)
        return jnp.max(active.astype(F32)) > 0.0

    def body(carry):
        lo, hi = carry
        mid, active = midpoint(lo, hi)
        cnt = jnp.sum((v >= mid).astype(F32), axis=1, keepdims=True)
        enough = cnt >= cap_f
        return (jnp.where(active & enough, mid, lo), jnp.where(active & (~enough), mid, hi))

    lo0 = jnp.zeros((n_e, 1), F32)
    hi0 = jnp.full((n_e, 1), 2.0, F32)
    thr, _ = lax.while_loop(cond, body, (lo0, hi0))

    above = v > thr
    tie = v == thr
    need = cap_f - jnp.sum(above.astype(F32), axis=1, keepdims=True)

    n_tiles = s // TOKEN_TILE
    tri = tri_ref[...]
    lane = lax.broadcasted_iota(jnp.int32, (n_e, LANES), 1)

    def prefix(mask_f, want_starts):
        run = jnp.zeros((n_e, 1), F32)
        starts = jnp.zeros((n_e, LANES), F32)
        pieces = []
        for t in range(n_tiles):
            m = mask_f[:, t * TOKEN_TILE:(t + 1) * TOKEN_TILE]
            incl = jnp.dot(m.astype(BF16), tri, preferred_element_type=F32)
            pieces.append(run + incl - m)
            if want_starts:
                starts = jnp.where(lane == t, run, starts)
            run = run + incl[:, TOKEN_TILE - 1:TOKEN_TILE]
        if want_starts:
            starts = jnp.where(lane == n_tiles, run, starts)
        return jnp.concatenate(pieces, axis=1), starts

    tie_rank, _ = prefix(tie.astype(F32), False)
    sel = above | (tie & (tie_rank < need))
    pos, starts = prefix(sel.astype(F32), True)
    pos_ref[0] = jnp.where(sel, pos, -1.0).astype(jnp.int32)
    starts_ref[0] = starts.astype(jnp.int32)


def _select(aff_t, cap):
    bsz, n_e, s = aff_t.shape
    idx = jnp.arange(TOKEN_TILE, dtype=jnp.int32)
    tri = (idx[:, None] <= idx[None, :]).astype(BF16)
    return pl.pallas_call(
        functools.partial(_select_kernel, cap=cap),
        grid=(bsz,),
        in_specs=[
            pl.BlockSpec((1, n_e, s), lambda b: (b, 0, 0)),
            pl.BlockSpec(tri.shape, lambda b: (0, 0)),
        ],
        out_specs=[
            pl.BlockSpec((1, n_e, s), lambda b: (b, 0, 0)),
            pl.BlockSpec((1, n_e, LANES), lambda b: (b, 0, 0)),
        ],
        out_shape=[
            jax.ShapeDtypeStruct((bsz, n_e, s), jnp.int32),
            jax.ShapeDtypeStruct((bsz, n_e, LANES), jnp.int32),
        ],
        compiler_params=_cparams(("parallel",)),
        name="select",
    )(aff_t, tri)


def _moe_kernel(starts_ref, pos_ref, aff_ref, h2_ref, wg_ref, wu_ref, wd_ref, eo_ref,
                x_scr, g_scr, acc_scr, *, n_experts):
    b = pl.program_id(0)
    e = pl.program_id(1)
    f = pl.program_id(2)
    n_blocks = x_scr.shape[0]
    n_k = pos_ref.shape[2]

    @pl.when(f == 0)
    def _gather():
        x_scr[...] = jnp.zeros_like(x_scr)
        g_scr[...] = jnp.zeros_like(g_scr)
        acc_scr[...] = jnp.zeros_like(acc_scr)
        base = (b * n_experts + e) * LANES
        slot = lax.broadcasted_iota(jnp.int32, (SLOT_BLOCK, TOKEN_TILE), 0)

        def tile(k, carry):
            s0 = starts_ref[base + k]
            s1 = starts_ref[base + k + 1]
            posrow = pos_ref[0, 0, k]
            gaterow = aff_ref[0, 0, k]
            rows = pl.ds(pl.multiple_of(k * TOKEN_TILE, TOKEN_TILE), TOKEN_TILE)
            m0 = s0 // SLOT_BLOCK

            def pair(m):
                hit = posrow == slot + m * SLOT_BLOCK
                onehot = jnp.where(hit, 1.0, 0.0).astype(BF16)
                x_scr[m] += jnp.dot(onehot, h2_ref[0, rows, :], preferred_element_type=F32)
                g_scr[m] += jnp.sum(jnp.where(hit, gaterow, 0.0), axis=1, keepdims=True)

            @pl.when(s1 > s0)
            def _():
                pair(m0)

            @pl.when(s1 > (m0 + 1) * SLOT_BLOCK)
            def _():
                pair(m0 + 1)

            return carry

        lax.fori_loop(0, n_k, tile, 0)

    wg = wg_ref[0].astype(BF16)
    wu = wu_ref[0].astype(BF16)
    wd = wd_ref[0].astype(BF16)
    for m in range(n_blocks):
        xm = x_scr[m].astype(BF16)
        g = jnp.dot(xm, wg, preferred_element_type=F32)
        u = jnp.dot(xm, wu, preferred_element_type=F32)
        act = (jax.nn.silu(g) * u).astype(BF16)
        acc_scr[m] += jnp.dot(act, wd, preferred_element_type=F32)

    @pl.when(f == pl.num_programs(2) - 1)
    def _():
        for m in range(n_blocks):
            rows = pl.ds(m * SLOT_BLOCK, SLOT_BLOCK)
            eo_ref[0, 0, rows, :] = (g_scr[m] * acc_scr[m]).astype(eo_ref.dtype)


def _moe(starts_flat, pos5, aff5, h2, wg, wu, wd, cap):
    bsz, s, d = h2.shape
    n_e, _, d_ff = wg.shape
    n_k = s // TOKEN_TILE
    n_blocks = cap // SLOT_BLOCK
    row5 = pl.BlockSpec((1, 1, n_k, 1, TOKEN_TILE), lambda b, e, f, st: (b, e, 0, 0, 0))
    grid_spec = pltpu.PrefetchScalarGridSpec(
        num_scalar_prefetch=1,
        grid=(bsz, n_e, d_ff // FF_CHUNK),
        in_specs=[
            row5,
            row5,
            pl.BlockSpec((1, s, d), lambda b, e, f, st: (b, 0, 0), pipeline_mode=pl.Buffered(1)),
            pl.BlockSpec((1, d, FF_CHUNK), lambda b, e, f, st: (e, 0, f)),
            pl.BlockSpec((1, d, FF_CHUNK), lambda b, e, f, st: (e, 0, f)),
            pl.BlockSpec((1, FF_CHUNK, d), lambda b, e, f, st: (e, f, 0)),
        ],
        out_specs=pl.BlockSpec((1, 1, cap, d), lambda b, e, f, st: (b, e, 0, 0)),
        scratch_shapes=[pltpu.VMEM((n_blocks, SLOT_BLOCK, d), F32),
                        pltpu.VMEM((n_blocks, SLOT_BLOCK, 1), F32),
                        pltpu.VMEM((n_blocks, SLOT_BLOCK, d), F32)],
    )
    return pl.pallas_call(
        functools.partial(_moe_kernel, n_experts=n_e),
        grid_spec=grid_spec,
        out_shape=jax.ShapeDtypeStruct((bsz, n_e, cap, d), BF16),
        compiler_params=_cparams(("parallel", "parallel", "arbitrary")),
        name="moe",
    )(starts_flat, pos5, aff5, h2, wg, wu, wd)


def _window_start(s0, cap):
    return jnp.minimum(s0 // WIN_ALIGN, (cap - WIN_ROWS) // WIN_ALIGN) * WIN_ALIGN


def _combine_kernel(starts_ref, x1_ref, pos_ref, gf_ref, *rest, n_experts, cap):
    win_refs = rest[:n_experts]
    out_ref, wcat_scr, acc_scr = rest[n_experts:]
    b = pl.program_id(0)
    k = pl.program_id(1)
    pos = pos_ref[0]
    lane = lax.broadcasted_iota(jnp.int32, (TOKEN_TILE, SLOT_BLOCK), 1)

    onehots = []
    rels = []
    for e in range(n_experts):
        base = (b * n_experts + e) * LANES
        ws = _window_start(starts_ref[base + k], cap)
        rel = pos[:, e:e + 1] - ws
        rels.append((rel, starts_ref[base + k + 1] - ws))
        onehots.append(jnp.where(rel == lane, 1.0, 0.0).astype(BF16))
        wcat_scr[pl.ds(e * SLOT_BLOCK, SLOT_BLOCK), :] = win_refs[e][pl.ds(0, SLOT_BLOCK), :]
    onehot = jnp.concatenate(onehots, axis=1)
    acc_scr[...] = x1_ref[0] + jnp.dot(onehot, wcat_scr[...], preferred_element_type=F32)

    tail_lane = lax.broadcasted_iota(jnp.int32, (TOKEN_TILE, WIN_ALIGN), 1)
    for e in range(n_experts):
        rel, end = rels[e]

        @pl.when(end > SLOT_BLOCK)
        def _(e=e, rel=rel):
            tail = jnp.where(rel - SLOT_BLOCK == tail_lane, 1.0, 0.0).astype(BF16)
            acc_scr[...] += jnp.dot(tail, win_refs[e][pl.ds(SLOT_BLOCK, WIN_ALIGN), :],
                                    preferred_element_type=F32)

    out_ref[0] = _rms_scale(acc_scr[...]) * gf_ref[...]


def _combine(starts_flat, x1, pos_c, eo, gf):
    bsz, s, d = x1.shape
    n_e = pos_c.shape[-1]
    cap = eo.shape[2]
    n_k = s // TOKEN_TILE

    def window_spec(e):
        def index_map(b, k, st):
            return (b, e, _window_start(st[(b * n_e + e) * LANES + k], cap), 0)
        return pl.BlockSpec((pl.Squeezed(), pl.Squeezed(), pl.Element(WIN_ROWS), pl.Element(d)),
                            index_map)

    grid_spec = pltpu.PrefetchScalarGridSpec(
        num_scalar_prefetch=1,
        grid=(bsz, n_k),
        in_specs=[
            pl.BlockSpec((1, TOKEN_TILE, d), lambda b, k, st: (b, k, 0)),
            pl.BlockSpec((1, TOKEN_TILE, n_e), lambda b, k, st: (b, k, 0)),
            pl.BlockSpec((1, d), lambda b, k, st: (0, 0)),
        ] + [window_spec(e) for e in range(n_e)],
        out_specs=pl.BlockSpec((1, TOKEN_TILE, d), lambda b, k, st: (b, k, 0)),
        scratch_shapes=[pltpu.VMEM((n_e * SLOT_BLOCK, d), BF16),
                        pltpu.VMEM((TOKEN_TILE, d), F32)],
    )
    return pl.pallas_call(
        functools.partial(_combine_kernel, n_experts=n_e, cap=cap),
        grid_spec=grid_spec,
        out_shape=jax.ShapeDtypeStruct((bsz, s, d), F32),
        compiler_params=_cparams(("parallel", "parallel")),
        name="combine",
    )(starts_flat, x1, pos_c, gf, *([eo] * n_e))


def _block_diag(w):
    h, hd, _ = w.shape
    eye = jnp.eye(h, dtype=w.dtype)
    return (eye[:, None, :, None] * w[:, :, None, :]).reshape(h * hd, h * hd)


def _lru_params(wa_f, wx_f, wa_b, wx_b, ba_f, bx_f, ba_b, bx_b, lam_f, lam_b):
    d_lru = ba_f.shape[0]
    n_blk = d_lru // LANES
    mats = [_block_diag(w) for w in (wa_f, wx_f, wa_b, wx_b)]
    w_cat = jnp.stack([
        jnp.concatenate([m[c * LANES:(c + 1) * LANES, c * LANES:(c + 1) * LANES] for m in mats], axis=1)
        for c in range(n_blk)]).astype(BF16)
    b_cat = jnp.stack([
        jnp.concatenate([v[c * LANES:(c + 1) * LANES] for v in (ba_f, bx_f, ba_b, bx_b)])
        for c in range(n_blk)])[:, None, :]
    lam_cat = jnp.stack([
        jnp.concatenate([v[c * LANES:(c + 1) * LANES] for v in (lam_f, lam_b)])
        for c in range(n_blk)])[:, None, :]
    return w_cat, b_cat, lam_cat


def kernel(x, norm1_g, w_in, conv_w, conv_b, lru_wa_f, lru_ba_f, lru_wx_f, lru_bx_f, lru_lam_f,
           lru_wa_b, lru_ba_b, lru_wx_b, lru_bx_b, lru_lam_b, w_out, norm2_g, w_router,
           w_gate, w_up, w_down, normf_g):
    bsz, s, d = x.shape
    d_lru = conv_b.shape[0]
    d_four = w_in.shape[1] - 2 * d_lru
    n_e = w_router.shape[1]
    cap = CAPACITY_FACTOR * s // n_e
    assert s % LRU_CHUNK == 0 and s % (SCAN_GROUPS * SUBLANES * SUBLANES) == 0
    assert s % FOURIER_N2 == 0 and cap % SLOT_BLOCK == 0 and s // TOKEN_TILE < LANES
    assert cap >= WIN_ROWS and w_gate.shape[2] % FF_CHUNK == 0

    lx, lg, fo = _inproj(x.reshape(bsz * s, d), norm1_g[None, :], w_in.astype(BF16), d_lru, d_four)
    lx = lx.reshape(bsz, s, d_lru)
    lg = lg.reshape(bsz, s, d_lru)
    fo = fo.reshape(bsz, s, d_four)

    w_cat, b_cat, lam_cat = _lru_params(lru_wa_f, lru_wx_f, lru_wa_b, lru_wx_b,
                                        lru_ba_f, lru_bx_f, lru_ba_b, lru_bx_b,
                                        lru_lam_f, lru_lam_b)
    y_lru = _lru(lx, lg, conv_w, conv_b[None, :], w_cat, b_cat, lam_cat)
    y_four = _fourier(fo, FOURIER_GROUPS)

    w_out_bf = w_out.astype(BF16)
    x1, h2, aff_t = _outproj(x, y_lru, y_four, w_out_bf[:d_lru], w_out_bf[d_lru:],
                             norm2_g[None, :], w_router.T)

    pos_r, starts = _select(aff_t, cap)
    starts_flat = starts.reshape(-1)
    n_k = s // TOKEN_TILE
    row5 = (bsz, n_e, n_k, 1, TOKEN_TILE)
    eo = _moe(starts_flat, pos_r.reshape(row5), aff_t.reshape(row5), h2,
              w_gate, w_up, w_down, cap)

    pos_c = jnp.transpose(pos_r, (0, 2, 1))
    return _combine(starts_flat, x1, pos_c, eo, normf_g[None, :])
```

```python
import functools
import math

import jax
import jax.numpy as jnp
import numpy as np
from jax import lax
from jax.experimental import pallas as pl
from jax.experimental.pallas import tpu as pltpu

F32 = jnp.float32
BF16 = jnp.bfloat16

EPS = 1e-6
LRU_C = 8.0
LRU_HEADS = 8
N_EXPERTS = 16
CAPACITY_FACTOR = 2
FOURIER_GROUPS = 4

SUBLANES = 8
LANES = 128
MXU_DIM = 256
VMEM_LIMIT_BYTES = 56 * 1024 * 1024

ROW_TILE = 512
LRU_CHUNK = 512
SCAN_GROUPS = 4
SLOT_BLOCK = MXU_DIM
TOKEN_TILE = MXU_DIM
SEG_PAD = SUBLANES
FOURIER_N2 = LANES
FOURIER_K1_BATCH = 8
FF_CHUNK = 512
GATHER_UNROLL = 8
WIN_ALIGN = 16
WIN_ROWS = SLOT_BLOCK + WIN_ALIGN


def _cparams(semantics):
    return pltpu.CompilerParams(dimension_semantics=semantics,
                                vmem_limit_bytes=VMEM_LIMIT_BYTES)


def _rms_scale(x):
    return x * lax.rsqrt(jnp.mean(x * x, axis=-1, keepdims=True) + EPS)


def _inproj_kernel(x_ref, g_ref, w_ref, lx_ref, lg_ref, fo_ref):
    h = _rms_scale(x_ref[...]) * g_ref[...]
    p = jnp.dot(h.astype(BF16), w_ref[...], preferred_element_type=F32)
    d = lx_ref.shape[-1]
    lx_ref[...] = p[:, :d]
    lg_ref[...] = p[:, d:2 * d]
    fo_ref[...] = p[:, 2 * d:]


def _inproj(x2, g, w_bf, d_lru, d_four):
    m, d = x2.shape
    n = w_bf.shape[1]
    return pl.pallas_call(
        _inproj_kernel,
        grid=(m // ROW_TILE,),
        in_specs=[
            pl.BlockSpec((ROW_TILE, d), lambda i: (i, 0)),
            pl.BlockSpec((1, d), lambda i: (0, 0)),
            pl.BlockSpec((d, n), lambda i: (0, 0)),
        ],
        out_specs=[
            pl.BlockSpec((ROW_TILE, d_lru), lambda i: (i, 0)),
            pl.BlockSpec((ROW_TILE, d_lru), lambda i: (i, 0)),
            pl.BlockSpec((ROW_TILE, d_four), lambda i: (i, 0)),
        ],
        out_shape=[
            jax.ShapeDtypeStruct((m, d_lru), F32),
            jax.ShapeDtypeStruct((m, d_lru), F32),
            jax.ShapeDtypeStruct((m, d_four), F32),
        ],
        compiler_params=_cparams(("parallel",)),
        name="inproj",
    )(x2, g, w_bf)


def _shift_rows(x, shift):
    n = x.shape[0]
    rows = lax.broadcasted_iota(jnp.int32, x.shape, 0)
    rolled = pltpu.roll(x, shift % n, axis=0)
    keep = (rows >= shift) if shift > 0 else (rows < n + shift)
    return jnp.where(keep, rolled, 0.0)


def _lru_kernel(lx_ref, lg_ref, cw_ref, cb_ref, w_ref, b_ref, lam_ref, y_ref,
                xpad, af, bf, ab, bb):
    s = lx_ref.shape[1]
    c_blk = lx_ref.shape[2]
    n_seg = SCAN_GROUPS * SUBLANES
    seg = s // n_seg
    seg_stride = seg + SEG_PAD
    piece = min(seg, LRU_CHUNK)
    n_chunks = s // LRU_CHUNK
    pad = SUBLANES

    zeros_pad = jnp.zeros((pad, c_blk), F32)
    xpad[pl.ds(0, pad), :] = zeros_pad
    xpad[pl.ds(pad + s, pad), :] = zeros_pad

    def copy_chunk(i, carry):
        t0 = pl.multiple_of(i * LRU_CHUNK, LRU_CHUNK)
        xpad[pl.ds(pad + t0, LRU_CHUNK), :] = lx_ref[0, pl.ds(t0, LRU_CHUNK), :]
        return carry

    lax.fori_loop(0, n_chunks, copy_chunk, 0)

    cw = cw_ref[...]
    conv_width = cw.shape[0]
    cb = cb_ref[...]
    bias = b_ref[0]
    log_sig = jax.nn.log_sigmoid(lam_ref[0])
    w_cat = w_ref[0]

    def gates_chunk(i, carry):
        t0 = pl.multiple_of(i * LRU_CHUNK, LRU_CHUNK)
        win = xpad[pl.ds(t0, LRU_CHUNK + 2 * pad), :]
        c = cb
        left = conv_width // 2
        for k in range(conv_width):
            c = c + cw[k:k + 1, :] * win[pad - left + k:pad - left + k + LRU_CHUNK, :]
        z = jnp.dot(c.astype(BF16), w_cat, preferred_element_type=F32) + bias
        for d, (a_scr, b_scr) in enumerate(((af, bf), (ab, bb))):
            r = jax.nn.sigmoid(z[:, (2 * d) * c_blk:(2 * d + 1) * c_blk])
            ig = jax.nn.sigmoid(z[:, (2 * d + 1) * c_blk:(2 * d + 2) * c_blk])
            log_a = LRU_C * r * log_sig[:, d * c_blk:(d + 1) * c_blk]
            a = jnp.exp(log_a)
            th = jnp.tanh(log_a)
            mult = jnp.sqrt(-2.0 * th / (1.0 - th))
            bt = mult * (ig * c)
            for p in range(LRU_CHUNK // piece):
                t = t0 + p * piece
                dst = pl.ds(pl.multiple_of((t // seg) * seg_stride + t % seg, SUBLANES), piece)
                a_scr[dst, :] = a[p * piece:(p + 1) * piece]
                b_scr[dst, :] = bt[p * piece:(p + 1) * piece]
        return carry

    lax.fori_loop(0, n_chunks, gates_chunk, 0)

    def scan_step(i, carry):
        out = []
        for (a_scr, b_scr, o), (hs, ps) in zip(((af, bf, i), (ab, bb, seg - 1 - i)), carry):
            new_h, new_p = [], []
            for q in range(SCAN_GROUPS):
                rows = pl.ds(q * SUBLANES * seg_stride + o, SUBLANES, stride=seg_stride)
                a = a_scr[rows, :]
                h = a * hs[q] + b_scr[rows, :]
                p = a * ps[q]
                b_scr[rows, :] = h
                a_scr[rows, :] = p
                new_h.append(h)
                new_p.append(p)
            out.append((tuple(new_h), tuple(new_p)))
        return tuple(out)

    zero = jnp.zeros((SUBLANES, c_blk), F32)
    one = jnp.ones((SUBLANES, c_blk), F32)
    init = ((zero,) * SCAN_GROUPS, (one,) * SCAN_GROUPS)
    (hf_end, pf_end), (hb_end, pb_end) = lax.fori_loop(0, seg, scan_step, (init, init))

    row = lax.broadcasted_iota(jnp.int32, (SUBLANES, c_blk), 0)

    def entering(h_end, p_end, carry_in, first_row, shift):
        c = zero
        for _ in range(SUBLANES):
            c = jnp.where(row == first_row, carry_in, _shift_rows(h_end + p_end * c, shift))
        return c

    cf = [None] * SCAN_GROUPS
    cbk = [None] * SCAN_GROUPS
    carry_f = jnp.zeros((1, c_blk), F32)
    carry_b = jnp.zeros((1, c_blk), F32)
    for q in range(SCAN_GROUPS):
        cf[q] = entering(hf_end[q], pf_end[q], carry_f, 0, 1)
        carry_f = (hf_end[q] + pf_end[q] * cf[q])[SUBLANES - 1:SUBLANES, :]
        qb = SCAN_GROUPS - 1 - q
        cbk[qb] = entering(hb_end[qb], pb_end[qb], carry_b, SUBLANES - 1, -1)
        carry_b = (hb_end[qb] + pb_end[qb] * cbk[qb])[0:1, :]

    for j in range(n_seg):
        q, i = divmod(j, SUBLANES)
        src = pl.ds(j * seg_stride, seg)
        rows = pl.ds(j * seg, seg)
        hsum = ((bf[src, :] + af[src, :] * cf[q][i:i + 1, :])
                + (bb[src, :] + ab[src, :] * cbk[q][i:i + 1, :]))
        y_ref[0, rows, :] = (jax.nn.gelu(lg_ref[0, rows, :]) * hsum).astype(y_ref.dtype)


def _lru(lx, lg, conv_w, conv_b, w_cat, b_cat, lam_cat):
    bsz, s, d_lru = lx.shape
    c_blk = LANES
    n_blk = d_lru // c_blk
    scr = pltpu.VMEM((s + SCAN_GROUPS * SUBLANES * SEG_PAD, c_blk), F32)
    return pl.pallas_call(
        _lru_kernel,
        grid=(bsz, n_blk),
        in_specs=[
            pl.BlockSpec((1, s, c_blk), lambda b, c: (b, 0, c)),
            pl.BlockSpec((1, s, c_blk), lambda b, c: (b, 0, c)),
            pl.BlockSpec((4, c_blk), lambda b, c: (0, c)),
            pl.BlockSpec((1, c_blk), lambda b, c: (0, c)),
            pl.BlockSpec((1, c_blk, 4 * c_blk), lambda b, c: (c, 0, 0)),
            pl.BlockSpec((1, 1, 4 * c_blk), lambda b, c: (c, 0, 0)),
            pl.BlockSpec((1, 1, 2 * c_blk), lambda b, c: (c, 0, 0)),
        ],
        out_specs=pl.BlockSpec((1, s, c_blk), lambda b, c: (b, 0, c)),
        out_shape=jax.ShapeDtypeStruct((bsz, s, d_lru), BF16),
        scratch_shapes=[pltpu.VMEM((s + 2 * SUBLANES, c_blk), F32), scr, scr, scr, scr],
        compiler_params=_cparams(("parallel", "parallel")),
        name="lru",
    )(lx, lg, conv_w, conv_b, w_cat, b_cat, lam_cat)


def _fourier_kernel(f_ref, ka_ref, kb_ref, cc_ref, sc_ref, twc_ref, tws_ref, y_ref,
                    ar_scr, ai_scr, *, scale):
    n1 = f_ref.shape[1]
    n_u = f_ref.shape[2]
    cols = f_ref.shape[4]
    r = SUBLANES * n1
    n2 = n_u * SUBLANES

    def stage_a(u, carry):
        xu = f_ref[0, :, u].reshape(r, cols)
        a = jnp.dot(ka_ref[...], xu.astype(BF16), preferred_element_type=F32)
        a_re, a_im = a[:r], a[r:]
        c = twc_ref[u]
        sn = tws_ref[u]
        ar_scr[u] = a_re * c - a_im * sn
        ai_scr[u] = a_re * sn + a_im * c
        return carry

    lax.fori_loop(0, n_u, stage_a, 0)

    nb = FOURIER_K1_BATCH

    def stage_b(i, carry):
        rows = pl.ds(pl.multiple_of(i * (nb * SUBLANES), nb * SUBLANES), nb * SUBLANES)
        a_re = ar_scr[:, rows, :]
        a_im = ai_scr[:, rows, :]

        def rows_s2(a, j):
            return a[:, j * SUBLANES:(j + 1) * SUBLANES, :].reshape(n2, cols)

        st = jnp.concatenate(
            [jnp.concatenate([rows_s2(a_re, j), rows_s2(a_im, j)], axis=0) for j in range(nb)],
            axis=1).astype(BF16)
        x = jnp.dot(kb_ref[...], st, preferred_element_type=F32)
        x_re = jnp.concatenate([x[:n2, j * cols:(j + 1) * cols] for j in range(nb)], axis=0)
        x_im = jnp.concatenate([x[n2:, j * cols:(j + 1) * cols] for j in range(nb)], axis=0)
        y = (jnp.dot(x_re.astype(BF16), cc_ref[...], preferred_element_type=F32)
             + jnp.dot(x_im.astype(BF16), sc_ref[...], preferred_element_type=F32)) * scale
        for j in range(nb):
            y_ref[0, pl.ds(i * nb + j, n2, stride=n1), :] = y[j * n2:(j + 1) * n2]
        return carry

    lax.fori_loop(0, n1 // nb, stage_b, 0)


def _dft_tables(s, c_grp):
    n2 = FOURIER_N2
    n1 = s // n2
    n_u = n2 // SUBLANES

    def cos_sin(num, den):
        ang = (2.0 * np.pi / den) * (num % den).astype(np.float64)
        return np.cos(ang), np.sin(ang)

    def const(a, dtype=F32):
        return jnp.asarray(a.astype(np.float32)).astype(dtype)

    i1 = np.arange(n1, dtype=np.int64)
    c1, s1 = cos_sin(i1[:, None] * i1[None, :], n1)
    eye8 = np.eye(SUBLANES)
    ka = const(np.concatenate([np.kron(c1, eye8), np.kron(-s1, eye8)], axis=0), BF16)

    i2 = np.arange(n2, dtype=np.int64)
    c2, s2 = cos_sin(i2[:, None] * i2[None, :], n2)
    kb = const(np.block([[c2, s2], [-s2, c2]]), BF16)

    ic = np.arange(c_grp, dtype=np.int64)
    cc, sc = cos_sin(ic[:, None] * ic[None, :], c_grp)

    u = np.arange(n_u, dtype=np.int64)[:, None, None]
    k1 = np.arange(n1, dtype=np.int64)[None, :, None]
    v = np.arange(SUBLANES, dtype=np.int64)[None, None, :]
    tc, ts = cos_sin(k1 * (SUBLANES * u + v), s)
    shape = (n_u, n1 * SUBLANES, c_grp)
    twc = jnp.broadcast_to(const(tc.reshape(n_u, -1, 1)), shape)
    tws = jnp.broadcast_to(const(-ts.reshape(n_u, -1, 1)), shape)
    return ka, kb, const(cc, BF16), const(sc, BF16), twc, tws


def _fourier(four, n_groups):
    bsz, s, d_four = four.shape
    c_grp = d_four // n_groups
    n2 = FOURIER_N2
    n1 = s // n2
    n_u = n2 // SUBLANES
    r = SUBLANES * n1
    ka, kb, cc, sc, twc, tws = _dft_tables(s, c_grp)
    f5 = four.reshape(bsz, n1, n_u, SUBLANES, d_four)
    scale = 1.0 / math.sqrt(s * c_grp)
    const2 = lambda b, g: (0, 0)
    const3 = lambda b, g: (0, 0, 0)
    return pl.pallas_call(
        functools.partial(_fourier_kernel, scale=scale),
        grid=(bsz, n_groups),
        in_specs=[
            pl.BlockSpec((1, n1, n_u, SUBLANES, c_grp), lambda b, g: (b, 0, 0, 0, g)),
            pl.BlockSpec(ka.shape, const2),
            pl.BlockSpec(kb.shape, const2),
            pl.BlockSpec(cc.shape, const2),
            pl.BlockSpec(sc.shape, const2),
            pl.BlockSpec(twc.shape, const3),
            pl.BlockSpec(tws.shape, const3),
        ],
        out_specs=pl.BlockSpec((1, s, c_grp), lambda b, g: (b, 0, g)),
        out_shape=jax.ShapeDtypeStruct((bsz, s, d_four), F32),
        scratch_shapes=[pltpu.VMEM((n_u, r, c_grp), F32), pltpu.VMEM((n_u, r, c_grp), F32)],
        compiler_params=_cparams(("parallel", "parallel")),
        name="fourier",
    )(f5, ka, kb, cc, sc, twc, tws)


def _outproj_kernel(x_ref, yl_ref, yf_ref, wo1_ref, wo2_ref, g2_ref, wr_ref,
                    x1_ref, h2_ref, aff_ref):
    x1 = (x_ref[0]
          + jnp.dot(yl_ref[0], wo1_ref[...], preferred_element_type=F32)
          + jnp.dot(yf_ref[0].astype(BF16), wo2_ref[...], preferred_element_type=F32))
    x1_ref[0] = x1
    h2 = _rms_scale(x1) * g2_ref[...]
    for j in range(h2.shape[1] // LANES):
        h2_ref[pl.ds(j, h2.shape[0], stride=SUBLANES), :] = h2[:, j * LANES:(j + 1) * LANES]
    logits = lax.dot_general(wr_ref[...], h2, (((1,), (1,)), ((), ())),
                             precision=lax.Precision.HIGHEST,
                             preferred_element_type=F32)
    ex = jnp.exp(logits - jnp.max(logits, axis=0, keepdims=True))
    aff_ref[0] = ex / jnp.sum(ex, axis=0, keepdims=True)


def _outproj(x, y_lru, y_four, wo1, wo2, g2, wr_t):
    bsz, s, d = x.shape
    d_lru = y_lru.shape[-1]
    d_four = y_four.shape[-1]
    n_e = wr_t.shape[0]
    assert d == SUBLANES * LANES
    n_i = s // ROW_TILE
    tile = lambda w: pl.BlockSpec((1, ROW_TILE, w), lambda b, i: (b, i, 0))
    const = lambda shape: pl.BlockSpec(shape, lambda b, i: (0, 0))
    return pl.pallas_call(
        _outproj_kernel,
        grid=(bsz, n_i),
        in_specs=[tile(d), tile(d_lru), tile(d_four), const(wo1.shape), const(wo2.shape),
                  const(g2.shape), const(wr_t.shape)],
        out_specs=[tile(d),
                   pl.BlockSpec((ROW_TILE * SUBLANES, LANES), lambda b, i: (b * n_i + i, 0)),
                   pl.BlockSpec((1, n_e, ROW_TILE), lambda b, i: (b, 0, i))],
        out_shape=[
            jax.ShapeDtypeStruct((bsz, s, d), F32),
            jax.ShapeDtypeStruct((bsz * s * SUBLANES, LANES), F32),
            jax.ShapeDtypeStruct((bsz, n_e, s), F32),
        ],
        compiler_params=_cparams(("parallel", "parallel")),
        name="outproj",
    )(x, y_lru, y_four, wo1, wo2, g2, wr_t)


def _select_kernel(aff_ref, tri_ref, pos_ref, starts_ref, *, cap):
    v = aff_ref[0]
    n_e, s = v.shape
    cap_f = float(cap)

    def midpoint(lo, hi):
        mid = 0.5 * (lo + hi)
        return mid, (mid > lo) & (mid < hi)

    def cond(carry):
        _, active = midpoint(*carry)
        return jnp.max(active.astype(F32)) > 0.0

    def body(carry):
        lo, hi = carry
        mid, active = midpoint(lo, hi)
        cnt = jnp.sum((v >= mid).astype(F32), axis=1, keepdims=True)
        enough = cnt >= cap_f
        return (jnp.where(active & enough, mid, lo), jnp.where(active & (~enough), mid, hi))

    lo0 = jnp.zeros((n_e, 1), F32)
    hi0 = jnp.full((n_e, 1), 2.0, F32)
    thr, _ = lax.while_loop(cond, body, (lo0, hi0))

    above = v > thr
    tie = v == thr
    need = cap_f - jnp.sum(above.astype(F32), axis=1, keepdims=True)

    n_tiles = s // TOKEN_TILE
    tri = tri_ref[...]
    lane = lax.broadcasted_iota(jnp.int32, (n_e, LANES), 1)

    def prefix(mask_f, want_starts):
        run = jnp.zeros((n_e, 1), F32)
        starts = jnp.zeros((n_e, LANES), F32)
        pieces = []
        for t in range(n_tiles):
            m = mask_f[:, t * TOKEN_TILE:(t + 1) * TOKEN_TILE]
            incl = jnp.dot(m.astype(BF16), tri, preferred_element_type=F32)
            pieces.append(run + incl - m)
            if want_starts:
                starts = jnp.where(lane == t, run, starts)
            run = run + incl[:, TOKEN_TILE - 1:TOKEN_TILE]
        if want_starts:
            starts = jnp.where(lane == n_tiles, run, starts)
        return jnp.concatenate(pieces, axis=1), starts

    tie_rank, _ = prefix(tie.astype(F32), False)
    sel = above | (tie & (tie_rank < need))
    pos, starts = prefix(sel.astype(F32), True)
    pos_ref[0] = jnp.where(sel, pos, -1.0).astype(jnp.int32)
    starts_ref[0] = starts.astype(jnp.int32)


def _select(aff_t, cap):
    bsz, n_e, s = aff_t.shape
    idx = jnp.arange(TOKEN_TILE, dtype=jnp.int32)
    tri = (idx[:, None] <= idx[None, :]).astype(BF16)
    return pl.pallas_call(
        functools.partial(_select_kernel, cap=cap),
        grid=(bsz,),
        in_specs=[
            pl.BlockSpec((1, n_e, s), lambda b: (b, 0, 0)),
            pl.BlockSpec(tri.shape, lambda b: (0, 0)),
        ],
        out_specs=[
            pl.BlockSpec((1, n_e, s), lambda b: (b, 0, 0)),
            pl.BlockSpec((1, n_e, LANES), lambda b: (b, 0, 0)),
        ],
        out_shape=[
            jax.ShapeDtypeStruct((bsz, n_e, s), jnp.int32),
            jax.ShapeDtypeStruct((bsz, n_e, LANES), jnp.int32),
        ],
        compiler_params=_cparams(("parallel",)),
        name="select",
    )(aff_t, tri)


def _compact_kernel(starts_ref, pos_ref, gate_ref, idx_ref, gs_ref, idx_scr, gs_scr):
    b = pl.program_id(0)
    n_e = pos_ref.shape[2]
    n_k = pos_ref.shape[1] // TOKEN_TILE
    idx_scr[...] = jnp.zeros_like(idx_scr)
    gs_scr[...] = jnp.zeros_like(gs_scr)
    lane = lax.broadcasted_iota(jnp.int32, (TOKEN_TILE, SLOT_BLOCK), 1)
    sub = lax.broadcasted_iota(jnp.int32, (TOKEN_TILE, SLOT_BLOCK), 0)

    def tile(k, carry):
        rows = pl.ds(pl.multiple_of(k * TOKEN_TILE, TOKEN_TILE), TOKEN_TILE)
        pos = pos_ref[0, rows, :]
        gate = gate_ref[0, rows, :]
        tok = (sub + k * TOKEN_TILE).astype(F32)
        for e in range(n_e):
            base = (b * n_e + e) * LANES
            s0 = starts_ref[base + k]
            s1 = starts_ref[base + k + 1]
            m0 = s0 // SLOT_BLOCK
            pcol = pos[:, e:e + 1]
            gcol = gate[:, e:e + 1]

            def pair(m, e=e, pcol=pcol, gcol=gcol):
                hit = pcol - m * SLOT_BLOCK == lane
                idx_scr[e, m] += jnp.sum(jnp.where(hit, tok, 0.0), axis=0, keepdims=True)
                gs_scr[e, m] += jnp.sum(jnp.where(hit, gcol, 0.0), axis=0, keepdims=True)

            @pl.when(s1 > s0)
            def _(pair=pair, m0=m0):
                pair(m0)

            @pl.when(s1 > (m0 + 1) * SLOT_BLOCK)
            def _(pair=pair, m0=m0):
                pair(m0 + 1)

        return carry

    lax.fori_loop(0, n_k, tile, 0)
    idx_ref[0] = idx_scr[...].astype(jnp.int32)
    gs_ref[0] = gs_scr[...]


def _compact(starts_flat, pos_c, gate_c, cap):
    bsz, s, n_e = pos_c.shape
    n_blocks = cap // SLOT_BLOCK
    out_block = (1, n_e, n_blocks, 1, SLOT_BLOCK)
    tok_spec = pl.BlockSpec((1, s, n_e), lambda b, st: (b, 0, 0))
    out_spec = pl.BlockSpec(out_block, lambda b, st: (b, 0, 0, 0, 0))
    grid_spec = pltpu.PrefetchScalarGridSpec(
        num_scalar_prefetch=1,
        grid=(bsz,),
        in_specs=[tok_spec, tok_spec],
        out_specs=[out_spec, out_spec],
        scratch_shapes=[pltpu.VMEM(out_block[1:], F32), pltpu.VMEM(out_block[1:], F32)],
    )
    return pl.pallas_call(
        _compact_kernel,
        grid_spec=grid_spec,
        out_shape=[jax.ShapeDtypeStruct((bsz,) + out_block[1:], jnp.int32),
                   jax.ShapeDtypeStruct((bsz,) + out_block[1:], F32)],
        compiler_params=_cparams(("parallel",)),
        name="compact",
    )(starts_flat, pos_c, gate_c)


def _moe_kernel(idx_ref, idx_next_ref, gs_ref, h2_hbm, wg_ref, wu_ref, wd_ref, eo_ref,
                xbuf, sem, xb_scr, acc_scr, *, seq_len, n_ff):
    b = pl.program_id(0)
    e = pl.program_id(1)
    f = pl.program_id(2)
    n_e = pl.num_programs(1)
    n_blocks = xb_scr.shape[0]
    cap = n_blocks * SLOT_BLOCK
    d = xb_scr.shape[2]
    step = b * n_e + e
    n_steps = pl.num_programs(0) * n_e
    cur = step % 2

    def token_copy(idx_smem, batch, buf, p):
        tok = idx_smem[0, 0, p]
        src = h2_hbm.at[pl.ds(pl.multiple_of((batch * seq_len + tok) * SUBLANES, SUBLANES), SUBLANES)]
        dst = xbuf.at[buf, pl.ds(pl.multiple_of(p * SUBLANES, SUBLANES), SUBLANES)]
        return pltpu.make_async_copy(src, dst, sem.at[buf])

    def request(idx_smem, batch, buf, first, count):
        def body(i, carry):
            for j in range(GATHER_UNROLL):
                token_copy(idx_smem, batch, buf, first + i * GATHER_UNROLL + j).start()
            return carry
        lax.fori_loop(0, count // GATHER_UNROLL, body, 0)

    @pl.when((step == 0) & (f == 0))
    def _():
        request(idx_ref, b, cur, 0, cap)

    @pl.when(step + 1 < n_steps)
    def _():
        request(idx_next_ref, (step + 1) // n_e, 1 - cur, f * (cap // n_ff), cap // n_ff)

    @pl.when(f == 0)
    def _():
        pltpu.make_async_copy(h2_hbm.at[pl.ds(0, cap * SUBLANES)], xbuf.at[cur], sem.at[cur]).wait()
        for m in range(n_blocks):
            first = m * SLOT_BLOCK * SUBLANES
            xm = jnp.concatenate(
                [xbuf[cur, pl.ds(first + j, SLOT_BLOCK, stride=SUBLANES), :] for j in range(d // LANES)],
                axis=1)
            xb_scr[m] = xm.astype(BF16)
        acc_scr[...] = jnp.zeros_like(acc_scr)

    wg = wg_ref[0].astype(BF16)
    wu = wu_ref[0].astype(BF16)
    wd = wd_ref[0].astype(BF16)
    for m in range(n_blocks):
        xm = xb_scr[m]
        g = jnp.dot(xm, wg, preferred_element_type=F32)
        u = jnp.dot(xm, wu, preferred_element_type=F32)
        act = (jax.nn.silu(g) * u).astype(BF16)
        acc_scr[m] += jnp.dot(act, wd, preferred_element_type=F32)

    @pl.when(f == n_ff - 1)
    def _():
        eye =(lax.broadcasted_iota(jnp.int32, (SLOT_BLOCK, SLOT_BLOCK), 0)
               == lax.broadcasted_iota(jnp.int32, (SLOT_BLOCK, SLOT_BLOCK), 1))
        for m in range(n_blocks):
            gcol = jnp.sum(jnp.where(eye, gs_ref[0, m], 0.0), axis=1, keepdims=True)
            rows = pl.ds(m * SLOT_BLOCK, SLOT_BLOCK)
            eo_ref[0, 0, rows, :] = (gcol * acc_scr[m]).astype(eo_ref.dtype)


def _moe(idx, gs, h2_tiles, wg, wu, wd, bsz, s):
    n_e, d, d_ff = wg.shape
    n_blocks = gs.shape[1]
    cap = n_blocks * SLOT_BLOCK
    n_ff = d_ff // FF_CHUNK
    assert d_ff % FF_CHUNK == 0 and cap % (n_ff * GATHER_UNROLL) == 0 and d == SUBLANES * LANES

    def next_step(b, e, f):
        nxt = jnp.minimum(b * n_e + e + 1, bsz * n_e - 1)
        return (nxt, 0, 0)

    smem = pltpu.SMEM
    return pl.pallas_call(
        functools.partial(_moe_kernel, seq_len=s, n_ff=n_ff),
        grid=(bsz, n_e, n_ff),
        in_specs=[
            pl.BlockSpec((1, 1, cap), lambda b, e, f: (b * n_e + e, 0, 0), memory_space=smem),
            pl.BlockSpec((1, 1, cap), next_step, memory_space=smem),
            pl.BlockSpec((1, n_blocks, 1, SLOT_BLOCK), lambda b, e, f: (b * n_e + e, 0, 0, 0)),
            pl.BlockSpec(memory_space=pl.ANY),
            pl.BlockSpec((1, d, FF_CHUNK), lambda b, e, f: (e, 0, f)),
            pl.BlockSpec((1, d, FF_CHUNK), lambda b, e, f: (e, 0, f)),
            pl.BlockSpec((1, FF_CHUNK, d), lambda b, e, f: (e, f, 0)),
        ],
        out_specs=pl.BlockSpec((1, 1, cap, d), lambda b, e, f: (b, e, 0, 0)),
        out_shape=jax.ShapeDtypeStruct((bsz, n_e, cap, d), BF16),
        scratch_shapes=[pltpu.VMEM((2, cap * SUBLANES, LANES), F32),
                        pltpu.SemaphoreType.DMA((2,)),
                        pltpu.VMEM((n_blocks, SLOT_BLOCK, d), BF16),
                        pltpu.VMEM((n_blocks, SLOT_BLOCK, d), F32)],
        compiler_params=_cparams(("arbitrary", "arbitrary", "arbitrary")),
        name="moe",
    )(idx, idx, gs, h2_tiles, wg, wu, wd)


def _window_start(s0, cap):
    return jnp.minimum(s0 // WIN_ALIGN, (cap - WIN_ROWS) // WIN_ALIGN) * WIN_ALIGN


def _combine_kernel(starts_ref, x1_ref, pos_ref, gf_ref, *rest, n_experts, cap):
    win_refs = rest[:n_experts]
    out_ref, wcat_scr, acc_scr = rest[n_experts:]
    b = pl.program_id(0)
    k = pl.program_id(1)
    pos = pos_ref[0]
    lane = lax.broadcasted_iota(jnp.int32, (TOKEN_TILE, SLOT_BLOCK), 1)

    onehots = []
    rels = []
    for e in range(n_experts):
        base = (b * n_experts + e) * LANES
        ws = _window_start(starts_ref[base + k], cap)
        rel = pos[:, e:e + 1] - ws
        rels.append((rel, starts_ref[base + k + 1] - ws))
        onehots.append(jnp.where(rel == lane, 1.0, 0.0).astype(BF16))
        wcat_scr[pl.ds(e * SLOT_BLOCK, SLOT_BLOCK), :] = win_refs[e][pl.ds(0, SLOT_BLOCK), :]
    onehot = jnp.concatenate(onehots, axis=1)
    acc_scr[...] = x1_ref[0] + jnp.dot(onehot, wcat_scr[...], preferred_element_type=F32)

    tail_lane = lax.broadcasted_iota(jnp.int32, (TOKEN_TILE, WIN_ALIGN), 1)
    for e in range(n_experts):
        rel, end = rels[e]

        @pl.when(end > SLOT_BLOCK)
        def _(e=e, rel=rel):
            tail = jnp.where(rel - SLOT_BLOCK == tail_lane, 1.0, 0.0).astype(BF16)
            acc_scr[...] += jnp.dot(tail, win_refs[e][pl.ds(SLOT_BLOCK, WIN_ALIGN), :],
                                    preferred_element_type=F32)

    out_ref[0] = _rms_scale(acc_scr[...]) * gf_ref[...]


def _combine(starts_flat, x1, pos_c, eo, gf):
    bsz, s, d = x1.shape
    n_e = pos_c.shape[-1]
    cap = eo.shape[2]
    n_k = s // TOKEN_TILE

    def window_spec(e):
        def index_map(b, k, st):
            return (b, e, _window_start(st[(b * n_e + e) * LANES + k], cap), 0)
        return pl.BlockSpec((pl.Squeezed(), pl.Squeezed(), pl.Element(WIN_ROWS), pl.Element(d)),
                            index_map)

    grid_spec = pltpu.PrefetchScalarGridSpec(
        num_scalar_prefetch=1,
        grid=(bsz, n_k),
        in_specs=[
            pl.BlockSpec((1, TOKEN_TILE, d), lambda b, k, st: (b, k, 0)),
            pl.BlockSpec((1, TOKEN_TILE, n_e), lambda b, k, st: (b, k, 0)),
            pl.BlockSpec((1, d), lambda b, k, st: (0, 0)),
        ] + [window_spec(e) for e in range(n_e)],
        out_specs=pl.BlockSpec((1, TOKEN_TILE, d), lambda b, k, st: (b, k, 0)),
        scratch_shapes=[pltpu.VMEM((n_e * SLOT_BLOCK, d), BF16),
                        pltpu.VMEM((TOKEN_TILE, d), F32)],
    )
    return pl.pallas_call(
        functools.partial(_combine_kernel, n_experts=n_e, cap=cap),
        grid_spec=grid_spec,
        out_shape=jax.ShapeDtypeStruct((bsz, s, d), F32),
        compiler_params=_cparams(("parallel", "parallel")),
        name="combine",
    )(starts_flat, x1, pos_c, gf, *([eo] * n_e))


def _block_diag(w):
    h, hd, _ = w.shape
    eye = jnp.eye(h, dtype=w.dtype)
    return (eye[:, None, :, None] * w[:, :, None, :]).reshape(h * hd, h * hd)


def _lru_params(wa_f, wx_f, wa_b, wx_b, ba_f, bx_f, ba_b, bx_b, lam_f, lam_b):
    d_lru = ba_f.shape[0]
    n_blk = d_lru // LANES
    mats = [_block_diag(w) for w in (wa_f, wx_f, wa_b, wx_b)]
    w_cat = jnp.stack([
        jnp.concatenate([m[c * LANES:(c + 1) * LANES, c * LANES:(c + 1) * LANES] for m in mats], axis=1)
        for c in range(n_blk)]).astype(BF16)
    b_cat = jnp.stack([
        jnp.concatenate([v[c * LANES:(c + 1) * LANES] for v in (ba_f, bx_f, ba_b, bx_b)])
        for c in range(n_blk)])[:, None, :]
    lam_cat = jnp.stack([
        jnp.concatenate([v[c * LANES:(c + 1) * LANES] for v in (lam_f, lam_b)])
        for c in range(n_blk)])[:, None, :]
    return w_cat, b_cat, lam_cat


def kernel(x, norm1_g, w_in, conv_w, conv_b, lru_wa_f, lru_ba_f, lru_wx_f, lru_bx_f, lru_lam_f,
           lru_wa_b, lru_ba_b, lru_wx_b, lru_bx_b, lru_lam_b, w_out, norm2_g, w_router,
           w_gate, w_up, w_down, normf_g):
    bsz, s, d = x.shape
    d_lru = conv_b.shape[0]
    d_four = w_in.shape[1] - 2 * d_lru
    n_e = w_router.shape[1]
    cap = CAPACITY_FACTOR * s // n_e
    assert s % LRU_CHUNK == 0 and s % (SCAN_GROUPS * SUBLANES * SUBLANES) == 0
    assert s % FOURIER_N2 == 0 and cap % SLOT_BLOCK == 0 and s // TOKEN_TILE < LANES
    assert cap >= WIN_ROWS and w_gate.shape[2] % FF_CHUNK == 0

    lx, lg, fo = _inproj(x.reshape(bsz * s, d), norm1_g[None, :], w_in.astype(BF16), d_lru, d_four)
    lx = lx.reshape(bsz, s, d_lru)
    lg = lg.reshape(bsz, s, d_lru)
    fo = fo.reshape(bsz, s, d_four)

    w_cat, b_cat, lam_cat = _lru_params(lru_wa_f, lru_wx_f, lru_wa_b, lru_wx_b,
                                        lru_ba_f, lru_bx_f, lru_ba_b, lru_bx_b,
                                        lru_lam_f, lru_lam_b)
    y_lru = _lru(lx, lg, conv_w, conv_b[None, :], w_cat, b_cat, lam_cat)
    y_four = _fourier(fo, FOURIER_GROUPS)

    w_out_bf = w_out.astype(BF16)
    x1, h2_tiles, aff_t = _outproj(x, y_lru, y_four, w_out_bf[:d_lru], w_out_bf[d_lru:],
                                   norm2_g[None, :], w_router.T)

    pos_r, starts = _select(aff_t, cap)
    starts_flat = starts.reshape(-1)
    pos_c = jnp.transpose(pos_r, (0, 2, 1))
    gate_c = jnp.transpose(aff_t, (0, 2, 1))
    idx, gs = _compact(starts_flat, pos_c, gate_c, cap)
    eo = _moe(idx.reshape(bsz * n_e, 1, cap), gs.reshape((bsz * n_e,) + gs.shape[2:]), h2_tiles,
              w_gate, w_up, w_down, bsz, s)
    return _combine(starts_flat, x1, pos_c, eo, normf_g[None, :])
```

```python
import functools
import math

import jax
import jax.numpy as jnp
import numpy as np
from jax import lax
from jax.experimental import pallas as pl
from jax.experimental.pallas import tpu as pltpu

F32 = jnp.float32
BF16 = jnp.bfloat16

EPS = 1e-6
LRU_C = 8.0
LRU_HEADS = 8
N_EXPERTS = 16
CAPACITY_FACTOR = 2
FOURIER_GROUPS = 4

SUBLANES = 8
LANES = 128
MXU_DIM = 256
VMEM_LIMIT_BYTES = 56 * 1024 * 1024

ROW_TILE = 512
LRU_CHUNK = 512
SCAN_GROUPS = 4
SLOT_BLOCK = MXU_DIM
TOKEN_TILE = MXU_DIM
SEG_PAD = SUBLANES
FOURIER_N2 = LANES
FOURIER_K1_BATCH = 8
FF_CHUNK = 512
GATHER_UNROLL = 8
COMPACT_WIN = 64
TOKEN_ID_BASE = 64
GATE_COL0 = 16
WIN_ALIGN = 16
WIN_ROWS = SLOT_BLOCK + WIN_ALIGN
NARROW_WIN = 64


def _cparams(semantics):
    return pltpu.CompilerParams(dimension_semantics=semantics,
                                vmem_limit_bytes=VMEM_LIMIT_BYTES)


def _rms_scale(x):
    return x * lax.rsqrt(jnp.mean(x * x, axis=-1, keepdims=True) + EPS)


def _inproj_kernel(x_ref, g_ref, w_ref, lx_ref, lg_ref, fo_ref):
    h = _rms_scale(x_ref[...]) * g_ref[...]
    p = jnp.dot(h.astype(BF16), w_ref[...], preferred_element_type=F32)
    d = lx_ref.shape[-1]
    lx_ref[...] = p[:, :d]
    lg_ref[...] = p[:, d:2 * d]
    fo_ref[...] = p[:, 2 * d:]


def _inproj(x2, g, w_bf, d_lru, d_four):
    m, d = x2.shape
    n = w_bf.shape[1]
    return pl.pallas_call(
        _inproj_kernel,
        grid=(m // ROW_TILE,),
        in_specs=[
            pl.BlockSpec((ROW_TILE, d), lambda i: (i, 0)),
            pl.BlockSpec((1, d), lambda i: (0, 0)),
            pl.BlockSpec((d, n), lambda i: (0, 0)),
        ],
        out_specs=[
            pl.BlockSpec((ROW_TILE, d_lru), lambda i: (i, 0)),
            pl.BlockSpec((ROW_TILE, d_lru), lambda i: (i, 0)),
            pl.BlockSpec((ROW_TILE, d_four), lambda i: (i, 0)),
        ],
        out_shape=[
            jax.ShapeDtypeStruct((m, d_lru), F32),
            jax.ShapeDtypeStruct((m, d_lru), F32),
            jax.ShapeDtypeStruct((m, d_four), F32),
        ],
        compiler_params=_cparams(("parallel",)),
        name="inproj",
    )(x2, g, w_bf)


def _shift_rows(x, shift):
    n = x.shape[0]
    rows = lax.broadcasted_iota(jnp.int32, x.shape, 0)
    rolled = pltpu.roll(x, shift % n, axis=0)
    keep = (rows >= shift) if shift > 0 else (rows < n + shift)
    return jnp.where(keep, rolled, 0.0)


def _lru_kernel(lx_ref, lg_ref, cw_ref, cb_ref, w_ref, b_ref, lam_ref, y_ref,
                xpad, af, bf, ab, bb):
    s = lx_ref.shape[1]
    c_blk = lx_ref.shape[2]
    n_seg = SCAN_GROUPS * SUBLANES
    seg = s // n_seg
    seg_stride = seg + SEG_PAD
    piece = min(seg, LRU_CHUNK)
    n_chunks = s // LRU_CHUNK
    pad = SUBLANES

    zeros_pad = jnp.zeros((pad, c_blk), F32)
    xpad[pl.ds(0, pad), :] = zeros_pad
    xpad[pl.ds(pad + s, pad), :] = zeros_pad

    def copy_chunk(i, carry):
        t0 = pl.multiple_of(i * LRU_CHUNK, LRU_CHUNK)
        xpad[pl.ds(pad + t0, LRU_CHUNK), :] = lx_ref[0, pl.ds(t0, LRU_CHUNK), :]
        return carry

    lax.fori_loop(0, n_chunks, copy_chunk, 0)

    cw = cw_ref[...]
    conv_width = cw.shape[0]
    cb = cb_ref[...]
    bias = b_ref[0]
    log_sig = jax.nn.log_sigmoid(lam_ref[0])
    w_cat = w_ref[0]

    def gates_chunk(i, carry):
        t0 = pl.multiple_of(i * LRU_CHUNK, LRU_CHUNK)
        win = xpad[pl.ds(t0, LRU_CHUNK + 2 * pad), :]
        c = cb
        left = conv_width // 2
        for k in range(conv_width):
            c = c + cw[k:k + 1, :] * win[pad - left + k:pad - left + k + LRU_CHUNK, :]
        z = jnp.dot(c.astype(BF16), w_cat, preferred_element_type=F32) + bias
        for d, (a_scr, b_scr) in enumerate(((af, bf), (ab, bb))):
            r = jax.nn.sigmoid(z[:, (2 * d) * c_blk:(2 * d + 1) * c_blk])
            ig = jax.nn.sigmoid(z[:, (2 * d + 1) * c_blk:(2 * d + 2) * c_blk])
            log_a = LRU_C * r * log_sig[:, d * c_blk:(d + 1) * c_blk]
            a = jnp.exp(log_a)
            th = jnp.tanh(log_a)
            mult = jnp.sqrt(-2.0 * th / (1.0 - th))
            bt = mult * (ig * c)
            for p in range(LRU_CHUNK // piece):
                t = t0 + p * piece
                dst = pl.ds(pl.multiple_of((t // seg) * seg_stride + t % seg, SUBLANES), piece)
                a_scr[dst, :] = a[p * piece:(p + 1) * piece]
                b_scr[dst, :] = bt[p * piece:(p + 1) * piece]
        return carry

    lax.fori_loop(0, n_chunks, gates_chunk, 0)

    def scan_step(i, carry):
        out = []
        for (a_scr, b_scr, o), (hs, ps) in zip(((af, bf, i), (ab, bb, seg - 1 - i)), carry):
            new_h, new_p = [], []
            for q in range(SCAN_GROUPS):
                rows = pl.ds(q * SUBLANES * seg_stride + o, SUBLANES, stride=seg_stride)
                a = a_scr[rows, :]
                h = a * hs[q] + b_scr[rows, :]
                p = a * ps[q]
                b_scr[rows, :] = h
                a_scr[rows, :] = p
                new_h.append(h)
                new_p.append(p)
            out.append((tuple(new_h), tuple(new_p)))
        return tuple(out)

    zero = jnp.zeros((SUBLANES, c_blk), F32)
    one = jnp.ones((SUBLANES, c_blk), F32)
    init = ((zero,) * SCAN_GROUPS, (one,) * SCAN_GROUPS)
    (hf_end, pf_end), (hb_end, pb_end) = lax.fori_loop(0, seg, scan_step, (init, init))

    row = lax.broadcasted_iota(jnp.int32, (SUBLANES, c_blk), 0)

    def entering(h_end, p_end, carry_in, first_row, shift):
        c = zero
        for _ in range(SUBLANES):
            c = jnp.where(row == first_row, carry_in, _shift_rows(h_end + p_end * c, shift))
        return c

    cf = [None] * SCAN_GROUPS
    cbk = [None] * SCAN_GROUPS
    carry_f = jnp.zeros((1, c_blk), F32)
    carry_b = jnp.zeros((1, c_blk), F32)
    for q in range(SCAN_GROUPS):
        cf[q] = entering(hf_end[q], pf_end[q], carry_f, 0, 1)
        carry_f = (hf_end[q] + pf_end[q] * cf[q])[SUBLANES - 1:SUBLANES, :]
        qb = SCAN_GROUPS - 1 - q
        cbk[qb] = entering(hb_end[qb], pb_end[qb], carry_b, SUBLANES - 1, -1)
        carry_b = (hb_end[qb] + pb_end[qb] * cbk[qb])[0:1, :]

    for j in range(n_seg):
        q, i = divmod(j, SUBLANES)
        src = pl.ds(j * seg_stride, seg)
        rows = pl.ds(j * seg, seg)
        hsum = ((bf[src, :] + af[src, :] * cf[q][i:i + 1, :])
                + (bb[src, :] + ab[src, :] * cbk[q][i:i + 1, :]))
        y_ref[0, rows, :] = (jax.nn.gelu(lg_ref[0, rows, :]) * hsum).astype(y_ref.dtype)


def _lru(lx, lg, conv_w, conv_b, w_cat, b_cat, lam_cat):
    bsz, s, d_lru = lx.shape
    c_blk = LANES
    n_blk = d_lru // c_blk
    scr = pltpu.VMEM((s + SCAN_GROUPS * SUBLANES * SEG_PAD, c_blk), F32)
    return pl.pallas_call(
        _lru_kernel,
        grid=(bsz, n_blk),
        in_specs=[
            pl.BlockSpec((1, s, c_blk), lambda b, c: (b, 0, c)),
            pl.BlockSpec((1, s, c_blk), lambda b, c: (b, 0, c)),
            pl.BlockSpec((4, c_blk), lambda b, c: (0, c)),
            pl.BlockSpec((1, c_blk), lambda b, c: (0, c)),
            pl.BlockSpec((1, c_blk, 4 * c_blk), lambda b, c: (c, 0, 0)),
            pl.BlockSpec((1, 1, 4 * c_blk), lambda b, c: (c, 0, 0)),
            pl.BlockSpec((1, 1, 2 * c_blk), lambda b, c: (c, 0, 0)),
        ],
        out_specs=pl.BlockSpec((1, s, c_blk), lambda b, c: (b, 0, c)),
        out_shape=jax.ShapeDtypeStruct((bsz, s, d_lru), BF16),
        scratch_shapes=[pltpu.VMEM((s + 2 * SUBLANES, c_blk), F32), scr, scr, scr, scr],
        compiler_params=_cparams(("parallel", "parallel")),
        name="lru",
    )(lx, lg, conv_w, conv_b, w_cat, b_cat, lam_cat)


def _fourier_kernel(f_ref, ka_ref, kb_ref, cc_ref, sc_ref, twc_ref, tws_ref, y_ref,
                    ar_scr, ai_scr, *, scale):
    n1 = f_ref.shape[1]
    n_u = f_ref.shape[2]
    cols = f_ref.shape[4]
    r = SUBLANES * n1
    n2 = n_u * SUBLANES

    def stage_a(u, carry):
        xu = f_ref[0, :, u].reshape(r, cols)
        a = jnp.dot(ka_ref[...], xu.astype(BF16), preferred_element_type=F32)
        a_re, a_im = a[:r], a[r:]
        c = twc_ref[u]
        sn = tws_ref[u]
        ar_scr[u] = a_re * c - a_im * sn
        ai_scr[u] = a_re * sn + a_im * c
        return carry

    lax.fori_loop(0, n_u, stage_a, 0)

    nb = FOURIER_K1_BATCH

    def stage_b(i, carry):
        rows = pl.ds(pl.multiple_of(i * (nb * SUBLANES), nb * SUBLANES), nb * SUBLANES)
        a_re = ar_scr[:, rows, :]
        a_im = ai_scr[:, rows, :]

        def rows_s2(a, j):
            return a[:, j * SUBLANES:(j + 1) * SUBLANES, :].reshape(n2, cols)

        st = jnp.concatenate(
            [jnp.concatenate([rows_s2(a_re, j), rows_s2(a_im, j)], axis=0) for j in range(nb)],
            axis=1).astype(BF16)
        x = jnp.dot(kb_ref[...], st, preferred_element_type=F32)
        x_re = jnp.concatenate([x[:n2, j * cols:(j + 1) * cols] for j in range(nb)], axis=0)
        x_im = jnp.concatenate([x[n2:, j * cols:(j + 1) * cols] for j in range(nb)], axis=0)
        y = (jnp.dot(x_re.astype(BF16), cc_ref[...], preferred_element_type=F32)
             + jnp.dot(x_im.astype(BF16), sc_ref[...], preferred_element_type=F32)) * scale
        for j in range(nb):
            y_ref[0, pl.ds(i * nb + j, n2, stride=n1), :] = y[j * n2:(j + 1) * n2]
        return carry

    lax.fori_loop(0, n1 // nb, stage_b, 0)


def _dft_tables(s, c_grp):
    n2 = FOURIER_N2
    n1 = s // n2
    n_u = n2 // SUBLANES

    def cos_sin(num, den):
        ang = (2.0 * np.pi / den) * (num % den).astype(np.float64)
        return np.cos(ang), np.sin(ang)

    def const(a, dtype=F32):
        return jnp.asarray(a.astype(np.float32)).astype(dtype)

    i1 = np.arange(n1, dtype=np.int64)
    c1, s1 = cos_sin(i1[:, None] * i1[None, :], n1)
    eye8 = np.eye(SUBLANES)
    ka = const(np.concatenate([np.kron(c1, eye8), np.kron(-s1, eye8)], axis=0), BF16)

    i2 = np.arange(n2, dtype=np.int64)
    c2, s2 = cos_sin(i2[:, None] * i2[None, :], n2)
    kb = const(np.block([[c2, s2], [-s2, c2]]), BF16)

    ic = np.arange(c_grp, dtype=np.int64)
    cc, sc = cos_sin(ic[:, None] * ic[None, :], c_grp)

    u = np.arange(n_u, dtype=np.int64)[:, None, None]
    k1 = np.arange(n1, dtype=np.int64)[None, :, None]
    v = np.arange(SUBLANES, dtype=np.int64)[None, None, :]
    tc, ts = cos_sin(k1 * (SUBLANES * u + v), s)
    shape = (n_u, n1 * SUBLANES, c_grp)
    twc = jnp.broadcast_to(const(tc.reshape(n_u, -1, 1)), shape)
    tws = jnp.broadcast_to(const(-ts.reshape(n_u, -1, 1)), shape)
    return ka, kb, const(cc, BF16), const(sc, BF16), twc, tws


def _fourier(four, n_groups):
    bsz, s, d_four = four.shape
    c_grp = d_four // n_groups
    n2 = FOURIER_N2
    n1 = s // n2
    n_u = n2 // SUBLANES
    r = SUBLANES * n1
    ka, kb, cc, sc, twc, tws = _dft_tables(s, c_grp)
    f5 = four.reshape(bsz, n1, n_u, SUBLANES, d_four)
    scale = 1.0 / math.sqrt(s * c_grp)
    const2 = lambda b, g: (0, 0)
    const3 = lambda b, g: (0, 0, 0)
    return pl.pallas_call(
        functools.partial(_fourier_kernel, scale=scale),
        grid=(bsz, n_groups),
        in_specs=[
            pl.BlockSpec((1, n1, n_u, SUBLANES, c_grp), lambda b, g: (b, 0, 0, 0, g)),
            pl.BlockSpec(ka.shape, const2),
            pl.BlockSpec(kb.shape, const2),
            pl.BlockSpec(cc.shape, const2),
            pl.BlockSpec(sc.shape, const2),
            pl.BlockSpec(twc.shape, const3),
            pl.BlockSpec(tws.shape, const3),
        ],
        out_specs=pl.BlockSpec((1, s, c_grp), lambda b, g: (b, 0, g)),
        out_shape=jax.ShapeDtypeStruct((bsz, s, d_four), F32),
        scratch_shapes=[pltpu.VMEM((n_u, r, c_grp), F32), pltpu.VMEM((n_u, r, c_grp), F32)],
        compiler_params=_cparams(("parallel", "parallel")),
        name="fourier",
    )(f5, ka, kb, cc, sc, twc, tws)


def _outproj_kernel(x_ref, yl_ref, yf_ref, wo1_ref, wo2_ref, g2_ref, wr_ref,
                    x1_ref, h2_ref, aff_ref):
    x1 = (x_ref[0]
          + jnp.dot(yl_ref[0], wo1_ref[...], preferred_element_type=F32)
          + jnp.dot(yf_ref[0].astype(BF16), wo2_ref[...], preferred_element_type=F32))
    x1_ref[0] = x1
    h2 = _rms_scale(x1) * g2_ref[...]
    for j in range(h2.shape[1] // LANES):
        h2_ref[pl.ds(j, h2.shape[0], stride=SUBLANES), :] = h2[:, j * LANES:(j + 1) * LANES]
    logits = lax.dot_general(wr_ref[...], h2, (((1,), (1,)), ((), ())),
                             precision=lax.Precision.HIGHEST,
                             preferred_element_type=F32)
    ex = jnp.exp(logits - jnp.max(logits, axis=0, keepdims=True))
    aff_ref[0] = ex / jnp.sum(ex, axis=0, keepdims=True)


def _outproj(x, y_lru, y_four, wo1, wo2, g2, wr_t):
    bsz, s, d = x.shape
    d_lru = y_lru.shape[-1]
    d_four = y_four.shape[-1]
    n_e = wr_t.shape[0]
    assert d == SUBLANES * LANES
    n_i = s // ROW_TILE
    tile = lambda w: pl.BlockSpec((1, ROW_TILE, w), lambda b, i: (b, i, 0))
    const = lambda shape: pl.BlockSpec(shape, lambda b, i: (0, 0))
    return pl.pallas_call(
        _outproj_kernel,
        grid=(bsz, n_i),
        in_specs=[tile(d), tile(d_lru), tile(d_four), const(wo1.shape), const(wo2.shape),
                  const(g2.shape), const(wr_t.shape)],
        out_specs=[tile(d),
                   pl.BlockSpec((ROW_TILE * SUBLANES, LANES), lambda b, i: (b * n_i + i, 0)),
                   pl.BlockSpec((1, n_e, ROW_TILE), lambda b, i: (b, 0, i))],
        out_shape=[
            jax.ShapeDtypeStruct((bsz, s, d), F32),
            jax.ShapeDtypeStruct((bsz * s * SUBLANES, LANES), F32),
            jax.ShapeDtypeStruct((bsz, n_e, s), F32),
        ],
        compiler_params=_cparams(("parallel", "parallel")),
        name="outproj",
    )(x, y_lru, y_four, wo1, wo2, g2, wr_t)


def _select_kernel(aff_ref, tri_ref, pos_ref, starts_ref, *, cap):
    v = aff_ref[0]
    n_e, s = v.shape
    cap_f = float(cap)

    def midpoint(lo, hi):
        mid = 0.5 * (lo + hi)
        return mid, (mid > lo) & (mid < hi)

    def cond(carry):
        _, active = midpoint(*carry)
        return jnp.max(active.astype(F32)) > 0.0

    def body(carry):
        lo, hi = carry
        mid, active = midpoint(lo, hi)
        cnt = jnp.sum((v >= mid).astype(F32), axis=1, keepdims=True)
        enough = cnt >= cap_f
        return (jnp.where(active & enough, mid, lo), jnp.where(active & (~enough), mid, hi))

    lo0 = jnp.zeros((n_e, 1), F32)
    hi0 = jnp.full((n_e, 1), 2.0, F32)
    thr, _ = lax.while_loop(cond, body, (lo0, hi0))

    above = v > thr
    tie = v == thr
    need = cap_f - jnp.sum(above.astype(F32), axis=1, keepdims=True)

    n_tiles = s // TOKEN_TILE
    tri = tri_ref[...]
    lane = lax.broadcasted_iota(jnp.int32, (n_e, LANES), 1)

    def prefix(mask_f, want_starts):
        run = jnp.zeros((n_e, 1), F32)
        starts = jnp.zeros((n_e, LANES), F32)
        pieces = []
        for t in range(n_tiles):
            m = mask_f[:, t * TOKEN_TILE:(t + 1) * TOKEN_TILE]
            incl = jnp.dot(m.astype(BF16), tri, preferred_element_type=F32)
            pieces.append(run + incl - m)
            if want_starts:
                starts = jnp.where(lane == t, run, starts)
            run = run + incl[:, TOKEN_TILE - 1:TOKEN_TILE]
        if want_starts:
            starts = jnp.where(lane == n_tiles, run, starts)
        return jnp.concatenate(pieces, axis=1), starts

    tie_rank, _ = prefix(tie.astype(F32), False)
    sel = above | (tie & (tie_rank < need))
    pos, starts = prefix(sel.astype(F32), True)
    pos_ref[0] = jnp.where(sel, pos, -1.0).astype(jnp.int32)
    starts_ref[0] = starts.astype(jnp.int32)


def _select(aff_t, cap):
    bsz, n_e, s = aff_t.shape
    idx = jnp.arange(TOKEN_TILE, dtype=jnp.int32)
    tri = (idx[:, None] <= idx[None, :]).astype(BF16)
    return pl.pallas_call(
        functools.partial(_select_kernel, cap=cap),
        grid=(bsz,),
        in_specs=[
            pl.BlockSpec((1, n_e, s), lambda b: (b, 0, 0)),
            pl.BlockSpec(tri.shape, lambda b: (0, 0)),
        ],
        out_specs=[
            pl.BlockSpec((1, n_e, s), lambda b: (b, 0, 0)),
            pl.BlockSpec((1, n_e, LANES), lambda b: (b, 0, 0)),
        ],
        out_shape=[
            jax.ShapeDtypeStruct((bsz, n_e, s), jnp.int32),
            jax.ShapeDtypeStruct((bsz, n_e, LANES), jnp.int32),
        ],
        compiler_params=_cparams(("parallel",)),
        name="select",
    )(aff_t, tri)


def _token_table(aff_t):
    bsz, n_e, s = aff_t.shape
    assert s <= TOKEN_ID_BASE * 256 and GATE_COL0 + 3 * n_e <= LANES
    gate = jnp.transpose(aff_t, (0, 2, 1))
    a1 = gate.astype(BF16)
    r1 = gate - a1.astype(F32)
    a2 = r1.astype(BF16)
    a3 = (r1 - a2.astype(F32)).astype(BF16)
    tok = jnp.arange(s, dtype=jnp.int32)
    ids = jnp.stack([tok // TOKEN_ID_BASE, tok % TOKEN_ID_BASE], axis=-1).astype(BF16)
    ids = jnp.broadcast_to(ids[None], (bsz, s, 2))
    zeros = lambda n: jnp.zeros((bsz, s, n), BF16)
    return jnp.concatenate([ids, zeros(GATE_COL0 - 2), a1, a2, a3,
                            zeros(LANES - GATE_COL0 - 3 * n_e)], axis=-1)


def _compact_kernel(starts_ref, pos_ref, v_ref, idx_ref, gs_ref, r_scr, *, cap):
    b = pl.program_id(0)
    n_e = pos_ref.shape[1]
    n_k = pos_ref.shape[2]
    n_blocks = cap // SLOT_BLOCK
    wide_windows = -(-(TOKEN_TILE + SUBLANES) // COMPACT_WIN)
    r_scr[...] = jnp.zeros_like(r_scr)
    sub = lax.broadcasted_iota(jnp.int32, (COMPACT_WIN, TOKEN_TILE), 0)

    def tile(k, carry):
        vk = v_ref[0, pl.ds(pl.multiple_of(k * TOKEN_TILE, TOKEN_TILE), TOKEN_TILE), :]
        starts = []
        fits = None
        for e in range(n_e):
            base = (b * n_e + e) * LANES
            ws = (starts_ref[base + k] // SUBLANES) * SUBLANES
            ok = starts_ref[base + k + 1] - ws <= COMPACT_WIN
            fits = ok if fits is None else fits & ok
            starts.append(ws)

        def place(e, ws):
            hit = pos_ref[0, e, k] - ws == sub
            vals = jnp.dot(jnp.where(hit, 1.0, 0.0).astype(BF16), vk, preferred_element_type=F32)
            r_scr[e, pl.ds(pl.multiple_of(ws, SUBLANES), COMPACT_WIN), :] += vals

        @pl.when(fits)
        def _():
            for e in range(n_e):
                place(e, starts[e])

        @pl.when(jnp.logical_not(fits))
        def _():
            for e in range(n_e):
                for j in range(wide_windows):
                    place(e, starts[e] + j * COMPACT_WIN)

        return carry

    lax.fori_loop(0, n_k, tile, 0)

    lane = lax.broadcasted_iota(jnp.int32, (SLOT_BLOCK, LANES), 1)
    eye = (lax.broadcasted_iota(jnp.int32, (SLOT_BLOCK, SLOT_BLOCK), 0)
           == lax.broadcasted_iota(jnp.int32, (SLOT_BLOCK, SLOT_BLOCK), 1))
    id_weight = jnp.where(lane == 0, float(TOKEN_ID_BASE), jnp.where(lane == 1, 1.0, 0.0))

    def to_row(col):
        return jnp.sum(jnp.where(eye, col, 0.0), axis=0, keepdims=True)

    def finish(e, carry):
        gate_lane = (lane >= GATE_COL0) & (lane < GATE_COL0 + 3 * n_e) & ((lane - GATE_COL0) % n_e == e)
        for m in range(n_blocks):
            blk = r_scr[e, pl.ds(m * SLOT_BLOCK, SLOT_BLOCK), :]
            ids = jnp.sum(blk * id_weight, axis=1, keepdims=True)
            gates = jnp.sum(jnp.where(gate_lane, blk, 0.0), axis=1, keepdims=True)
            idx_ref[0, e, m] = to_row(ids).astype(jnp.int32)
            gs_ref[0, e, m] = to_row(gates)
        return carry

    lax.fori_loop(0, n_e, finish, 0)


def _compact(starts_flat, pos5, table, cap):
    bsz, n_e, n_k = pos5.shape[:3]
    s = table.shape[1]
    n_blocks = cap // SLOT_BLOCK
    out_block = (1, n_e, n_blocks, 1, SLOT_BLOCK)
    out_spec = pl.BlockSpec(out_block, lambda b, st: (b, 0, 0, 0, 0))
    wide_rows = -(-(TOKEN_TILE + SUBLANES) // COMPACT_WIN) * COMPACT_WIN
    grid_spec = pltpu.PrefetchScalarGridSpec(
        num_scalar_prefetch=1,
        grid=(bsz,),
        in_specs=[pl.BlockSpec((1, n_e, n_k, 1, TOKEN_TILE), lambda b, st: (b, 0, 0, 0, 0)),
                  pl.BlockSpec((1, s, LANES), lambda b, st: (b, 0, 0))],
        out_specs=[out_spec, out_spec],
        scratch_shapes=[pltpu.VMEM((n_e, cap + wide_rows, LANES), F32)],
    )
    return pl.pallas_call(
        functools.partial(_compact_kernel, cap=cap),
        grid_spec=grid_spec,
        out_shape=[jax.ShapeDtypeStruct((bsz,) + out_block[1:], jnp.int32),
                   jax.ShapeDtypeStruct((bsz,) + out_block[1:], F32)],
        compiler_params=_cparams(("parallel",)),
        name="compact",
    )(starts_flat, pos5, table)


def _moe_kernel(idx_ref, idx_next_ref, gs_ref, h2_hbm, wg_ref, wu_ref, wd_ref, eo_ref,
                xbuf, sem, xb_scr, acc_scr, *, seq_len, n_ff):
    b = pl.program_id(0)
    e = pl.program_id(1)
    f = pl.program_id(2)
    n_e = pl.num_programs(1)
    n_blocks = xb_scr.shape[0]
    cap = n_blocks * SLOT_BLOCK
    d = xb_scr.shape[2]
    step = b * n_e + e
    n_steps = pl.num_programs(0) * n_e
    cur = step % 2

    def token_copy(idx_smem, batch, buf, p):
        tok = idx_smem[0, 0, p]
        src = h2_hbm.at[pl.ds(pl.multiple_of((batch * seq_len + tok) * SUBLANES, SUBLANES), SUBLANES)]
        dst = xbuf.at[buf, pl.ds(pl.multiple_of(p * SUBLANES, SUBLANES), SUBLANES)]
        return pltpu.make_async_copy(src, dst, sem.at[buf])

    def request(idx_smem, batch, buf, first, count):
        def body(i, carry):
            for j in range(GATHER_UNROLL):
                token_copy(idx_smem, batch, buf, first + i * GATHER_UNROLL + j).start()
            return carry
        lax.fori_loop(0, count // GATHER_UNROLL, body, 0)

    @pl.when((step == 0) & (f == 0))
    def _():
        request(idx_ref, b, cur, 0, cap)

    @pl.when(step + 1 < n_steps)
    def _():
        request(idx_next_ref, (step + 1) // n_e, 1 - cur, f * (cap // n_ff), cap // n_ff)

    @pl.when(f == 0)
    def _():
        pltpu.make_async_copy(h2_hbm.at[pl.ds(0, cap * SUBLANES)], xbuf.at[cur], sem.at[cur]).wait()
        for m in range(n_blocks):
            first = m * SLOT_BLOCK * SUBLANES
            xm = jnp.concatenate(
                [xbuf[cur, pl.ds(first + j, SLOT_BLOCK, stride=SUBLANES), :] for j in range(d // LANES)],
                axis=1)
            xb_scr[m] = xm.astype(BF16)
        acc_scr[...] = jnp.zeros_like(acc_scr)

    wg = wg_ref[0].astype(BF16)
    wu = wu_ref[0].astype(BF16)
    wd = wd_ref[0].astype(BF16)
    for m in range(n_blocks):
        xm = xb_scr[m]
        g = jnp.dot(xm, wg, preferred_element_type=F32)
        u = jnp.dot(xm, wu, preferred_element_type=F32)
        act = (jax.nn.silu(g) * u).astype(BF16)
        acc_scr[m] += jnp.dot(act, wd, preferred_element_type=F32)

    @pl.when(f == n_ff - 1)
    def _():
        eye =(lax.broadcasted_iota(jnp.int32, (SLOT_BLOCK, SLOT_BLOCK), 0)
               == lax.broadcasted_iota(jnp.int32, (SLOT_BLOCK, SLOT_BLOCK), 1))
        for m in range(n_blocks):
            gcol = jnp.sum(jnp.where(eye, gs_ref[0, m], 0.0), axis=1, keepdims=True)
            rows = pl.ds(m * SLOT_BLOCK, SLOT_BLOCK)
            eo_ref[0, 0, rows, :] = (gcol * acc_scr[m]).astype(eo_ref.dtype)


def _moe(idx, gs, h2_tiles, wg, wu, wd, bsz, s):
    n_e, d, d_ff = wg.shape
    n_blocks = gs.shape[1]
    cap = n_blocks * SLOT_BLOCK
    n_ff = d_ff // FF_CHUNK
    assert d_ff % FF_CHUNK == 0 and cap % (n_ff * GATHER_UNROLL) == 0 and d == SUBLANES * LANES

    def next_step(b, e, f):
        nxt = jnp.minimum(b * n_e + e + 1, bsz * n_e - 1)
        return (nxt, 0, 0)

    smem = pltpu.SMEM
    return pl.pallas_call(
        functools.partial(_moe_kernel, seq_len=s, n_ff=n_ff),
        grid=(bsz, n_e, n_ff),
        in_specs=[
            pl.BlockSpec((1, 1, cap), lambda b, e, f: (b * n_e + e, 0, 0), memory_space=smem),
            pl.BlockSpec((1, 1, cap), next_step, memory_space=smem),
            pl.BlockSpec((1, n_blocks, 1, SLOT_BLOCK), lambda b, e, f: (b * n_e + e, 0, 0, 0)),
            pl.BlockSpec(memory_space=pl.ANY),
            pl.BlockSpec((1, d, FF_CHUNK), lambda b, e, f: (e, 0, f)),
            pl.BlockSpec((1, d, FF_CHUNK), lambda b, e, f: (e, 0, f)),
            pl.BlockSpec((1, FF_CHUNK, d), lambda b, e, f: (e, f, 0)),
        ],
        out_specs=pl.BlockSpec((1, 1, cap, d), lambda b, e, f: (b, e, 0, 0)),
        out_shape=jax.ShapeDtypeStruct((bsz, n_e, cap, d), BF16),
        scratch_shapes=[pltpu.VMEM((2, cap * SUBLANES, LANES), F32),
                        pltpu.SemaphoreType.DMA((2,)),
                        pltpu.VMEM((n_blocks, SLOT_BLOCK, d), BF16),
                        pltpu.VMEM((n_blocks, SLOT_BLOCK, d), F32)],
        compiler_params=_cparams(("arbitrary", "arbitrary", "arbitrary")),
        name="moe",
    )(idx, idx, gs, h2_tiles, wg, wu, wd)


def _combine_kernel(starts_ref, x1_ref, pos_ref, gf_ref, eo_ref, out_ref, wcat_scr, acc_scr,
                    *, cap):
    b = pl.program_id(0)
    k = pl.program_id(1)
    n_e = pos_ref.shape[2]
    pos = pos_ref[0]
    expert_lane = lax.broadcasted_iota(jnp.int32, (1, n_e), 1)

    s0 = [starts_ref[(b * n_e + e) * LANES + k] for e in range(n_e)]
    s1 = [starts_ref[(b * n_e + e) * LANES + k + 1] for e in range(n_e)]

    def window_starts(rows):
        starts = [jnp.minimum(s // WIN_ALIGN, (cap - rows) // WIN_ALIGN) * WIN_ALIGN for s in s0]
        vec = jnp.zeros((1, n_e), jnp.int32)
        for e in range(n_e):
            vec = jnp.where(expert_lane == e, starts[e], vec)
        return starts, vec

    def onehot(rel, first, count, width):
        n_lanes = count * width
        lane = lax.broadcasted_iota(jnp.int32, (n_e, n_lanes), 1)
        sub = lax.broadcasted_iota(jnp.int32, (n_e, n_lanes), 0)
        expand = jnp.where(sub == first + lane // width, 1.0, 0.0).astype(BF16)
        spread = jnp.dot(jnp.clip(rel, -1, width).astype(F32).astype(BF16), expand,
                         preferred_element_type=F32)
        want = (lax.broadcasted_iota(jnp.int32, (TOKEN_TILE, n_lanes), 1) % width).astype(F32)
        return jnp.where(spread == want, 1.0, 0.0).astype(BF16)

    def finish(acc):
        out_ref[0] = _rms_scale(acc) * gf_ref[...]

    narrow, narrow_vec = window_starts(NARROW_WIN)
    fits = None
    for e in range(n_e):
        ok = s1[e] - narrow[e] <= NARROW_WIN
        fits = ok if fits is None else fits & ok

    @pl.when(fits)
    def _():
        for e in range(n_e):
            src = pl.ds(pl.multiple_of(narrow[e], WIN_ALIGN), NARROW_WIN)
            wcat_scr[pl.ds(e * NARROW_WIN, NARROW_WIN), :] = eo_ref[0, e, src, :]
        hit = onehot(pos - narrow_vec, 0, n_e, NARROW_WIN)
        finish(x1_ref[0] + jnp.dot(hit, wcat_scr[pl.ds(0, n_e * NARROW_WIN), :],
                                   preferred_element_type=F32))

    @pl.when(jnp.logical_not(fits))
    def _():
        wide, wide_vec = window_starts(SLOT_BLOCK + WIN_ALIGN)
        rel = pos - wide_vec
        group = wcat_scr.shape[0] // SLOT_BLOCK
        acc_scr[...] = x1_ref[0]
        for g in range(n_e // group):
            for i in range(group):
                e = g * group + i
                src = pl.ds(pl.multiple_of(wide[e], WIN_ALIGN), SLOT_BLOCK)
                wcat_scr[pl.ds(i * SLOT_BLOCK, SLOT_BLOCK), :] = eo_ref[0, e, src, :]
            acc_scr[...] += jnp.dot(onehot(rel, g * group, group, SLOT_BLOCK), wcat_scr[...],
                                    preferred_element_type=F32)
        for e in range(n_e):
            src = pl.ds(pl.multiple_of(wide[e] + SLOT_BLOCK, WIN_ALIGN), WIN_ALIGN)
            wcat_scr[pl.ds(e * WIN_ALIGN, WIN_ALIGN), :] = eo_ref[0, e, src, :]
        tail = onehot(rel - SLOT_BLOCK, 0, n_e, WIN_ALIGN)
        finish(acc_scr[...] + jnp.dot(tail, wcat_scr[pl.ds(0, n_e * WIN_ALIGN), :],
                                      preferred_element_type=F32))


def _combine(starts_flat, x1, pos_c, eo, gf):
    bsz, s, d = x1.shape
    n_e = pos_c.shape[-1]
    cap = eo.shape[2]
    n_k = s // TOKEN_TILE
    wide_group = 4
    assert n_e % wide_group == 0 and n_e * NARROW_WIN <= wide_group * SLOT_BLOCK
    grid_spec = pltpu.PrefetchScalarGridSpec(
        num_scalar_prefetch=1,
        grid=(bsz, n_k),
        in_specs=[
            pl.BlockSpec((1, TOKEN_TILE, d), lambda b, k, st: (b, k, 0)),
            pl.BlockSpec((1, TOKEN_TILE, n_e), lambda b, k, st: (b, k, 0)),
            pl.BlockSpec((1, d), lambda b, k, st: (0, 0)),
            pl.BlockSpec((1, n_e, cap, d), lambda b, k, st: (b, 0, 0, 0),
                         pipeline_mode=pl.Buffered(1)),
        ],
        out_specs=pl.BlockSpec((1, TOKEN_TILE, d), lambda b, k, st: (b, k, 0)),
        scratch_shapes=[pltpu.VMEM((wide_group * SLOT_BLOCK, d), BF16),
                        pltpu.VMEM((TOKEN_TILE, d), F32)],
    )
    return pl.pallas_call(
        functools.partial(_combine_kernel, cap=cap),
        grid_spec=grid_spec,
        out_shape=jax.ShapeDtypeStruct((bsz, s, d), F32),
        compiler_params=_cparams(("parallel", "parallel")),
        name="combine",
    )(starts_flat, x1, pos_c, gf, eo)


def _block_diag(w):
    h, hd, _ = w.shape
    eye = jnp.eye(h, dtype=w.dtype)
    return (eye[:, None, :, None] * w[:, :, None, :]).reshape(h * hd, h * hd)


def _lru_params(wa_f, wx_f, wa_b, wx_b, ba_f, bx_f, ba_b, bx_b, lam_f, lam_b):
    d_lru = ba_f.shape[0]
    n_blk = d_lru // LANES
    mats = [_block_diag(w) for w in (wa_f, wx_f, wa_b, wx_b)]
    w_cat = jnp.stack([
        jnp.concatenate([m[c * LANES:(c + 1) * LANES, c * LANES:(c + 1) * LANES] for m in mats], axis=1)
        for c in range(n_blk)]).astype(BF16)
    b_cat = jnp.stack([
        jnp.concatenate([v[c * LANES:(c + 1) * LANES] for v in (ba_f, bx_f, ba_b, bx_b)])
        for c in range(n_blk)])[:, None, :]
    lam_cat = jnp.stack([
        jnp.concatenate([v[c * LANES:(c + 1) * LANES] for v in (lam_f, lam_b)])
        for c in range(n_blk)])[:, None, :]
    return w_cat, b_cat, lam_cat


def kernel(x, norm1_g, w_in, conv_w, conv_b, lru_wa_f, lru_ba_f, lru_wx_f, lru_bx_f, lru_lam_f,
           lru_wa_b, lru_ba_b, lru_wx_b, lru_bx_b, lru_lam_b, w_out, norm2_g, w_router,
           w_gate, w_up, w_down, normf_g):
    bsz, s, d = x.shape
    d_lru = conv_b.shape[0]
    d_four = w_in.shape[1] - 2 * d_lru
    n_e = w_router.shape[1]
    cap = CAPACITY_FACTOR * s // n_e
    assert s % LRU_CHUNK == 0 and s % (SCAN_GROUPS * SUBLANES * SUBLANES) == 0
    assert s % FOURIER_N2 == 0 and cap % SLOT_BLOCK == 0 and s // TOKEN_TILE < LANES
    assert cap >= WIN_ROWS and w_gate.shape[2] % FF_CHUNK == 0

    lx, lg, fo = _inproj(x.reshape(bsz * s, d), norm1_g[None, :], w_in.astype(BF16), d_lru, d_four)
    lx = lx.reshape(bsz, s, d_lru)
    lg = lg.reshape(bsz, s, d_lru)
    fo = fo.reshape(bsz, s, d_four)

    w_cat, b_cat, lam_cat = _lru_params(lru_wa_f, lru_wx_f, lru_wa_b, lru_wx_b,
                                        lru_ba_f, lru_bx_f, lru_ba_b, lru_bx_b,
                                        lru_lam_f, lru_lam_b)
    y_lru = _lru(lx, lg, conv_w, conv_b[None, :], w_cat, b_cat, lam_cat)
    y_four = _fourier(fo, FOURIER_GROUPS)

    w_out_bf = w_out.astype(BF16)
    x1, h2_tiles, aff_t = _outproj(x, y_lru, y_four, w_out_bf[:d_lru], w_out_bf[d_lru:],
                                   norm2_g[None, :], w_router.T)

    pos_r, starts = _select(aff_t, cap)
    starts_flat = starts.reshape(-1)
    pos_c = jnp.transpose(pos_r, (0, 2, 1))
    pos5 = pos_r.reshape(bsz, n_e, s // TOKEN_TILE, 1, TOKEN_TILE)
    idx, gs = _compact(starts_flat, pos5, _token_table(aff_t), cap)
    eo = _moe(idx.reshape(bsz * n_e, 1, cap), gs.reshape((bsz * n_e,) + gs.shape[2:]), h2_tiles,
              w_gate, w_up, w_down, bsz, s)
    return _combine(starts_flat, x1, pos_c, eo, normf_g[None, :])
```

```python
import functools
import math

import jax
import jax.numpy as jnp
import numpy as np
from jax import lax
from jax.experimental import pallas as pl
from jax.experimental.pallas import tpu as pltpu

F32 = jnp.float32
BF16 = jnp.bfloat16

EPS = 1e-6
LRU_C = 8.0
LRU_HEADS = 8
N_EXPERTS = 16
CAPACITY_FACTOR = 2
FOURIER_GROUPS = 4

SUBLANES = 8
LANES = 128
MXU_DIM = 256
VMEM_LIMIT_BYTES = 56 * 1024 * 1024

ROW_TILE = 512
LRU_CHUNK = 512
SCAN_GROUPS = 4
SLOT_BLOCK = MXU_DIM
TOKEN_TILE = MXU_DIM
SEG_PAD = SUBLANES
FOURIER_N2 = LANES
FOURIER_K1_BATCH = 8
FF_CHUNK = 1024
GATHER_UNROLL = 8
COMPACT_WIN = 64
TOKEN_ID_BASE = 64
GATE_COL0 = 16
WIN_ALIGN = 16
WIN_ROWS = SLOT_BLOCK + WIN_ALIGN
NARROW_WIN = 64


def _cparams(semantics):
    return pltpu.CompilerParams(dimension_semantics=semantics,
                                vmem_limit_bytes=VMEM_LIMIT_BYTES)


def _rms_scale(x):
    return x * lax.rsqrt(jnp.mean(x * x, axis=-1, keepdims=True) + EPS)


def _sigmoid(x):
    return 0.5 * (1.0 + jnp.tanh(0.5 * x))


def _inproj_kernel(x_ref, g_ref, w_ref, lx_ref, lg_ref, fo_ref):
    h = _rms_scale(x_ref[...]) * g_ref[...]
    p = jnp.dot(h.astype(BF16), w_ref[...], preferred_element_type=F32)
    d = lx_ref.shape[-1]
    lx_ref[...] = p[:, :d]
    lg_ref[...] = p[:, d:2 * d]
    fo_ref[...] = p[:, 2 * d:]


def _inproj(x2, g, w_bf, d_lru, d_four):
    m, d = x2.shape
    n = w_bf.shape[1]
    return pl.pallas_call(
        _inproj_kernel,
        grid=(m // ROW_TILE,),
        in_specs=[
            pl.BlockSpec((ROW_TILE, d), lambda i: (i, 0)),
            pl.BlockSpec((1, d), lambda i: (0, 0)),
            pl.BlockSpec((d, n), lambda i: (0, 0)),
        ],
        out_specs=[
            pl.BlockSpec((ROW_TILE, d_lru), lambda i: (i, 0)),
            pl.BlockSpec((ROW_TILE, d_lru), lambda i: (i, 0)),
            pl.BlockSpec((ROW_TILE, d_four), lambda i: (i, 0)),
        ],
        out_shape=[
            jax.ShapeDtypeStruct((m, d_lru), F32),
            jax.ShapeDtypeStruct((m, d_lru), F32),
            jax.ShapeDtypeStruct((m, d_four), F32),
        ],
        compiler_params=_cparams(("parallel",)),
        name="inproj",
    )(x2, g, w_bf)


def _shift_rows(x, shift):
    n = x.shape[0]
    rows = lax.broadcasted_iota(jnp.int32, x.shape, 0)
    rolled = pltpu.roll(x, shift % n, axis=0)
    keep = (rows >= shift) if shift > 0 else (rows < n + shift)
    return jnp.where(keep, rolled, 0.0)


def _lru_kernel(lx_ref, lg_ref, cw_ref, cb_ref, w_ref, b_ref, lam_ref, y_ref,
                xpad, af, bf, ab, bb):
    s = lx_ref.shape[1]
    c_blk = lx_ref.shape[2]
    n_seg = SCAN_GROUPS * SUBLANES
    seg = s // n_seg
    seg_stride = seg + SEG_PAD
    piece = min(seg, LRU_CHUNK)
    n_chunks = s // LRU_CHUNK
    pad = SUBLANES

    zeros_pad = jnp.zeros((pad, c_blk), F32)
    xpad[pl.ds(0, pad), :] = zeros_pad
    xpad[pl.ds(pad + s, pad), :] = zeros_pad

    def copy_chunk(i, carry):
        t0 = pl.multiple_of(i * LRU_CHUNK, LRU_CHUNK)
        xpad[pl.ds(pad + t0, LRU_CHUNK), :] = lx_ref[0, pl.ds(t0, LRU_CHUNK), :]
        return carry

    lax.fori_loop(0, n_chunks, copy_chunk, 0)

    cw = cw_ref[...]
    conv_width = cw.shape[0]
    cb = cb_ref[...]
    bias = b_ref[0]
    log_sig = jax.nn.log_sigmoid(lam_ref[0])
    w_cat = w_ref[0]

    def gates_chunk(i, carry):
        t0 = pl.multiple_of(i * LRU_CHUNK, LRU_CHUNK)
        win = xpad[pl.ds(t0, LRU_CHUNK + 2 * pad), :]
        c = cb
        left = conv_width // 2
        for k in range(conv_width):
            c = c + cw[k:k + 1, :] * win[pad - left + k:pad - left + k + LRU_CHUNK, :]
        z = jnp.dot(c.astype(BF16), w_cat, preferred_element_type=F32) + bias
        for d, (a_scr, b_scr) in enumerate(((af, bf), (ab, bb))):
            r = _sigmoid(z[:, (2 * d) * c_blk:(2 * d + 1) * c_blk])
            ig = _sigmoid(z[:, (2 * d + 1) * c_blk:(2 * d + 2) * c_blk])
            log_a = LRU_C * r * log_sig[:, d * c_blk:(d + 1) * c_blk]
            t = jnp.tanh(0.5 * log_a)
            inv = 1.0 / (1.0 - t)
            a = (1.0 + t) * inv
            bt = (2.0 * jnp.sqrt(-t) * inv) * (ig * c)
            for p in range(LRU_CHUNK // piece):
                t = t0 + p * piece
                dst = pl.ds(pl.multiple_of((t // seg) * seg_stride + t % seg, SUBLANES), piece)
                a_scr[dst, :] = a[p * piece:(p + 1) * piece]
                b_scr[dst, :] = bt[p * piece:(p + 1) * piece]
        return carry

    lax.fori_loop(0, n_chunks, gates_chunk, 0)

    def scan_step(i, carry):
        out = []
        for (a_scr, b_scr, o), (hs, ps) in zip(((af, bf, i), (ab, bb, seg - 1 - i)), carry):
            new_h, new_p = [], []
            for q in range(SCAN_GROUPS):
                rows = pl.ds(q * SUBLANES * seg_stride + o, SUBLANES, stride=seg_stride)
                a = a_scr[rows, :]
                h = a * hs[q] + b_scr[rows, :]
                p = a * ps[q]
                b_scr[rows, :] = h
                a_scr[rows, :] = p
                new_h.append(h)
                new_p.append(p)
            out.append((tuple(new_h), tuple(new_p)))
        return tuple(out)

    zero = jnp.zeros((SUBLANES, c_blk), F32)
    one = jnp.ones((SUBLANES, c_blk), F32)
    init = ((zero,) * SCAN_GROUPS, (one,) * SCAN_GROUPS)
    (hf_end, pf_end), (hb_end, pb_end) = lax.fori_loop(0, seg, scan_step, (init, init))

    row = lax.broadcasted_iota(jnp.int32, (SUBLANES, c_blk), 0)

    def entering(h_end, p_end, carry_in, first_row, shift):
        c = zero
        for _ in range(SUBLANES):
            c = jnp.where(row == first_row, carry_in, _shift_rows(h_end + p_end * c, shift))
        return c

    cf = [None] * SCAN_GROUPS
    cbk = [None] * SCAN_GROUPS
    carry_f = jnp.zeros((1, c_blk), F32)
    carry_b = jnp.zeros((1, c_blk), F32)
    for q in range(SCAN_GROUPS):
        cf[q] = entering(hf_end[q], pf_end[q], carry_f, 0, 1)
        carry_f = (hf_end[q] + pf_end[q] * cf[q])[SUBLANES - 1:SUBLANES, :]
        qb = SCAN_GROUPS - 1 - q
        cbk[qb] = entering(hb_end[qb], pb_end[qb], carry_b, SUBLANES - 1, -1)
        carry_b = (hb_end[qb] + pb_end[qb] * cbk[qb])[0:1, :]

    for j in range(n_seg):
        q, i = divmod(j, SUBLANES)
        src = pl.ds(j * seg_stride, seg)
        rows = pl.ds(j * seg, seg)
        hsum = ((bf[src, :] + af[src, :] * cf[q][i:i + 1, :])
                + (bb[src, :] + ab[src, :] * cbk[q][i:i + 1, :]))
        y_ref[0, rows, :] = (jax.nn.gelu(lg_ref[0, rows, :]) * hsum).astype(y_ref.dtype)


def _lru(lx, lg, conv_w, conv_b, w_cat, b_cat, lam_cat):
    bsz, s, d_lru = lx.shape
    c_blk = LANES
    n_blk = d_lru // c_blk
    scr = pltpu.VMEM((s + SCAN_GROUPS * SUBLANES * SEG_PAD, c_blk), F32)
    return pl.pallas_call(
        _lru_kernel,
        grid=(bsz, n_blk),
        in_specs=[
            pl.BlockSpec((1, s, c_blk), lambda b, c: (b, 0, c)),
            pl.BlockSpec((1, s, c_blk), lambda b, c: (b, 0, c)),
            pl.BlockSpec((4, c_blk), lambda b, c: (0, c)),
            pl.BlockSpec((1, c_blk), lambda b, c: (0, c)),
            pl.BlockSpec((1, c_blk, 4 * c_blk), lambda b, c: (c, 0, 0)),
            pl.BlockSpec((1, 1, 4 * c_blk), lambda b, c: (c, 0, 0)),
            pl.BlockSpec((1, 1, 2 * c_blk), lambda b, c: (c, 0, 0)),
        ],
        out_specs=pl.BlockSpec((1, s, c_blk), lambda b, c: (b, 0, c)),
        out_shape=jax.ShapeDtypeStruct((bsz, s, d_lru), BF16),
        scratch_shapes=[pltpu.VMEM((s + 2 * SUBLANES, c_blk), F32), scr, scr, scr, scr],
        compiler_params=_cparams(("parallel", "parallel")),
        name="lru",
    )(lx, lg, conv_w, conv_b, w_cat, b_cat, lam_cat)


def _fourier_kernel(f_ref, ka_ref, kb_ref, cc_ref, sc_ref, twc_ref, tws_ref, y_ref,
                    ar_scr, ai_scr, *, scale):
    n1 = f_ref.shape[1]
    n_u = f_ref.shape[2]
    cols = f_ref.shape[4]
    r = SUBLANES * n1
    n2 = n_u * SUBLANES

    def stage_a(u, carry):
        xu = f_ref[0, :, u].reshape(r, cols)
        a = jnp.dot(ka_ref[...], xu.astype(BF16), preferred_element_type=F32)
        a_re, a_im = a[:r], a[r:]
        c = twc_ref[u]
        sn = tws_ref[u]
        ar_scr[u] = a_re * c - a_im * sn
        ai_scr[u] = a_re * sn + a_im * c
        return carry

    lax.fori_loop(0, n_u, stage_a, 0)

    nb = FOURIER_K1_BATCH

    def stage_b(i, carry):
        rows = pl.ds(pl.multiple_of(i * (nb * SUBLANES), nb * SUBLANES), nb * SUBLANES)
        a_re = ar_scr[:, rows, :]
        a_im = ai_scr[:, rows, :]

        def rows_s2(a, j):
            return a[:, j * SUBLANES:(j + 1) * SUBLANES, :].reshape(n2, cols)

        st = jnp.concatenate(
            [jnp.concatenate([rows_s2(a_re, j), rows_s2(a_im, j)], axis=0) for j in range(nb)],
            axis=1).astype(BF16)
        x = jnp.dot(kb_ref[...], st, preferred_element_type=F32)
        x_re = jnp.concatenate([x[:n2, j * cols:(j + 1) * cols] for j in range(nb)], axis=0)
        x_im = jnp.concatenate([x[n2:, j * cols:(j + 1) * cols] for j in range(nb)], axis=0)
        y = (jnp.dot(x_re.astype(BF16), cc_ref[...], preferred_element_type=F32)
             + jnp.dot(x_im.astype(BF16), sc_ref[...], preferred_element_type=F32)) * scale
        for j in range(nb):
            y_ref[0, pl.ds(i * nb + j, n2, stride=n1), :] = y[j * n2:(j + 1) * n2]
        return carry

    lax.fori_loop(0, n1 // nb, stage_b, 0)


def _dft_tables(s, c_grp):
    n2 = FOURIER_N2
    n1 = s // n2
    n_u = n2 // SUBLANES

    def cos_sin(num, den):
        ang = (2.0 * np.pi / den) * (num % den).astype(np.float64)
        return np.cos(ang), np.sin(ang)

    def const(a, dtype=F32):
        return jnp.asarray(a.astype(np.float32)).astype(dtype)

    i1 = np.arange(n1, dtype=np.int64)
    c1, s1 = cos_sin(i1[:, None] * i1[None, :], n1)
    eye8 = np.eye(SUBLANES)
    ka = const(np.concatenate([np.kron(c1, eye8), np.kron(-s1, eye8)], axis=0), BF16)

    i2 = np.arange(n2, dtype=np.int64)
    c2, s2 = cos_sin(i2[:, None] * i2[None, :], n2)
    kb = const(np.block([[c2, s2], [-s2, c2]]), BF16)

    ic = np.arange(c_grp, dtype=np.int64)
    cc, sc = cos_sin(ic[:, None] * ic[None, :], c_grp)

    u = np.arange(n_u, dtype=np.int64)[:, None, None]
    k1 = np.arange(n1, dtype=np.int64)[None, :, None]
    v = np.arange(SUBLANES, dtype=np.int64)[None, None, :]
    tc, ts = cos_sin(k1 * (SUBLANES * u + v), s)
    shape = (n_u, n1 * SUBLANES, c_grp)
    twc = jnp.broadcast_to(const(tc.reshape(n_u, -1, 1)), shape)
    tws = jnp.broadcast_to(const(-ts.reshape(n_u, -1, 1)), shape)
    return ka, kb, const(cc, BF16), const(sc, BF16), twc, tws


def _fourier(four, n_groups):
    bsz, s, d_four = four.shape
    c_grp = d_four // n_groups
    n2 = FOURIER_N2
    n1 = s // n2
    n_u = n2 // SUBLANES
    r = SUBLANES * n1
    ka, kb, cc, sc, twc, tws = _dft_tables(s, c_grp)
    f5 = four.reshape(bsz, n1, n_u, SUBLANES, d_four)
    scale = 1.0 / math.sqrt(s * c_grp)
    const2 = lambda b, g: (0, 0)
    const3 = lambda b, g: (0, 0, 0)
    return pl.pallas_call(
        functools.partial(_fourier_kernel, scale=scale),
        grid=(bsz, n_groups),
        in_specs=[
            pl.BlockSpec((1, n1, n_u, SUBLANES, c_grp), lambda b, g: (b, 0, 0, 0, g)),
            pl.BlockSpec(ka.shape, const2),
            pl.BlockSpec(kb.shape, const2),
            pl.BlockSpec(cc.shape, const2),
            pl.BlockSpec(sc.shape, const2),
            pl.BlockSpec(twc.shape, const3),
            pl.BlockSpec(tws.shape, const3),
        ],
        out_specs=pl.BlockSpec((1, s, c_grp), lambda b, g: (b, 0, g)),
        out_shape=jax.ShapeDtypeStruct((bsz, s, d_four), F32),
        scratch_shapes=[pltpu.VMEM((n_u, r, c_grp), F32), pltpu.VMEM((n_u, r, c_grp), F32)],
        compiler_params=_cparams(("parallel", "parallel")),
        name="fourier",
    )(f5, ka, kb, cc, sc, twc, tws)


def _outproj_kernel(x_ref, yl_ref, yf_ref, wo1_ref, wo2_ref, g2_ref, wr_ref,
                    x1_ref, h2_ref, aff_ref):
    x1 = (x_ref[0]
          + jnp.dot(yl_ref[0], wo1_ref[...], preferred_element_type=F32)
          + jnp.dot(yf_ref[0].astype(BF16), wo2_ref[...], preferred_element_type=F32))
    x1_ref[0] = x1
    h2 = _rms_scale(x1) * g2_ref[...]
    for j in range(h2.shape[1] // LANES):
        h2_ref[pl.ds(j, h2.shape[0], stride=SUBLANES), :] = h2[:, j * LANES:(j + 1) * LANES]
    logits = lax.dot_general(wr_ref[...], h2, (((1,), (1,)), ((), ())),
                             precision=lax.Precision.HIGHEST,
                             preferred_element_type=F32)
    ex = jnp.exp(logits - jnp.max(logits, axis=0, keepdims=True))
    aff_ref[0] = ex / jnp.sum(ex, axis=0, keepdims=True)


def _outproj(x, y_lru, y_four, wo1, wo2, g2, wr_t):
    bsz, s, d = x.shape
    d_lru = y_lru.shape[-1]
    d_four = y_four.shape[-1]
    n_e = wr_t.shape[0]
    assert d == SUBLANES * LANES
    n_i = s // ROW_TILE
    tile = lambda w: pl.BlockSpec((1, ROW_TILE, w), lambda b, i: (b, i, 0))
    const = lambda shape: pl.BlockSpec(shape, lambda b, i: (0, 0))
    return pl.pallas_call(
        _outproj_kernel,
        grid=(bsz, n_i),
        in_specs=[tile(d), tile(d_lru), tile(d_four), const(wo1.shape), const(wo2.shape),
                  const(g2.shape), const(wr_t.shape)],
        out_specs=[tile(d),
                   pl.BlockSpec((ROW_TILE * SUBLANES, LANES), lambda b, i: (b * n_i + i, 0)),
                   pl.BlockSpec((1, n_e, ROW_TILE), lambda b, i: (b, 0, i))],
        out_shape=[
            jax.ShapeDtypeStruct((bsz, s, d), F32),
            jax.ShapeDtypeStruct((bsz * s * SUBLANES, LANES), F32),
            jax.ShapeDtypeStruct((bsz, n_e, s), F32),
        ],
        compiler_params=_cparams(("parallel", "parallel")),
        name="outproj",
    )(x, y_lru, y_four, wo1, wo2, g2, wr_t)


def _select_kernel(aff_ref, tri_ref, pos_ref, starts_ref, *, cap):
    v = aff_ref[0]
    n_e, s = v.shape
    cap_f = float(cap)

    def midpoint(lo, hi):
        mid = 0.5 * (lo + hi)
        return mid, (mid > lo) & (mid < hi)

    def cond(carry):
        _, active = midpoint(*carry)
        return jnp.max(active.astype(F32)) > 0.0

    def body(carry):
        lo, hi = carry
        mid, active = midpoint(lo, hi)
        cnt = jnp.sum((v >= mid).astype(F32), axis=1, keepdims=True)
        enough = cnt >= cap_f
        return (jnp.where(active & enough, mid, lo), jnp.where(active & (~enough), mid, hi))

    lo0 = jnp.zeros((n_e, 1), F32)
    hi0 = jnp.full((n_e, 1), 2.0, F32)
    thr, _ = lax.while_loop(cond, body, (lo0, hi0))

    above = v > thr
    tie = v == thr
    need = cap_f - jnp.sum(above.astype(F32), axis=1, keepdims=True)

    n_tiles = s // TOKEN_TILE
    tri = tri_ref[...]
    lane = lax.broadcasted_iota(jnp.int32, (n_e, LANES), 1)

    def prefix(mask_f, want_starts):
        run = jnp.zeros((n_e, 1), F32)
        starts = jnp.zeros((n_e, LANES), F32)
        pieces = []
        for t in range(n_tiles):
            m = mask_f[:, t * TOKEN_TILE:(t + 1) * TOKEN_TILE]
            incl = jnp.dot(m.astype(BF16), tri, preferred_element_type=F32)
            pieces.append(run + incl - m)
            if want_starts:
                starts = jnp.where(lane == t, run, starts)
            run = run + incl[:, TOKEN_TILE - 1:TOKEN_TILE]
        if want_starts:
            starts = jnp.where(lane == n_tiles, run, starts)
        return jnp.concatenate(pieces, axis=1), starts

    tie_rank, _ = prefix(tie.astype(F32), False)
    sel = above | (tie & (tie_rank < need))
    pos, starts = prefix(sel.astype(F32), True)
    pos_ref[0] = jnp.where(sel, pos, -1.0).astype(jnp.int32)
    starts_ref[0] = starts.astype(jnp.int32)


def _select(aff_t, cap):
    bsz, n_e, s = aff_t.shape
    idx = jnp.arange(TOKEN_TILE, dtype=jnp.int32)
    tri = (idx[:, None] <= idx[None, :]).astype(BF16)
    return pl.pallas_call(
        functools.partial(_select_kernel, cap=cap),
        grid=(bsz,),
        in_specs=[
            pl.BlockSpec((1, n_e, s), lambda b: (b, 0, 0)),
            pl.BlockSpec(tri.shape, lambda b: (0, 0)),
        ],
        out_specs=[
            pl.BlockSpec((1, n_e, s), lambda b: (b, 0, 0)),
            pl.BlockSpec((1, n_e, LANES), lambda b: (b, 0, 0)),
        ],
        out_shape=[
            jax.ShapeDtypeStruct((bsz, n_e, s), jnp.int32),
            jax.ShapeDtypeStruct((bsz, n_e, LANES), jnp.int32),
        ],
        compiler_params=_cparams(("parallel",)),
        name="select",
    )(aff_t, tri)


def _token_table(aff_t):
    bsz, n_e, s = aff_t.shape
    assert s <= TOKEN_ID_BASE * 256 and GATE_COL0 + 3 * n_e <= LANES
    gate = jnp.transpose(aff_t, (0, 2, 1))
    a1 = gate.astype(BF16)
    r1 = gate - a1.astype(F32)
    a2 = r1.astype(BF16)
    a3 = (r1 - a2.astype(F32)).astype(BF16)
    tok = jnp.arange(s, dtype=jnp.int32)
    ids = jnp.stack([tok // TOKEN_ID_BASE, tok % TOKEN_ID_BASE], axis=-1).astype(BF16)
    ids = jnp.broadcast_to(ids[None], (bsz, s, 2))
    zeros = lambda n: jnp.zeros((bsz, s, n), BF16)
    return jnp.concatenate([ids, zeros(GATE_COL0 - 2), a1, a2, a3,
                            zeros(LANES - GATE_COL0 - 3 * n_e)], axis=-1)


def _compact_kernel(starts_ref, pos_ref, v_ref, idx_ref, gs_ref, r_scr, *, cap):
    b = pl.program_id(0)
    n_e = pos_ref.shape[1]
    n_k = pos_ref.shape[2]
    n_blocks = cap // SLOT_BLOCK
    wide_windows = -(-(TOKEN_TILE + SUBLANES) // COMPACT_WIN)
    r_scr[...] = jnp.zeros_like(r_scr)
    sub = lax.broadcasted_iota(jnp.int32, (COMPACT_WIN, TOKEN_TILE), 0)

    def tile(k, carry):
        vk = v_ref[0, pl.ds(pl.multiple_of(k * TOKEN_TILE, TOKEN_TILE), TOKEN_TILE), :]
        starts = []
        fits = None
        for e in range(n_e):
            base = (b * n_e + e) * LANES
            ws = (starts_ref[base + k] // SUBLANES) * SUBLANES
            ok = starts_ref[base + k + 1] - ws <= COMPACT_WIN
            fits = ok if fits is None else fits & ok
            starts.append(ws)

        def place(e, ws):
            hit = pos_ref[0, e, k] - ws == sub
            vals = jnp.dot(jnp.where(hit, 1.0, 0.0).astype(BF16), vk, preferred_element_type=F32)
            r_scr[e, pl.ds(pl.multiple_of(ws, SUBLANES), COMPACT_WIN), :] += vals

        @pl.when(fits)
        def _():
            for e in range(n_e):
                place(e, starts[e])

        @pl.when(jnp.logical_not(fits))
        def _():
            for e in range(n_e):
                for j in range(wide_windows):
                    place(e, starts[e] + j * COMPACT_WIN)

        return carry

    lax.fori_loop(0, n_k, tile, 0)

    lane = lax.broadcasted_iota(jnp.int32, (SLOT_BLOCK, LANES), 1)
    eye = (lax.broadcasted_iota(jnp.int32, (SLOT_BLOCK, SLOT_BLOCK), 0)
           == lax.broadcasted_iota(jnp.int32, (SLOT_BLOCK, SLOT_BLOCK), 1))
    id_weight = jnp.where(lane == 0, float(TOKEN_ID_BASE), jnp.where(lane == 1, 1.0, 0.0))

    def to_row(col):
        return jnp.sum(jnp.where(eye, col, 0.0), axis=0, keepdims=True)

    def finish(e, carry):
        gate_lane = (lane >= GATE_COL0) & (lane < GATE_COL0 + 3 * n_e) & ((lane - GATE_COL0) % n_e == e)
        for m in range(n_blocks):
            blk = r_scr[e, pl.ds(m * SLOT_BLOCK, SLOT_BLOCK), :]
            ids = jnp.sum(blk * id_weight, axis=1, keepdims=True)
            gates = jnp.sum(jnp.where(gate_lane, blk, 0.0), axis=1, keepdims=True)
            idx_ref[0, e, m] = to_row(ids).astype(jnp.int32)
            gs_ref[0, e, m] = to_row(gates)
        return carry

    lax.fori_loop(0, n_e, finish, 0)


def _compact(starts_flat, pos5, table, cap):
    bsz, n_e, n_k = pos5.shape[:3]
    s = table.shape[1]
    n_blocks = cap // SLOT_BLOCK
    out_block = (1, n_e, n_blocks, 1, SLOT_BLOCK)
    out_spec = pl.BlockSpec(out_block, lambda b, st: (b, 0, 0, 0, 0))
    wide_rows = -(-(TOKEN_TILE + SUBLANES) // COMPACT_WIN) * COMPACT_WIN
    grid_spec = pltpu.PrefetchScalarGridSpec(
        num_scalar_prefetch=1,
        grid=(bsz,),
        in_specs=[pl.BlockSpec((1, n_e, n_k, 1, TOKEN_TILE), lambda b, st: (b, 0, 0, 0, 0)),
                  pl.BlockSpec((1, s, LANES), lambda b, st: (b, 0, 0))],
        out_specs=[out_spec, out_spec],
        scratch_shapes=[pltpu.VMEM((n_e, cap + wide_rows, LANES), F32)],
    )
    return pl.pallas_call(
        functools.partial(_compact_kernel, cap=cap),
        grid_spec=grid_spec,
        out_shape=[jax.ShapeDtypeStruct((bsz,) + out_block[1:], jnp.int32),
                   jax.ShapeDtypeStruct((bsz,) + out_block[1:], F32)],
        compiler_params=_cparams(("parallel",)),
        name="compact",
    )(starts_flat, pos5, table)


def _moe_kernel(idx_ref, idx_next_ref, gs_ref, h2_hbm, wg_ref, wu_ref, wd_ref, eo_ref,
                xbuf, sem, xb_scr, acc_scr, *, seq_len, n_ff):
    b = pl.program_id(0)
    e = pl.program_id(1)
    f = pl.program_id(2)
    n_e = pl.num_programs(1)
    n_blocks = xb_scr.shape[0]
    cap = n_blocks * SLOT_BLOCK
    d = xb_scr.shape[2]
    step = b * n_e + e
    n_steps = pl.num_programs(0) * n_e
    cur = step % 2

    def token_copy(idx_smem, batch, buf, p):
        tok = idx_smem[0, 0, p]
        src = h2_hbm.at[pl.ds(pl.multiple_of((batch * seq_len + tok) * SUBLANES, SUBLANES), SUBLANES)]
        dst = xbuf.at[buf, pl.ds(pl.multiple_of(p * SUBLANES, SUBLANES), SUBLANES)]
        return pltpu.make_async_copy(src, dst, sem.at[buf])

    def request(idx_smem, batch, buf, first, count):
        def body(i, carry):
            for j in range(GATHER_UNROLL):
                token_copy(idx_smem, batch, buf, first + i * GATHER_UNROLL + j).start()
            return carry
        lax.fori_loop(0, count // GATHER_UNROLL, body, 0)

    @pl.when((step == 0) & (f == 0))
    def _():
        request(idx_ref, b, cur, 0, cap)

    next_batch = jnp.where(step + 1 < n_steps, step + 1, 0) // n_e
    per_block = cap // (n_ff * n_blocks)

    @pl.when(f == 0)
    def _():
        pltpu.make_async_copy(h2_hbm.at[pl.ds(0, cap * SUBLANES)], xbuf.at[cur], sem.at[cur]).wait()
        for m in range(n_blocks):
            first = m * SLOT_BLOCK * SUBLANES
            xm = jnp.concatenate(
                [xbuf[cur, pl.ds(first + j, SLOT_BLOCK, stride=SUBLANES), :] for j in range(d // LANES)],
                axis=1)
            xb_scr[m] = xm.astype(BF16)
        acc_scr[...] = jnp.zeros_like(acc_scr)

    wg = wg_ref[0].astype(BF16)
    wu = wu_ref[0].astype(BF16)
    wd = wd_ref[0].astype(BF16)
    for m in range(n_blocks):
        first = (f * n_blocks + m) * per_block
        for j in range(per_block):
            token_copy(idx_next_ref, next_batch, 1 - cur, first + j).start()
        xm = xb_scr[m]
        g = jnp.dot(xm, wg, preferred_element_type=F32)
        u = jnp.dot(xm, wu, preferred_element_type=F32)
        act = (jax.nn.silu(g) * u).astype(BF16)
        acc_scr[m] += jnp.dot(act, wd, preferred_element_type=F32)

    @pl.when((step == n_steps - 1) & (f == n_ff - 1))
    def _():
        pltpu.make_async_copy(h2_hbm.at[pl.ds(0, cap * SUBLANES)], xbuf.at[1 - cur],
                              sem.at[1 - cur]).wait()

    @pl.when(f == n_ff - 1)
    def _():
        eye =(lax.broadcasted_iota(jnp.int32, (SLOT_BLOCK, SLOT_BLOCK), 0)
               == lax.broadcasted_iota(jnp.int32, (SLOT_BLOCK, SLOT_BLOCK), 1))
        for m in range(n_blocks):
            gcol = jnp.sum(jnp.where(eye, gs_ref[0, m], 0.0), axis=1, keepdims=True)
            rows = pl.ds(m * SLOT_BLOCK, SLOT_BLOCK)
            eo_ref[0, 0, rows, :] = (gcol * acc_scr[m]).astype(eo_ref.dtype)


def _moe(idx, gs, h2_tiles, wg, wu, wd, bsz, s):
    n_e, d, d_ff = wg.shape
    n_blocks = gs.shape[1]
    cap = n_blocks * SLOT_BLOCK
    n_ff = d_ff // FF_CHUNK
    assert d_ff % FF_CHUNK == 0 and cap % (n_ff * n_blocks) == 0 and cap % GATHER_UNROLL == 0
    assert d == SUBLANES * LANES

    def next_step(b, e, f):
        return ((b * n_e + e + 1) % (bsz * n_e), 0, 0)

    smem = pltpu.SMEM
    return pl.pallas_call(
        functools.partial(_moe_kernel, seq_len=s, n_ff=n_ff),
        grid=(bsz, n_e, n_ff),
        in_specs=[
            pl.BlockSpec((1, 1, cap), lambda b, e, f: (b * n_e + e, 0, 0), memory_space=smem),
            pl.BlockSpec((1, 1, cap), next_step, memory_space=smem),
            pl.BlockSpec((1, n_blocks, 1, SLOT_BLOCK), lambda b, e, f: (b * n_e + e, 0, 0, 0)),
            pl.BlockSpec(memory_space=pl.ANY),
            pl.BlockSpec((1, d, FF_CHUNK), lambda b, e, f: (e, 0, f)),
            pl.BlockSpec((1, d, FF_CHUNK), lambda b, e, f: (e, 0, f)),
            pl.BlockSpec((1, FF_CHUNK, d), lambda b, e, f: (e, f, 0)),
        ],
        out_specs=pl.BlockSpec((1, 1, cap, d), lambda b, e, f: (b, e, 0, 0)),
        out_shape=jax.ShapeDtypeStruct((bsz, n_e, cap, d), BF16),
        scratch_shapes=[pltpu.VMEM((2, cap * SUBLANES, LANES), F32),
                        pltpu.SemaphoreType.DMA((2,)),
                        pltpu.VMEM((n_blocks, SLOT_BLOCK, d), BF16),
                        pltpu.VMEM((n_blocks, SLOT_BLOCK, d), F32)],
        compiler_params=_cparams(("arbitrary", "arbitrary", "arbitrary")),
        name="moe",
    )(idx, idx, gs, h2_tiles, wg, wu, wd)


def _combine_kernel(starts_ref, x1_ref, pos_ref, gf_ref, eo_ref, out_ref, wcat_scr, acc_scr,
                    *, cap):
    b = pl.program_id(0)
    k = pl.program_id(1)
    n_e = pos_ref.shape[2]
    pos = pos_ref[0]
    expert_lane = lax.broadcasted_iota(jnp.int32, (1, n_e), 1)

    s0 = [starts_ref[(b * n_e + e) * LANES + k] for e in range(n_e)]
    s1 = [starts_ref[(b * n_e + e) * LANES + k + 1] for e in range(n_e)]

    def window_starts(rows):
        starts = [jnp.minimum(s // WIN_ALIGN, (cap - rows) // WIN_ALIGN) * WIN_ALIGN for s in s0]
        vec = jnp.zeros((1, n_e), jnp.int32)
        for e in range(n_e):
            vec = jnp.where(expert_lane == e, starts[e], vec)
        return starts, vec

    def onehot(rel, first, count, width):
        n_lanes = count * width
        lane = lax.broadcasted_iota(jnp.int32, (n_e, n_lanes), 1)
        sub = lax.broadcasted_iota(jnp.int32, (n_e, n_lanes), 0)
        expand = jnp.where(sub == first + lane // width, 1.0, 0.0).astype(BF16)
        spread = jnp.dot(jnp.clip(rel, -1, width).astype(F32).astype(BF16), expand,
                         preferred_element_type=F32)
        want = (lax.broadcasted_iota(jnp.int32, (TOKEN_TILE, n_lanes), 1) % width).astype(F32)
        return jnp.where(spread == want, 1.0, 0.0).astype(BF16)

    def finish(acc):
        out_ref[0] = _rms_scale(acc) * gf_ref[...]

    narrow, narrow_vec = window_starts(NARROW_WIN)
    fits = None
    for e in range(n_e):
        ok = s1[e] - narrow[e] <= NARROW_WIN
        fits = ok if fits is None else fits & ok

    @pl.when(fits)
    def _():
        for e in range(n_e):
            src = pl.ds(pl.multiple_of(narrow[e], WIN_ALIGN), NARROW_WIN)
            wcat_scr[pl.ds(e * NARROW_WIN, NARROW_WIN), :] = eo_ref[0, e, src, :]
        hit = onehot(pos - narrow_vec, 0, n_e, NARROW_WIN)
        finish(x1_ref[0] + jnp.dot(hit, wcat_scr[pl.ds(0, n_e * NARROW_WIN), :],
                                   preferred_element_type=F32))

    @pl.when(jnp.logical_not(fits))
    def _():
        wide, wide_vec = window_starts(SLOT_BLOCK + WIN_ALIGN)
        rel = pos - wide_vec
        group = wcat_scr.shape[0] // SLOT_BLOCK
        acc_scr[...] = x1_ref[0]
        for g in range(n_e // group):
            for i in range(group):
                e = g * group + i
                src = pl.ds(pl.multiple_of(wide[e], WIN_ALIGN), SLOT_BLOCK)
                wcat_scr[pl.ds(i * SLOT_BLOCK, SLOT_BLOCK), :] = eo_ref[0, e, src, :]
            acc_scr[...] += jnp.dot(onehot(rel, g * group, group, SLOT_BLOCK), wcat_scr[...],
                                    preferred_element_type=F32)
        for e in range(n_e):
            src = pl.ds(pl.multiple_of(wide[e] + SLOT_BLOCK, WIN_ALIGN), WIN_ALIGN)
            wcat_scr[pl.ds(e * WIN_ALIGN, WIN_ALIGN), :] = eo_ref[0, e, src, :]
        tail = onehot(rel - SLOT_BLOCK, 0, n_e, WIN_ALIGN)
        finish(acc_scr[...] + jnp.dot(tail, wcat_scr[pl.ds(0, n_e * WIN_ALIGN), :],
                                      preferred_element_type=F32))


def _combine(starts_flat, x1, pos_c, eo, gf):
    bsz, s, d = x1.shape
    n_e = pos_c.shape[-1]
    cap = eo.shape[2]
    n_k = s // TOKEN_TILE
    wide_group = 4
    assert n_e % wide_group == 0 and n_e * NARROW_WIN <= wide_group * SLOT_BLOCK
    grid_spec = pltpu.PrefetchScalarGridSpec(
        num_scalar_prefetch=1,
        grid=(bsz, n_k),
        in_specs=[
            pl.BlockSpec((1, TOKEN_TILE, d), lambda b, k, st: (b, k, 0)),
            pl.BlockSpec((1, TOKEN_TILE, n_e), lambda b, k, st: (b, k, 0)),
            pl.BlockSpec((1, d), lambda b, k, st: (0, 0)),
            pl.BlockSpec((1, n_e, cap, d), lambda b, k, st: (b, 0, 0, 0),
                         pipeline_mode=pl.Buffered(1)),
        ],
        out_specs=pl.BlockSpec((1, TOKEN_TILE, d), lambda b, k, st: (b, k, 0)),
        scratch_shapes=[pltpu.VMEM((wide_group * SLOT_BLOCK, d), BF16),
                        pltpu.VMEM((TOKEN_TILE, d), F32)],
    )
    return pl.pallas_call(
        functools.partial(_combine_kernel, cap=cap),
        grid_spec=grid_spec,
        out_shape=jax.ShapeDtypeStruct((bsz, s, d), F32),
        compiler_params=_cparams(("parallel", "parallel")),
        name="combine",
    )(starts_flat, x1, pos_c, gf, eo)


def _block_diag(w):
    h, hd, _ = w.shape
    eye = jnp.eye(h, dtype=w.dtype)
    return (eye[:, None, :, None] * w[:, :, None, :]).reshape(h * hd, h * hd)


def _lru_params(wa_f, wx_f, wa_b, wx_b, ba_f, bx_f, ba_b, bx_b, lam_f, lam_b):
    d_lru = ba_f.shape[0]
    n_blk = d_lru // LANES
    mats = [_block_diag(w) for w in (wa_f, wx_f, wa_b, wx_b)]
    w_cat = jnp.stack([
        jnp.concatenate([m[c * LANES:(c + 1) * LANES, c * LANES:(c + 1) * LANES] for m in mats], axis=1)
        for c in range(n_blk)]).astype(BF16)
    b_cat = jnp.stack([
        jnp.concatenate([v[c * LANES:(c + 1) * LANES] for v in (ba_f, bx_f, ba_b, bx_b)])
        for c in range(n_blk)])[:, None, :]
    lam_cat = jnp.stack([
        jnp.concatenate([v[c * LANES:(c + 1) * LANES] for v in (lam_f, lam_b)])
        for c in range(n_blk)])[:, None, :]
    return w_cat, b_cat, lam_cat


def kernel(x, norm1_g, w_in, conv_w, conv_b, lru_wa_f, lru_ba_f, lru_wx_f, lru_bx_f, lru_lam_f,
           lru_wa_b, lru_ba_b, lru_wx_b, lru_bx_b, lru_lam_b, w_out, norm2_g, w_router,
           w_gate, w_up, w_down, normf_g):
    bsz, s, d = x.shape
    d_lru = conv_b.shape[0]
    d_four = w_in.shape[1] - 2 * d_lru
    n_e = w_router.shape[1]
    cap = CAPACITY_FACTOR * s // n_e
    assert s % LRU_CHUNK == 0 and s % (SCAN_GROUPS * SUBLANES * SUBLANES) == 0
    assert s % FOURIER_N2 == 0 and cap % SLOT_BLOCK == 0 and s // TOKEN_TILE < LANES
    assert cap >= WIN_ROWS and w_gate.shape[2] % FF_CHUNK == 0

    lx, lg, fo = _inproj(x.reshape(bsz * s, d), norm1_g[None, :], w_in.astype(BF16), d_lru, d_four)
    lx = lx.reshape(bsz, s, d_lru)
    lg = lg.reshape(bsz, s, d_lru)
    fo = fo.reshape(bsz, s, d_four)

    w_cat, b_cat, lam_cat = _lru_params(lru_wa_f, lru_wx_f, lru_wa_b, lru_wx_b,
                                        lru_ba_f, lru_bx_f, lru_ba_b, lru_bx_b,
                                        lru_lam_f, lru_lam_b)
    y_lru = _lru(lx, lg, conv_w, conv_b[None, :], w_cat, b_cat, lam_cat)
    y_four = _fourier(fo, FOURIER_GROUPS)

    w_out_bf = w_out.astype(BF16)
    x1, h2_tiles, aff_t = _outproj(x, y_lru, y_four, w_out_bf[:d_lru], w_out_bf[d_lru:],
                                   norm2_g[None, :], w_router.T)

    pos_r, starts = _select(aff_t, cap)
    starts_flat = starts.reshape(-1)
    pos_c = jnp.transpose(pos_r, (0, 2, 1))
    pos5 = pos_r.reshape(bsz, n_e, s // TOKEN_TILE, 1, TOKEN_TILE)
    idx, gs = _compact(starts_flat, pos5, _token_table(aff_t), cap)
    eo = _moe(idx.reshape(bsz * n_e, 1, cap), gs.reshape((bsz * n_e,) + gs.shape[2:]), h2_tiles,
              w_gate, w_up, w_down, bsz, s)
    return _combine(starts_flat, x1, pos_c, eo, normf_g[None, :])
```

```python
import functools
import math

import jax
import jax.numpy as jnp
import numpy as np
from jax import lax
from jax.experimental import pallas as pl
from jax.experimental.pallas import tpu as pltpu

F32 = jnp.float32
BF16 = jnp.bfloat16

EPS = 1e-6
LRU_C = 8.0
LRU_HEADS = 8
N_EXPERTS = 16
CAPACITY_FACTOR = 2
FOURIER_GROUPS = 4

SUBLANES = 8
LANES = 128
MXU_DIM = 256
VMEM_LIMIT_BYTES = 56 * 1024 * 1024

ROW_TILE = 512
LRU_CHUNK = 512
SCAN_GROUPS = 4
SLOT_BLOCK = MXU_DIM
TOKEN_TILE = MXU_DIM
SEG_PAD = SUBLANES
FOURIER_N2 = LANES
FOURIER_K1_BATCH = 8
FF_CHUNK = 1024
GATHER_UNROLL = 8
COMPACT_WIN = 64
TOKEN_ID_BASE = 64
GATE_COL0 = 16
WIN_ALIGN = 16
WIN_ROWS = SLOT_BLOCK + WIN_ALIGN
NARROW_WIN = 64


def _cparams(semantics):
    return pltpu.CompilerParams(dimension_semantics=semantics,
                                vmem_limit_bytes=VMEM_LIMIT_BYTES)


def _rms_scale(x):
    return x * lax.rsqrt(jnp.mean(x * x, axis=-1, keepdims=True) + EPS)


def _inproj_kernel(x_ref, g_ref, w_ref, lx_ref, lg_ref, fo_ref):
    h = _rms_scale(x_ref[...]) * g_ref[...]
    p = jnp.dot(h.astype(BF16), w_ref[...], preferred_element_type=F32)
    d = lx_ref.shape[-1]
    lx_ref[...] = p[:, :d]
    lg_ref[...] = p[:, d:2 * d]
    fo_ref[...] = p[:, 2 * d:]


def _inproj(x2, g, w_bf, d_lru, d_four):
    m, d = x2.shape
    n = w_bf.shape[1]
    return pl.pallas_call(
        _inproj_kernel,
        grid=(m // ROW_TILE,),
        in_specs=[
            pl.BlockSpec((ROW_TILE, d), lambda i: (i, 0)),
            pl.BlockSpec((1, d), lambda i: (0, 0)),
            pl.BlockSpec((d, n), lambda i: (0, 0)),
        ],
        out_specs=[
            pl.BlockSpec((ROW_TILE, d_lru), lambda i: (i, 0)),
            pl.BlockSpec((ROW_TILE, d_lru), lambda i: (i, 0)),
            pl.BlockSpec((ROW_TILE, d_four), lambda i: (i, 0)),
        ],
        out_shape=[
            jax.ShapeDtypeStruct((m, d_lru), F32),
            jax.ShapeDtypeStruct((m, d_lru), F32),
            jax.ShapeDtypeStruct((m, d_four), F32),
        ],
        compiler_params=_cparams(("parallel",)),
        name="inproj",
    )(x2, g, w_bf)


def _shift_rows(x, shift):
    n = x.shape[0]
    rows = lax.broadcasted_iota(jnp.int32, x.shape, 0)
    rolled = pltpu.roll(x, shift % n, axis=0)
    keep = (rows >= shift) if shift > 0 else (rows < n + shift)
    return jnp.where(keep, rolled, 0.0)


def _lru_kernel(lx_ref, lg_ref, cw_ref, cb_ref, w_ref, b_ref, lam_ref, y_ref,
                xpad, af, bf, ab, bb):
    s = lx_ref.shape[1]
    c_blk = lx_ref.shape[2]
    n_seg = SCAN_GROUPS * SUBLANES
    seg = s // n_seg
    seg_stride = seg + SEG_PAD
    piece = min(seg, LRU_CHUNK)
    n_chunks = s // LRU_CHUNK
    pad = SUBLANES

    zeros_pad = jnp.zeros((pad, c_blk), F32)
    xpad[pl.ds(0, pad), :] = zeros_pad
    xpad[pl.ds(pad + s, pad), :] = zeros_pad

    def copy_chunk(i, carry):
        t0 = pl.multiple_of(i * LRU_CHUNK, LRU_CHUNK)
        xpad[pl.ds(pad + t0, LRU_CHUNK), :] = lx_ref[0, pl.ds(t0, LRU_CHUNK), :]
        return carry

    lax.fori_loop(0, n_chunks, copy_chunk, 0)

    cw = cw_ref[...]
    conv_width = cw.shape[0]
    cb = cb_ref[...]
    bias = b_ref[0]
    quarter_c_log_sig = (0.25 * LRU_C) * jax.nn.log_sigmoid(lam_ref[0])
    w_cat = w_ref[0]

    def gates_chunk(i, carry):
        t0 = pl.multiple_of(i * LRU_CHUNK, LRU_CHUNK)
        win = xpad[pl.ds(t0, LRU_CHUNK + 2 * pad), :]
        c = cb
        left = conv_width // 2
        for k in range(conv_width):
            c = c + cw[k:k + 1, :] * win[pad - left + k:pad - left + k + LRU_CHUNK, :]
        z = jnp.dot(c.astype(BF16), w_cat, preferred_element_type=F32) + bias
        for d, (a_scr, b_scr) in enumerate(((af, bf), (ab, bb))):
            q = quarter_c_log_sig[:, d * c_blk:(d + 1) * c_blk]
            th_r = jnp.tanh(0.5 * z[:, (2 * d) * c_blk:(2 * d + 1) * c_blk])
            th_i = jnp.tanh(0.5 * z[:, (2 * d + 1) * c_blk:(2 * d + 2) * c_blk])
            t = jnp.tanh(q + q * th_r)
            inv = 1.0 / (1.0 - t)
            a = (1.0 + t) * inv
            bt = (jnp.sqrt(-t) * inv) * (c + c * th_i)
            for p in range(LRU_CHUNK // piece):
                t = t0 + p * piece
                dst = pl.ds(pl.multiple_of((t // seg) * seg_stride + t % seg, SUBLANES), piece)
                a_scr[dst, :] = a[p * piece:(p + 1) * piece]
                b_scr[dst, :] = bt[p * piece:(p + 1) * piece]
        return carry

    lax.fori_loop(0, n_chunks, gates_chunk, 0)

    def scan_step(i, carry):
        out = []
        for (a_scr, b_scr, o), (hs, ps) in zip(((af, bf, i), (ab, bb, seg - 1 - i)), carry):
            new_h, new_p = [], []
            for q in range(SCAN_GROUPS):
                rows = pl.ds(q * SUBLANES * seg_stride + o, SUBLANES, stride=seg_stride)
                a = a_scr[rows, :]
                h = a * hs[q] + b_scr[rows, :]
                p = a * ps[q]
                b_scr[rows, :] = h
                a_scr[rows, :] = p
                new_h.append(h)
                new_p.append(p)
            out.append((tuple(new_h), tuple(new_p)))
        return tuple(out)

    zero = jnp.zeros((SUBLANES, c_blk), F32)
    one = jnp.ones((SUBLANES, c_blk), F32)
    init = ((zero,) * SCAN_GROUPS, (one,) * SCAN_GROUPS)
    (hf_end, pf_end), (hb_end, pb_end) = lax.fori_loop(0, seg, scan_step, (init, init))

    row = lax.broadcasted_iota(jnp.int32, (SUBLANES, c_blk), 0)

    def entering(h_end, p_end, carry_in, first_row, shift):
        c = zero
        for _ in range(SUBLANES):
            c = jnp.where(row == first_row, carry_in, _shift_rows(h_end + p_end * c, shift))
        return c

    cf = [None] * SCAN_GROUPS
    cbk = [None] * SCAN_GROUPS
    carry_f = jnp.zeros((1, c_blk), F32)
    carry_b = jnp.zeros((1, c_blk), F32)
    for q in range(SCAN_GROUPS):
        cf[q] = entering(hf_end[q], pf_end[q], carry_f, 0, 1)
        carry_f = (hf_end[q] + pf_end[q] * cf[q])[SUBLANES - 1:SUBLANES, :]
        qb = SCAN_GROUPS - 1 - q
        cbk[qb] = entering(hb_end[qb], pb_end[qb], carry_b, SUBLANES - 1, -1)
        carry_b = (hb_end[qb] + pb_end[qb] * cbk[qb])[0:1, :]

    for j in range(n_seg):
        q, i = divmod(j, SUBLANES)
        src = pl.ds(j * seg_stride, seg)
        rows = pl.ds(j * seg, seg)
        hsum = ((bf[src, :] + af[src, :] * cf[q][i:i + 1, :])
                + (bb[src, :] + ab[src, :] * cbk[q][i:i + 1, :]))
        y_ref[0, rows, :] = (jax.nn.gelu(lg_ref[0, rows, :]) * hsum).astype(y_ref.dtype)


def _lru(lx, lg, conv_w, conv_b, w_cat, b_cat, lam_cat):
    bsz, s, d_lru = lx.shape
    c_blk = LANES
    n_blk = d_lru // c_blk
    scr = pltpu.VMEM((s + SCAN_GROUPS * SUBLANES * SEG_PAD, c_blk), F32)
    return pl.pallas_call(
        _lru_kernel,
        grid=(bsz, n_blk),
        in_specs=[
            pl.BlockSpec((1, s, c_blk), lambda b, c: (b, 0, c)),
            pl.BlockSpec((1, s, c_blk), lambda b, c: (b, 0, c)),
            pl.BlockSpec((4, c_blk), lambda b, c: (0, c)),
            pl.BlockSpec((1, c_blk), lambda b, c: (0, c)),
            pl.BlockSpec((1, c_blk, 4 * c_blk), lambda b, c: (c, 0, 0)),
            pl.BlockSpec((1, 1, 4 * c_blk), lambda b, c: (c, 0, 0)),
            pl.BlockSpec((1, 1, 2 * c_blk), lambda b, c: (c, 0, 0)),
        ],
        out_specs=pl.BlockSpec((1, s, c_blk), lambda b, c: (b, 0, c)),
        out_shape=jax.ShapeDtypeStruct((bsz, s, d_lru), BF16),
        scratch_shapes=[pltpu.VMEM((s + 2 * SUBLANES, c_blk), F32), scr, scr, scr, scr],
        compiler_params=_cparams(("parallel", "parallel")),
        name="lru",
    )(lx, lg, conv_w, conv_b, w_cat, b_cat, lam_cat)


def _fourier_kernel(f_ref, ka_ref, kb_ref, cc_ref, sc_ref, twc_ref, tws_ref, y_ref,
                    ar_scr, ai_scr, *, scale):
    n1 = f_ref.shape[1]
    n_u = f_ref.shape[2]
    cols = f_ref.shape[4]
    r = SUBLANES * n1
    n2 = n_u * SUBLANES

    def stage_a(i, carry):
        us = (2 * i, 2 * i + 1)
        xu = jnp.concatenate([f_ref[0, :, u].reshape(r, cols) for u in us], axis=1)
        a = jnp.dot(ka_ref[...], xu.astype(BF16), preferred_element_type=F32)
        for j, u in enumerate(us):
            a_re = a[:r, j * cols:(j + 1) * cols]
            a_im = a[r:, j * cols:(j + 1) * cols]
            c = twc_ref[u]
            sn = tws_ref[u]
            ar_scr[u] = a_re * c - a_im * sn
            ai_scr[u] = a_re * sn + a_im * c
        return carry

    lax.fori_loop(0, n_u // 2, stage_a, 0)

    nb = FOURIER_K1_BATCH

    def stage_b(i, carry):
        rows = pl.ds(pl.multiple_of(i * (nb * SUBLANES), nb * SUBLANES), nb * SUBLANES)
        a_re = ar_scr[:, rows, :]
        a_im = ai_scr[:, rows, :]

        def rows_s2(a, j):
            return a[:, j * SUBLANES:(j + 1) * SUBLANES, :].reshape(n2, cols)

        st = jnp.concatenate(
            [jnp.concatenate([rows_s2(a_re, j), rows_s2(a_im, j)], axis=0) for j in range(nb)],
            axis=1).astype(BF16)
        x = jnp.dot(kb_ref[...], st, preferred_element_type=F32)
        x_re = jnp.concatenate([x[:n2, j * cols:(j + 1) * cols] for j in range(nb)], axis=0)
        x_im = jnp.concatenate([x[n2:, j * cols:(j + 1) * cols] for j in range(nb)], axis=0)
        y = (jnp.dot(x_re.astype(BF16), cc_ref[...], preferred_element_type=F32)
             + jnp.dot(x_im.astype(BF16), sc_ref[...], preferred_element_type=F32)) * scale
        for j in range(nb):
            y_ref[0, pl.ds(i * nb + j, n2, stride=n1), :] = y[j * n2:(j + 1) * n2]
        return carry

    lax.fori_loop(0, n1 // nb, stage_b, 0)


def _dft_tables(s, c_grp):
    n2 = FOURIER_N2
    n1 = s // n2
    n_u = n2 // SUBLANES

    def cos_sin(num, den):
        ang = (2.0 * np.pi / den) * (num % den).astype(np.float64)
        return np.cos(ang), np.sin(ang)

    def const(a, dtype=F32):
        return jnp.asarray(a.astype(np.float32)).astype(dtype)

    i1 = np.arange(n1, dtype=np.int64)
    c1, s1 = cos_sin(i1[:, None] * i1[None, :], n1)
    eye8 = np.eye(SUBLANES)
    ka = const(np.concatenate([np.kron(c1, eye8), np.kron(-s1, eye8)], axis=0), BF16)

    i2 = np.arange(n2, dtype=np.int64)
    c2, s2 = cos_sin(i2[:, None] * i2[None, :], n2)
    kb = const(np.block([[c2, s2], [-s2, c2]]), BF16)

    ic = np.arange(c_grp, dtype=np.int64)
    cc, sc = cos_sin(ic[:, None] * ic[None, :], c_grp)

    u = np.arange(n_u, dtype=np.int64)[:, None, None]
    k1 = np.arange(n1, dtype=np.int64)[None, :, None]
    v = np.arange(SUBLANES, dtype=np.int64)[None, None, :]
    tc, ts = cos_sin(k1 * (SUBLANES * u + v), s)
    shape = (n_u, n1 * SUBLANES, c_grp)
    twc = jnp.broadcast_to(const(tc.reshape(n_u, -1, 1)), shape)
    tws = jnp.broadcast_to(const(-ts.reshape(n_u, -1, 1)), shape)
    return ka, kb, const(cc, BF16), const(sc, BF16), twc, tws


def _fourier(four, n_groups):
    bsz, s, d_four = four.shape
    c_grp = d_four // n_groups
    n2 = FOURIER_N2
    n1 = s // n2
    n_u = n2 // SUBLANES
    r = SUBLANES * n1
    ka, kb, cc, sc, twc, tws = _dft_tables(s, c_grp)
    f5 = four.reshape(bsz, n1, n_u, SUBLANES, d_four)
    scale = 1.0 / math.sqrt(s * c_grp)
    const2 = lambda b, g: (0, 0)
    const3 = lambda b, g: (0, 0, 0)
    return pl.pallas_call(
        functools.partial(_fourier_kernel, scale=scale),
        grid=(bsz, n_groups),
        in_specs=[
            pl.BlockSpec((1, n1, n_u, SUBLANES, c_grp), lambda b, g: (b, 0, 0, 0, g)),
            pl.BlockSpec(ka.shape, const2),
            pl.BlockSpec(kb.shape, const2),
            pl.BlockSpec(cc.shape, const2),
            pl.BlockSpec(sc.shape, const2),
            pl.BlockSpec(twc.shape, const3),
            pl.BlockSpec(tws.shape, const3),
        ],
        out_specs=pl.BlockSpec((1, s, c_grp), lambda b, g: (b, 0, g)),
        out_shape=jax.ShapeDtypeStruct((bsz, s, d_four), F32),
        scratch_shapes=[pltpu.VMEM((n_u, r, c_grp), F32), pltpu.VMEM((n_u, r, c_grp), F32)],
        compiler_params=_cparams(("parallel", "parallel")),
        name="fourier",
    )(f5, ka, kb, cc, sc, twc, tws)


def _outproj_kernel(x_ref, yl_ref, yf_ref, wo1_ref, wo2_ref, g2_ref, wr_ref,
                    x1_ref, h2_ref, aff_ref):
    x1 = (x_ref[0]
          + jnp.dot(yl_ref[0], wo1_ref[...], preferred_element_type=F32)
          + jnp.dot(yf_ref[0].astype(BF16), wo2_ref[...], preferred_element_type=F32))
    x1_ref[0] = x1
    h2 = _rms_scale(x1) * g2_ref[...]
    for j in range(h2.shape[1] // LANES):
        h2_ref[pl.ds(j, h2.shape[0], stride=SUBLANES), :] = h2[:, j * LANES:(j + 1) * LANES]
    logits = lax.dot_general(wr_ref[...], h2, (((1,), (1,)), ((), ())),
                             precision=lax.Precision.HIGHEST,
                             preferred_element_type=F32)
    ex = jnp.exp(logits - jnp.max(logits, axis=0, keepdims=True))
    aff_ref[0] = ex / jnp.sum(ex, axis=0, keepdims=True)


def _outproj(x, y_lru, y_four, wo1, wo2, g2, wr_t):
    bsz, s, d = x.shape
    d_lru = y_lru.shape[-1]
    d_four = y_four.shape[-1]
    n_e = wr_t.shape[0]
    assert d == SUBLANES * LANES
    n_i = s // ROW_TILE
    tile = lambda w: pl.BlockSpec((1, ROW_TILE, w), lambda b, i: (b, i, 0))
    const = lambda shape: pl.BlockSpec(shape, lambda b, i: (0, 0))
    return pl.pallas_call(
        _outproj_kernel,
        grid=(bsz, n_i),
        in_specs=[tile(d), tile(d_lru), tile(d_four), const(wo1.shape), const(wo2.shape),
                  const(g2.shape), const(wr_t.shape)],
        out_specs=[tile(d),
                   pl.BlockSpec((ROW_TILE * SUBLANES, LANES), lambda b, i: (b * n_i + i, 0)),
                   pl.BlockSpec((1, n_e, ROW_TILE), lambda b, i: (b, 0, i))],
        out_shape=[
            jax.ShapeDtypeStruct((bsz, s, d), F32),
            jax.ShapeDtypeStruct((bsz * s * SUBLANES, LANES), F32),
            jax.ShapeDtypeStruct((bsz, n_e, s), F32),
        ],
        compiler_params=_cparams(("parallel", "parallel")),
        name="outproj",
    )(x, y_lru, y_four, wo1, wo2, g2, wr_t)


def _select_kernel(aff_ref, tri_ref, pos_ref, starts_ref, *, cap):
    v = aff_ref[0]
    n_e, s = v.shape
    cap_f = float(cap)

    def midpoint(lo, hi):
        mid = 0.5 * (lo + hi)
        return mid, (mid > lo) & (mid < hi)

    def cond(carry):
        _, active = midpoint(*carry)
        return jnp.max(active.astype(F32)) > 0.0

    def body(carry):
        lo, hi = carry
        mid, active = midpoint(lo, hi)
        cnt = jnp.sum((v >= mid).astype(F32), axis=1, keepdims=True)
        enough = cnt >= cap_f
        return (jnp.where(active & enough, mid, lo), jnp.where(active & (~enough), mid, hi))

    lo0 = jnp.zeros((n_e, 1), F32)
    hi0 = jnp.full((n_e, 1), 2.0, F32)
    thr, _ = lax.while_loop(cond, body, (lo0, hi0))

    above = v > thr
    tie = v == thr
    need = cap_f - jnp.sum(above.astype(F32), axis=1, keepdims=True)

    n_tiles = s // TOKEN_TILE
    tri = tri_ref[...]
    lane = lax.broadcasted_iota(jnp.int32, (n_e, LANES), 1)

    def prefix(mask_f, want_starts):
        run = jnp.zeros((n_e, 1), F32)
        starts = jnp.zeros((n_e, LANES), F32)
        pieces = []
        for t in range(n_tiles):
            m = mask_f[:, t * TOKEN_TILE:(t + 1) * TOKEN_TILE]
            incl = jnp.dot(m.astype(BF16), tri, preferred_element_type=F32)
            pieces.append(run + incl - m)
            if want_starts:
                starts = jnp.where(lane == t, run, starts)
            run = run + incl[:, TOKEN_TILE - 1:TOKEN_TILE]
        if want_starts:
            starts = jnp.where(lane == n_tiles, run, starts)
        return jnp.concatenate(pieces, axis=1), starts

    tie_rank, _ = prefix(tie.astype(F32), False)
    sel = above | (tie & (tie_rank < need))
    pos, starts = prefix(sel.astype(F32), True)
    pos_ref[0] = jnp.where(sel, pos, -1.0).astype(jnp.int32)
    starts_ref[0] = starts.astype(jnp.int32)


def _select(aff_t, cap):
    bsz, n_e, s = aff_t.shape
    idx = jnp.arange(TOKEN_TILE, dtype=jnp.int32)
    tri = (idx[:, None] <= idx[None, :]).astype(BF16)
    return pl.pallas_call(
        functools.partial(_select_kernel, cap=cap),
        grid=(bsz,),
        in_specs=[
            pl.BlockSpec((1, n_e, s), lambda b: (b, 0, 0)),
            pl.BlockSpec(tri.shape, lambda b: (0, 0)),
        ],
        out_specs=[
            pl.BlockSpec((1, n_e, s), lambda b: (b, 0, 0)),
            pl.BlockSpec((1, n_e, LANES), lambda b: (b, 0, 0)),
        ],
        out_shape=[
            jax.ShapeDtypeStruct((bsz, n_e, s), jnp.int32),
            jax.ShapeDtypeStruct((bsz, n_e, LANES), jnp.int32),
        ],
        compiler_params=_cparams(("parallel",)),
        name="select",
    )(aff_t, tri)


def _token_table(gate, first_token):
    n_tok, n_e = gate.shape
    lane = lax.broadcasted_iota(jnp.int32, (n_tok, LANES), 1)
    tok = first_token + lax.broadcasted_iota(jnp.int32, (n_tok, LANES), 0)
    table = jnp.where(lane == 0, tok // TOKEN_ID_BASE, jnp.where(lane == 1, tok % TOKEN_ID_BASE, 0))
    table = table.astype(F32)
    place_lane = lax.broadcasted_iota(jnp.int32, (n_e, LANES), 1)
    place_sub = lax.broadcasted_iota(jnp.int32, (n_e, LANES), 0)
    rest = gate
    for j in range(3):
        piece = rest.astype(BF16)
        rest = rest - piece.astype(F32)
        place = jnp.where(place_lane == GATE_COL0 + j * n_e + place_sub, 1.0, 0.0).astype(BF16)
        table = table + jnp.dot(piece, place, preferred_element_type=F32)
    return table.astype(BF16)


def _compact_kernel(starts_ref, pos_ref, gate_ref, idx_ref, gs_ref, r_scr, *, cap):
    b = pl.program_id(0)
    n_e = pos_ref.shape[1]
    n_k = pos_ref.shape[2]
    n_blocks = cap // SLOT_BLOCK
    wide_windows = -(-(TOKEN_TILE + SUBLANES) // COMPACT_WIN)
    r_scr[...] = jnp.zeros_like(r_scr)
    sub = lax.broadcasted_iota(jnp.int32, (COMPACT_WIN, TOKEN_TILE), 0)

    def tile(k, carry):
        rows = pl.ds(pl.multiple_of(k * TOKEN_TILE, TOKEN_TILE), TOKEN_TILE)
        vk = _token_table(gate_ref[0, rows, :], k * TOKEN_TILE)
        starts = []
        fits = None
        for e in range(n_e):
            base = (b * n_e + e) * LANES
            ws = (starts_ref[base + k] // SUBLANES) * SUBLANES
            ok = starts_ref[base + k + 1] - ws <= COMPACT_WIN
            fits = ok if fits is None else fits & ok
            starts.append(ws)

        def place(e, ws):
            hit = pos_ref[0, e, k] - ws == sub
            vals = jnp.dot(jnp.where(hit, 1.0, 0.0).astype(BF16), vk, preferred_element_type=F32)
            r_scr[e, pl.ds(pl.multiple_of(ws, SUBLANES), COMPACT_WIN), :] += vals

        @pl.when(fits)
        def _():
            for e in range(n_e):
                place(e, starts[e])

        @pl.when(jnp.logical_not(fits))
        def _():
            for e in range(n_e):
                for j in range(wide_windows):
                    place(e, starts[e] + j * COMPACT_WIN)

        return carry

    lax.fori_loop(0, n_k, tile, 0)

    lane = lax.broadcasted_iota(jnp.int32, (SLOT_BLOCK, LANES), 1)
    eye = (lax.broadcasted_iota(jnp.int32, (SLOT_BLOCK, SLOT_BLOCK), 0)
           == lax.broadcasted_iota(jnp.int32, (SLOT_BLOCK, SLOT_BLOCK), 1))
    id_weight = jnp.where(lane == 0, float(TOKEN_ID_BASE), jnp.where(lane == 1, 1.0, 0.0))

    def to_row(col):
        return jnp.sum(jnp.where(eye, col, 0.0), axis=0, keepdims=True)

    def finish(e, carry):
        gate_lane = (lane >= GATE_COL0) & (lane < GATE_COL0 + 3 * n_e) & ((lane - GATE_COL0) % n_e == e)
        for m in range(n_blocks):
            blk = r_scr[e, pl.ds(m * SLOT_BLOCK, SLOT_BLOCK), :]
            ids = jnp.sum(blk * id_weight, axis=1, keepdims=True)
            gates = jnp.sum(jnp.where(gate_lane, blk, 0.0), axis=1, keepdims=True)
            idx_ref[0, e, m] = to_row(ids).astype(jnp.int32)
            gs_ref[0, e, m] = to_row(gates)
        return carry

    lax.fori_loop(0, n_e, finish, 0)


def _compact(starts_flat, pos5, gate_c, cap):
    bsz, n_e, n_k = pos5.shape[:3]
    s = gate_c.shape[1]
    assert s <= TOKEN_ID_BASE * 256 and GATE_COL0 + 3 * n_e <= LANES
    n_blocks = cap // SLOT_BLOCK
    out_block = (1, n_e, n_blocks, 1, SLOT_BLOCK)
    out_spec = pl.BlockSpec(out_block, lambda b, st: (b, 0, 0, 0, 0))
    wide_rows = -(-(TOKEN_TILE + SUBLANES) // COMPACT_WIN) * COMPACT_WIN
    grid_spec = pltpu.PrefetchScalarGridSpec(
        num_scalar_prefetch=1,
        grid=(bsz,),
        in_specs=[pl.BlockSpec((1, n_e, n_k, 1, TOKEN_TILE), lambda b, st: (b, 0, 0, 0, 0)),
                  pl.BlockSpec((1, s, n_e), lambda b, st: (b, 0, 0))],
        out_specs=[out_spec, out_spec],
        scratch_shapes=[pltpu.VMEM((n_e, cap + wide_rows, LANES), F32)],
    )
    return pl.pallas_call(
        functools.partial(_compact_kernel, cap=cap),
        grid_spec=grid_spec,
        out_shape=[jax.ShapeDtypeStruct((bsz,) + out_block[1:], jnp.int32),
                   jax.ShapeDtypeStruct((bsz,) + out_block[1:], F32)],
        compiler_params=_cparams(("parallel",)),
        name="compact",
    )(starts_flat, pos5, gate_c)


def _moe_kernel(idx_ref, idx_next_ref, gs_ref, h2_hbm, wg_ref, wu_ref, wd_ref, eo_ref,
                xbuf, sem, xb_scr, acc_scr, *, seq_len, n_ff):
    b = pl.program_id(0)
    e = pl.program_id(1)
    f = pl.program_id(2)
    n_e = pl.num_programs(1)
    n_blocks = xb_scr.shape[0]
    cap = n_blocks * SLOT_BLOCK
    d = xb_scr.shape[2]
    step = b * n_e + e
    n_steps = pl.num_programs(0) * n_e
    cur = step % 2

    def token_copy(idx_smem, batch, buf, p):
        tok = idx_smem[0, 0, p]
        src = h2_hbm.at[pl.ds(pl.multiple_of((batch * seq_len + tok) * SUBLANES, SUBLANES), SUBLANES)]
        dst = xbuf.at[buf, pl.ds(pl.multiple_of(p * SUBLANES, SUBLANES), SUBLANES)]
        return pltpu.make_async_copy(src, dst, sem.at[buf])

    def request(idx_smem, batch, buf, first, count):
        def body(i, carry):
            for j in range(GATHER_UNROLL):
                token_copy(idx_smem, batch, buf, first + i * GATHER_UNROLL + j).start()
            return carry
        lax.fori_loop(0, count // GATHER_UNROLL, body, 0)

    @pl.when((step == 0) & (f == 0))
    def _():
        request(idx_ref, b, cur, 0, cap)

    next_batch = jnp.where(step + 1 < n_steps, step + 1, 0) // n_e
    per_block = cap // (n_ff * n_blocks)

    @pl.when(f == 0)
    def _():
        pltpu.make_async_copy(h2_hbm.at[pl.ds(0, cap * SUBLANES)], xbuf.at[cur], sem.at[cur]).wait()
        for m in range(n_blocks):
            first = m * SLOT_BLOCK * SUBLANES
            xm = jnp.concatenate(
                [xbuf[cur, pl.ds(first + j, SLOT_BLOCK, stride=SUBLANES), :] for j in range(d // LANES)],
                axis=1)
            xb_scr[m] = xm.astype(BF16)
        acc_scr[...] = jnp.zeros_like(acc_scr)

    wg = wg_ref[0].astype(BF16)
    wu = wu_ref[0].astype(BF16)
    wd = wd_ref[0].astype(BF16)
    for m in range(n_blocks):
        first = (f * n_blocks + m) * per_block
        for j in range(per_block):
            token_copy(idx_next_ref, next_batch, 1 - cur, first + j).start()
        xm = xb_scr[m]
        g = jnp.dot(xm, wg, preferred_element_type=F32)
        u = jnp.dot(xm, wu, preferred_element_type=F32)
        act = (jax.nn.silu(g) * u).astype(BF16)
        acc_scr[m] += jnp.dot(act, wd, preferred_element_type=F32)

    @pl.when((step == n_steps - 1) & (f == n_ff - 1))
    def _():
        pltpu.make_async_copy(h2_hbm.at[pl.ds(0, cap * SUBLANES)], xbuf.at[1 - cur],
                              sem.at[1 - cur]).wait()

    @pl.when(f == n_ff - 1)
    def _():
        eye =(lax.broadcasted_iota(jnp.int32, (SLOT_BLOCK, SLOT_BLOCK), 0)
               == lax.broadcasted_iota(jnp.int32, (SLOT_BLOCK, SLOT_BLOCK), 1))
        for m in range(n_blocks):
            gcol = jnp.sum(jnp.where(eye, gs_ref[0, m], 0.0), axis=1, keepdims=True)
            rows = pl.ds(m * SLOT_BLOCK, SLOT_BLOCK)
            eo_ref[0, 0, rows, :] = (gcol * acc_scr[m]).astype(eo_ref.dtype)


def _moe(idx, gs, h2_tiles, wg, wu, wd, bsz, s):
    n_e, d, d_ff = wg.shape
    n_blocks = gs.shape[1]
    cap = n_blocks * SLOT_BLOCK
    n_ff = d_ff // FF_CHUNK
    assert d_ff % FF_CHUNK == 0 and cap % (n_ff * n_blocks) == 0 and cap % GATHER_UNROLL == 0
    assert d == SUBLANES * LANES

    def next_step(b, e, f):
        return ((b * n_e + e + 1) % (bsz * n_e), 0, 0)

    smem = pltpu.SMEM
    return pl.pallas_call(
        functools.partial(_moe_kernel, seq_len=s, n_ff=n_ff),
        grid=(bsz, n_e, n_ff),
        in_specs=[
            pl.BlockSpec((1, 1, cap), lambda b, e, f: (b * n_e + e, 0, 0), memory_space=smem),
            pl.BlockSpec((1, 1, cap), next_step, memory_space=smem),
            pl.BlockSpec((1, n_blocks, 1, SLOT_BLOCK), lambda b, e, f: (b * n_e + e, 0, 0, 0)),
            pl.BlockSpec(memory_space=pl.ANY),
            pl.BlockSpec((1, d, FF_CHUNK), lambda b, e, f: (e, 0, f)),
            pl.BlockSpec((1, d, FF_CHUNK), lambda b, e, f: (e, 0, f)),
            pl.BlockSpec((1, FF_CHUNK, d), lambda b, e, f: (e, f, 0)),
        ],
        out_specs=pl.BlockSpec((1, 1, cap, d), lambda b, e, f: (b, e, 0, 0)),
        out_shape=jax.ShapeDtypeStruct((bsz, n_e, cap, d), BF16),
        scratch_shapes=[pltpu.VMEM((2, cap * SUBLANES, LANES), F32),
                        pltpu.SemaphoreType.DMA((2,)),
                        pltpu.VMEM((n_blocks, SLOT_BLOCK, d), BF16),
                        pltpu.VMEM((n_blocks, SLOT_BLOCK, d), F32)],
        compiler_params=_cparams(("arbitrary", "arbitrary", "arbitrary")),
        name="moe",
    )(idx, idx, gs, h2_tiles, wg, wu, wd)


def _combine_kernel(starts_ref, x1_ref, pos_ref, gf_ref, eo_ref, out_ref, wcat_scr, acc_scr,
                    *, cap):
    b = pl.program_id(0)
    k = pl.program_id(1)
    n_e = pos_ref.shape[2]
    pos = pos_ref[0]
    expert_lane = lax.broadcasted_iota(jnp.int32, (1, n_e), 1)

    s0 = [starts_ref[(b * n_e + e) * LANES + k] for e in range(n_e)]
    s1 = [starts_ref[(b * n_e + e) * LANES + k + 1] for e in range(n_e)]

    def window_starts(rows):
        starts = [jnp.minimum(s // WIN_ALIGN, (cap - rows) // WIN_ALIGN) * WIN_ALIGN for s in s0]
        vec = jnp.zeros((1, n_e), jnp.int32)
        for e in range(n_e):
            vec = jnp.where(expert_lane == e, starts[e], vec)
        return starts, vec

    def onehot(rel, first, count, width):
        n_lanes = count * width
        lane = lax.broadcasted_iota(jnp.int32, (n_e, n_lanes), 1)
        sub = lax.broadcasted_iota(jnp.int32, (n_e, n_lanes), 0)
        expand = jnp.where(sub == first + lane // width, 1.0, 0.0).astype(BF16)
        spread = jnp.dot(jnp.clip(rel, -1, width).astype(F32).astype(BF16), expand,
                         preferred_element_type=F32)
        want = (lax.broadcasted_iota(jnp.int32, (TOKEN_TILE, n_lanes), 1) % width).astype(F32)
        return jnp.where(spread == want, 1.0, 0.0).astype(BF16)

    def finish(acc):
        out_ref[0] = _rms_scale(acc) * gf_ref[...]

    narrow, narrow_vec = window_starts(NARROW_WIN)
    fits = None
    for e in range(n_e):
        ok = s1[e] - narrow[e] <= NARROW_WIN
        fits = ok if fits is None else fits & ok

    @pl.when(fits)
    def _():
        for e in range(n_e):
            src = pl.ds(pl.multiple_of(narrow[e], WIN_ALIGN), NARROW_WIN)
            wcat_scr[pl.ds(e * NARROW_WIN, NARROW_WIN), :] = eo_ref[0, e, src, :]
        hit = onehot(pos - narrow_vec, 0, n_e, NARROW_WIN)
        finish(x1_ref[0] + jnp.dot(hit, wcat_scr[pl.ds(0, n_e * NARROW_WIN), :],
                                   preferred_element_type=F32))

    @pl.when(jnp.logical_not(fits))
    def _():
        wide, wide_vec = window_starts(SLOT_BLOCK + WIN_ALIGN)
        rel = pos - wide_vec
        group = wcat_scr.shape[0] // SLOT_BLOCK
        acc_scr[...] = x1_ref[0]
        for g in range(n_e // group):
            for i in range(group):
                e = g * group + i
                src = pl.ds(pl.multiple_of(wide[e], WIN_ALIGN), SLOT_BLOCK)
                wcat_scr[pl.ds(i * SLOT_BLOCK, SLOT_BLOCK), :] = eo_ref[0, e, src, :]
            acc_scr[...] += jnp.dot(onehot(rel, g * group, group, SLOT_BLOCK), wcat_scr[...],
                                    preferred_element_type=F32)
        for e in range(n_e):
            src = pl.ds(pl.multiple_of(wide[e] + SLOT_BLOCK, WIN_ALIGN), WIN_ALIGN)
            wcat_scr[pl.ds(e * WIN_ALIGN, WIN_ALIGN), :] = eo_ref[0, e, src, :]
        tail = onehot(rel - SLOT_BLOCK, 0, n_e, WIN_ALIGN)
        finish(acc_scr[...] + jnp.dot(tail, wcat_scr[pl.ds(0, n_e * WIN_ALIGN), :],
                                      preferred_element_type=F32))


def _combine(starts_flat, x1, pos_c, eo, gf):
    bsz, s, d = x1.shape
    n_e = pos_c.shape[-1]
    cap = eo.shape[2]
    n_k = s // TOKEN_TILE
    wide_group = 4
    assert n_e % wide_group == 0 and n_e * NARROW_WIN <= wide_group * SLOT_BLOCK
    grid_spec = pltpu.PrefetchScalarGridSpec(
        num_scalar_prefetch=1,
        grid=(bsz, n_k),
        in_specs=[
            pl.BlockSpec((1, TOKEN_TILE, d), lambda b, k, st: (b, k, 0)),
            pl.BlockSpec((1, TOKEN_TILE, n_e), lambda b, k, st: (b, k, 0)),
            pl.BlockSpec((1, d), lambda b, k, st: (0, 0)),
            pl.BlockSpec((1, n_e, cap, d), lambda b, k, st: (b, 0, 0, 0),
                         pipeline_mode=pl.Buffered(1)),
        ],
        out_specs=pl.BlockSpec((1, TOKEN_TILE, d), lambda b, k, st: (b, k, 0)),
        scratch_shapes=[pltpu.VMEM((wide_group * SLOT_BLOCK, d), BF16),
                        pltpu.VMEM((TOKEN_TILE, d), F32)],
    )
    return pl.pallas_call(
        functools.partial(_combine_kernel, cap=cap),
        grid_spec=grid_spec,
        out_shape=jax.ShapeDtypeStruct((bsz, s, d), F32),
        compiler_params=_cparams(("parallel", "parallel")),
        name="combine",
    )(starts_flat, x1, pos_c, gf, eo)


def _block_diag(w):
    h, hd, _ = w.shape
    eye = jnp.eye(h, dtype=w.dtype)
    return (eye[:, None, :, None] * w[:, :, None, :]).reshape(h * hd, h * hd)


def _lru_params(wa_f, wx_f, wa_b, wx_b, ba_f, bx_f, ba_b, bx_b, lam_f, lam_b):
    d_lru = ba_f.shape[0]
    n_blk = d_lru // LANES
    mats = [_block_diag(w) for w in (wa_f, wx_f, wa_b, wx_b)]
    w_cat = jnp.stack([
        jnp.concatenate([m[c * LANES:(c + 1) * LANES, c * LANES:(c + 1) * LANES] for m in mats], axis=1)
        for c in range(n_blk)]).astype(BF16)
    b_cat = jnp.stack([
        jnp.concatenate([v[c * LANES:(c + 1) * LANES] for v in (ba_f, bx_f, ba_b, bx_b)])
        for c in range(n_blk)])[:, None, :]
    lam_cat = jnp.stack([
        jnp.concatenate([v[c * LANES:(c + 1) * LANES] for v in (lam_f, lam_b)])
        for c in range(n_blk)])[:, None, :]
    return w_cat, b_cat, lam_cat


def kernel(x, norm1_g, w_in, conv_w, conv_b, lru_wa_f, lru_ba_f, lru_wx_f, lru_bx_f, lru_lam_f,
           lru_wa_b, lru_ba_b, lru_wx_b, lru_bx_b, lru_lam_b, w_out, norm2_g, w_router,
           w_gate, w_up, w_down, normf_g):
    bsz, s, d = x.shape
    d_lru = conv_b.shape[0]
    d_four = w_in.shape[1] - 2 * d_lru
    n_e = w_router.shape[1]
    cap = CAPACITY_FACTOR * s // n_e
    assert s % LRU_CHUNK == 0 and s % (SCAN_GROUPS * SUBLANES * SUBLANES) == 0
    assert s % FOURIER_N2 == 0 and cap % SLOT_BLOCK == 0 and s // TOKEN_TILE < LANES
    assert cap >= WIN_ROWS and w_gate.shape[2] % FF_CHUNK == 0

    lx, lg, fo = _inproj(x.reshape(bsz * s, d), norm1_g[None, :], w_in.astype(BF16), d_lru, d_four)
    lx = lx.reshape(bsz, s, d_lru)
    lg = lg.reshape(bsz, s, d_lru)
    fo = fo.reshape(bsz, s, d_four)

    w_cat, b_cat, lam_cat = _lru_params(lru_wa_f, lru_wx_f, lru_wa_b, lru_wx_b,
                                        lru_ba_f, lru_bx_f, lru_ba_b, lru_bx_b,
                                        lru_lam_f, lru_lam_b)
    y_lru = _lru(lx, lg, conv_w, conv_b[None, :], w_cat, b_cat, lam_cat)
    y_four = _fourier(fo, FOURIER_GROUPS)

    w_out_bf = w_out.astype(BF16)
    x1, h2_tiles, aff_t = _outproj(x, y_lru, y_four, w_out_bf[:d_lru], w_out_bf[d_lru:],
                                   norm2_g[None, :], w_router.T)

    pos_r, starts = _select(aff_t, cap)
    starts_flat = starts.reshape(-1)
    pos_c = jnp.transpose(pos_r, (0, 2, 1))
    pos5 = pos_r.reshape(bsz, n_e, s // TOKEN_TILE, 1, TOKEN_TILE)
    idx, gs = _compact(starts_flat, pos5, jnp.transpose(aff_t, (0, 2, 1)), cap)
    eo = _moe(idx.reshape(bsz * n_e, 1, cap), gs.reshape((bsz * n_e,) + gs.shape[2:]), h2_tiles,
              w_gate, w_up, w_down, bsz, s)
    return _combine(starts_flat, x1, pos_c, eo, normf_g[None, :])
```

```python
import functools
import math

import jax
import jax.numpy as jnp
import numpy as np
from jax import lax
from jax.experimental import pallas as pl
from jax.experimental.pallas import tpu as pltpu

F32 = jnp.float32
BF16 = jnp.bfloat16

EPS = 1e-6
LRU_C = 8.0
LRU_HEADS = 8
N_EXPERTS = 16
CAPACITY_FACTOR = 2
FOURIER_GROUPS = 4

SUBLANES = 8
LANES = 128
MXU_DIM = 256
VMEM_LIMIT_BYTES = 56 * 1024 * 1024

ROW_TILE = 512
LRU_CHUNK = 512
SCAN_GROUPS = 4
SLOT_BLOCK = MXU_DIM
TOKEN_TILE = MXU_DIM
SEG_PAD = 4
FOURIER_N2 = LANES
FOURIER_K1_BATCH = 8
STAGE_PAD = 4
FF_CHUNK = 1024
GATHER_UNROLL = 8
COMPACT_WIN = 64
TOKEN_ID_BASE = 64
GATE_COL0 = 16
WIN_ALIGN = 16
WIN_ROWS = SLOT_BLOCK + WIN_ALIGN
NARROW_WIN = 64


def _cparams(semantics):
    return pltpu.CompilerParams(dimension_semantics=semantics,
                                vmem_limit_bytes=VMEM_LIMIT_BYTES)


def _rms_scale(x):
    return x * lax.rsqrt(jnp.mean(x * x, axis=-1, keepdims=True) + EPS)


def _inproj_kernel(x_ref, g_ref, w_ref, lx_ref, lg_ref, fo_ref):
    h = _rms_scale(x_ref[...]) * g_ref[...]
    p = jnp.dot(h.astype(BF16), w_ref[...], preferred_element_type=F32)
    d = lx_ref.shape[-1]
    lx_ref[...] = p[:, :d]
    lg_ref[...] = p[:, d:2 * d]
    fo_ref[...] = p[:, 2 * d:]


def _inproj(x2, g, w_bf, d_lru, d_four):
    m, d = x2.shape
    n = w_bf.shape[1]
    return pl.pallas_call(
        _inproj_kernel,
        grid=(m // ROW_TILE,),
        in_specs=[
            pl.BlockSpec((ROW_TILE, d), lambda i: (i, 0)),
            pl.BlockSpec((1, d), lambda i: (0, 0)),
            pl.BlockSpec((d, n), lambda i: (0, 0)),
        ],
        out_specs=[
            pl.BlockSpec((ROW_TILE, d_lru), lambda i: (i, 0)),
            pl.BlockSpec((ROW_TILE, d_lru), lambda i: (i, 0)),
            pl.BlockSpec((ROW_TILE, d_four), lambda i: (i, 0)),
        ],
        out_shape=[
            jax.ShapeDtypeStruct((m, d_lru), F32),
            jax.ShapeDtypeStruct((m, d_lru), F32),
            jax.ShapeDtypeStruct((m, d_four), F32),
        ],
        compiler_params=_cparams(("parallel",)),
        name="inproj",
    )(x2, g, w_bf)


def _shift_rows(x, shift):
    n = x.shape[0]
    rows = lax.broadcasted_iota(jnp.int32, x.shape, 0)
    rolled = pltpu.roll(x, shift % n, axis=0)
    keep = (rows >= shift) if shift > 0 else (rows < n + shift)
    return jnp.where(keep, rolled, 0.0)


def _lru_kernel(lx_ref, lg_ref, cw_ref, cb_ref, w_ref, b_ref, lam_ref, y_ref,
                xpad, af, bf, ab, bb):
    s = lx_ref.shape[1]
    c_blk = lx_ref.shape[2]
    n_seg = SCAN_GROUPS * SUBLANES
    seg = s // n_seg
    seg_stride = seg + SEG_PAD
    piece = min(seg, LRU_CHUNK)
    n_chunks = s // LRU_CHUNK
    pad = SUBLANES

    zeros_pad = jnp.zeros((pad, c_blk), F32)
    xpad[pl.ds(0, pad), :] = zeros_pad
    xpad[pl.ds(pad + s, pad), :] = zeros_pad

    def copy_chunk(i, carry):
        t0 = pl.multiple_of(i * LRU_CHUNK, LRU_CHUNK)
        xpad[pl.ds(pad + t0, LRU_CHUNK), :] = lx_ref[0, pl.ds(t0, LRU_CHUNK), :]
        return carry

    lax.fori_loop(0, n_chunks, copy_chunk, 0)

    cw = cw_ref[...]
    conv_width = cw.shape[0]
    cb = cb_ref[...]
    bias = b_ref[0]
    quarter_c_log_sig = (0.25 * LRU_C) * jax.nn.log_sigmoid(lam_ref[0])
    w_cat = w_ref[0]

    def gates_chunk(i, carry):
        t0 = pl.multiple_of(i * LRU_CHUNK, LRU_CHUNK)
        c = cb
        left = conv_width // 2
        for k in range(conv_width):
            c = c + cw[k:k + 1, :] * xpad[pl.ds(t0 + (pad - left + k), LRU_CHUNK), :]
        z = jnp.dot(c.astype(BF16), w_cat, preferred_element_type=F32) + bias
        for d, (a_scr, b_scr) in enumerate(((af, bf), (ab, bb))):
            q = quarter_c_log_sig[:, d * c_blk:(d + 1) * c_blk]
            th_r = jnp.tanh(0.5 * z[:, (2 * d) * c_blk:(2 * d + 1) * c_blk])
            th_i = jnp.tanh(0.5 * z[:, (2 * d + 1) * c_blk:(2 * d + 2) * c_blk])
            t = jnp.tanh(q + q * th_r)
            inv = 1.0 / (1.0 - t)
            a = (1.0 + t) * inv
            bt = (jnp.sqrt(-t) * inv) * (c + c * th_i)
            for p in range(LRU_CHUNK // piece):
                t = t0 + p * piece
                dst = pl.ds((t // seg) * seg_stride + t % seg, piece)
                a_scr[dst, :] = a[p * piece:(p + 1) * piece]
                b_scr[dst, :] = bt[p * piece:(p + 1) * piece]
        return carry

    lax.fori_loop(0, n_chunks, gates_chunk, 0)

    def scan_step(i, carry):
        out = []
        for (a_scr, b_scr, o), (hs, ps) in zip(((af, bf, i), (ab, bb, seg - 1 - i)), carry):
            new_h, new_p = [], []
            for q in range(SCAN_GROUPS):
                rows = pl.ds(q * SUBLANES * seg_stride + o, SUBLANES, stride=seg_stride)
                a = a_scr[rows, :]
                h = a * hs[q] + b_scr[rows, :]
                p = a * ps[q]
                b_scr[rows, :] = h
                a_scr[rows, :] = p
                new_h.append(h)
                new_p.append(p)
            out.append((tuple(new_h), tuple(new_p)))
        return tuple(out)

    zero = jnp.zeros((SUBLANES, c_blk), F32)
    one = jnp.ones((SUBLANES, c_blk), F32)
    init = ((zero,) * SCAN_GROUPS, (one,) * SCAN_GROUPS)
    (hf_end, pf_end), (hb_end, pb_end) = lax.fori_loop(0, seg, scan_step, (init, init))

    row = lax.broadcasted_iota(jnp.int32, (SUBLANES, c_blk), 0)

    def entering(h_end, p_end, carry_in, first_row, shift):
        c = zero
        for _ in range(SUBLANES):
            c = jnp.where(row == first_row, carry_in, _shift_rows(h_end + p_end * c, shift))
        return c

    cf = [None] * SCAN_GROUPS
    cbk = [None] * SCAN_GROUPS
    carry_f = jnp.zeros((1, c_blk), F32)
    carry_b = jnp.zeros((1, c_blk), F32)
    for q in range(SCAN_GROUPS):
        cf[q] = entering(hf_end[q], pf_end[q], carry_f, 0, 1)
        carry_f = (hf_end[q] + pf_end[q] * cf[q])[SUBLANES - 1:SUBLANES, :]
        qb = SCAN_GROUPS - 1 - q
        cbk[qb] = entering(hb_end[qb], pb_end[qb], carry_b, SUBLANES - 1, -1)
        carry_b = (hb_end[qb] + pb_end[qb] * cbk[qb])[0:1, :]

    for j in range(n_seg):
        q, i = divmod(j, SUBLANES)
        src = pl.ds(j * seg_stride, seg)
        rows = pl.ds(j * seg, seg)
        hsum = ((bf[src, :] + af[src, :] * cf[q][i:i + 1, :])
                + (bb[src, :] + ab[src, :] * cbk[q][i:i + 1, :]))
        y_ref[0, rows, :] = (jax.nn.gelu(lg_ref[0, rows, :]) * hsum).astype(y_ref.dtype)


def _lru(lx, lg, conv_w, conv_b, w_cat, b_cat, lam_cat):
    bsz, s, d_lru = lx.shape
    c_blk = LANES
    n_blk = d_lru // c_blk
    scr = pltpu.VMEM((s + SCAN_GROUPS * SUBLANES * SEG_PAD, c_blk), F32)
    return pl.pallas_call(
        _lru_kernel,
        grid=(bsz, n_blk),
        in_specs=[
            pl.BlockSpec((1, s, c_blk), lambda b, c: (b, 0, c)),
            pl.BlockSpec((1, s, c_blk), lambda b, c: (b, 0, c)),
            pl.BlockSpec((4, c_blk), lambda b, c: (0, c)),
            pl.BlockSpec((1, c_blk), lambda b, c: (0, c)),
            pl.BlockSpec((1, c_blk, 4 * c_blk), lambda b, c: (c, 0, 0)),
            pl.BlockSpec((1, 1, 4 * c_blk), lambda b, c: (c, 0, 0)),
            pl.BlockSpec((1, 1, 2 * c_blk), lambda b, c: (c, 0, 0)),
        ],
        out_specs=pl.BlockSpec((1, s, c_blk), lambda b, c: (b, 0, c)),
        out_shape=jax.ShapeDtypeStruct((bsz, s, d_lru), BF16),
        scratch_shapes=[pltpu.VMEM((s + 2 * SUBLANES, c_blk), F32), scr, scr, scr, scr],
        compiler_params=_cparams(("parallel", "parallel")),
        name="lru",
    )(lx, lg, conv_w, conv_b, w_cat, b_cat, lam_cat)


def _fourier_kernel(f_ref, ka_ref, kb_ref, cc_ref, sc_ref, twc_ref, tws_ref, y_ref,
                    ar_scr, ai_scr, y_scr, *, scale):
    n1 = f_ref.shape[1]
    n_u = f_ref.shape[2]
    cols = f_ref.shape[4]
    r = SUBLANES * n1
    n2 = n_u * SUBLANES
    stage_stride = n1 + STAGE_PAD

    def stage_a(i, carry):
        us = (2 * i, 2 * i + 1)
        xu = jnp.concatenate([f_ref[0, :, u].reshape(r, cols) for u in us], axis=1)
        a = jnp.dot(ka_ref[...], xu.astype(BF16), preferred_element_type=F32)
        for j, u in enumerate(us):
            a_re = a[:r, j * cols:(j + 1) * cols]
            a_im = a[r:, j * cols:(j + 1) * cols]
            c = twc_ref[u]
            sn = tws_ref[u]
            ar_scr[u] = a_re * c - a_im * sn
            ai_scr[u] = a_re * sn + a_im * c
        return carry

    lax.fori_loop(0, n_u // 2, stage_a, 0)

    nb = FOURIER_K1_BATCH

    def stage_b(i, carry):
        rows = pl.ds(pl.multiple_of(i * (nb * SUBLANES), nb * SUBLANES), nb * SUBLANES)
        a_re = ar_scr[:, rows, :]
        a_im = ai_scr[:, rows, :]

        def rows_s2(a, j):
            return a[:, j * SUBLANES:(j + 1) * SUBLANES, :].reshape(n2, cols)

        st = jnp.concatenate(
            [jnp.concatenate([rows_s2(a_re, j), rows_s2(a_im, j)], axis=0) for j in range(nb)],
            axis=1).astype(BF16)
        x = jnp.dot(kb_ref[...], st, preferred_element_type=F32)
        x_re = jnp.concatenate([x[:n2, j * cols:(j + 1) * cols] for j in range(nb)], axis=0)
        x_im = jnp.concatenate([x[n2:, j * cols:(j + 1) * cols] for j in range(nb)], axis=0)
        y = (jnp.dot(x_re.astype(BF16), cc_ref[...], preferred_element_type=F32)
             + jnp.dot(x_im.astype(BF16), sc_ref[...], preferred_element_type=F32)) * scale
        for j in range(nb):
            y_scr[pl.ds(i * nb + j, n2, stride=stage_stride), :] = y[j * n2:(j + 1) * n2]
        return carry

    lax.fori_loop(0, n1 // nb, stage_b, 0)

    def copy_out(k2, carry):
        y_ref[0, pl.ds(pl.multiple_of(k2 * n1, n1), n1), :] = y_scr[pl.ds(k2 * stage_stride, n1), :]
        return carry

    lax.fori_loop(0, n2, copy_out, 0)


def _dft_tables(s, c_grp):
    n2 = FOURIER_N2
    n1 = s // n2
    n_u = n2 // SUBLANES

    def cos_sin(num, den):
        ang = (2.0 * np.pi / den) * (num % den).astype(np.float64)
        return np.cos(ang), np.sin(ang)

    def const(a, dtype=F32):
        return jnp.asarray(a.astype(np.float32)).astype(dtype)

    i1 = np.arange(n1, dtype=np.int64)
    c1, s1 = cos_sin(i1[:, None] * i1[None, :], n1)
    eye8 = np.eye(SUBLANES)
    ka = const(np.concatenate([np.kron(c1, eye8), np.kron(-s1, eye8)], axis=0), BF16)

    i2 = np.arange(n2, dtype=np.int64)
    c2, s2 = cos_sin(i2[:, None] * i2[None, :], n2)
    kb = const(np.block([[c2, s2], [-s2, c2]]), BF16)

    ic = np.arange(c_grp, dtype=np.int64)
    cc, sc = cos_sin(ic[:, None] * ic[None, :], c_grp)

    u = np.arange(n_u, dtype=np.int64)[:, None, None]
    k1 = np.arange(n1, dtype=np.int64)[None, :, None]
    v = np.arange(SUBLANES, dtype=np.int64)[None, None, :]
    tc, ts = cos_sin(k1 * (SUBLANES * u + v), s)
    shape = (n_u, n1 * SUBLANES, c_grp)
    twc = jnp.broadcast_to(const(tc.reshape(n_u, -1, 1)), shape)
    tws = jnp.broadcast_to(const(-ts.reshape(n_u, -1, 1)), shape)
    return ka, kb, const(cc, BF16), const(sc, BF16), twc, tws


def _fourier(four, n_groups):
    bsz, s, d_four = four.shape
    c_grp = d_four // n_groups
    n2 = FOURIER_N2
    n1 = s // n2
    n_u = n2 // SUBLANES
    r = SUBLANES * n1
    ka, kb, cc, sc, twc, tws = _dft_tables(s, c_grp)
    f5 = four.reshape(bsz, n1, n_u, SUBLANES, d_four)
    scale = 1.0 / math.sqrt(s * c_grp)
    const2 = lambda b, g: (0, 0)
    const3 = lambda b, g: (0, 0, 0)
    return pl.pallas_call(
        functools.partial(_fourier_kernel, scale=scale),
        grid=(bsz, n_groups),
        in_specs=[
            pl.BlockSpec((1, n1, n_u, SUBLANES, c_grp), lambda b, g: (b, 0, 0, 0, g)),
            pl.BlockSpec(ka.shape, const2),
            pl.BlockSpec(kb.shape, const2),
            pl.BlockSpec(cc.shape, const2),
            pl.BlockSpec(sc.shape, const2),
            pl.BlockSpec(twc.shape, const3),
            pl.BlockSpec(tws.shape, const3),
        ],
        out_specs=pl.BlockSpec((1, s, c_grp), lambda b, g: (b, 0, g)),
        out_shape=jax.ShapeDtypeStruct((bsz, s, d_four), F32),
        scratch_shapes=[pltpu.VMEM((n_u, r, c_grp), F32), pltpu.VMEM((n_u, r, c_grp), F32),
                        pltpu.VMEM((n2 * (n1 + STAGE_PAD), c_grp), F32)],
        compiler_params=_cparams(("parallel", "parallel")),
        name="fourier",
    )(f5, ka, kb, cc, sc, twc, tws)


def _outproj_kernel(x_ref, yl_ref, yf_ref, wo1_ref, wo2_ref, g2_ref, wr_ref,
                    x1_ref, h2_ref, aff_ref):
    x1 = (x_ref[0]
          + jnp.dot(yl_ref[0], wo1_ref[...], preferred_element_type=F32)
          + jnp.dot(yf_ref[0].astype(BF16), wo2_ref[...], preferred_element_type=F32))
    x1_ref[0] = x1
    h2 = _rms_scale(x1) * g2_ref[...]
    for j in range(h2.shape[1] // LANES):
        h2_ref[pl.ds(j, h2.shape[0], stride=SUBLANES), :] = h2[:, j * LANES:(j + 1) * LANES]
    logits = lax.dot_general(wr_ref[...], h2, (((1,), (1,)), ((), ())),
                             precision=lax.Precision.HIGHEST,
                             preferred_element_type=F32)
    ex = jnp.exp(logits - jnp.max(logits, axis=0, keepdims=True))
    aff_ref[0] = ex / jnp.sum(ex, axis=0, keepdims=True)


def _outproj(x, y_lru, y_four, wo1, wo2, g2, wr_t):
    bsz, s, d = x.shape
    d_lru = y_lru.shape[-1]
    d_four = y_four.shape[-1]
    n_e = wr_t.shape[0]
    assert d == SUBLANES * LANES
    n_i = s // ROW_TILE
    tile = lambda w: pl.BlockSpec((1, ROW_TILE, w), lambda b, i: (b, i, 0))
    const = lambda shape: pl.BlockSpec(shape, lambda b, i: (0, 0))
    return pl.pallas_call(
        _outproj_kernel,
        grid=(bsz, n_i),
        in_specs=[tile(d), tile(d_lru), tile(d_four), const(wo1.shape), const(wo2.shape),
                  const(g2.shape), const(wr_t.shape)],
        out_specs=[tile(d),
                   pl.BlockSpec((ROW_TILE * SUBLANES, LANES), lambda b, i: (b * n_i + i, 0)),
                   pl.BlockSpec((1, n_e, ROW_TILE), lambda b, i: (b, 0, i))],
        out_shape=[
            jax.ShapeDtypeStruct((bsz, s, d), F32),
            jax.ShapeDtypeStruct((bsz * s * SUBLANES, LANES), F32),
            jax.ShapeDtypeStruct((bsz, n_e, s), F32),
        ],
        compiler_params=_cparams(("parallel", "parallel")),
        name="outproj",
    )(x, y_lru, y_four, wo1, wo2, g2, wr_t)


def _select_kernel(aff_ref, tri_ref, pos_ref, starts_ref, *, cap):
    v = aff_ref[0]
    n_e, s = v.shape
    cap_f = float(cap)

    def midpoint(lo, hi):
        mid = 0.5 * (lo + hi)
        return mid, (mid > lo) & (mid < hi)

    def cond(carry):
        _, active = midpoint(*carry)
        return jnp.max(active.astype(F32)) > 0.0

    def body(carry):
        lo, hi = carry
        mid, active = midpoint(lo, hi)
        cnt = jnp.sum((v >= mid).astype(F32), axis=1, keepdims=True)
        enough = cnt >= cap_f
        return (jnp.where(active & enough, mid, lo), jnp.where(active & (~enough), mid, hi))

    lo0 = jnp.zeros((n_e, 1), F32)
    hi0 = jnp.full((n_e, 1), 2.0, F32)
    thr, _ = lax.while_loop(cond, body, (lo0, hi0))

    above = v > thr
    tie = v == thr
    need = cap_f - jnp.sum(above.astype(F32), axis=1, keepdims=True)

    n_tiles = s // TOKEN_TILE
    tri = tri_ref[...]
    lane = lax.broadcasted_iota(jnp.int32, (n_e, LANES), 1)

    def prefix(mask_f, want_starts):
        run = jnp.zeros((n_e, 1), F32)
        starts = jnp.zeros((n_e, LANES), F32)
        pieces = []
        for t in range(n_tiles):
            m = mask_f[:, t * TOKEN_TILE:(t + 1) * TOKEN_TILE]
            incl = jnp.dot(m.astype(BF16), tri, preferred_element_type=F32)
            pieces.append(run + incl - m)
            if want_starts:
                starts = jnp.where(lane == t, run, starts)
            run = run + incl[:, TOKEN_TILE - 1:TOKEN_TILE]
        if want_starts:
            starts = jnp.where(lane == n_tiles, run, starts)
        return jnp.concatenate(pieces, axis=1), starts

    tie_rank, _ = prefix(tie.astype(F32), False)
    sel = above | (tie & (tie_rank < need))
    pos, starts = prefix(sel.astype(F32), True)
    pos_ref[0] = jnp.where(sel, pos, -1.0).astype(jnp.int32)
    starts_ref[0] = starts.astype(jnp.int32)


def _select(aff_t, cap):
    bsz, n_e, s = aff_t.shape
    idx = jnp.arange(TOKEN_TILE, dtype=jnp.int32)
    tri = (idx[:, None] <= idx[None, :]).astype(BF16)
    return pl.pallas_call(
        functools.partial(_select_kernel, cap=cap),
        grid=(bsz,),
        in_specs=[
            pl.BlockSpec((1, n_e, s), lambda b: (b, 0, 0)),
            pl.BlockSpec(tri.shape, lambda b: (0, 0)),
        ],
        out_specs=[
            pl.BlockSpec((1, n_e, s), lambda b: (b, 0, 0)),
            pl.BlockSpec((1, n_e, LANES), lambda b: (b, 0, 0)),
        ],
        out_shape=[
            jax.ShapeDtypeStruct((bsz, n_e, s), jnp.int32),
            jax.ShapeDtypeStruct((bsz, n_e, LANES), jnp.int32),
        ],
        compiler_params=_cparams(("parallel",)),
        name="select",
    )(aff_t, tri)


def _token_table(gate, first_token):
    n_tok, n_e = gate.shape
    lane = lax.broadcasted_iota(jnp.int32, (n_tok, LANES), 1)
    tok = first_token + lax.broadcasted_iota(jnp.int32, (n_tok, LANES), 0)
    table = jnp.where(lane == 0, tok // TOKEN_ID_BASE, jnp.where(lane == 1, tok % TOKEN_ID_BASE, 0))
    table = table.astype(F32)
    place_lane = lax.broadcasted_iota(jnp.int32, (n_e, LANES), 1)
    place_sub = lax.broadcasted_iota(jnp.int32, (n_e, LANES), 0)
    rest = gate
    for j in range(3):
        piece = rest.astype(BF16)
        rest = rest - piece.astype(F32)
        place = jnp.where(place_lane == GATE_COL0 + j * n_e + place_sub, 1.0, 0.0).astype(BF16)
        table = table + jnp.dot(piece, place, preferred_element_type=F32)
    return table.astype(BF16)


def _compact_kernel(starts_ref, pos_ref, gate_ref, idx_ref, gs_ref, r_scr, *, cap):
    b = pl.program_id(0)
    n_e = pos_ref.shape[1]
    n_k = pos_ref.shape[2]
    n_blocks = cap // SLOT_BLOCK
    wide_windows = -(-(TOKEN_TILE + SUBLANES) // COMPACT_WIN)
    r_scr[...] = jnp.zeros_like(r_scr)
    sub = lax.broadcasted_iota(jnp.int32, (COMPACT_WIN, TOKEN_TILE), 0)

    def tile(k, carry):
        rows = pl.ds(pl.multiple_of(k * TOKEN_TILE, TOKEN_TILE), TOKEN_TILE)
        vk = _token_table(gate_ref[0, rows, :], k * TOKEN_TILE)
        starts = []
        fits = None
        for e in range(n_e):
            base = (b * n_e + e) * LANES
            ws = (starts_ref[base + k] // SUBLANES) * SUBLANES
            ok = starts_ref[base + k + 1] - ws <= COMPACT_WIN
            fits = ok if fits is None else fits & ok
            starts.append(ws)

        def place(e, ws):
            hit = pos_ref[0, e, k] - ws == sub
            vals = jnp.dot(jnp.where(hit, 1.0, 0.0).astype(BF16), vk, preferred_element_type=F32)
            r_scr[e, pl.ds(pl.multiple_of(ws, SUBLANES), COMPACT_WIN), :] += vals

        @pl.when(fits)
        def _():
            for e in range(n_e):
                place(e, starts[e])

        @pl.when(jnp.logical_not(fits))
        def _():
            for e in range(n_e):
                for j in range(wide_windows):
                    place(e, starts[e] + j * COMPACT_WIN)

        return carry

    lax.fori_loop(0, n_k, tile, 0)

    lane = lax.broadcasted_iota(jnp.int32, (SLOT_BLOCK, LANES), 1)
    eye = (lax.broadcasted_iota(jnp.int32, (SLOT_BLOCK, SLOT_BLOCK), 0)
           == lax.broadcasted_iota(jnp.int32, (SLOT_BLOCK, SLOT_BLOCK), 1))
    id_weight = jnp.where(lane == 0, float(TOKEN_ID_BASE), jnp.where(lane == 1, 1.0, 0.0))

    def to_row(col):
        return jnp.sum(jnp.where(eye, col, 0.0), axis=0, keepdims=True)

    def finish(e, carry):
        gate_lane = (lane >= GATE_COL0) & (lane < GATE_COL0 + 3 * n_e) & ((lane - GATE_COL0) % n_e == e)
        for m in range(n_blocks):
            blk = r_scr[e, pl.ds(m * SLOT_BLOCK, SLOT_BLOCK), :]
            ids = jnp.sum(blk * id_weight, axis=1, keepdims=True)
            gates = jnp.sum(jnp.where(gate_lane, blk, 0.0), axis=1, keepdims=True)
            idx_ref[0, e, m] = to_row(ids).astype(jnp.int32)
            gs_ref[0, e, m] = to_row(gates)
        return carry

    lax.fori_loop(0, n_e, finish, 0)


def _compact(starts_flat, pos5, gate_c, cap):
    bsz, n_e, n_k = pos5.shape[:3]
    s = gate_c.shape[1]
    assert s <= TOKEN_ID_BASE * 256 and GATE_COL0 + 3 * n_e <= LANES
    n_blocks = cap // SLOT_BLOCK
    out_block = (1, n_e, n_blocks, 1, SLOT_BLOCK)
    out_spec = pl.BlockSpec(out_block, lambda b, st: (b, 0, 0, 0, 0))
    wide_rows = -(-(TOKEN_TILE + SUBLANES) // COMPACT_WIN) * COMPACT_WIN
    grid_spec = pltpu.PrefetchScalarGridSpec(
        num_scalar_prefetch=1,
        grid=(bsz,),
        in_specs=[pl.BlockSpec((1, n_e, n_k, 1, TOKEN_TILE), lambda b, st: (b, 0, 0, 0, 0)),
                  pl.BlockSpec((1, s, n_e), lambda b, st: (b, 0, 0))],
        out_specs=[out_spec, out_spec],
        scratch_shapes=[pltpu.VMEM((n_e, cap + wide_rows, LANES), F32)],
    )
    return pl.pallas_call(
        functools.partial(_compact_kernel, cap=cap),
        grid_spec=grid_spec,
        out_shape=[jax.ShapeDtypeStruct((bsz,) + out_block[1:], jnp.int32),
                   jax.ShapeDtypeStruct((bsz,) + out_block[1:], F32)],
        compiler_params=_cparams(("parallel",)),
        name="compact",
    )(starts_flat, pos5, gate_c)


def _moe_kernel(idx_ref, idx_next_ref, gs_ref, h2_hbm, wg_ref, wu_ref, wd_ref, eo_ref,
                xbuf, sem, xb_scr, acc_scr, *, seq_len, n_ff):
    b = pl.program_id(0)
    e = pl.program_id(1)
    f = pl.program_id(2)
    n_e = pl.num_programs(1)
    n_blocks = xb_scr.shape[0]
    cap = n_blocks * SLOT_BLOCK
    d = xb_scr.shape[2]
    step = b * n_e + e
    n_steps = pl.num_programs(0) * n_e
    cur = step % 2

    def token_copy(idx_smem, batch, buf, p):
        tok = idx_smem[0, 0, p]
        src = h2_hbm.at[pl.ds(pl.multiple_of((batch * seq_len + tok) * SUBLANES, SUBLANES), SUBLANES)]
        dst = xbuf.at[buf, pl.ds(pl.multiple_of(p * SUBLANES, SUBLANES), SUBLANES)]
        return pltpu.make_async_copy(src, dst, sem.at[buf])

    def request(idx_smem, batch, buf, first, count):
        def body(i, carry):
            for j in range(GATHER_UNROLL):
                token_copy(idx_smem, batch, buf, first + i * GATHER_UNROLL + j).start()
            return carry
        lax.fori_loop(0, count // GATHER_UNROLL, body, 0)

    @pl.when((step == 0) & (f == 0))
    def _():
        request(idx_ref, b, cur, 0, cap)

    next_batch = jnp.where(step + 1 < n_steps, step + 1, 0) // n_e
    per_block = cap // (n_ff * n_blocks)

    @pl.when(f == 0)
    def _():
        pltpu.make_async_copy(h2_hbm.at[pl.ds(0, cap * SUBLANES)], xbuf.at[cur], sem.at[cur]).wait()
        for m in range(n_blocks):
            first = m * SLOT_BLOCK * SUBLANES
            xm = jnp.concatenate(
                [xbuf[cur, pl.ds(first + j, SLOT_BLOCK, stride=SUBLANES), :] for j in range(d // LANES)],
                axis=1)
            xb_scr[m] = xm.astype(BF16)
        acc_scr[...] = jnp.zeros_like(acc_scr)

    wg = wg_ref[0].astype(BF16)
    wu = wu_ref[0].astype(BF16)
    wd = wd_ref[0].astype(BF16)
    for m in range(n_blocks):
        first = (f * n_blocks + m) * per_block
        for j in range(per_block):
            token_copy(idx_next_ref, next_batch, 1 - cur, first + j).start()
        xm = xb_scr[m]
        g = jnp.dot(xm, wg, preferred_element_type=F32)
        u = jnp.dot(xm, wu, preferred_element_type=F32)
        act = (jax.nn.silu(g) * u).astype(BF16)
        acc_scr[m] += jnp.dot(act, wd, preferred_element_type=F32)

    @pl.when((step == n_steps - 1) & (f == n_ff - 1))
    def _():
        pltpu.make_async_copy(h2_hbm.at[pl.ds(0, cap * SUBLANES)], xbuf.at[1 - cur],
                              sem.at[1 - cur]).wait()

    @pl.when(f == n_ff - 1)
    def _():
        eye =(lax.broadcasted_iota(jnp.int32, (SLOT_BLOCK, SLOT_BLOCK), 0)
               == lax.broadcasted_iota(jnp.int32, (SLOT_BLOCK, SLOT_BLOCK), 1))
        for m in range(n_blocks):
            gcol = jnp.sum(jnp.where(eye, gs_ref[0, m], 0.0), axis=1, keepdims=True)
            rows = pl.ds(m * SLOT_BLOCK, SLOT_BLOCK)
            eo_ref[0, 0, rows, :] = (gcol * acc_scr[m]).astype(eo_ref.dtype)


def _moe(idx, gs, h2_tiles, wg, wu, wd, bsz, s):
    n_e, d, d_ff = wg.shape
    n_blocks = gs.shape[1]
    cap = n_blocks * SLOT_BLOCK
    n_ff = d_ff // FF_CHUNK
    assert d_ff % FF_CHUNK == 0 and cap % (n_ff * n_blocks) == 0 and cap % GATHER_UNROLL == 0
    assert d == SUBLANES * LANES

    def next_step(b, e, f):
        return ((b * n_e + e + 1) % (bsz * n_e), 0, 0)

    smem = pltpu.SMEM
    return pl.pallas_call(
        functools.partial(_moe_kernel, seq_len=s, n_ff=n_ff),
        grid=(bsz, n_e, n_ff),
        in_specs=[
            pl.BlockSpec((1, 1, cap), lambda b, e, f: (b * n_e + e, 0, 0), memory_space=smem),
            pl.BlockSpec((1, 1, cap), next_step, memory_space=smem),
            pl.BlockSpec((1, n_blocks, 1, SLOT_BLOCK), lambda b, e, f: (b * n_e + e, 0, 0, 0)),
            pl.BlockSpec(memory_space=pl.ANY),
            pl.BlockSpec((1, d, FF_CHUNK), lambda b, e, f: (e, 0, f)),
            pl.BlockSpec((1, d, FF_CHUNK), lambda b, e, f: (e, 0, f)),
            pl.BlockSpec((1, FF_CHUNK, d), lambda b, e, f: (e, f, 0)),
        ],
        out_specs=pl.BlockSpec((1, 1, cap, d), lambda b, e, f: (b, e, 0, 0)),
        out_shape=jax.ShapeDtypeStruct((bsz, n_e, cap, d), BF16),
        scratch_shapes=[pltpu.VMEM((2, cap * SUBLANES, LANES), F32),
                        pltpu.SemaphoreType.DMA((2,)),
                        pltpu.VMEM((n_blocks, SLOT_BLOCK, d), BF16),
                        pltpu.VMEM((n_blocks, SLOT_BLOCK, d), F32)],
        compiler_params=_cparams(("arbitrary", "arbitrary", "arbitrary")),
        name="moe",
    )(idx, idx, gs, h2_tiles, wg, wu, wd)


def _combine_kernel(starts_ref, x1_ref, pos_ref, gf_ref, eo_ref, out_ref, wcat_scr, acc_scr,
                    *, cap):
    b = pl.program_id(0)
    k = pl.program_id(1)
    n_e = pos_ref.shape[2]
    pos = pos_ref[0]
    expert_lane = lax.broadcasted_iota(jnp.int32, (1, n_e), 1)

    s0 = [starts_ref[(b * n_e + e) * LANES + k] for e in range(n_e)]
    s1 = [starts_ref[(b * n_e + e) * LANES + k + 1] for e in range(n_e)]

    def window_starts(rows):
        starts = [jnp.minimum(s // WIN_ALIGN, (cap - rows) // WIN_ALIGN) * WIN_ALIGN for s in s0]
        vec = jnp.zeros((1, n_e), jnp.int32)
        for e in range(n_e):
            vec = jnp.where(expert_lane == e, starts[e], vec)
        return starts, vec

    def onehot(rel, first, count, width):
        n_lanes = count * width
        lane = lax.broadcasted_iota(jnp.int32, (n_e, n_lanes), 1)
        sub = lax.broadcasted_iota(jnp.int32, (n_e, n_lanes), 0)
        expand = jnp.where(sub == first + lane // width, 1.0, 0.0).astype(BF16)
        spread = jnp.dot(jnp.clip(rel, -1, width).astype(F32).astype(BF16), expand,
                         preferred_element_type=F32)
        want = (lax.broadcasted_iota(jnp.int32, (TOKEN_TILE, n_lanes), 1) % width).astype(F32)
        return jnp.where(spread == want, 1.0, 0.0).astype(BF16)

    def finish(acc):
        out_ref[0] = _rms_scale(acc) * gf_ref[...]

    narrow, narrow_vec = window_starts(NARROW_WIN)
    fits = None
    for e in range(n_e):
        ok = s1[e] - narrow[e] <= NARROW_WIN
        fits = ok if fits is None else fits & ok

    @pl.when(fits)
    def _():
        for e in range(n_e):
            src = pl.ds(pl.multiple_of(narrow[e], WIN_ALIGN), NARROW_WIN)
            wcat_scr[pl.ds(e * NARROW_WIN, NARROW_WIN), :] = eo_ref[0, e, src, :]
        hit = onehot(pos - narrow_vec, 0, n_e, NARROW_WIN)
        finish(x1_ref[0] + jnp.dot(hit, wcat_scr[pl.ds(0, n_e * NARROW_WIN), :],
                                   preferred_element_type=F32))

    @pl.when(jnp.logical_not(fits))
    def _():
        wide, wide_vec = window_starts(SLOT_BLOCK + WIN_ALIGN)
        rel = pos - wide_vec
        group = wcat_scr.shape[0] // SLOT_BLOCK
        acc_scr[...] = x1_ref[0]
        for g in range(n_e // group):
            for i in range(group):
                e = g * group + i
                src = pl.ds(pl.multiple_of(wide[e], WIN_ALIGN), SLOT_BLOCK)
                wcat_scr[pl.ds(i * SLOT_BLOCK, SLOT_BLOCK), :] = eo_ref[0, e, src, :]
            acc_scr[...] += jnp.dot(onehot(rel, g * group, group, SLOT_BLOCK), wcat_scr[...],
                                    preferred_element_type=F32)
        for e in range(n_e):
            src = pl.ds(pl.multiple_of(wide[e] + SLOT_BLOCK, WIN_ALIGN), WIN_ALIGN)
            wcat_scr[pl.ds(e * WIN_ALIGN, WIN_ALIGN), :] = eo_ref[0, e, src, :]
        tail = onehot(rel - SLOT_BLOCK, 0, n_e, WIN_ALIGN)
        finish(acc_scr[...] + jnp.dot(tail, wcat_scr[pl.ds(0, n_e * WIN_ALIGN), :],
                                      preferred_element_type=F32))


def _combine(starts_flat, x1, pos_c, eo, gf):
    bsz, s, d = x1.shape
    n_e = pos_c.shape[-1]
    cap = eo.shape[2]
    n_k = s // TOKEN_TILE
    wide_group = 4
    assert n_e % wide_group == 0 and n_e * NARROW_WIN <= wide_group * SLOT_BLOCK
    grid_spec = pltpu.PrefetchScalarGridSpec(
        num_scalar_prefetch=1,
        grid=(bsz, n_k),
        in_specs=[
            pl.BlockSpec((1, TOKEN_TILE, d), lambda b, k, st: (b, k, 0)),
            pl.BlockSpec((1, TOKEN_TILE, n_e), lambda b, k, st: (b, k, 0)),
            pl.BlockSpec((1, d), lambda b, k, st: (0, 0)),
            pl.BlockSpec((1, n_e, cap, d), lambda b, k, st: (b, 0, 0, 0),
                         pipeline_mode=pl.Buffered(1)),
        ],
        out_specs=pl.BlockSpec((1, TOKEN_TILE, d), lambda b, k, st: (b, k, 0)),
        scratch_shapes=[pltpu.VMEM((wide_group * SLOT_BLOCK, d), BF16),
                        pltpu.VMEM((TOKEN_TILE, d), F32)],
    )
    return pl.pallas_call(
        functools.partial(_combine_kernel, cap=cap),
        grid_spec=grid_spec,
        out_shape=jax.ShapeDtypeStruct((bsz, s, d), F32),
        compiler_params=_cparams(("parallel", "parallel")),
        name="combine",
    )(starts_flat, x1, pos_c, gf, eo)


def _block_diag(w):
    h, hd, _ = w.shape
    eye = jnp.eye(h, dtype=w.dtype)
    return (eye[:, None, :, None] * w[:, :, None, :]).reshape(h * hd, h * hd)


def _lru_params(wa_f, wx_f, wa_b, wx_b, ba_f, bx_f, ba_b, bx_b, lam_f, lam_b):
    d_lru = ba_f.shape[0]
    n_blk = d_lru // LANES
    mats = [_block_diag(w) for w in (wa_f, wx_f, wa_b, wx_b)]
    w_cat = jnp.stack([
        jnp.concatenate([m[c * LANES:(c + 1) * LANES, c * LANES:(c + 1) * LANES] for m in mats], axis=1)
        for c in range(n_blk)]).astype(BF16)
    b_cat = jnp.stack([
        jnp.concatenate([v[c * LANES:(c + 1) * LANES] for v in (ba_f, bx_f, ba_b, bx_b)])
        for c in range(n_blk)])[:, None, :]
    lam_cat = jnp.stack([
        jnp.concatenate([v[c * LANES:(c + 1) * LANES] for v in (lam_f, lam_b)])
        for c in range(n_blk)])[:, None, :]
    return w_cat, b_cat, lam_cat


def kernel(x, norm1_g, w_in, conv_w, conv_b, lru_wa_f, lru_ba_f, lru_wx_f, lru_bx_f, lru_lam_f,
           lru_wa_b, lru_ba_b, lru_wx_b, lru_bx_b, lru_lam_b, w_out, norm2_g, w_router,
           w_gate, w_up, w_down, normf_g):
    bsz, s, d = x.shape
    d_lru = conv_b.shape[0]
    d_four = w_in.shape[1] - 2 * d_lru
    n_e = w_router.shape[1]
    cap = CAPACITY_FACTOR * s // n_e
    assert s % LRU_CHUNK == 0 and s % (SCAN_GROUPS * SUBLANES * SUBLANES) == 0
    assert s % FOURIER_N2 == 0 and cap % SLOT_BLOCK == 0 and s // TOKEN_TILE < LANES
    assert cap >= WIN_ROWS and w_gate.shape[2] % FF_CHUNK == 0

    lx, lg, fo = _inproj(x.reshape(bsz * s, d), norm1_g[None, :], w_in.astype(BF16), d_lru, d_four)
    lx = lx.reshape(bsz, s, d_lru)
    lg = lg.reshape(bsz, s, d_lru)
    fo = fo.reshape(bsz, s, d_four)

    w_cat, b_cat, lam_cat = _lru_params(lru_wa_f, lru_wx_f, lru_wa_b, lru_wx_b,
                                        lru_ba_f, lru_bx_f, lru_ba_b, lru_bx_b,
                                        lru_lam_f, lru_lam_b)
    y_lru = _lru(lx, lg, conv_w, conv_b[None, :], w_cat, b_cat, lam_cat)
    y_four = _fourier(fo, FOURIER_GROUPS)

    w_out_bf = w_out.astype(BF16)
    x1, h2_tiles, aff_t = _outproj(x, y_lru, y_four, w_out_bf[:d_lru], w_out_bf[d_lru:],
                                   norm2_g[None, :], w_router.T)

    pos_r, starts = _select(aff_t, cap)
    starts_flat = starts.reshape(-1)
    pos_c = jnp.transpose(pos_r, (0, 2, 1))
    pos5 = pos_r.reshape(bsz, n_e, s // TOKEN_TILE, 1, TOKEN_TILE)
    idx, gs = _compact(starts_flat, pos5, jnp.transpose(aff_t, (0, 2, 1)), cap)
    eo = _moe(idx.reshape(bsz * n_e, 1, cap), gs.reshape((bsz * n_e,) + gs.shape[2:]), h2_tiles,
              w_gate, w_up, w_down, bsz, s)
    return _combine(starts_flat, x1, pos_c, eo, normf_g[None, :])
```

```python
import functools
import math

import jax
import jax.numpy as jnp
import numpy as np
from jax import lax
from jax.experimental import pallas as pl
from jax.experimental.pallas import tpu as pltpu

F32 = jnp.float32
BF16 = jnp.bfloat16

EPS = 1e-6
LRU_C = 8.0
LRU_HEADS = 8
N_EXPERTS = 16
CAPACITY_FACTOR = 2
FOURIER_GROUPS = 4

SUBLANES = 8
LANES = 128
MXU_DIM = 256
VMEM_LIMIT_BYTES = 56 * 1024 * 1024

ROW_TILE = 1024
LRU_CHUNK = 512
SCAN_GROUPS = 4
SLOT_BLOCK = MXU_DIM
TOKEN_TILE = MXU_DIM
SEG_PAD = 4
FOURIER_N2 = LANES
FOURIER_K1_BATCH = 8
STAGE_PAD = 4
FF_CHUNK = 1024
GATHER_UNROLL = 8
COMPACT_WIN = 64
TOKEN_ID_BASE = 64
GATE_COL0 = 16
WIN_ALIGN = 16
WIN_ROWS = SLOT_BLOCK + WIN_ALIGN
NARROW_WIN = 64
COMBINE_TILES = 2


def _cparams(semantics):
    return pltpu.CompilerParams(dimension_semantics=semantics,
                                vmem_limit_bytes=VMEM_LIMIT_BYTES)


def _rms_scale(x):
    return x * lax.rsqrt(jnp.mean(x * x, axis=-1, keepdims=True) + EPS)


def _inproj_kernel(x_ref, g_ref, w_ref, lx_ref, lg_ref, fo_ref):
    h = _rms_scale(x_ref[...]) * g_ref[...]
    p = jnp.dot(h.astype(BF16), w_ref[...], preferred_element_type=F32)
    d = lx_ref.shape[-1]
    lx_ref[...] = p[:, :d]
    lg_ref[...] = p[:, d:2 * d]
    fo_ref[...] = p[:, 2 * d:]


def _inproj(x2, g, w_bf, d_lru, d_four):
    m, d = x2.shape
    n = w_bf.shape[1]
    return pl.pallas_call(
        _inproj_kernel,
        grid=(m // ROW_TILE,),
        in_specs=[
            pl.BlockSpec((ROW_TILE, d), lambda i: (i, 0)),
            pl.BlockSpec((1, d), lambda i: (0, 0)),
            pl.BlockSpec((d, n), lambda i: (0, 0)),
        ],
        out_specs=[
            pl.BlockSpec((ROW_TILE, d_lru), lambda i: (i, 0)),
            pl.BlockSpec((ROW_TILE, d_lru), lambda i: (i, 0)),
            pl.BlockSpec((ROW_TILE, d_four), lambda i: (i, 0)),
        ],
        out_shape=[
            jax.ShapeDtypeStruct((m, d_lru), F32),
            jax.ShapeDtypeStruct((m, d_lru), F32),
            jax.ShapeDtypeStruct((m, d_four), F32),
        ],
        compiler_params=_cparams(("parallel",)),
        name="inproj",
    )(x2, g, w_bf)


def _shift_rows(x, shift):
    n = x.shape[0]
    rows = lax.broadcasted_iota(jnp.int32, x.shape, 0)
    rolled = pltpu.roll(x, shift % n, axis=0)
    keep = (rows >= shift) if shift > 0 else (rows < n + shift)
    return jnp.where(keep, rolled, 0.0)


def _lru_kernel(lx_ref, lg_ref, cw_ref, cb_ref, w_ref, b_ref, lam_ref, y_ref,
                xpad, af, bf, ab, bb):
    s = lx_ref.shape[1]
    c_blk = lx_ref.shape[2]
    n_seg = SCAN_GROUPS * SUBLANES
    seg = s // n_seg
    seg_stride = seg + SEG_PAD
    piece = min(seg, LRU_CHUNK)
    n_chunks = s // LRU_CHUNK
    pad = SUBLANES

    zeros_pad = jnp.zeros((pad, c_blk), F32)
    xpad[pl.ds(0, pad), :] = zeros_pad
    xpad[pl.ds(pad + s, pad), :] = zeros_pad

    def copy_chunk(i, carry):
        t0 = pl.multiple_of(i * LRU_CHUNK, LRU_CHUNK)
        xpad[pl.ds(pad + t0, LRU_CHUNK), :] = lx_ref[0, pl.ds(t0, LRU_CHUNK), :]
        return carry

    lax.fori_loop(0, n_chunks, copy_chunk, 0)

    cw = cw_ref[...]
    conv_width = cw.shape[0]
    cb = cb_ref[...]
    bias = b_ref[0]
    quarter_c_log_sig = (0.25 * LRU_C) * jax.nn.log_sigmoid(lam_ref[0])
    w_cat = w_ref[0]

    def gates_chunk(i, carry):
        t0 = pl.multiple_of(i * LRU_CHUNK, LRU_CHUNK)
        c = cb
        left = conv_width // 2
        for k in range(conv_width):
            c = c + cw[k:k + 1, :] * xpad[pl.ds(t0 + (pad - left + k), LRU_CHUNK), :]
        z = jnp.dot(c.astype(BF16), w_cat, preferred_element_type=F32) + bias
        for d, (a_scr, b_scr) in enumerate(((af, bf), (ab, bb))):
            q = quarter_c_log_sig[:, d * c_blk:(d + 1) * c_blk]
            th_r = jnp.tanh(0.5 * z[:, (2 * d) * c_blk:(2 * d + 1) * c_blk])
            th_i = jnp.tanh(0.5 * z[:, (2 * d + 1) * c_blk:(2 * d + 2) * c_blk])
            t = jnp.tanh(q + q * th_r)
            inv = 1.0 / (1.0 - t)
            a = (1.0 + t) * inv
            bt = (jnp.sqrt(-t) * inv) * (c + c * th_i)
            for p in range(LRU_CHUNK // piece):
                t = t0 + p * piece
                dst = pl.ds((t // seg) * seg_stride + t % seg, piece)
                a_scr[dst, :] = a[p * piece:(p + 1) * piece]
                b_scr[dst, :] = bt[p * piece:(p + 1) * piece]
        return carry

    lax.fori_loop(0, n_chunks, gates_chunk, 0)

    def scan_step(i, carry):
        out = []
        for (a_scr, b_scr, o), (hs, ps) in zip(((af, bf, i), (ab, bb, seg - 1 - i)), carry):
            new_h, new_p = [], []
            for q in range(SCAN_GROUPS):
                rows = pl.ds(q * SUBLANES * seg_stride + o, SUBLANES, stride=seg_stride)
                a = a_scr[rows, :]
                h = a * hs[q] + b_scr[rows, :]
                p = a * ps[q]
                b_scr[rows, :] = h
                a_scr[rows, :] = p
                new_h.append(h)
                new_p.append(p)
            out.append((tuple(new_h), tuple(new_p)))
        return tuple(out)

    zero = jnp.zeros((SUBLANES, c_blk), F32)
    one = jnp.ones((SUBLANES, c_blk), F32)
    init = ((zero,) * SCAN_GROUPS, (one,) * SCAN_GROUPS)
    (hf_end, pf_end), (hb_end, pb_end) = lax.fori_loop(0, seg, scan_step, (init, init))

    row = lax.broadcasted_iota(jnp.int32, (SUBLANES, c_blk), 0)

    def entering(h_end, p_end, carry_in, first_row, shift):
        c = zero
        for _ in range(SUBLANES):
            c = jnp.where(row == first_row, carry_in, _shift_rows(h_end + p_end * c, shift))
        return c

    cf = [None] * SCAN_GROUPS
    cbk = [None] * SCAN_GROUPS
    carry_f = jnp.zeros((1, c_blk), F32)
    carry_b = jnp.zeros((1, c_blk), F32)
    for q in range(SCAN_GROUPS):
        cf[q] = entering(hf_end[q], pf_end[q], carry_f, 0, 1)
        carry_f = (hf_end[q] + pf_end[q] * cf[q])[SUBLANES - 1:SUBLANES, :]
        qb = SCAN_GROUPS - 1 - q
        cbk[qb] = entering(hb_end[qb], pb_end[qb], carry_b, SUBLANES - 1, -1)
        carry_b = (hb_end[qb] + pb_end[qb] * cbk[qb])[0:1, :]

    for j in range(n_seg):
        q, i = divmod(j, SUBLANES)
        src = pl.ds(j * seg_stride, seg)
        rows = pl.ds(j * seg, seg)
        hsum = ((bf[src, :] + af[src, :] * cf[q][i:i + 1, :])
                + (bb[src, :] + ab[src, :] * cbk[q][i:i + 1, :]))
        y_ref[0, rows, :] = (jax.nn.gelu(lg_ref[0, rows, :]) * hsum).astype(y_ref.dtype)


def _lru(lx, lg, conv_w, conv_b, w_cat, b_cat, lam_cat):
    bsz, s, d_lru = lx.shape
    c_blk = LANES
    n_blk = d_lru // c_blk
    scr = pltpu.VMEM((s + SCAN_GROUPS * SUBLANES * SEG_PAD, c_blk), F32)
    return pl.pallas_call(
        _lru_kernel,
        grid=(bsz, n_blk),
        in_specs=[
            pl.BlockSpec((1, s, c_blk), lambda b, c: (b, 0, c)),
            pl.BlockSpec((1, s, c_blk), lambda b, c: (b, 0, c)),
            pl.BlockSpec((4, c_blk), lambda b, c: (0, c)),
            pl.BlockSpec((1, c_blk), lambda b, c: (0, c)),
            pl.BlockSpec((1, c_blk, 4 * c_blk), lambda b, c: (c, 0, 0)),
            pl.BlockSpec((1, 1, 4 * c_blk), lambda b, c: (c, 0, 0)),
            pl.BlockSpec((1, 1, 2 * c_blk), lambda b, c: (c, 0, 0)),
        ],
        out_specs=pl.BlockSpec((1, s, c_blk), lambda b, c: (b, 0, c)),
        out_shape=jax.ShapeDtypeStruct((bsz, s, d_lru), BF16),
        scratch_shapes=[pltpu.VMEM((s + 2 * SUBLANES, c_blk), F32), scr, scr, scr, scr],
        compiler_params=_cparams(("parallel", "parallel")),
        name="lru",
    )(lx, lg, conv_w, conv_b, w_cat, b_cat, lam_cat)


def _fourier_kernel(f_ref, ka_ref, kb_ref, cc_ref, sc_ref, twc_ref, tws_ref, y_ref,
                    ar_scr, ai_scr, y_scr, *, scale):
    n1 = f_ref.shape[1]
    n_u = f_ref.shape[2]
    cols = f_ref.shape[4]
    r = SUBLANES * n1
    n2 = n_u * SUBLANES
    stage_stride = n1 + STAGE_PAD

    def stage_a(i, carry):
        us = (2 * i, 2 * i + 1)
        xu = jnp.concatenate([f_ref[0, :, u].reshape(r, cols) for u in us], axis=1)
        a = jnp.dot(ka_ref[...], xu.astype(BF16), preferred_element_type=F32)
        for j, u in enumerate(us):
            a_re = a[:r, j * cols:(j + 1) * cols]
            a_im = a[r:, j * cols:(j + 1) * cols]
            c = twc_ref[u]
            sn = tws_ref[u]
            ar_scr[u] = a_re * c - a_im * sn
            ai_scr[u] = a_re * sn + a_im * c
        return carry

    lax.fori_loop(0, n_u // 2, stage_a, 0)

    nb = FOURIER_K1_BATCH

    def stage_b(i, carry):
        rows = pl.ds(pl.multiple_of(i * (nb * SUBLANES), nb * SUBLANES), nb * SUBLANES)
        a_re = ar_scr[:, rows, :]
        a_im = ai_scr[:, rows, :]

        def rows_s2(a, j):
            return a[:, j * SUBLANES:(j + 1) * SUBLANES, :].reshape(n2, cols)

        st = jnp.concatenate(
            [jnp.concatenate([rows_s2(a_re, j), rows_s2(a_im, j)], axis=0) for j in range(nb)],
            axis=1).astype(BF16)
        x = jnp.dot(kb_ref[...], st, preferred_element_type=F32)
        x_re = jnp.concatenate([x[:n2, j * cols:(j + 1) * cols] for j in range(nb)], axis=0)
        x_im = jnp.concatenate([x[n2:, j * cols:(j + 1) * cols] for j in range(nb)], axis=0)
        y = (jnp.dot(x_re.astype(BF16), cc_ref[...], preferred_element_type=F32)
             + jnp.dot(x_im.astype(BF16), sc_ref[...], preferred_element_type=F32)) * scale
        for j in range(nb):
            y_scr[pl.ds(i * nb + j, n2, stride=stage_stride), :] = y[j * n2:(j + 1) * n2]
        return carry

    lax.fori_loop(0, n1 // nb, stage_b, 0)

    def copy_out(k2, carry):
        y_ref[0, pl.ds(pl.multiple_of(k2 * n1, n1), n1), :] = y_scr[pl.ds(k2 * stage_stride, n1), :]
        return carry

    lax.fori_loop(0, n2, copy_out, 0)


def _dft_tables(s, c_grp):
    n2 = FOURIER_N2
    n1 = s // n2
    n_u = n2 // SUBLANES

    def cos_sin(num, den):
        ang = (2.0 * np.pi / den) * (num % den).astype(np.float64)
        return np.cos(ang), np.sin(ang)

    def const(a, dtype=F32):
        return jnp.asarray(a.astype(np.float32)).astype(dtype)

    i1 = np.arange(n1, dtype=np.int64)
    c1, s1 = cos_sin(i1[:, None] * i1[None, :], n1)
    eye8 = np.eye(SUBLANES)
    ka = const(np.concatenate([np.kron(c1, eye8), np.kron(-s1, eye8)], axis=0), BF16)

    i2 = np.arange(n2, dtype=np.int64)
    c2, s2 = cos_sin(i2[:, None] * i2[None, :], n2)
    kb = const(np.block([[c2, s2], [-s2, c2]]), BF16)

    ic = np.arange(c_grp, dtype=np.int64)
    cc, sc = cos_sin(ic[:, None] * ic[None, :], c_grp)

    u = np.arange(n_u, dtype=np.int64)[:, None, None]
    k1 = np.arange(n1, dtype=np.int64)[None, :, None]
    v = np.arange(SUBLANES, dtype=np.int64)[None, None, :]
    tc, ts = cos_sin(k1 * (SUBLANES * u + v), s)
    shape = (n_u, n1 * SUBLANES, c_grp)
    twc = jnp.broadcast_to(const(tc.reshape(n_u, -1, 1)), shape)
    tws = jnp.broadcast_to(const(-ts.reshape(n_u, -1, 1)), shape)
    return ka, kb, const(cc, BF16), const(sc, BF16), twc, tws


def _fourier(four, n_groups):
    bsz, s, d_four = four.shape
    c_grp = d_four // n_groups
    n2 = FOURIER_N2
    n1 = s // n2
    n_u = n2 // SUBLANES
    r = SUBLANES * n1
    ka, kb, cc, sc, twc, tws = _dft_tables(s, c_grp)
    f5 = four.reshape(bsz, n1, n_u, SUBLANES, d_four)
    scale = 1.0 / math.sqrt(s * c_grp)
    const2 = lambda b, g: (0, 0)
    const3 = lambda b, g: (0, 0, 0)
    return pl.pallas_call(
        functools.partial(_fourier_kernel, scale=scale),
        grid=(bsz, n_groups),
        in_specs=[
            pl.BlockSpec((1, n1, n_u, SUBLANES, c_grp), lambda b, g: (b, 0, 0, 0, g)),
            pl.BlockSpec(ka.shape, const2),
            pl.BlockSpec(kb.shape, const2),
            pl.BlockSpec(cc.shape, const2),
            pl.BlockSpec(sc.shape, const2),
            pl.BlockSpec(twc.shape, const3),
            pl.BlockSpec(tws.shape, const3),
        ],
        out_specs=pl.BlockSpec((1, s, c_grp), lambda b, g: (b, 0, g)),
        out_shape=jax.ShapeDtypeStruct((bsz, s, d_four), F32),
        scratch_shapes=[pltpu.VMEM((n_u, r, c_grp), F32), pltpu.VMEM((n_u, r, c_grp), F32),
                        pltpu.VMEM((n2 * (n1 + STAGE_PAD), c_grp), F32)],
        compiler_params=_cparams(("parallel", "parallel")),
        name="fourier",
    )(f5, ka, kb, cc, sc, twc, tws)


def _outproj_kernel(x_ref, yl_ref, yf_ref, wo1_ref, wo2_ref, g2_ref, wr_ref,
                    x1_ref, h2_ref, aff_ref):
    x1 = (x_ref[0]
          + jnp.dot(yl_ref[0], wo1_ref[...], preferred_element_type=F32)
          + jnp.dot(yf_ref[0].astype(BF16), wo2_ref[...], preferred_element_type=F32))
    x1_ref[0] = x1
    h2 = _rms_scale(x1) * g2_ref[...]
    for j in range(h2.shape[1] // LANES):
        h2_ref[pl.ds(j, h2.shape[0], stride=SUBLANES), :] = h2[:, j * LANES:(j + 1) * LANES]
    logits = lax.dot_general(wr_ref[...], h2, (((1,), (1,)), ((), ())),
                             precision=lax.Precision.HIGHEST,
                             preferred_element_type=F32)
    ex = jnp.exp(logits - jnp.max(logits, axis=0, keepdims=True))
    aff_ref[0] = ex / jnp.sum(ex, axis=0, keepdims=True)


def _outproj(x, y_lru, y_four, wo1, wo2, g2, wr_t):
    bsz, s, d = x.shape
    d_lru = y_lru.shape[-1]
    d_four = y_four.shape[-1]
    n_e = wr_t.shape[0]
    assert d == SUBLANES * LANES
    n_i = s // ROW_TILE
    tile = lambda w: pl.BlockSpec((1, ROW_TILE, w), lambda b, i: (b, i, 0))
    const = lambda shape: pl.BlockSpec(shape, lambda b, i: (0, 0))
    return pl.pallas_call(
        _outproj_kernel,
        grid=(bsz, n_i),
        in_specs=[tile(d), tile(d_lru), tile(d_four), const(wo1.shape), const(wo2.shape),
                  const(g2.shape), const(wr_t.shape)],
        out_specs=[tile(d),
                   pl.BlockSpec((ROW_TILE * SUBLANES, LANES), lambda b, i: (b * n_i + i, 0)),
                   pl.BlockSpec((1, n_e, ROW_TILE), lambda b, i: (b, 0, i))],
        out_shape=[
            jax.ShapeDtypeStruct((bsz, s, d), F32),
            jax.ShapeDtypeStruct((bsz * s * SUBLANES, LANES), F32),
            jax.ShapeDtypeStruct((bsz, n_e, s), F32),
        ],
        compiler_params=_cparams(("parallel", "parallel")),
        name="outproj",
    )(x, y_lru, y_four, wo1, wo2, g2, wr_t)


def _select_kernel(aff_ref, tri_ref, pos_ref, starts_ref, *, cap):
    v = aff_ref[0]
    n_e, s = v.shape
    cap_f = float(cap)

    def midpoint(lo, hi):
        mid = 0.5 * (lo + hi)
        return mid, (mid > lo) & (mid < hi)

    def cond(carry):
        _, active = midpoint(*carry)
        return jnp.max(active.astype(F32)) > 0.0

    def body(carry):
        lo, hi = carry
        mid, active = midpoint(lo, hi)
        cnt = jnp.sum((v >= mid).astype(F32), axis=1, keepdims=True)
        enough = cnt >= cap_f
        return (jnp.where(active & enough, mid, lo), jnp.where(active & (~enough), mid, hi))

    lo0 = jnp.zeros((n_e, 1), F32)
    hi0 = jnp.full((n_e, 1), 2.0, F32)
    thr, _ = lax.while_loop(cond, body, (lo0, hi0))

    above = v > thr
    tie = v == thr
    need = cap_f - jnp.sum(above.astype(F32), axis=1, keepdims=True)

    n_tiles = s // TOKEN_TILE
    tri = tri_ref[...]
    lane = lax.broadcasted_iota(jnp.int32, (n_e, LANES), 1)

    def prefix(mask_f, want_starts):
        run = jnp.zeros((n_e, 1), F32)
        starts = jnp.zeros((n_e, LANES), F32)
        pieces = []
        for t in range(n_tiles):
            m = mask_f[:, t * TOKEN_TILE:(t + 1) * TOKEN_TILE]
            incl = jnp.dot(m.astype(BF16), tri, preferred_element_type=F32)
            pieces.append(run + incl - m)
            if want_starts:
                starts = jnp.where(lane == t, run, starts)
            run = run + incl[:, TOKEN_TILE - 1:TOKEN_TILE]
        if want_starts:
            starts = jnp.where(lane == n_tiles, run, starts)
        return jnp.concatenate(pieces, axis=1), starts

    tie_rank, _ = prefix(tie.astype(F32), False)
    sel = above | (tie & (tie_rank < need))
    pos, starts = prefix(sel.astype(F32), True)
    pos_ref[0] = jnp.where(sel, pos, -1.0).astype(jnp.int32)
    starts_ref[0] = starts.astype(jnp.int32)


def _select(aff_t, cap):
    bsz, n_e, s = aff_t.shape
    idx = jnp.arange(TOKEN_TILE, dtype=jnp.int32)
    tri = (idx[:, None] <= idx[None, :]).astype(BF16)
    return pl.pallas_call(
        functools.partial(_select_kernel, cap=cap),
        grid=(bsz,),
        in_specs=[
            pl.BlockSpec((1, n_e, s), lambda b: (b, 0, 0)),
            pl.BlockSpec(tri.shape, lambda b: (0, 0)),
        ],
        out_specs=[
            pl.BlockSpec((1, n_e, s), lambda b: (b, 0, 0)),
            pl.BlockSpec((1, n_e, LANES), lambda b: (b, 0, 0)),
        ],
        out_shape=[
            jax.ShapeDtypeStruct((bsz, n_e, s), jnp.int32),
            jax.ShapeDtypeStruct((bsz, n_e, LANES), jnp.int32),
        ],
        compiler_params=_cparams(("parallel",)),
        name="select",
    )(aff_t, tri)


def _token_table(gate, first_token):
    n_tok, n_e = gate.shape
    lane = lax.broadcasted_iota(jnp.int32, (n_tok, LANES), 1)
    tok = first_token + lax.broadcasted_iota(jnp.int32, (n_tok, LANES), 0)
    table = jnp.where(lane == 0, tok // TOKEN_ID_BASE, jnp.where(lane == 1, tok % TOKEN_ID_BASE, 0))
    table = table.astype(F32)
    place_lane = lax.broadcasted_iota(jnp.int32, (n_e, LANES), 1)
    place_sub = lax.broadcasted_iota(jnp.int32, (n_e, LANES), 0)
    rest = gate
    for j in range(3):
        piece = rest.astype(BF16)
        rest = rest - piece.astype(F32)
        place = jnp.where(place_lane == GATE_COL0 + j * n_e + place_sub, 1.0, 0.0).astype(BF16)
        table = table + jnp.dot(piece, place, preferred_element_type=F32)
    return table.astype(BF16)


def _compact_kernel(starts_ref, pos_ref, gate_ref, idx_ref, gs_ref, r_scr, *, cap):
    b = pl.program_id(0)
    n_e = pos_ref.shape[1]
    n_k = pos_ref.shape[2]
    n_blocks = cap // SLOT_BLOCK
    wide_windows = -(-(TOKEN_TILE + SUBLANES) // COMPACT_WIN)
    r_scr[...] = jnp.zeros_like(r_scr)
    sub = lax.broadcasted_iota(jnp.int32, (COMPACT_WIN, TOKEN_TILE), 0)

    def tile(k, carry):
        rows = pl.ds(pl.multiple_of(k * TOKEN_TILE, TOKEN_TILE), TOKEN_TILE)
        vk = _token_table(gate_ref[0, rows, :], k * TOKEN_TILE)
        starts = []
        fits = None
        for e in range(n_e):
            base = (b * n_e + e) * LANES
            ws = (starts_ref[base + k] // SUBLANES) * SUBLANES
            ok = starts_ref[base + k + 1] - ws <= COMPACT_WIN
            fits = ok if fits is None else fits & ok
            starts.append(ws)

        def place(e, ws):
            hit = pos_ref[0, e, k] - ws == sub
            vals = jnp.dot(jnp.where(hit, 1.0, 0.0).astype(BF16), vk, preferred_element_type=F32)
            r_scr[e, pl.ds(pl.multiple_of(ws, SUBLANES), COMPACT_WIN), :] += vals

        @pl.when(fits)
        def _():
            for e in range(n_e):
                place(e, starts[e])

        @pl.when(jnp.logical_not(fits))
        def _():
            for e in range(n_e):
                for j in range(wide_windows):
                    place(e, starts[e] + j * COMPACT_WIN)

        return carry

    lax.fori_loop(0, n_k, tile, 0)

    lane = lax.broadcasted_iota(jnp.int32, (SLOT_BLOCK, LANES), 1)
    eye = (lax.broadcasted_iota(jnp.int32, (SLOT_BLOCK, SLOT_BLOCK), 0)
           == lax.broadcasted_iota(jnp.int32, (SLOT_BLOCK, SLOT_BLOCK), 1))
    id_weight = jnp.where(lane == 0, float(TOKEN_ID_BASE), jnp.where(lane == 1, 1.0, 0.0))

    def to_row(col):
        return jnp.sum(jnp.where(eye, col, 0.0), axis=0, keepdims=True)

    def finish(e, carry):
        gate_lane = (lane >= GATE_COL0) & (lane < GATE_COL0 + 3 * n_e) & ((lane - GATE_COL0) % n_e == e)
        for m in range(n_blocks):
            blk = r_scr[e, pl.ds(m * SLOT_BLOCK, SLOT_BLOCK), :]
            ids = jnp.sum(blk * id_weight, axis=1, keepdims=True)
            gates = jnp.sum(jnp.where(gate_lane, blk, 0.0), axis=1, keepdims=True)
            idx_ref[0, e, m] = to_row(ids).astype(jnp.int32)
            gs_ref[0, e, m] = to_row(gates)
        return carry

    lax.fori_loop(0, n_e, finish, 0)


def _compact(starts_flat, pos5, gate_c, cap):
    bsz, n_e, n_k = pos5.shape[:3]
    s = gate_c.shape[1]
    assert s <= TOKEN_ID_BASE * 256 and GATE_COL0 + 3 * n_e <= LANES
    n_blocks = cap // SLOT_BLOCK
    out_block = (1, n_e, n_blocks, 1, SLOT_BLOCK)
    out_spec = pl.BlockSpec(out_block, lambda b, st: (b, 0, 0, 0, 0))
    wide_rows = -(-(TOKEN_TILE + SUBLANES) // COMPACT_WIN) * COMPACT_WIN
    grid_spec = pltpu.PrefetchScalarGridSpec(
        num_scalar_prefetch=1,
        grid=(bsz,),
        in_specs=[pl.BlockSpec((1, n_e, n_k, 1, TOKEN_TILE), lambda b, st: (b, 0, 0, 0, 0)),
                  pl.BlockSpec((1, s, n_e), lambda b, st: (b, 0, 0))],
        out_specs=[out_spec, out_spec],
        scratch_shapes=[pltpu.VMEM((n_e, cap + wide_rows, LANES), F32)],
    )
    return pl.pallas_call(
        functools.partial(_compact_kernel, cap=cap),
        grid_spec=grid_spec,
        out_shape=[jax.ShapeDtypeStruct((bsz,) + out_block[1:], jnp.int32),
                   jax.ShapeDtypeStruct((bsz,) + out_block[1:], F32)],
        compiler_params=_cparams(("parallel",)),
        name="compact",
    )(starts_flat, pos5, gate_c)


def _moe_kernel(idx_ref, idx_next_ref, gs_ref, h2_hbm, wg_ref, wu_ref, wd_ref, eo_ref,
                xbuf, sem, xb_scr, acc_scr, *, seq_len, n_ff):
    b = pl.program_id(0)
    e = pl.program_id(1)
    f = pl.program_id(2)
    n_e = pl.num_programs(1)
    n_blocks = xb_scr.shape[0]
    cap = n_blocks * SLOT_BLOCK
    d = xb_scr.shape[2]
    step = b * n_e + e
    n_steps = pl.num_programs(0) * n_e
    cur = step % 2

    def token_copy(idx_smem, batch, buf, p):
        tok = idx_smem[0, 0, p]
        src = h2_hbm.at[pl.ds(pl.multiple_of((batch * seq_len + tok) * SUBLANES, SUBLANES), SUBLANES)]
        dst = xbuf.at[buf, pl.ds(pl.multiple_of(p * SUBLANES, SUBLANES), SUBLANES)]
        return pltpu.make_async_copy(src, dst, sem.at[buf])

    def request(idx_smem, batch, buf, first, count):
        def body(i, carry):
            for j in range(GATHER_UNROLL):
                token_copy(idx_smem, batch, buf, first + i * GATHER_UNROLL + j).start()
            return carry
        lax.fori_loop(0, count // GATHER_UNROLL, body, 0)

    @pl.when((step == 0) & (f == 0))
    def _():
        request(idx_ref, b, cur, 0, cap)

    next_batch = jnp.where(step + 1 < n_steps, step + 1, 0) // n_e
    per_block = cap // (n_ff * n_blocks)

    @pl.when(f == 0)
    def _():
        pltpu.make_async_copy(h2_hbm.at[pl.ds(0, cap * SUBLANES)], xbuf.at[cur], sem.at[cur]).wait()
        for m in range(n_blocks):
            first = m * SLOT_BLOCK * SUBLANES
            xm = jnp.concatenate(
                [xbuf[cur, pl.ds(first + j, SLOT_BLOCK, stride=SUBLANES), :] for j in range(d // LANES)],
                axis=1)
            xb_scr[m] = xm.astype(BF16)
        acc_scr[...] = jnp.zeros_like(acc_scr)

    wg = wg_ref[0].astype(BF16)
    wu = wu_ref[0].astype(BF16)
    wd = wd_ref[0].astype(BF16)
    for m in range(n_blocks):
        first = (f * n_blocks + m) * per_block
        for j in range(per_block):
            token_copy(idx_next_ref, next_batch, 1 - cur, first + j).start()
        xm = xb_scr[m]
        g = jnp.dot(xm, wg, preferred_element_type=F32)
        u = jnp.dot(xm, wu, preferred_element_type=F32)
        act = (jax.nn.silu(g) * u).astype(BF16)
        acc_scr[m] += jnp.dot(act, wd, preferred_element_type=F32)

    @pl.when((step == n_steps - 1) & (f == n_ff - 1))
    def _():
        pltpu.make_async_copy(h2_hbm.at[pl.ds(0, cap * SUBLANES)], xbuf.at[1 - cur],
                              sem.at[1 - cur]).wait()

    @pl.when(f == n_ff - 1)
    def _():
        eye =(lax.broadcasted_iota(jnp.int32, (SLOT_BLOCK, SLOT_BLOCK), 0)
               == lax.broadcasted_iota(jnp.int32, (SLOT_BLOCK, SLOT_BLOCK), 1))
        for m in range(n_blocks):
            gcol = jnp.sum(jnp.where(eye, gs_ref[0, m], 0.0), axis=1, keepdims=True)
            rows = pl.ds(m * SLOT_BLOCK, SLOT_BLOCK)
            eo_ref[0, 0, rows, :] = (gcol * acc_scr[m]).astype(eo_ref.dtype)


def _moe(idx, gs, h2_tiles, wg, wu, wd, bsz, s):
    n_e, d, d_ff = wg.shape
    n_blocks = gs.shape[1]
    cap = n_blocks * SLOT_BLOCK
    n_ff = d_ff // FF_CHUNK
    assert d_ff % FF_CHUNK == 0 and cap % (n_ff * n_blocks) == 0 and cap % GATHER_UNROLL == 0
    assert d == SUBLANES * LANES

    def next_step(b, e, f):
        return ((b * n_e + e + 1) % (bsz * n_e), 0, 0)

    smem = pltpu.SMEM
    return pl.pallas_call(
        functools.partial(_moe_kernel, seq_len=s, n_ff=n_ff),
        grid=(bsz, n_e, n_ff),
        in_specs=[
            pl.BlockSpec((1, 1, cap), lambda b, e, f: (b * n_e + e, 0, 0), memory_space=smem),
            pl.BlockSpec((1, 1, cap), next_step, memory_space=smem),
            pl.BlockSpec((1, n_blocks, 1, SLOT_BLOCK), lambda b, e, f: (b * n_e + e, 0, 0, 0)),
            pl.BlockSpec(memory_space=pl.ANY),
            pl.BlockSpec((1, d, FF_CHUNK), lambda b, e, f: (e, 0, f)),
            pl.BlockSpec((1, d, FF_CHUNK), lambda b, e, f: (e, 0, f)),
            pl.BlockSpec((1, FF_CHUNK, d), lambda b, e, f: (e, f, 0)),
        ],
        out_specs=pl.BlockSpec((1, 1, cap, d), lambda b, e, f: (b, e, 0, 0)),
        out_shape=jax.ShapeDtypeStruct((bsz, n_e, cap, d), BF16),
        scratch_shapes=[pltpu.VMEM((2, cap * SUBLANES, LANES), F32),
                        pltpu.SemaphoreType.DMA((2,)),
                        pltpu.VMEM((n_blocks, SLOT_BLOCK, d), BF16),
                        pltpu.VMEM((n_blocks, SLOT_BLOCK, d), F32)],
        compiler_params=_cparams(("arbitrary", "arbitrary", "arbitrary")),
        name="moe",
    )(idx, idx, gs, h2_tiles, wg, wu, wd)


def _combine_kernel(starts_ref, x1_ref, pos_ref, gf_ref, eo_ref, out_ref, wcat_scr, acc_scr,
                    *, cap):
    for i in range(x1_ref.shape[1] // TOKEN_TILE):
        _combine_tile(starts_ref, x1_ref, pos_ref, gf_ref, eo_ref, out_ref, wcat_scr, acc_scr,
                      pl.program_id(1) * (x1_ref.shape[1] // TOKEN_TILE) + i,
                      pl.ds(i * TOKEN_TILE, TOKEN_TILE), cap)


def _combine_tile(starts_ref, x1_ref, pos_ref, gf_ref, eo_ref, out_ref, wcat_scr, acc_scr,
                  k, rows, cap):
    b = pl.program_id(0)
    n_e = pos_ref.shape[2]
    pos = pos_ref[0, rows, :]
    expert_lane = lax.broadcasted_iota(jnp.int32, (1, n_e), 1)

    s0 = [starts_ref[(b * n_e + e) * LANES + k] for e in range(n_e)]
    s1 = [starts_ref[(b * n_e + e) * LANES + k + 1] for e in range(n_e)]

    def window_starts(rows):
        starts = [jnp.minimum(s // WIN_ALIGN, (cap - rows) // WIN_ALIGN) * WIN_ALIGN for s in s0]
        vec = jnp.zeros((1, n_e), jnp.int32)
        for e in range(n_e):
            vec = jnp.where(expert_lane == e, starts[e], vec)
        return starts, vec

    def onehot(rel, first, count, width):
        n_lanes = count * width
        lane = lax.broadcasted_iota(jnp.int32, (n_e, n_lanes), 1)
        sub = lax.broadcasted_iota(jnp.int32, (n_e, n_lanes), 0)
        expand = jnp.where(sub == first + lane // width, 1.0, 0.0).astype(BF16)
        spread = jnp.dot(jnp.clip(rel, -1, width).astype(F32).astype(BF16), expand,
                         preferred_element_type=F32)
        want = (lax.broadcasted_iota(jnp.int32, (TOKEN_TILE, n_lanes), 1) % width).astype(F32)
        return jnp.where(spread == want, 1.0, 0.0).astype(BF16)

    def finish(acc):
        out_ref[0, rows, :] = _rms_scale(acc) * gf_ref[...]

    narrow, narrow_vec = window_starts(NARROW_WIN)
    fits = None
    for e in range(n_e):
        ok = s1[e] - narrow[e] <= NARROW_WIN
        fits = ok if fits is None else fits & ok

    @pl.when(fits)
    def _():
        for e in range(n_e):
            src = pl.ds(pl.multiple_of(narrow[e], WIN_ALIGN), NARROW_WIN)
            wcat_scr[pl.ds(e * NARROW_WIN, NARROW_WIN), :] = eo_ref[0, e, src, :]
        hit = onehot(pos - narrow_vec, 0, n_e, NARROW_WIN)
        finish(x1_ref[0, rows, :] + jnp.dot(hit, wcat_scr[pl.ds(0, n_e * NARROW_WIN), :],
                                            preferred_element_type=F32))

    @pl.when(jnp.logical_not(fits))
    def _():
        wide, wide_vec = window_starts(SLOT_BLOCK + WIN_ALIGN)
        rel = pos - wide_vec
        group = wcat_scr.shape[0] // SLOT_BLOCK
        acc_scr[...] = x1_ref[0, rows, :]
        for g in range(n_e // group):
            for i in range(group):
                e = g * group + i
                src = pl.ds(pl.multiple_of(wide[e], WIN_ALIGN), SLOT_BLOCK)
                wcat_scr[pl.ds(i * SLOT_BLOCK, SLOT_BLOCK), :] = eo_ref[0, e, src, :]
            acc_scr[...] += jnp.dot(onehot(rel, g * group, group, SLOT_BLOCK), wcat_scr[...],
                                    preferred_element_type=F32)
        for e in range(n_e):
            src = pl.ds(pl.multiple_of(wide[e] + SLOT_BLOCK, WIN_ALIGN), WIN_ALIGN)
            wcat_scr[pl.ds(e * WIN_ALIGN, WIN_ALIGN), :] = eo_ref[0, e, src, :]
        tail = onehot(rel - SLOT_BLOCK, 0, n_e, WIN_ALIGN)
        finish(acc_scr[...] + jnp.dot(tail, wcat_scr[pl.ds(0, n_e * WIN_ALIGN), :],
                                      preferred_element_type=F32))


def _combine(starts_flat, x1, pos_c, eo, gf):
    bsz, s, d = x1.shape
    n_e = pos_c.shape[-1]
    cap = eo.shape[2]
    step_rows = COMBINE_TILES * TOKEN_TILE
    wide_group = 4
    assert n_e % wide_group == 0 and n_e * NARROW_WIN <= wide_group * SLOT_BLOCK
    assert s % step_rows == 0
    grid_spec = pltpu.PrefetchScalarGridSpec(
        num_scalar_prefetch=1,
        grid=(bsz, s // step_rows),
        in_specs=[
            pl.BlockSpec((1, step_rows, d), lambda b, k, st: (b, k, 0)),
            pl.BlockSpec((1, step_rows, n_e), lambda b, k, st: (b, k, 0)),
            pl.BlockSpec((1, d), lambda b, k, st: (0, 0)),
            pl.BlockSpec((1, n_e, cap, d), lambda b, k, st: (b, 0, 0, 0),
                         pipeline_mode=pl.Buffered(1)),
        ],
        out_specs=pl.BlockSpec((1, step_rows, d), lambda b, k, st: (b, k, 0)),
        scratch_shapes=[pltpu.VMEM((wide_group * SLOT_BLOCK, d), BF16),
                        pltpu.VMEM((TOKEN_TILE, d), F32)],
    )
    return pl.pallas_call(
        functools.partial(_combine_kernel, cap=cap),
        grid_spec=grid_spec,
        out_shape=jax.ShapeDtypeStruct((bsz, s, d), F32),
        compiler_params=_cparams(("parallel", "parallel")),
        name="combine",
    )(starts_flat, x1, pos_c, gf, eo)


def _block_diag(w):
    h, hd, _ = w.shape
    eye = jnp.eye(h, dtype=w.dtype)
    return (eye[:, None, :, None] * w[:, :, None, :]).reshape(h * hd, h * hd)


def _lru_params(wa_f, wx_f, wa_b, wx_b, ba_f, bx_f, ba_b, bx_b, lam_f, lam_b):
    d_lru = ba_f.shape[0]
    n_blk = d_lru // LANES
    mats = [_block_diag(w) for w in (wa_f, wx_f, wa_b, wx_b)]
    w_cat = jnp.stack([
        jnp.concatenate([m[c * LANES:(c + 1) * LANES, c * LANES:(c + 1) * LANES] for m in mats], axis=1)
        for c in range(n_blk)]).astype(BF16)
    b_cat = jnp.stack([
        jnp.concatenate([v[c * LANES:(c + 1) * LANES] for v in (ba_f, bx_f, ba_b, bx_b)])
        for c in range(n_blk)])[:, None, :]
    lam_cat = jnp.stack([
        jnp.concatenate([v[c * LANES:(c + 1) * LANES] for v in (lam_f, lam_b)])
        for c in range(n_blk)])[:, None, :]
    return w_cat, b_cat, lam_cat


def kernel(x, norm1_g, w_in, conv_w, conv_b, lru_wa_f, lru_ba_f, lru_wx_f, lru_bx_f, lru_lam_f,
           lru_wa_b, lru_ba_b, lru_wx_b, lru_bx_b, lru_lam_b, w_out, norm2_g, w_router,
           w_gate, w_up, w_down, normf_g):
    bsz, s, d = x.shape
    d_lru = conv_b.shape[0]
    d_four = w_in.shape[1] - 2 * d_lru
    n_e = w_router.shape[1]
    cap = CAPACITY_FACTOR * s // n_e
    assert s % LRU_CHUNK == 0 and s % (SCAN_GROUPS * SUBLANES * SUBLANES) == 0
    assert s % FOURIER_N2 == 0 and cap % SLOT_BLOCK == 0 and s // TOKEN_TILE < LANES
    assert cap >= WIN_ROWS and w_gate.shape[2] % FF_CHUNK == 0

    lx, lg, fo = _inproj(x.reshape(bsz * s, d), norm1_g[None, :], w_in.astype(BF16), d_lru, d_four)
    lx = lx.reshape(bsz, s, d_lru)
    lg = lg.reshape(bsz, s, d_lru)
    fo = fo.reshape(bsz, s, d_four)

    w_cat, b_cat, lam_cat = _lru_params(lru_wa_f, lru_wx_f, lru_wa_b, lru_wx_b,
                                        lru_ba_f, lru_bx_f, lru_ba_b, lru_bx_b,
                                        lru_lam_f, lru_lam_b)
    y_lru = _lru(lx, lg, conv_w, conv_b[None, :], w_cat, b_cat, lam_cat)
    y_four = _fourier(fo, FOURIER_GROUPS)

    w_out_bf = w_out.astype(BF16)
    x1, h2_tiles, aff_t = _outproj(x, y_lru, y_four, w_out_bf[:d_lru], w_out_bf[d_lru:],
                                   norm2_g[None, :], w_router.T)

    pos_r, starts = _select(aff_t, cap)
    starts_flat = starts.reshape(-1)
    pos_c = jnp.transpose(pos_r, (0, 2, 1))
    pos5 = pos_r.reshape(bsz, n_e, s // TOKEN_TILE, 1, TOKEN_TILE)
    idx, gs = _compact(starts_flat, pos5, jnp.transpose(aff_t, (0, 2, 1)), cap)
    eo = _moe(idx.reshape(bsz * n_e, 1, cap), gs.reshape((bsz * n_e,) + gs.shape[2:]), h2_tiles,
              w_gate, w_up, w_down, bsz, s)
    return _combine(starts_flat, x1, pos_c, eo, normf_g[None, :])
```

```python
import functools
import math

import jax
import jax.numpy as jnp
import numpy as np
from jax import lax
from jax.experimental import pallas as pl
from jax.experimental.pallas import tpu as pltpu

F32 = jnp.float32
BF16 = jnp.bfloat16

EPS = 1e-6
LRU_C = 8.0
LRU_HEADS = 8
N_EXPERTS = 16
CAPACITY_FACTOR = 2
FOURIER_GROUPS = 4

SUBLANES = 8
LANES = 128
MXU_DIM = 256
VMEM_LIMIT_BYTES = 56 * 1024 * 1024

ROW_TILE = 1024
LRU_CHUNK = 512
SCAN_GROUPS = 4
SLOT_BLOCK = MXU_DIM
TOKEN_TILE = MXU_DIM
SEG_PAD = 4
FOURIER_N2 = LANES
FOURIER_K1_BATCH = 8
STAGE_PAD = 4
FF_CHUNK = 1024
GATHER_UNROLL = 8
COMPACT_WIN = 64
TOKEN_ID_BASE = 64
GATE_COL0 = 16
WIN_ALIGN = 16
WIN_ROWS = SLOT_BLOCK + WIN_ALIGN
NARROW_WIN = 64
COMBINE_TILES = 2


def _cparams(semantics):
    return pltpu.CompilerParams(dimension_semantics=semantics,
                                vmem_limit_bytes=VMEM_LIMIT_BYTES)


def _rms_scale(x):
    return x * lax.rsqrt(jnp.mean(x * x, axis=-1, keepdims=True) + EPS)


def _inproj_kernel(x_ref, g_ref, w_ref, lx_ref, lg_ref, fo_ref):
    h = _rms_scale(x_ref[...]) * g_ref[...]
    p = jnp.dot(h.astype(BF16), w_ref[...], preferred_element_type=F32)
    d = lx_ref.shape[-1]
    lx_ref[...] = p[:, :d]
    lg_ref[...] = p[:, d:2 * d]
    fo_ref[...] = p[:, 2 * d:]


def _inproj(x2, g, w_bf, d_lru, d_four):
    m, d = x2.shape
    n = w_bf.shape[1]
    return pl.pallas_call(
        _inproj_kernel,
        grid=(m // ROW_TILE,),
        in_specs=[
            pl.BlockSpec((ROW_TILE, d), lambda i: (i, 0)),
            pl.BlockSpec((1, d), lambda i: (0, 0)),
            pl.BlockSpec((d, n), lambda i: (0, 0)),
        ],
        out_specs=[
            pl.BlockSpec((ROW_TILE, d_lru), lambda i: (i, 0)),
            pl.BlockSpec((ROW_TILE, d_lru), lambda i: (i, 0)),
            pl.BlockSpec((ROW_TILE, d_four), lambda i: (i, 0)),
        ],
        out_shape=[
            jax.ShapeDtypeStruct((m, d_lru), F32),
            jax.ShapeDtypeStruct((m, d_lru), F32),
            jax.ShapeDtypeStruct((m, d_four), F32),
        ],
        compiler_params=_cparams(("parallel",)),
        name="inproj",
    )(x2, g, w_bf)


def _shift_rows(x, shift):
    n = x.shape[0]
    rows = lax.broadcasted_iota(jnp.int32, x.shape, 0)
    rolled = pltpu.roll(x, shift % n, axis=0)
    keep = (rows >= shift) if shift > 0 else (rows < n + shift)
    return jnp.where(keep, rolled, 0.0)


def _lru_kernel(lx_ref, lg_ref, cw_ref, cb_ref, w_ref, b_ref, lam_ref, y_ref,
                xpad, af, bf, ab, bb):
    s = lx_ref.shape[1]
    c_blk = lx_ref.shape[2]
    n_seg = SCAN_GROUPS * SUBLANES
    seg = s // n_seg
    seg_stride = seg + SEG_PAD
    piece = min(seg, LRU_CHUNK)
    n_chunks = s // LRU_CHUNK
    pad = SUBLANES

    zeros_pad = jnp.zeros((pad, c_blk), F32)
    xpad[pl.ds(0, pad), :] = zeros_pad
    xpad[pl.ds(pad + s, pad), :] = zeros_pad

    def copy_chunk(i, carry):
        t0 = pl.multiple_of(i * LRU_CHUNK, LRU_CHUNK)
        xpad[pl.ds(pad + t0, LRU_CHUNK), :] = lx_ref[0, pl.ds(t0, LRU_CHUNK), :]
        return carry

    lax.fori_loop(0, n_chunks, copy_chunk, 0)

    cw = cw_ref[...]
    conv_width = cw.shape[0]
    cb = cb_ref[...]
    bias = b_ref[0]
    quarter_c_log_sig = (0.25 * LRU_C) * jax.nn.log_sigmoid(lam_ref[0])
    w_cat = w_ref[0]

    def gates_chunk(i, carry):
        t0 = pl.multiple_of(i * LRU_CHUNK, LRU_CHUNK)
        c = cb
        left = conv_width // 2
        for k in range(conv_width):
            c = c + cw[k:k + 1, :] * xpad[pl.ds(t0 + (pad - left + k), LRU_CHUNK), :]
        z = jnp.dot(c.astype(BF16), w_cat, preferred_element_type=F32) + bias
        for d, (a_scr, b_scr) in enumerate(((af, bf), (ab, bb))):
            q = quarter_c_log_sig[:, d * c_blk:(d + 1) * c_blk]
            th_r = jnp.tanh(0.5 * z[:, (2 * d) * c_blk:(2 * d + 1) * c_blk])
            th_i = jnp.tanh(0.5 * z[:, (2 * d + 1) * c_blk:(2 * d + 2) * c_blk])
            t = jnp.tanh(q + q * th_r)
            inv = 1.0 / (1.0 - t)
            a = (1.0 + t) * inv
            bt = (jnp.sqrt(-t) * inv) * (c + c * th_i)
            for p in range(LRU_CHUNK // piece):
                t = t0 + p * piece
                dst = pl.ds((t // seg) * seg_stride + t % seg, piece)
                a_scr[dst, :] = a[p * piece:(p + 1) * piece]
                b_scr[dst, :] = bt[p * piece:(p + 1) * piece]
        return carry

    lax.fori_loop(0, n_chunks, gates_chunk, 0)

    def scan_step(i, carry):
        out = []
        for (a_scr, b_scr, o), (hs, ps) in zip(((af, bf, i), (ab, bb, seg - 1 - i)), carry):
            new_h, new_p = [], []
            for q in range(SCAN_GROUPS):
                rows = pl.ds(q * SUBLANES * seg_stride + o, SUBLANES, stride=seg_stride)
                a = a_scr[rows, :]
                h = a * hs[q] + b_scr[rows, :]
                p = a * ps[q]
                b_scr[rows, :] = h
                a_scr[rows, :] = p
                new_h.append(h)
                new_p.append(p)
            out.append((tuple(new_h), tuple(new_p)))
        return tuple(out)

    zero = jnp.zeros((SUBLANES, c_blk), F32)
    one = jnp.ones((SUBLANES, c_blk), F32)
    init = ((zero,) * SCAN_GROUPS, (one,) * SCAN_GROUPS)
    (hf_end, pf_end), (hb_end, pb_end) = lax.fori_loop(0, seg, scan_step, (init, init))

    row = lax.broadcasted_iota(jnp.int32, (SUBLANES, c_blk), 0)

    def entering(h_end, p_end, carry_in, first_row, shift):
        c = zero
        for _ in range(SUBLANES):
            c = jnp.where(row == first_row, carry_in, _shift_rows(h_end + p_end * c, shift))
        return c

    cf = [None] * SCAN_GROUPS
    cbk = [None] * SCAN_GROUPS
    carry_f = jnp.zeros((1, c_blk), F32)
    carry_b = jnp.zeros((1, c_blk), F32)
    for q in range(SCAN_GROUPS):
        cf[q] = entering(hf_end[q], pf_end[q], carry_f, 0, 1)
        carry_f = (hf_end[q] + pf_end[q] * cf[q])[SUBLANES - 1:SUBLANES, :]
        qb = SCAN_GROUPS - 1 - q
        cbk[qb] = entering(hb_end[qb], pb_end[qb], carry_b, SUBLANES - 1, -1)
        carry_b = (hb_end[qb] + pb_end[qb] * cbk[qb])[0:1, :]

    for j in range(n_seg):
        q, i = divmod(j, SUBLANES)
        src = pl.ds(j * seg_stride, seg)
        rows = pl.ds(j * seg, seg)
        hsum = ((bf[src, :] + af[src, :] * cf[q][i:i + 1, :])
                + (bb[src, :] + ab[src, :] * cbk[q][i:i + 1, :]))
        y_ref[0, rows, :] = (jax.nn.gelu(lg_ref[0, rows, :]) * hsum).astype(y_ref.dtype)


def _lru(lx, lg, conv_w, conv_b, w_cat, b_cat, lam_cat):
    bsz, s, d_lru = lx.shape
    c_blk = LANES
    n_blk = d_lru // c_blk
    scr = pltpu.VMEM((s + SCAN_GROUPS * SUBLANES * SEG_PAD, c_blk), F32)
    return pl.pallas_call(
        _lru_kernel,
        grid=(bsz, n_blk),
        in_specs=[
            pl.BlockSpec((1, s, c_blk), lambda b, c: (b, 0, c)),
            pl.BlockSpec((1, s, c_blk), lambda b, c: (b, 0, c)),
            pl.BlockSpec((4, c_blk), lambda b, c: (0, c)),
            pl.BlockSpec((1, c_blk), lambda b, c: (0, c)),
            pl.BlockSpec((1, c_blk, 4 * c_blk), lambda b, c: (c, 0, 0)),
            pl.BlockSpec((1, 1, 4 * c_blk), lambda b, c: (c, 0, 0)),
            pl.BlockSpec((1, 1, 2 * c_blk), lambda b, c: (c, 0, 0)),
        ],
        out_specs=pl.BlockSpec((1, s, c_blk), lambda b, c: (b, 0, c)),
        out_shape=jax.ShapeDtypeStruct((bsz, s, d_lru), BF16),
        scratch_shapes=[pltpu.VMEM((s + 2 * SUBLANES, c_blk), F32), scr, scr, scr, scr],
        compiler_params=_cparams(("parallel", "parallel")),
        name="lru",
    )(lx, lg, conv_w, conv_b, w_cat, b_cat, lam_cat)


def _fourier_kernel(f_ref, ka_ref, kb_ref, cc_ref, sc_ref, twc_ref, tws_ref, y_ref,
                    ar_scr, ai_scr, y_scr, *, scale):
    n1 = f_ref.shape[1]
    n_u = f_ref.shape[2]
    cols = f_ref.shape[4]
    r = SUBLANES * n1
    n2 = n_u * SUBLANES
    stage_stride = n1 + STAGE_PAD

    def stage_a(i, carry):
        us = (2 * i, 2 * i + 1)
        xu = jnp.concatenate([f_ref[0, :, u].reshape(r, cols) for u in us], axis=1)
        a = jnp.dot(ka_ref[...], xu.astype(BF16), preferred_element_type=F32)
        for j, u in enumerate(us):
            a_re = a[:r, j * cols:(j + 1) * cols]
            a_im = a[r:, j * cols:(j + 1) * cols]
            c = twc_ref[u]
            sn = tws_ref[u]
            ar_scr[u] = a_re * c - a_im * sn
            ai_scr[u] = a_re * sn + a_im * c
        return carry

    lax.fori_loop(0, n_u // 2, stage_a, 0)

    nb = FOURIER_K1_BATCH

    def stage_b(i, carry):
        rows = pl.ds(pl.multiple_of(i * (nb * SUBLANES), nb * SUBLANES), nb * SUBLANES)
        a_re = ar_scr[:, rows, :]
        a_im = ai_scr[:, rows, :]

        def rows_s2(a, j):
            return a[:, j * SUBLANES:(j + 1) * SUBLANES, :].reshape(n2, cols)

        st = jnp.concatenate(
            [jnp.concatenate([rows_s2(a_re, j), rows_s2(a_im, j)], axis=0) for j in range(nb)],
            axis=1).astype(BF16)
        x = jnp.dot(kb_ref[...], st, preferred_element_type=F32)
        x_re = jnp.concatenate([x[:n2, j * cols:(j + 1) * cols] for j in range(nb)], axis=0)
        x_im = jnp.concatenate([x[n2:, j * cols:(j + 1) * cols] for j in range(nb)], axis=0)
        y = (jnp.dot(x_re.astype(BF16), cc_ref[...], preferred_element_type=F32)
             + jnp.dot(x_im.astype(BF16), sc_ref[...], preferred_element_type=F32)) * scale
        for j in range(nb):
            y_scr[pl.ds(i * nb + j, n2, stride=stage_stride), :] = y[j * n2:(j + 1) * n2]
        return carry

    lax.fori_loop(0, n1 // nb, stage_b, 0)

    def copy_out(k2, carry):
        y_ref[0, pl.ds(pl.multiple_of(k2 * n1, n1), n1), :] = y_scr[pl.ds(k2 * stage_stride, n1), :]
        return carry

    lax.fori_loop(0, n2, copy_out, 0)


def _dft_tables(s, c_grp):
    n2 = FOURIER_N2
    n1 = s // n2
    n_u = n2 // SUBLANES

    def cos_sin(num, den):
        ang = (2.0 * np.pi / den) * (num % den).astype(np.float64)
        return np.cos(ang), np.sin(ang)

    def const(a, dtype=F32):
        return jnp.asarray(a.astype(np.float32)).astype(dtype)

    i1 = np.arange(n1, dtype=np.int64)
    c1, s1 = cos_sin(i1[:, None] * i1[None, :], n1)
    eye8 = np.eye(SUBLANES)
    ka = const(np.concatenate([np.kron(c1, eye8), np.kron(-s1, eye8)], axis=0), BF16)

    i2 = np.arange(n2, dtype=np.int64)
    c2, s2 = cos_sin(i2[:, None] * i2[None, :], n2)
    kb = const(np.block([[c2, s2], [-s2, c2]]), BF16)

    ic = np.arange(c_grp, dtype=np.int64)
    cc, sc = cos_sin(ic[:, None] * ic[None, :], c_grp)

    u = np.arange(n_u, dtype=np.int64)[:, None, None]
    k1 = np.arange(n1, dtype=np.int64)[None, :, None]
    v = np.arange(SUBLANES, dtype=np.int64)[None, None, :]
    tc, ts = cos_sin(k1 * (SUBLANES * u + v), s)
    shape = (n_u, n1 * SUBLANES, c_grp)
    twc = jnp.broadcast_to(const(tc.reshape(n_u, -1, 1)), shape)
    tws = jnp.broadcast_to(const(-ts.reshape(n_u, -1, 1)), shape)
    return ka, kb, const(cc, BF16), const(sc, BF16), twc, tws


def _fourier(four, n_groups):
    bsz, s, d_four = four.shape
    c_grp = d_four // n_groups
    n2 = FOURIER_N2
    n1 = s // n2
    n_u = n2 // SUBLANES
    r = SUBLANES * n1
    ka, kb, cc, sc, twc, tws = _dft_tables(s, c_grp)
    f5 = four.reshape(bsz, n1, n_u, SUBLANES, d_four)
    scale = 1.0 / math.sqrt(s * c_grp)
    const2 = lambda b, g: (0, 0)
    const3 = lambda b, g: (0, 0, 0)
    return pl.pallas_call(
        functools.partial(_fourier_kernel, scale=scale),
        grid=(bsz, n_groups),
        in_specs=[
            pl.BlockSpec((1, n1, n_u, SUBLANES, c_grp), lambda b, g: (b, 0, 0, 0, g)),
            pl.BlockSpec(ka.shape, const2),
            pl.BlockSpec(kb.shape, const2),
            pl.BlockSpec(cc.shape, const2),
            pl.BlockSpec(sc.shape, const2),
            pl.BlockSpec(twc.shape, const3),
            pl.BlockSpec(tws.shape, const3),
        ],
        out_specs=pl.BlockSpec((1, s, c_grp), lambda b, g: (b, 0, g)),
        out_shape=jax.ShapeDtypeStruct((bsz, s, d_four), F32),
        scratch_shapes=[pltpu.VMEM((n_u, r, c_grp), F32), pltpu.VMEM((n_u, r, c_grp), F32),
                        pltpu.VMEM((n2 * (n1 + STAGE_PAD), c_grp), F32)],
        compiler_params=_cparams(("parallel", "parallel")),
        name="fourier",
    )(f5, ka, kb, cc, sc, twc, tws)


def _outproj_kernel(x_ref, yl_ref, yf_ref, wo1_ref, wo2_ref, g2_ref, wr_ref,
                    x1_ref, h2_ref, aff_ref):
    x1 = (x_ref[0]
          + jnp.dot(yl_ref[0], wo1_ref[...], preferred_element_type=F32)
          + jnp.dot(yf_ref[0].astype(BF16), wo2_ref[...], preferred_element_type=F32))
    x1_ref[0] = x1
    h2 = _rms_scale(x1) * g2_ref[...]
    for j in range(h2.shape[1] // LANES):
        h2_ref[pl.ds(j, h2.shape[0], stride=SUBLANES), :] = h2[:, j * LANES:(j + 1) * LANES]
    def split(v):
        hi = v.astype(BF16)
        return hi, (v - hi.astype(F32)).astype(BF16)

    def dot_nt(a, bm):
        return lax.dot_general(a, bm, (((1,), (1,)), ((), ())), preferred_element_type=F32)

    w_hi, w_lo = split(wr_ref[...])
    h_hi, h_lo = split(h2)
    logits = dot_nt(w_hi, h_hi) + (dot_nt(w_lo, h_hi) + dot_nt(w_hi, h_lo))
    ex = jnp.exp(logits - jnp.max(logits, axis=0, keepdims=True))
    aff_ref[0] = ex / jnp.sum(ex, axis=0, keepdims=True)


def _outproj(x, y_lru, y_four, wo1, wo2, g2, wr_t):
    bsz, s, d = x.shape
    d_lru = y_lru.shape[-1]
    d_four = y_four.shape[-1]
    n_e = wr_t.shape[0]
    assert d == SUBLANES * LANES
    n_i = s // ROW_TILE
    tile = lambda w: pl.BlockSpec((1, ROW_TILE, w), lambda b, i: (b, i, 0))
    const = lambda shape: pl.BlockSpec(shape, lambda b, i: (0, 0))
    return pl.pallas_call(
        _outproj_kernel,
        grid=(bsz, n_i),
        in_specs=[tile(d), tile(d_lru), tile(d_four), const(wo1.shape), const(wo2.shape),
                  const(g2.shape), const(wr_t.shape)],
        out_specs=[tile(d),
                   pl.BlockSpec((ROW_TILE * SUBLANES, LANES), lambda b, i: (b * n_i + i, 0)),
                   pl.BlockSpec((1, n_e, ROW_TILE), lambda b, i: (b, 0, i))],
        out_shape=[
            jax.ShapeDtypeStruct((bsz, s, d), F32),
            jax.ShapeDtypeStruct((bsz * s * SUBLANES, LANES), F32),
            jax.ShapeDtypeStruct((bsz, n_e, s), F32),
        ],
        compiler_params=_cparams(("parallel", "parallel")),
        name="outproj",
    )(x, y_lru, y_four, wo1, wo2, g2, wr_t)


def _select_kernel(aff_ref, tri_ref, pos_ref, starts_ref, *, cap):
    v = aff_ref[0]
    n_e, s = v.shape
    cap_f = float(cap)

    def midpoint(lo, hi):
        mid = 0.5 * (lo + hi)
        return mid, (mid > lo) & (mid < hi)

    def cond(carry):
        _, active = midpoint(*carry)
        return jnp.max(active.astype(F32)) > 0.0

    def body(carry):
        lo, hi = carry
        mid, active = midpoint(lo, hi)
        cnt = jnp.sum((v >= mid).astype(F32), axis=1, keepdims=True)
        enough = cnt >= cap_f
        return (jnp.where(active & enough, mid, lo), jnp.where(active & (~enough), mid, hi))

    lo0 = jnp.zeros((n_e, 1), F32)
    hi0 = jnp.full((n_e, 1), 2.0, F32)
    thr, _ = lax.while_loop(cond, body, (lo0, hi0))

    above = v > thr
    tie = v == thr
    need = cap_f - jnp.sum(above.astype(F32), axis=1, keepdims=True)

    n_tiles = s // TOKEN_TILE
    tri = tri_ref[...]
    lane = lax.broadcasted_iota(jnp.int32, (n_e, LANES), 1)

    def prefix(mask_f, want_starts):
        run = jnp.zeros((n_e, 1), F32)
        starts = jnp.zeros((n_e, LANES), F32)
        pieces = []
        for t in range(n_tiles):
            m = mask_f[:, t * TOKEN_TILE:(t + 1) * TOKEN_TILE]
            incl = jnp.dot(m.astype(BF16), tri, preferred_element_type=F32)
            pieces.append(run + incl - m)
            if want_starts:
                starts = jnp.where(lane == t, run, starts)
            run = run + incl[:, TOKEN_TILE - 1:TOKEN_TILE]
        if want_starts:
            starts = jnp.where(lane == n_tiles, run, starts)
        return jnp.concatenate(pieces, axis=1), starts

    tie_rank, _ = prefix(tie.astype(F32), False)
    sel = above | (tie & (tie_rank < need))
    pos, starts = prefix(sel.astype(F32), True)
    pos_ref[0] = jnp.where(sel, pos, -1.0).astype(jnp.int32)
    starts_ref[0] = starts.astype(jnp.int32)


def _select(aff_t, cap):
    bsz, n_e, s = aff_t.shape
    idx = jnp.arange(TOKEN_TILE, dtype=jnp.int32)
    tri = (idx[:, None] <= idx[None, :]).astype(BF16)
    return pl.pallas_call(
        functools.partial(_select_kernel, cap=cap),
        grid=(bsz,),
        in_specs=[
            pl.BlockSpec((1, n_e, s), lambda b: (b, 0, 0)),
            pl.BlockSpec(tri.shape, lambda b: (0, 0)),
        ],
        out_specs=[
            pl.BlockSpec((1, n_e, s), lambda b: (b, 0, 0)),
            pl.BlockSpec((1, n_e, LANES), lambda b: (b, 0, 0)),
        ],
        out_shape=[
            jax.ShapeDtypeStruct((bsz, n_e, s), jnp.int32),
            jax.ShapeDtypeStruct((bsz, n_e, LANES), jnp.int32),
        ],
        compiler_params=_cparams(("parallel",)),
        name="select",
    )(aff_t, tri)


def _token_table(gate, first_token):
    n_tok, n_e = gate.shape
    lane = lax.broadcasted_iota(jnp.int32, (n_tok, LANES), 1)
    tok = first_token + lax.broadcasted_iota(jnp.int32, (n_tok, LANES), 0)
    table = jnp.where(lane == 0, tok // TOKEN_ID_BASE, jnp.where(lane == 1, tok % TOKEN_ID_BASE, 0))
    table = table.astype(F32)
    place_lane = lax.broadcasted_iota(jnp.int32, (n_e, LANES), 1)
    place_sub = lax.broadcasted_iota(jnp.int32, (n_e, LANES), 0)
    rest = gate
    for j in range(3):
        piece = rest.astype(BF16)
        rest = rest - piece.astype(F32)
        place = jnp.where(place_lane == GATE_COL0 + j * n_e + place_sub, 1.0, 0.0).astype(BF16)
        table = table + jnp.dot(piece, place, preferred_element_type=F32)
    return table.astype(BF16)


def _compact_kernel(starts_ref, pos_ref, gate_ref, idx_ref, gs_ref, r_scr, *, cap):
    b = pl.program_id(0)
    n_e = pos_ref.shape[1]
    n_k = pos_ref.shape[2]
    n_blocks = cap // SLOT_BLOCK
    wide_windows = -(-(TOKEN_TILE + SUBLANES) // COMPACT_WIN)
    r_scr[...] = jnp.zeros_like(r_scr)
    sub = lax.broadcasted_iota(jnp.int32, (COMPACT_WIN, TOKEN_TILE), 0)

    def tile(k, carry):
        rows = pl.ds(pl.multiple_of(k * TOKEN_TILE, TOKEN_TILE), TOKEN_TILE)
        vk = _token_table(gate_ref[0, rows, :], k * TOKEN_TILE)
        starts = []
        fits = None
        for e in range(n_e):
            base = (b * n_e + e) * LANES
            ws = (starts_ref[base + k] // SUBLANES) * SUBLANES
            ok = starts_ref[base + k + 1] - ws <= COMPACT_WIN
            fits = ok if fits is None else fits & ok
            starts.append(ws)

        def place(e, ws):
            hit = pos_ref[0, e, k] - ws == sub
            vals = jnp.dot(jnp.where(hit, 1.0, 0.0).astype(BF16), vk, preferred_element_type=F32)
            r_scr[e, pl.ds(pl.multiple_of(ws, SUBLANES), COMPACT_WIN), :] += vals

        @pl.when(fits)
        def _():
            for e in range(n_e):
                place(e, starts[e])

        @pl.when(jnp.logical_not(fits))
        def _():
            for e in range(n_e):
                for j in range(wide_windows):
                    place(e, starts[e] + j * COMPACT_WIN)

        return carry

    lax.fori_loop(0, n_k, tile, 0)

    lane = lax.broadcasted_iota(jnp.int32, (SLOT_BLOCK, LANES), 1)
    eye = (lax.broadcasted_iota(jnp.int32, (SLOT_BLOCK, SLOT_BLOCK), 0)
           == lax.broadcasted_iota(jnp.int32, (SLOT_BLOCK, SLOT_BLOCK), 1))
    id_weight = jnp.where(lane == 0, float(TOKEN_ID_BASE), jnp.where(lane == 1, 1.0, 0.0))

    def to_row(col):
        return jnp.sum(jnp.where(eye, col, 0.0), axis=0, keepdims=True)

    def finish(e, carry):
        gate_lane = (lane >= GATE_COL0) & (lane < GATE_COL0 + 3 * n_e) & ((lane - GATE_COL0) % n_e == e)
        for m in range(n_blocks):
            blk = r_scr[e, pl.ds(m * SLOT_BLOCK, SLOT_BLOCK), :]
            ids = jnp.sum(blk * id_weight, axis=1, keepdims=True)
            gates = jnp.sum(jnp.where(gate_lane, blk, 0.0), axis=1, keepdims=True)
            idx_ref[0, e, m] = to_row(ids).astype(jnp.int32)
            gs_ref[0, e, m] = to_row(gates)
        return carry

    lax.fori_loop(0, n_e, finish, 0)


def _compact(starts_flat, pos5, gate_c, cap):
    bsz, n_e, n_k = pos5.shape[:3]
    s = gate_c.shape[1]
    assert s <= TOKEN_ID_BASE * 256 and GATE_COL0 + 3 * n_e <= LANES
    n_blocks = cap // SLOT_BLOCK
    out_block = (1, n_e, n_blocks, 1, SLOT_BLOCK)
    out_spec = pl.BlockSpec(out_block, lambda b, st: (b, 0, 0, 0, 0))
    wide_rows = -(-(TOKEN_TILE + SUBLANES) // COMPACT_WIN) * COMPACT_WIN
    grid_spec = pltpu.PrefetchScalarGridSpec(
        num_scalar_prefetch=1,
        grid=(bsz,),
        in_specs=[pl.BlockSpec((1, n_e, n_k, 1, TOKEN_TILE), lambda b, st: (b, 0, 0, 0, 0)),
                  pl.BlockSpec((1, s, n_e), lambda b, st: (b, 0, 0))],
        out_specs=[out_spec, out_spec],
        scratch_shapes=[pltpu.VMEM((n_e, cap + wide_rows, LANES), F32)],
    )
    return pl.pallas_call(
        functools.partial(_compact_kernel, cap=cap),
        grid_spec=grid_spec,
        out_shape=[jax.ShapeDtypeStruct((bsz,) + out_block[1:], jnp.int32),
                   jax.ShapeDtypeStruct((bsz,) + out_block[1:], F32)],
        compiler_params=_cparams(("parallel",)),
        name="compact",
    )(starts_flat, pos5, gate_c)


def _moe_kernel(idx_ref, idx_next_ref, gs_ref, h2_hbm, wg_ref, wu_ref, wd_ref, eo_ref,
                xbuf, sem, xb_scr, acc_scr, *, seq_len, n_ff):
    b = pl.program_id(0)
    e = pl.program_id(1)
    f = pl.program_id(2)
    n_e = pl.num_programs(1)
    n_blocks = xb_scr.shape[0]
    cap = n_blocks * SLOT_BLOCK
    d = xb_scr.shape[2]
    step = b * n_e + e
    n_steps = pl.num_programs(0) * n_e
    cur = step % 2

    def token_copy(idx_smem, batch, buf, p):
        tok = idx_smem[0, 0, p]
        src = h2_hbm.at[pl.ds(pl.multiple_of((batch * seq_len + tok) * SUBLANES, SUBLANES), SUBLANES)]
        dst = xbuf.at[buf, pl.ds(pl.multiple_of(p * SUBLANES, SUBLANES), SUBLANES)]
        return pltpu.make_async_copy(src, dst, sem.at[buf])

    def request(idx_smem, batch, buf, first, count):
        def body(i, carry):
            for j in range(GATHER_UNROLL):
                token_copy(idx_smem, batch, buf, first + i * GATHER_UNROLL + j).start()
            return carry
        lax.fori_loop(0, count // GATHER_UNROLL, body, 0)

    @pl.when((step == 0) & (f == 0))
    def _():
        request(idx_ref, b, cur, 0, cap)

    next_batch = jnp.where(step + 1 < n_steps, step + 1, 0) // n_e
    per_block = cap // (n_ff * n_blocks)

    @pl.when(f == 0)
    def _():
        pltpu.make_async_copy(h2_hbm.at[pl.ds(0, cap * SUBLANES)], xbuf.at[cur], sem.at[cur]).wait()
        for m in range(n_blocks):
            first = m * SLOT_BLOCK * SUBLANES
            xm = jnp.concatenate(
                [xbuf[cur, pl.ds(first + j, SLOT_BLOCK, stride=SUBLANES), :] for j in range(d // LANES)],
                axis=1)
            xb_scr[m] = xm.astype(BF16)
        acc_scr[...] = jnp.zeros_like(acc_scr)

    wg = wg_ref[0].astype(BF16)
    wu = wu_ref[0].astype(BF16)
    wd = wd_ref[0].astype(BF16)
    for m in range(n_blocks):
        first = (f * n_blocks + m) * per_block
        for j in range(per_block):
            token_copy(idx_next_ref, next_batch, 1 - cur, first + j).start()
        xm = xb_scr[m]
        g = jnp.dot(xm, wg, preferred_element_type=F32)
        u = jnp.dot(xm, wu, preferred_element_type=F32)
        act = (jax.nn.silu(g) * u).astype(BF16)
        acc_scr[m] += jnp.dot(act, wd, preferred_element_type=F32)

    @pl.when((step == n_steps - 1) & (f == n_ff - 1))
    def _():
        pltpu.make_async_copy(h2_hbm.at[pl.ds(0, cap * SUBLANES)], xbuf.at[1 - cur],
                              sem.at[1 - cur]).wait()

    @pl.when(f == n_ff - 1)
    def _():
        eye =(lax.broadcasted_iota(jnp.int32, (SLOT_BLOCK, SLOT_BLOCK), 0)
               == lax.broadcasted_iota(jnp.int32, (SLOT_BLOCK, SLOT_BLOCK), 1))
        for m in range(n_blocks):
            gcol = jnp.sum(jnp.where(eye, gs_ref[0, m], 0.0), axis=1, keepdims=True)
            rows = pl.ds(m * SLOT_BLOCK, SLOT_BLOCK)
            eo_ref[0, 0, rows, :] = (gcol * acc_scr[m]).astype(eo_ref.dtype)


def _moe(idx, gs, h2_tiles, wg, wu, wd, bsz, s):
    n_e, d, d_ff = wg.shape
    n_blocks = gs.shape[1]
    cap = n_blocks * SLOT_BLOCK
    n_ff = d_ff // FF_CHUNK
    assert d_ff % FF_CHUNK == 0 and cap % (n_ff * n_blocks) == 0 and cap % GATHER_UNROLL == 0
    assert d == SUBLANES * LANES

    def next_step(b, e, f):
        return ((b * n_e + e + 1) % (bsz * n_e), 0, 0)

    smem = pltpu.SMEM
    return pl.pallas_call(
        functools.partial(_moe_kernel, seq_len=s, n_ff=n_ff),
        grid=(bsz, n_e, n_ff),
        in_specs=[
            pl.BlockSpec((1, 1, cap), lambda b, e, f: (b * n_e + e, 0, 0), memory_space=smem),
            pl.BlockSpec((1, 1, cap), next_step, memory_space=smem),
            pl.BlockSpec((1, n_blocks, 1, SLOT_BLOCK), lambda b, e, f: (b * n_e + e, 0, 0, 0)),
            pl.BlockSpec(memory_space=pl.ANY),
            pl.BlockSpec((1, d, FF_CHUNK), lambda b, e, f: (e, 0, f)),
            pl.BlockSpec((1, d, FF_CHUNK), lambda b, e, f: (e, 0, f)),
            pl.BlockSpec((1, FF_CHUNK, d), lambda b, e, f: (e, f, 0)),
        ],
        out_specs=pl.BlockSpec((1, 1, cap, d), lambda b, e, f: (b, e, 0, 0)),
        out_shape=jax.ShapeDtypeStruct((bsz, n_e, cap, d), BF16),
        scratch_shapes=[pltpu.VMEM((2, cap * SUBLANES, LANES), F32),
                        pltpu.SemaphoreType.DMA((2,)),
                        pltpu.VMEM((n_blocks, SLOT_BLOCK, d), BF16),
                        pltpu.VMEM((n_blocks, SLOT_BLOCK, d), F32)],
        compiler_params=_cparams(("arbitrary", "arbitrary", "arbitrary")),
        name="moe",
    )(idx, idx, gs, h2_tiles, wg, wu, wd)


def _combine_kernel(starts_ref, x1_ref, pos_ref, gf_ref, eo_ref, out_ref, wcat_scr, acc_scr,
                    *, cap):
    for i in range(x1_ref.shape[1] // TOKEN_TILE):
        _combine_tile(starts_ref, x1_ref, pos_ref, gf_ref, eo_ref, out_ref, wcat_scr, acc_scr,
                      pl.program_id(1) * (x1_ref.shape[1] // TOKEN_TILE) + i,
                      pl.ds(i * TOKEN_TILE, TOKEN_TILE), cap)


def _combine_tile(starts_ref, x1_ref, pos_ref, gf_ref, eo_ref, out_ref, wcat_scr, acc_scr,
                  k, rows, cap):
    b = pl.program_id(0)
    n_e = pos_ref.shape[2]
    pos = pos_ref[0, rows, :]
    expert_lane = lax.broadcasted_iota(jnp.int32, (1, n_e), 1)

    s0 = [starts_ref[(b * n_e + e) * LANES + k] for e in range(n_e)]
    s1 = [starts_ref[(b * n_e + e) * LANES + k + 1] for e in range(n_e)]

    def window_starts(rows):
        starts = [jnp.minimum(s // WIN_ALIGN, (cap - rows) // WIN_ALIGN) * WIN_ALIGN for s in s0]
        vec = jnp.zeros((1, n_e), jnp.int32)
        for e in range(n_e):
            vec = jnp.where(expert_lane == e, starts[e], vec)
        return starts, vec

    def onehot(rel, first, count, width):
        n_lanes = count * width
        lane = lax.broadcasted_iota(jnp.int32, (n_e, n_lanes), 1)
        sub = lax.broadcasted_iota(jnp.int32, (n_e, n_lanes), 0)
        expand = jnp.where(sub == first + lane // width, 1.0, 0.0).astype(BF16)
        spread = jnp.dot(jnp.clip(rel, -1, width).astype(F32).astype(BF16), expand,
                         preferred_element_type=F32)
        want = (lax.broadcasted_iota(jnp.int32, (TOKEN_TILE, n_lanes), 1) % width).astype(F32)
        return jnp.where(spread == want, 1.0, 0.0).astype(BF16)

    def finish(acc):
        out_ref[0, rows, :] = _rms_scale(acc) * gf_ref[...]

    narrow, narrow_vec = window_starts(NARROW_WIN)
    fits = None
    for e in range(n_e):
        ok = s1[e] - narrow[e] <= NARROW_WIN
        fits = ok if fits is None else fits & ok

    @pl.when(fits)
    def _():
        for e in range(n_e):
            src = pl.ds(pl.multiple_of(narrow[e], WIN_ALIGN), NARROW_WIN)
            wcat_scr[pl.ds(e * NARROW_WIN, NARROW_WIN), :] = eo_ref[0, e, src, :]
        hit = onehot(pos - narrow_vec, 0, n_e, NARROW_WIN)
        finish(x1_ref[0, rows, :] + jnp.dot(hit, wcat_scr[pl.ds(0, n_e * NARROW_WIN), :],
                                            preferred_element_type=F32))

    @pl.when(jnp.logical_not(fits))
    def _():
        wide, wide_vec = window_starts(SLOT_BLOCK + WIN_ALIGN)
        rel = pos - wide_vec
        group = wcat_scr.shape[0] // SLOT_BLOCK
        acc_scr[...] = x1_ref[0, rows, :]
        for g in range(n_e // group):
            for i in range(group):
                e = g * group + i
                src = pl.ds(pl.multiple_of(wide[e], WIN_ALIGN), SLOT_BLOCK)
                wcat_scr[pl.ds(i * SLOT_BLOCK, SLOT_BLOCK), :] = eo_ref[0, e, src, :]
            acc_scr[...] += jnp.dot(onehot(rel, g * group, group, SLOT_BLOCK), wcat_scr[...],
                                    preferred_element_type=F32)
        for e in range(n_e):
            src = pl.ds(pl.multiple_of(wide[e] + SLOT_BLOCK, WIN_ALIGN), WIN_ALIGN)
            wcat_scr[pl.ds(e * WIN_ALIGN, WIN_ALIGN), :] = eo_ref[0, e, src, :]
        tail = onehot(rel - SLOT_BLOCK, 0, n_e, WIN_ALIGN)
        finish(acc_scr[...] + jnp.dot(tail, wcat_scr[pl.ds(0, n_e * WIN_ALIGN), :],
                                      preferred_element_type=F32))


def _combine(starts_flat, x1, pos_c, eo, gf):
    bsz, s, d = x1.shape
    n_e = pos_c.shape[-1]
    cap = eo.shape[2]
    step_rows = COMBINE_TILES * TOKEN_TILE
    wide_group = 4
    assert n_e % wide_group == 0 and n_e * NARROW_WIN <= wide_group * SLOT_BLOCK
    assert s % step_rows == 0
    grid_spec = pltpu.PrefetchScalarGridSpec(
        num_scalar_prefetch=1,
        grid=(bsz, s // step_rows),
        in_specs=[
            pl.BlockSpec((1, step_rows, d), lambda b, k, st: (b, k, 0)),
            pl.BlockSpec((1, step_rows, n_e), lambda b, k, st: (b, k, 0)),
            pl.BlockSpec((1, d), lambda b, k, st: (0, 0)),
            pl.BlockSpec((1, n_e, cap, d), lambda b, k, st: (b, 0, 0, 0),
                         pipeline_mode=pl.Buffered(1)),
        ],
        out_specs=pl.BlockSpec((1, step_rows, d), lambda b, k, st: (b, k, 0)),
        scratch_shapes=[pltpu.VMEM((wide_group * SLOT_BLOCK, d), BF16),
                        pltpu.VMEM((TOKEN_TILE, d), F32)],
    )
    return pl.pallas_call(
        functools.partial(_combine_kernel, cap=cap),
        grid_spec=grid_spec,
        out_shape=jax.ShapeDtypeStruct((bsz, s, d), F32),
        compiler_params=_cparams(("parallel", "parallel")),
        name="combine",
    )(starts_flat, x1, pos_c, gf, eo)


def _block_diag(w):
    h, hd, _ = w.shape
    eye = jnp.eye(h, dtype=w.dtype)
    return (eye[:, None, :, None] * w[:, :, None, :]).reshape(h * hd, h * hd)


def _lru_params(wa_f, wx_f, wa_b, wx_b, ba_f, bx_f, ba_b, bx_b, lam_f, lam_b):
    d_lru = ba_f.shape[0]
    n_blk = d_lru // LANES
    mats = [_block_diag(w) for w in (wa_f, wx_f, wa_b, wx_b)]
    w_cat = jnp.stack([
        jnp.concatenate([m[c * LANES:(c + 1) * LANES, c * LANES:(c + 1) * LANES] for m in mats], axis=1)
        for c in range(n_blk)]).astype(BF16)
    b_cat = jnp.stack([
        jnp.concatenate([v[c * LANES:(c + 1) * LANES] for v in (ba_f, bx_f, ba_b, bx_b)])
        for c in range(n_blk)])[:, None, :]
    lam_cat = jnp.stack([
        jnp.concatenate([v[c * LANES:(c + 1) * LANES] for v in (lam_f, lam_b)])
        for c in range(n_blk)])[:, None, :]
    return w_cat, b_cat, lam_cat


def kernel(x, norm1_g, w_in, conv_w, conv_b, lru_wa_f, lru_ba_f, lru_wx_f, lru_bx_f, lru_lam_f,
           lru_wa_b, lru_ba_b, lru_wx_b, lru_bx_b, lru_lam_b, w_out, norm2_g, w_router,
           w_gate, w_up, w_down, normf_g):
    bsz, s, d = x.shape
    d_lru = conv_b.shape[0]
    d_four = w_in.shape[1] - 2 * d_lru
    n_e = w_router.shape[1]
    cap = CAPACITY_FACTOR * s // n_e
    assert s % LRU_CHUNK == 0 and s % (SCAN_GROUPS * SUBLANES * SUBLANES) == 0
    assert s % FOURIER_N2 == 0 and cap % SLOT_BLOCK == 0 and s // TOKEN_TILE < LANES
    assert cap >= WIN_ROWS and w_gate.shape[2] % FF_CHUNK == 0

    lx, lg, fo = _inproj(x.reshape(bsz * s, d), norm1_g[None, :], w_in.astype(BF16), d_lru, d_four)
    lx = lx.reshape(bsz, s, d_lru)
    lg = lg.reshape(bsz, s, d_lru)
    fo = fo.reshape(bsz, s, d_four)

    w_cat, b_cat, lam_cat = _lru_params(lru_wa_f, lru_wx_f, lru_wa_b, lru_wx_b,
                                        lru_ba_f, lru_bx_f, lru_ba_b, lru_bx_b,
                                        lru_lam_f, lru_lam_b)
    y_lru = _lru(lx, lg, conv_w, conv_b[None, :], w_cat, b_cat, lam_cat)
    y_four = _fourier(fo, FOURIER_GROUPS)

    w_out_bf = w_out.astype(BF16)
    x1, h2_tiles, aff_t = _outproj(x, y_lru, y_four, w_out_bf[:d_lru], w_out_bf[d_lru:],
                                   norm2_g[None, :], w_router.T)

    pos_r, starts = _select(aff_t, cap)
    starts_flat = starts.reshape(-1)
    pos_c = jnp.transpose(pos_r, (0, 2, 1))
    pos5 = pos_r.reshape(bsz, n_e, s // TOKEN_TILE, 1, TOKEN_TILE)
    idx, gs = _compact(starts_flat, pos5, jnp.transpose(aff_t, (0, 2, 1)), cap)
    eo = _moe(idx.reshape(bsz * n_e, 1, cap), gs.reshape((bsz * n_e,) + gs.shape[2:]), h2_tiles,
              w_gate, w_up, w_down, bsz, s)
    return _combine(starts_flat, x1, pos_c, eo, normf_g[None, :])
```

```python
import functools
import math

import jax
import jax.numpy as jnp
import numpy as np
from jax import lax
from jax.experimental import pallas as pl
from jax.experimental.pallas import tpu as pltpu

F32 = jnp.float32
BF16 = jnp.bfloat16

EPS = 1e-6
LRU_C = 8.0
LRU_HEADS = 8
N_EXPERTS = 16
CAPACITY_FACTOR = 2
FOURIER_GROUPS = 4

SUBLANES = 8
LANES = 128
MXU_DIM = 256
VMEM_LIMIT_BYTES = 56 * 1024 * 1024

ROW_TILE = 1024
LRU_CHUNK = 512
SCAN_GROUPS = 4
SLOT_BLOCK = MXU_DIM
TOKEN_TILE = MXU_DIM
SEG_PAD = 4
FOURIER_N2 = LANES
FOURIER_K1_BATCH = 8
STAGE_PAD = 4
FF_CHUNK = 1024
GATHER_UNROLL = 8
COMPACT_WIN = 64
TOKEN_ID_BASE = 64
GATE_COL0 = 16
WIN_ALIGN = 16
WIN_ROWS = SLOT_BLOCK + WIN_ALIGN
NARROW_WIN = 64
COMBINE_TILES = 2


def _cparams(semantics):
    return pltpu.CompilerParams(dimension_semantics=semantics,
                                vmem_limit_bytes=VMEM_LIMIT_BYTES)


def _rms_scale(x):
    return x * lax.rsqrt(jnp.mean(x * x, axis=-1, keepdims=True) + EPS)


def _inproj_kernel(x_ref, g_ref, w_ref, lx_ref, lg_ref, fo_ref):
    h = _rms_scale(x_ref[...]) * g_ref[...]
    p = jnp.dot(h.astype(BF16), w_ref[...], preferred_element_type=F32)
    d = lx_ref.shape[-1]
    lx_ref[...] = p[:, :d]
    lg_ref[...] = p[:, d:2 * d]
    fo_ref[...] = p[:, 2 * d:]


def _inproj(x2, g, w_bf, d_lru, d_four):
    m, d = x2.shape
    n = w_bf.shape[1]
    return pl.pallas_call(
        _inproj_kernel,
        grid=(m // ROW_TILE,),
        in_specs=[
            pl.BlockSpec((ROW_TILE, d), lambda i: (i, 0)),
            pl.BlockSpec((1, d), lambda i: (0, 0)),
            pl.BlockSpec((d, n), lambda i: (0, 0)),
        ],
        out_specs=[
            pl.BlockSpec((ROW_TILE, d_lru), lambda i: (i, 0)),
            pl.BlockSpec((ROW_TILE, d_lru), lambda i: (i, 0)),
            pl.BlockSpec((ROW_TILE, d_four), lambda i: (i, 0)),
        ],
        out_shape=[
            jax.ShapeDtypeStruct((m, d_lru), F32),
            jax.ShapeDtypeStruct((m, d_lru), F32),
            jax.ShapeDtypeStruct((m, d_four), F32),
        ],
        compiler_params=_cparams(("parallel",)),
        name="inproj",
    )(x2, g, w_bf)


def _shift_rows(x, shift):
    n = x.shape[0]
    rows = lax.broadcasted_iota(jnp.int32, x.shape, 0)
    rolled = pltpu.roll(x, shift % n, axis=0)
    keep = (rows >= shift) if shift > 0 else (rows < n + shift)
    return jnp.where(keep, rolled, 0.0)


def _lru_kernel(lx_ref, lg_ref, cw_ref, cb_ref, w_ref, b_ref, lam_ref, y_ref,
                xpad, af, bf, ab, bb):
    s = lx_ref.shape[1]
    c_blk = lx_ref.shape[2]
    n_seg = SCAN_GROUPS * SUBLANES
    seg = s // n_seg
    seg_stride = seg + SEG_PAD
    piece = min(seg, LRU_CHUNK)
    n_chunks = s // LRU_CHUNK
    pad = SUBLANES

    zeros_pad = jnp.zeros((pad, c_blk), F32)
    xpad[pl.ds(0, pad), :] = zeros_pad
    xpad[pl.ds(pad + s, pad), :] = zeros_pad

    def copy_chunk(i, carry):
        t0 = pl.multiple_of(i * LRU_CHUNK, LRU_CHUNK)
        xpad[pl.ds(pad + t0, LRU_CHUNK), :] = lx_ref[0, pl.ds(t0, LRU_CHUNK), :]
        return carry

    lax.fori_loop(0, n_chunks, copy_chunk, 0)

    cw = cw_ref[...]
    conv_width = cw.shape[0]
    cb = cb_ref[...]
    bias = b_ref[0]
    quarter_c_log_sig = (0.25 * LRU_C) * jax.nn.log_sigmoid(lam_ref[0])
    w_cat = w_ref[0]

    def gates_chunk(i, carry):
        t0 = pl.multiple_of(i * LRU_CHUNK, LRU_CHUNK)
        c = cb
        left = conv_width // 2
        for k in range(conv_width):
            c = c + cw[k:k + 1, :] * xpad[pl.ds(t0 + (pad - left + k), LRU_CHUNK), :]
        z = jnp.dot(c.astype(BF16), w_cat, preferred_element_type=F32) + bias
        for d, (a_scr, b_scr) in enumerate(((af, bf), (ab, bb))):
            q = quarter_c_log_sig[:, d * c_blk:(d + 1) * c_blk]
            th_r = jnp.tanh(z[:, (2 * d) * c_blk:(2 * d + 1) * c_blk])
            th_i = jnp.tanh(z[:, (2 * d + 1) * c_blk:(2 * d + 2) * c_blk])
            t = jnp.tanh(q + q * th_r)
            inv = 1.0 / (1.0 - t)
            a = (1.0 + t) * inv
            bt = (jnp.sqrt(-t) * inv) * (c + c * th_i)
            for p in range(LRU_CHUNK // piece):
                t = t0 + p * piece
                dst = pl.ds((t // seg) * seg_stride + t % seg, piece)
                a_scr[dst, :] = a[p * piece:(p + 1) * piece]
                b_scr[dst, :] = bt[p * piece:(p + 1) * piece]
        return carry

    lax.fori_loop(0, n_chunks, gates_chunk, 0)

    def scan_step(i, carry):
        out = []
        for (a_scr, b_scr, o), (hs, ps) in zip(((af, bf, i), (ab, bb, seg - 1 - i)), carry):
            new_h, new_p = [], []
            for q in range(SCAN_GROUPS):
                rows = pl.ds(q * SUBLANES * seg_stride + o, SUBLANES, stride=seg_stride)
                a = a_scr[rows, :]
                h = a * hs[q] + b_scr[rows, :]
                p = a * ps[q]
                b_scr[rows, :] = h
                a_scr[rows, :] = p
                new_h.append(h)
                new_p.append(p)
            out.append((tuple(new_h), tuple(new_p)))
        return tuple(out)

    zero = jnp.zeros((SUBLANES, c_blk), F32)
    one = jnp.ones((SUBLANES, c_blk), F32)
    init = ((zero,) * SCAN_GROUPS, (one,) * SCAN_GROUPS)
    (hf_end, pf_end), (hb_end, pb_end) = lax.fori_loop(0, seg, scan_step, (init, init))

    row = lax.broadcasted_iota(jnp.int32, (SUBLANES, c_blk), 0)

    def entering(h_end, p_end, carry_in, first_row, shift):
        c = zero
        for _ in range(SUBLANES):
            c = jnp.where(row == first_row, carry_in, _shift_rows(h_end + p_end * c, shift))
        return c

    cf = [None] * SCAN_GROUPS
    cbk = [None] * SCAN_GROUPS
    carry_f = jnp.zeros((1, c_blk), F32)
    carry_b = jnp.zeros((1, c_blk), F32)
    for q in range(SCAN_GROUPS):
        cf[q] = entering(hf_end[q], pf_end[q], carry_f, 0, 1)
        carry_f = (hf_end[q] + pf_end[q] * cf[q])[SUBLANES - 1:SUBLANES, :]
        qb = SCAN_GROUPS - 1 - q
        cbk[qb] = entering(hb_end[qb], pb_end[qb], carry_b, SUBLANES - 1, -1)
        carry_b = (hb_end[qb] + pb_end[qb] * cbk[qb])[0:1, :]

    for j in range(n_seg):
        q, i = divmod(j, SUBLANES)
        src = pl.ds(j * seg_stride, seg)
        rows = pl.ds(j * seg, seg)
        hsum = ((bf[src, :] + af[src, :] * cf[q][i:i + 1, :])
                + (bb[src, :] + ab[src, :] * cbk[q][i:i + 1, :]))
        y_ref[0, rows, :] = (jax.nn.gelu(lg_ref[0, rows, :]) * hsum).astype(y_ref.dtype)


def _lru(lx, lg, conv_w, conv_b, w_cat, b_cat, lam_cat):
    bsz, s, d_lru = lx.shape
    c_blk = LANES
    n_blk = d_lru // c_blk
    scr = pltpu.VMEM((s + SCAN_GROUPS * SUBLANES * SEG_PAD, c_blk), F32)
    return pl.pallas_call(
        _lru_kernel,
        grid=(bsz, n_blk),
        in_specs=[
            pl.BlockSpec((1, s, c_blk), lambda b, c: (b, 0, c)),
            pl.BlockSpec((1, s, c_blk), lambda b, c: (b, 0, c)),
            pl.BlockSpec((4, c_blk), lambda b, c: (0, c)),
            pl.BlockSpec((1, c_blk), lambda b, c: (0, c)),
            pl.BlockSpec((1, c_blk, 4 * c_blk), lambda b, c: (c, 0, 0)),
            pl.BlockSpec((1, 1, 4 * c_blk), lambda b, c: (c, 0, 0)),
            pl.BlockSpec((1, 1, 2 * c_blk), lambda b, c: (c, 0, 0)),
        ],
        out_specs=pl.BlockSpec((1, s, c_blk), lambda b, c: (b, 0, c)),
        out_shape=jax.ShapeDtypeStruct((bsz, s, d_lru), BF16),
        scratch_shapes=[pltpu.VMEM((s + 2 * SUBLANES, c_blk), F32), scr, scr, scr, scr],
        compiler_params=_cparams(("parallel", "parallel")),
        name="lru",
    )(lx, lg, conv_w, conv_b, w_cat, b_cat, lam_cat)


def _fourier_kernel(f_ref, ka_ref, kb_ref, cc_ref, sc_ref, twc_ref, tws_ref, y_ref,
                    ar_scr, ai_scr, y_scr, *, scale):
    n1 = f_ref.shape[1]
    n_u = f_ref.shape[2]
    cols = f_ref.shape[4]
    r = SUBLANES * n1
    n2 = n_u * SUBLANES
    stage_stride = n1 + STAGE_PAD

    def stage_a(i, carry):
        us = (2 * i, 2 * i + 1)
        xu = jnp.concatenate([f_ref[0, :, u].reshape(r, cols) for u in us], axis=1)
        a = jnp.dot(ka_ref[...], xu.astype(BF16), preferred_element_type=F32)
        for j, u in enumerate(us):
            a_re = a[:r, j * cols:(j + 1) * cols]
            a_im = a[r:, j * cols:(j + 1) * cols]
            c = twc_ref[u]
            sn = tws_ref[u]
            ar_scr[u] = a_re * c - a_im * sn
            ai_scr[u] = a_re * sn + a_im * c
        return carry

    lax.fori_loop(0, n_u // 2, stage_a, 0)

    nb = FOURIER_K1_BATCH

    def stage_b(i, carry):
        rows = pl.ds(pl.multiple_of(i * (nb * SUBLANES), nb * SUBLANES), nb * SUBLANES)
        a_re = ar_scr[:, rows, :]
        a_im = ai_scr[:, rows, :]

        def rows_s2(a, j):
            return a[:, j * SUBLANES:(j + 1) * SUBLANES, :].reshape(n2, cols)

        st = jnp.concatenate(
            [jnp.concatenate([rows_s2(a_re, j), rows_s2(a_im, j)], axis=0) for j in range(nb)],
            axis=1).astype(BF16)
        x = jnp.dot(kb_ref[...], st, preferred_element_type=F32)
        x_re = jnp.concatenate([x[:n2, j * cols:(j + 1) * cols] for j in range(nb)], axis=0)
        x_im = jnp.concatenate([x[n2:, j * cols:(j + 1) * cols] for j in range(nb)], axis=0)
        y = (jnp.dot(x_re.astype(BF16), cc_ref[...], preferred_element_type=F32)
             + jnp.dot(x_im.astype(BF16), sc_ref[...], preferred_element_type=F32)) * scale
        for j in range(nb):
            y_scr[pl.ds(i * nb + j, n2, stride=stage_stride), :] = y[j * n2:(j + 1) * n2]
        return carry

    lax.fori_loop(0, n1 // nb, stage_b, 0)

    def copy_out(k2, carry):
        y_ref[0, pl.ds(pl.multiple_of(k2 * n1, n1), n1), :] = y_scr[pl.ds(k2 * stage_stride, n1), :]
        return carry

    lax.fori_loop(0, n2, copy_out, 0)


def _dft_tables(s, c_grp):
    n2 = FOURIER_N2
    n1 = s // n2
    n_u = n2 // SUBLANES

    def cos_sin(num, den):
        ang = (2.0 * np.pi / den) * (num % den).astype(np.float64)
        return np.cos(ang), np.sin(ang)

    def const(a, dtype=F32):
        return jnp.asarray(a.astype(np.float32)).astype(dtype)

    i1 = np.arange(n1, dtype=np.int64)
    c1, s1 = cos_sin(i1[:, None] * i1[None, :], n1)
    eye8 = np.eye(SUBLANES)
    ka = const(np.concatenate([np.kron(c1, eye8), np.kron(-s1, eye8)], axis=0), BF16)

    i2 = np.arange(n2, dtype=np.int64)
    c2, s2 = cos_sin(i2[:, None] * i2[None, :], n2)
    kb = const(np.block([[c2, s2], [-s2, c2]]), BF16)

    ic = np.arange(c_grp, dtype=np.int64)
    cc, sc = cos_sin(ic[:, None] * ic[None, :], c_grp)

    u = np.arange(n_u, dtype=np.int64)[:, None, None]
    k1 = np.arange(n1, dtype=np.int64)[None, :, None]
    v = np.arange(SUBLANES, dtype=np.int64)[None, None, :]
    tc, ts = cos_sin(k1 * (SUBLANES * u + v), s)
    shape = (n_u, n1 * SUBLANES, c_grp)
    twc = jnp.broadcast_to(const(tc.reshape(n_u, -1, 1)), shape)
    tws = jnp.broadcast_to(const(-ts.reshape(n_u, -1, 1)), shape)
    return ka, kb, const(cc, BF16), const(sc, BF16), twc, tws


def _fourier(four, n_groups):
    bsz, s, d_four = four.shape
    c_grp = d_four // n_groups
    n2 = FOURIER_N2
    n1 = s // n2
    n_u = n2 // SUBLANES
    r = SUBLANES * n1
    ka, kb, cc, sc, twc, tws = _dft_tables(s, c_grp)
    f5 = four.reshape(bsz, n1, n_u, SUBLANES, d_four)
    scale = 1.0 / math.sqrt(s * c_grp)
    const2 = lambda b, g: (0, 0)
    const3 = lambda b, g: (0, 0, 0)
    return pl.pallas_call(
        functools.partial(_fourier_kernel, scale=scale),
        grid=(bsz, n_groups),
        in_specs=[
            pl.BlockSpec((1, n1, n_u, SUBLANES, c_grp), lambda b, g: (b, 0, 0, 0, g)),
            pl.BlockSpec(ka.shape, const2),
            pl.BlockSpec(kb.shape, const2),
            pl.BlockSpec(cc.shape, const2),
            pl.BlockSpec(sc.shape, const2),
            pl.BlockSpec(twc.shape, const3),
            pl.BlockSpec(tws.shape, const3),
        ],
        out_specs=pl.BlockSpec((1, s, c_grp), lambda b, g: (b, 0, g)),
        out_shape=jax.ShapeDtypeStruct((bsz, s, d_four), F32),
        scratch_shapes=[pltpu.VMEM((n_u, r, c_grp), F32), pltpu.VMEM((n_u, r, c_grp), F32),
                        pltpu.VMEM((n2 * (n1 + STAGE_PAD), c_grp), F32)],
        compiler_params=_cparams(("parallel", "parallel")),
        name="fourier",
    )(f5, ka, kb, cc, sc, twc, tws)


def _outproj_kernel(x_ref, yl_ref, yf_ref, wo1_ref, wo2_ref, g2_ref, wr_ref,
                    x1_ref, h2_ref, aff_ref):
    x1 = (x_ref[0]
          + jnp.dot(yl_ref[0], wo1_ref[...], preferred_element_type=F32)
          + jnp.dot(yf_ref[0].astype(BF16), wo2_ref[...], preferred_element_type=F32))
    x1_ref[0] = x1
    h2 = _rms_scale(x1) * g2_ref[...]
    for j in range(h2.shape[1] // LANES):
        h2_ref[pl.ds(j, h2.shape[0], stride=SUBLANES), :] = h2[:, j * LANES:(j + 1) * LANES]
    def split(v):
        hi = v.astype(BF16)
        return hi, (v - hi.astype(F32)).astype(BF16)

    def dot_nt(a, bm):
        return lax.dot_general(a, bm, (((1,), (1,)), ((), ())), preferred_element_type=F32)

    w_hi, w_lo = split(wr_ref[...])
    h_hi, h_lo = split(h2)
    logits = dot_nt(w_hi, h_hi) + (dot_nt(w_lo, h_hi) + dot_nt(w_hi, h_lo))
    ex = jnp.exp(logits - jnp.max(logits, axis=0, keepdims=True))
    aff_ref[0] = ex / jnp.sum(ex, axis=0, keepdims=True)


def _outproj(x, y_lru, y_four, wo1, wo2, g2, wr_t):
    bsz, s, d = x.shape
    d_lru = y_lru.shape[-1]
    d_four = y_four.shape[-1]
    n_e = wr_t.shape[0]
    assert d == SUBLANES * LANES
    n_i = s // ROW_TILE
    tile = lambda w: pl.BlockSpec((1, ROW_TILE, w), lambda b, i: (b, i, 0))
    const = lambda shape: pl.BlockSpec(shape, lambda b, i: (0, 0))
    return pl.pallas_call(
        _outproj_kernel,
        grid=(bsz, n_i),
        in_specs=[tile(d), tile(d_lru), tile(d_four), const(wo1.shape), const(wo2.shape),
                  const(g2.shape), const(wr_t.shape)],
        out_specs=[tile(d),
                   pl.BlockSpec((ROW_TILE * SUBLANES, LANES), lambda b, i: (b * n_i + i, 0)),
                   pl.BlockSpec((1, n_e, ROW_TILE), lambda b, i: (b, 0, i))],
        out_shape=[
            jax.ShapeDtypeStruct((bsz, s, d), F32),
            jax.ShapeDtypeStruct((bsz * s * SUBLANES, LANES), F32),
            jax.ShapeDtypeStruct((bsz, n_e, s), F32),
        ],
        compiler_params=_cparams(("parallel", "parallel")),
        name="outproj",
    )(x, y_lru, y_four, wo1, wo2, g2, wr_t)


def _select_kernel(aff_ref, tri_ref, pos_ref, starts_ref, *, cap):
    v = aff_ref[0]
    n_e, s = v.shape
    cap_f = float(cap)

    def midpoint(lo, hi):
        mid = 0.5 * (lo + hi)
        return mid, (mid > lo) & (mid < hi)

    def cond(carry):
        _, active = midpoint(*carry)
        return jnp.max(active.astype(F32)) > 0.0

    def body(carry):
        lo, hi = carry
        mid, active = midpoint(lo, hi)
        cnt = jnp.sum((v >= mid).astype(F32), axis=1, keepdims=True)
        enough = cnt >= cap_f
        return (jnp.where(active & enough, mid, lo), jnp.where(active & (~enough), mid, hi))

    lo0 = jnp.zeros((n_e, 1), F32)
    hi0 = jnp.full((n_e, 1), 2.0, F32)
    thr, _ = lax.while_loop(cond, body, (lo0, hi0))

    above = v > thr
    tie = v == thr
    need = cap_f - jnp.sum(above.astype(F32), axis=1, keepdims=True)

    n_tiles = s // TOKEN_TILE
    tri = tri_ref[...]
    lane = lax.broadcasted_iota(jnp.int32, (n_e, LANES), 1)

    def prefix(mask_f, want_starts):
        run = jnp.zeros((n_e, 1), F32)
        starts = jnp.zeros((n_e, LANES), F32)
        pieces = []
        for t in range(n_tiles):
            m = mask_f[:, t * TOKEN_TILE:(t + 1) * TOKEN_TILE]
            incl = jnp.dot(m.astype(BF16), tri, preferred_element_type=F32)
            pieces.append(run + incl - m)
            if want_starts:
                starts = jnp.where(lane == t, run, starts)
            run = run + incl[:, TOKEN_TILE - 1:TOKEN_TILE]
        if want_starts:
            starts = jnp.where(lane == n_tiles, run, starts)
        return jnp.concatenate(pieces, axis=1), starts

    tie_rank, _ = prefix(tie.astype(F32), False)
    sel = above | (tie & (tie_rank < need))
    pos, starts = prefix(sel.astype(F32), True)
    pos_ref[0] = jnp.where(sel, pos, -1.0).astype(jnp.int32)
    starts_ref[0] = starts.astype(jnp.int32)


def _select(aff_t, cap):
    bsz, n_e, s = aff_t.shape
    idx = jnp.arange(TOKEN_TILE, dtype=jnp.int32)
    tri = (idx[:, None] <= idx[None, :]).astype(BF16)
    return pl.pallas_call(
        functools.partial(_select_kernel, cap=cap),
        grid=(bsz,),
        in_specs=[
            pl.BlockSpec((1, n_e, s), lambda b: (b, 0, 0)),
            pl.BlockSpec(tri.shape, lambda b: (0, 0)),
        ],
        out_specs=[
            pl.BlockSpec((1, n_e, s), lambda b: (b, 0, 0)),
            pl.BlockSpec((1, n_e, LANES), lambda b: (b, 0, 0)),
        ],
        out_shape=[
            jax.ShapeDtypeStruct((bsz, n_e, s), jnp.int32),
            jax.ShapeDtypeStruct((bsz, n_e, LANES), jnp.int32),
        ],
        compiler_params=_cparams(("parallel",)),
        name="select",
    )(aff_t, tri)


def _token_table(gate, first_token):
    n_tok, n_e = gate.shape
    lane = lax.broadcasted_iota(jnp.int32, (n_tok, LANES), 1)
    tok = first_token + lax.broadcasted_iota(jnp.int32, (n_tok, LANES), 0)
    table = jnp.where(lane == 0, tok // TOKEN_ID_BASE, jnp.where(lane == 1, tok % TOKEN_ID_BASE, 0))
    table = table.astype(F32)
    place_lane = lax.broadcasted_iota(jnp.int32, (n_e, LANES), 1)
    place_sub = lax.broadcasted_iota(jnp.int32, (n_e, LANES), 0)
    rest = gate
    for j in range(3):
        piece = rest.astype(BF16)
        rest = rest - piece.astype(F32)
        place = jnp.where(place_lane == GATE_COL0 + j * n_e + place_sub, 1.0, 0.0).astype(BF16)
        table = table + jnp.dot(piece, place, preferred_element_type=F32)
    return table.astype(BF16)


def _compact_kernel(starts_ref, pos_ref, gate_ref, idx_ref, gs_ref, r_scr, *, cap):
    b = pl.program_id(0)
    n_e = pos_ref.shape[1]
    n_k = pos_ref.shape[2]
    n_blocks = cap // SLOT_BLOCK
    wide_windows = -(-(TOKEN_TILE + SUBLANES) // COMPACT_WIN)
    r_scr[...] = jnp.zeros_like(r_scr)
    sub = lax.broadcasted_iota(jnp.int32, (COMPACT_WIN, TOKEN_TILE), 0)

    def tile(k, carry):
        rows = pl.ds(pl.multiple_of(k * TOKEN_TILE, TOKEN_TILE), TOKEN_TILE)
        vk = _token_table(gate_ref[0, rows, :], k * TOKEN_TILE)
        starts = []
        fits = None
        for e in range(n_e):
            base = (b * n_e + e) * LANES
            ws = (starts_ref[base + k] // SUBLANES) * SUBLANES
            ok = starts_ref[base + k + 1] - ws <= COMPACT_WIN
            fits = ok if fits is None else fits & ok
            starts.append(ws)

        def place(e, ws):
            hit = pos_ref[0, e, k] - ws == sub
            vals = jnp.dot(jnp.where(hit, 1.0, 0.0).astype(BF16), vk, preferred_element_type=F32)
            r_scr[e, pl.ds(pl.multiple_of(ws, SUBLANES), COMPACT_WIN), :] += vals

        @pl.when(fits)
        def _():
            for e in range(n_e):
                place(e, starts[e])

        @pl.when(jnp.logical_not(fits))
        def _():
            for e in range(n_e):
                for j in range(wide_windows):
                    place(e, starts[e] + j * COMPACT_WIN)

        return carry

    lax.fori_loop(0, n_k, tile, 0)

    lane = lax.broadcasted_iota(jnp.int32, (SUBLANES, LANES), 1)
    sub8 = lax.broadcasted_iota(jnp.int32, (SUBLANES, LANES), 0)
    id_rows = jnp.where(sub8 == 0, jnp.where(lane == 0, float(TOKEN_ID_BASE),
                                             jnp.where(lane == 1, 1.0, 0.0)), 0.0)

    def finish(e, carry):
        gate_lane = (lane >= GATE_COL0) & (lane < GATE_COL0 + 3 * n_e) & ((lane - GATE_COL0) % n_e == e)
        selector = (id_rows + jnp.where((sub8 == 1) & gate_lane, 1.0, 0.0)).astype(BF16)
        for m in range(n_blocks):
            blk = r_scr[e, pl.ds(m * SLOT_BLOCK, SLOT_BLOCK), :].astype(BF16)
            rows = lax.dot_general(selector, blk, (((1,), (1,)), ((), ())),
                                   preferred_element_type=F32)
            idx_ref[0, e, m] = rows[0:1, :].astype(jnp.int32)
            gs_ref[0, e, m] = rows[1:2, :]
        return carry

    lax.fori_loop(0, n_e, finish, 0)


def _compact(starts_flat, pos5, gate_c, cap):
    bsz, n_e, n_k = pos5.shape[:3]
    s = gate_c.shape[1]
    assert s <= TOKEN_ID_BASE * 256 and GATE_COL0 + 3 * n_e <= LANES
    n_blocks = cap // SLOT_BLOCK
    out_block = (1, n_e, n_blocks, 1, SLOT_BLOCK)
    out_spec = pl.BlockSpec(out_block, lambda b, st: (b, 0, 0, 0, 0))
    wide_rows = -(-(TOKEN_TILE + SUBLANES) // COMPACT_WIN) * COMPACT_WIN
    grid_spec = pltpu.PrefetchScalarGridSpec(
        num_scalar_prefetch=1,
        grid=(bsz,),
        in_specs=[pl.BlockSpec((1, n_e, n_k, 1, TOKEN_TILE), lambda b, st: (b, 0, 0, 0, 0)),
                  pl.BlockSpec((1, s, n_e), lambda b, st: (b, 0, 0))],
        out_specs=[out_spec, out_spec],
        scratch_shapes=[pltpu.VMEM((n_e, cap + wide_rows, LANES), F32)],
    )
    return pl.pallas_call(
        functools.partial(_compact_kernel, cap=cap),
        grid_spec=grid_spec,
        out_shape=[jax.ShapeDtypeStruct((bsz,) + out_block[1:], jnp.int32),
                   jax.ShapeDtypeStruct((bsz,) + out_block[1:], F32)],
        compiler_params=_cparams(("parallel",)),
        name="compact",
    )(starts_flat, pos5, gate_c)


def _moe_kernel(idx_ref, idx_next_ref, gs_ref, h2_hbm, wg_ref, wu_ref, wd_ref, eo_ref,
                xbuf, sem, xb_scr, acc_scr, *, seq_len, n_ff):
    b = pl.program_id(0)
    e = pl.program_id(1)
    f = pl.program_id(2)
    n_e = pl.num_programs(1)
    n_blocks = xb_scr.shape[0]
    cap = n_blocks * SLOT_BLOCK
    d = xb_scr.shape[2]
    step = b * n_e + e
    n_steps = pl.num_programs(0) * n_e
    cur = step % 2

    def token_copy(idx_smem, batch, buf, p):
        tok = idx_smem[0, 0, p]
        src = h2_hbm.at[pl.ds(pl.multiple_of((batch * seq_len + tok) * SUBLANES, SUBLANES), SUBLANES)]
        dst = xbuf.at[buf, pl.ds(pl.multiple_of(p * SUBLANES, SUBLANES), SUBLANES)]
        return pltpu.make_async_copy(src, dst, sem.at[buf])

    def request(idx_smem, batch, buf, first, count):
        def body(i, carry):
            for j in range(GATHER_UNROLL):
                token_copy(idx_smem, batch, buf, first + i * GATHER_UNROLL + j).start()
            return carry
        lax.fori_loop(0, count // GATHER_UNROLL, body, 0)

    @pl.when((step == 0) & (f == 0))
    def _():
        request(idx_ref, b, cur, 0, cap)

    next_batch = jnp.where(step + 1 < n_steps, step + 1, 0) // n_e
    per_block = cap // (n_ff * n_blocks)

    @pl.when(f == 0)
    def _():
        pltpu.make_async_copy(h2_hbm.at[pl.ds(0, cap * SUBLANES)], xbuf.at[cur], sem.at[cur]).wait()
        for m in range(n_blocks):
            first = m * SLOT_BLOCK * SUBLANES
            xm = jnp.concatenate(
                [xbuf[cur, pl.ds(first + j, SLOT_BLOCK, stride=SUBLANES), :] for j in range(d // LANES)],
                axis=1)
            xb_scr[m] = xm.astype(BF16)
        acc_scr[...] = jnp.zeros_like(acc_scr)

    wg = wg_ref[0].astype(BF16)
    wu = wu_ref[0].astype(BF16)
    wd = wd_ref[0].astype(BF16)
    for m in range(n_blocks):
        first = (f * n_blocks + m) * per_block
        for j in range(per_block):
            token_copy(idx_next_ref, next_batch, 1 - cur, first + j).start()
        xm = xb_scr[m]
        g = jnp.dot(xm, wg, preferred_element_type=F32)
        u = jnp.dot(xm, wu, preferred_element_type=F32)
        act = (jax.nn.silu(g) * u).astype(BF16)
        acc_scr[m] += jnp.dot(act, wd, preferred_element_type=F32)

    @pl.when((step == n_steps - 1) & (f == n_ff - 1))
    def _():
        pltpu.make_async_copy(h2_hbm.at[pl.ds(0, cap * SUBLANES)], xbuf.at[1 - cur],
                              sem.at[1 - cur]).wait()

    @pl.when(f == n_ff - 1)
    def _():
        eye =(lax.broadcasted_iota(jnp.int32, (SLOT_BLOCK, SLOT_BLOCK), 0)
               == lax.broadcasted_iota(jnp.int32, (SLOT_BLOCK, SLOT_BLOCK), 1))
        for m in range(n_blocks):
            gcol = jnp.sum(jnp.where(eye, gs_ref[0, m], 0.0), axis=1, keepdims=True)
            rows = pl.ds(m * SLOT_BLOCK, SLOT_BLOCK)
            eo_ref[0, 0, rows, :] = (gcol * acc_scr[m]).astype(eo_ref.dtype)


def _moe(idx, gs, h2_tiles, wg, wu, wd, bsz, s):
    n_e, d, d_ff = wg.shape
    n_blocks = gs.shape[1]
    cap = n_blocks * SLOT_BLOCK
    n_ff = d_ff // FF_CHUNK
    assert d_ff % FF_CHUNK == 0 and cap % (n_ff * n_blocks) == 0 and cap % GATHER_UNROLL == 0
    assert d == SUBLANES * LANES

    def next_step(b, e, f):
        return ((b * n_e + e + 1) % (bsz * n_e), 0, 0)

    smem = pltpu.SMEM
    return pl.pallas_call(
        functools.partial(_moe_kernel, seq_len=s, n_ff=n_ff),
        grid=(bsz, n_e, n_ff),
        in_specs=[
            pl.BlockSpec((1, 1, cap), lambda b, e, f: (b * n_e + e, 0, 0), memory_space=smem),
            pl.BlockSpec((1, 1, cap), next_step, memory_space=smem),
            pl.BlockSpec((1, n_blocks, 1, SLOT_BLOCK), lambda b, e, f: (b * n_e + e, 0, 0, 0)),
            pl.BlockSpec(memory_space=pl.ANY),
            pl.BlockSpec((1, d, FF_CHUNK), lambda b, e, f: (e, 0, f)),
            pl.BlockSpec((1, d, FF_CHUNK), lambda b, e, f: (e, 0, f)),
            pl.BlockSpec((1, FF_CHUNK, d), lambda b, e, f: (e, f, 0)),
        ],
        out_specs=pl.BlockSpec((1, 1, cap, d), lambda b, e, f: (b, e, 0, 0)),
        out_shape=jax.ShapeDtypeStruct((bsz, n_e, cap, d), BF16),
        scratch_shapes=[pltpu.VMEM((2, cap * SUBLANES, LANES), F32),
                        pltpu.SemaphoreType.DMA((2,)),
                        pltpu.VMEM((n_blocks, SLOT_BLOCK, d), BF16),
                        pltpu.VMEM((n_blocks, SLOT_BLOCK, d), F32)],
        compiler_params=_cparams(("arbitrary", "arbitrary", "arbitrary")),
        name="moe",
    )(idx, idx, gs, h2_tiles, wg, wu, wd)


def _combine_kernel(starts_ref, x1_ref, pos_ref, gf_ref, eo_ref, out_ref, wcat_scr, acc_scr,
                    *, cap):
    for i in range(x1_ref.shape[1] // TOKEN_TILE):
        _combine_tile(starts_ref, x1_ref, pos_ref, gf_ref, eo_ref, out_ref, wcat_scr, acc_scr,
                      pl.program_id(1) * (x1_ref.shape[1] // TOKEN_TILE) + i,
                      pl.ds(i * TOKEN_TILE, TOKEN_TILE), cap)


def _combine_tile(starts_ref, x1_ref, pos_ref, gf_ref, eo_ref, out_ref, wcat_scr, acc_scr,
                  k, rows, cap):
    b = pl.program_id(0)
    n_e = pos_ref.shape[2]
    pos = pos_ref[0, rows, :]
    expert_lane = lax.broadcasted_iota(jnp.int32, (1, n_e), 1)

    s0 = [starts_ref[(b * n_e + e) * LANES + k] for e in range(n_e)]
    s1 = [starts_ref[(b * n_e + e) * LANES + k + 1] for e in range(n_e)]

    def window_starts(rows):
        starts = [jnp.minimum(s // WIN_ALIGN, (cap - rows) // WIN_ALIGN) * WIN_ALIGN for s in s0]
        vec = jnp.zeros((1, n_e), jnp.int32)
        for e in range(n_e):
            vec = jnp.where(expert_lane == e, starts[e], vec)
        return starts, vec

    def onehot(rel, first, count, width):
        n_lanes = count * width
        lane = lax.broadcasted_iota(jnp.int32, (n_e, n_lanes), 1)
        sub = lax.broadcasted_iota(jnp.int32, (n_e, n_lanes), 0)
        expand = jnp.where(sub == first + lane // width, 1.0, 0.0).astype(BF16)
        spread = jnp.dot(jnp.clip(rel, -1, width).astype(F32).astype(BF16), expand,
                         preferred_element_type=F32)
        want = (lax.broadcasted_iota(jnp.int32, (TOKEN_TILE, n_lanes), 1) % width).astype(F32)
        return jnp.where(spread == want, 1.0, 0.0).astype(BF16)

    def finish(acc):
        out_ref[0, rows, :] = _rms_scale(acc) * gf_ref[...]

    narrow, narrow_vec = window_starts(NARROW_WIN)
    fits = None
    for e in range(n_e):
        ok = s1[e] - narrow[e] <= NARROW_WIN
        fits = ok if fits is None else fits & ok

    @pl.when(fits)
    def _():
        for e in range(n_e):
            src = pl.ds(pl.multiple_of(narrow[e], WIN_ALIGN), NARROW_WIN)
            wcat_scr[pl.ds(e * NARROW_WIN, NARROW_WIN), :] = eo_ref[0, e, src, :]
        hit = onehot(pos - narrow_vec, 0, n_e, NARROW_WIN)
        finish(x1_ref[0, rows, :] + jnp.dot(hit, wcat_scr[pl.ds(0, n_e * NARROW_WIN), :],
                                            preferred_element_type=F32))

    @pl.when(jnp.logical_not(fits))
    def _():
        wide, wide_vec = window_starts(SLOT_BLOCK + WIN_ALIGN)
        rel = pos - wide_vec
        group = wcat_scr.shape[0] // SLOT_BLOCK
        acc_scr[...] = x1_ref[0, rows, :]
        for g in range(n_e // group):
            for i in range(group):
                e = g * group + i
                src = pl.ds(pl.multiple_of(wide[e], WIN_ALIGN), SLOT_BLOCK)
                wcat_scr[pl.ds(i * SLOT_BLOCK, SLOT_BLOCK), :] = eo_ref[0, e, src, :]
            acc_scr[...] += jnp.dot(onehot(rel, g * group, group, SLOT_BLOCK), wcat_scr[...],
                                    preferred_element_type=F32)
        for e in range(n_e):
            src = pl.ds(pl.multiple_of(wide[e] + SLOT_BLOCK, WIN_ALIGN), WIN_ALIGN)
            wcat_scr[pl.ds(e * WIN_ALIGN, WIN_ALIGN), :] = eo_ref[0, e, src, :]
        tail = onehot(rel - SLOT_BLOCK, 0, n_e, WIN_ALIGN)
        finish(acc_scr[...] + jnp.dot(tail, wcat_scr[pl.ds(0, n_e * WIN_ALIGN), :],
                                      preferred_element_type=F32))


def _combine(starts_flat, x1, pos_c, eo, gf):
    bsz, s, d = x1.shape
    n_e = pos_c.shape[-1]
    cap = eo.shape[2]
    step_rows = COMBINE_TILES * TOKEN_TILE
    wide_group = 4
    assert n_e % wide_group == 0 and n_e * NARROW_WIN <= wide_group * SLOT_BLOCK
    assert s % step_rows == 0
    grid_spec = pltpu.PrefetchScalarGridSpec(
        num_scalar_prefetch=1,
        grid=(bsz, s // step_rows),
        in_specs=[
            pl.BlockSpec((1, step_rows, d), lambda b, k, st: (b, k, 0)),
            pl.BlockSpec((1, step_rows, n_e), lambda b, k, st: (b, k, 0)),
            pl.BlockSpec((1, d), lambda b, k, st: (0, 0)),
            pl.BlockSpec((1, n_e, cap, d), lambda b, k, st: (b, 0, 0, 0),
                         pipeline_mode=pl.Buffered(1)),
        ],
        out_specs=pl.BlockSpec((1, step_rows, d), lambda b, k, st: (b, k, 0)),
        scratch_shapes=[pltpu.VMEM((wide_group * SLOT_BLOCK, d), BF16),
                        pltpu.VMEM((TOKEN_TILE, d), F32)],
    )
    return pl.pallas_call(
        functools.partial(_combine_kernel, cap=cap),
        grid_spec=grid_spec,
        out_shape=jax.ShapeDtypeStruct((bsz, s, d), F32),
        compiler_params=_cparams(("parallel", "parallel")),
        name="combine",
    )(starts_flat, x1, pos_c, gf, eo)


def _block_diag(w):
    h, hd, _ = w.shape
    eye = jnp.eye(h, dtype=w.dtype)
    return (eye[:, None, :, None] * w[:, :, None, :]).reshape(h * hd, h * hd)


def _lru_params(wa_f, wx_f, wa_b, wx_b, ba_f, bx_f, ba_b, bx_b, lam_f, lam_b):
    d_lru = ba_f.shape[0]
    n_blk = d_lru // LANES
    mats = [0.5 * _block_diag(w) for w in (wa_f, wx_f, wa_b, wx_b)]
    ba_f, bx_f, ba_b, bx_b = (0.5 * v for v in (ba_f, bx_f, ba_b, bx_b))
    w_cat = jnp.stack([
        jnp.concatenate([m[c * LANES:(c + 1) * LANES, c * LANES:(c + 1) * LANES] for m in mats], axis=1)
        for c in range(n_blk)]).astype(BF16)
    b_cat = jnp.stack([
        jnp.concatenate([v[c * LANES:(c + 1) * LANES] for v in (ba_f, bx_f, ba_b, bx_b)])
        for c in range(n_blk)])[:, None, :]
    lam_cat = jnp.stack([
        jnp.concatenate([v[c * LANES:(c + 1) * LANES] for v in (lam_f, lam_b)])
        for c in range(n_blk)])[:, None, :]
    return w_cat, b_cat, lam_cat


def kernel(x, norm1_g, w_in, conv_w, conv_b, lru_wa_f, lru_ba_f, lru_wx_f, lru_bx_f, lru_lam_f,
           lru_wa_b, lru_ba_b, lru_wx_b, lru_bx_b, lru_lam_b, w_out, norm2_g, w_router,
           w_gate, w_up, w_down, normf_g):
    bsz, s, d = x.shape
    d_lru = conv_b.shape[0]
    d_four = w_in.shape[1] - 2 * d_lru
    n_e = w_router.shape[1]
    cap = CAPACITY_FACTOR * s // n_e
    assert s % LRU_CHUNK == 0 and s % (SCAN_GROUPS * SUBLANES * SUBLANES) == 0
    assert s % FOURIER_N2 == 0 and cap % SLOT_BLOCK == 0 and s // TOKEN_TILE < LANES
    assert cap >= WIN_ROWS and w_gate.shape[2] % FF_CHUNK == 0

    lx, lg, fo = _inproj(x.reshape(bsz * s, d), norm1_g[None, :], w_in.astype(BF16), d_lru, d_four)
    lx = lx.reshape(bsz, s, d_lru)
    lg = lg.reshape(bsz, s, d_lru)
    fo = fo.reshape(bsz, s, d_four)

    w_cat, b_cat, lam_cat = _lru_params(lru_wa_f, lru_wx_f, lru_wa_b, lru_wx_b,
                                        lru_ba_f, lru_bx_f, lru_ba_b, lru_bx_b,
                                        lru_lam_f, lru_lam_b)
    y_lru = _lru(lx, lg, conv_w, conv_b[None, :], w_cat, b_cat, lam_cat)
    y_four = _fourier(fo, FOURIER_GROUPS)

    w_out_bf = w_out.astype(BF16)
    x1, h2_tiles, aff_t = _outproj(x, y_lru, y_four, w_out_bf[:d_lru], w_out_bf[d_lru:],
                                   norm2_g[None, :], w_router.T)

    pos_r, starts = _select(aff_t, cap)
    starts_flat = starts.reshape(-1)
    pos_c = jnp.transpose(pos_r, (0, 2, 1))
    pos5 = pos_r.reshape(bsz, n_e, s // TOKEN_TILE, 1, TOKEN_TILE)
    idx, gs = _compact(starts_flat, pos5, jnp.transpose(aff_t, (0, 2, 1)), cap)
    eo = _moe(idx.reshape(bsz * n_e, 1, cap), gs.reshape((bsz * n_e,) + gs.shape[2:]), h2_tiles,
              w_gate, w_up, w_down, bsz, s)
    return _combine(starts_flat, x1, pos_c, eo, normf_g[None, :])
```

```python
import functools
import math

import jax
import jax.numpy as jnp
import numpy as np
from jax import lax
from jax.experimental import pallas as pl
from jax.experimental.pallas import tpu as pltpu

F32 = jnp.float32
BF16 = jnp.bfloat16

EPS = 1e-6
LRU_C = 8.0
CAPACITY_FACTOR = 2
FOURIER_GROUPS = 4

SUBLANES = 8
LANES = 128
MXU_DIM = 256
VMEM_LIMIT_BYTES = 56 * 1024 * 1024

ROW_TILE = 1024
LRU_CHUNK = 512
SCAN_GROUPS = 4
SLOT_BLOCK = MXU_DIM
TOKEN_TILE = MXU_DIM
SEG_PAD = 4
FOURIER_N2 = LANES
FOURIER_K1_BATCH = 8
STAGE_PAD = 4
FF_CHUNK = 1024
GATHER_UNROLL = 8
COMPACT_WIN = 64
TOKEN_ID_BASE = 64
GATE_COL0 = 16
WIN_ALIGN = 16
WIN_ROWS = SLOT_BLOCK + WIN_ALIGN
NARROW_WIN = 64
COMBINE_TILES = 2


def _cparams(semantics):
    return pltpu.CompilerParams(dimension_semantics=semantics,
                                vmem_limit_bytes=VMEM_LIMIT_BYTES)


def _rms_scale(x):
    return x * lax.rsqrt(jnp.mean(x * x, axis=-1, keepdims=True) + EPS)


def _inproj_kernel(x_ref, g_ref, w_ref, lx_ref, lg_ref, fo_ref):
    h = _rms_scale(x_ref[...]) * g_ref[...]
    p = jnp.dot(h.astype(BF16), w_ref[...], preferred_element_type=F32)
    d = lx_ref.shape[-1]
    lx_ref[...] = p[:, :d]
    lg_ref[...] = p[:, d:2 * d]
    fo_ref[...] = p[:, 2 * d:]


def _inproj(x2, g, w_bf, d_lru, d_four):
    m, d = x2.shape
    n = w_bf.shape[1]
    return pl.pallas_call(
        _inproj_kernel,
        grid=(m // ROW_TILE,),
        in_specs=[
            pl.BlockSpec((ROW_TILE, d), lambda i: (i, 0)),
            pl.BlockSpec((1, d), lambda i: (0, 0)),
            pl.BlockSpec((d, n), lambda i: (0, 0)),
        ],
        out_specs=[
            pl.BlockSpec((ROW_TILE, d_lru), lambda i: (i, 0)),
            pl.BlockSpec((ROW_TILE, d_lru), lambda i: (i, 0)),
            pl.BlockSpec((ROW_TILE, d_four), lambda i: (i, 0)),
        ],
        out_shape=[
            jax.ShapeDtypeStruct((m, d_lru), F32),
            jax.ShapeDtypeStruct((m, d_lru), F32),
            jax.ShapeDtypeStruct((m, d_four), F32),
        ],
        compiler_params=_cparams(("parallel",)),
        name="inproj",
    )(x2, g, w_bf)


def _shift_rows(x, shift):
    n = x.shape[0]
    rows = lax.broadcasted_iota(jnp.int32, x.shape, 0)
    rolled = pltpu.roll(x, shift % n, axis=0)
    keep = (rows >= shift) if shift > 0 else (rows < n + shift)
    return jnp.where(keep, rolled, 0.0)


def _lru_kernel(lx_ref, lg_ref, cw_ref, cb_ref, w_ref, b_ref, lam_ref, y_ref,
                xpad, af, bf, ab, bb):
    s = lx_ref.shape[1]
    c_blk = lx_ref.shape[2]
    n_seg = SCAN_GROUPS * SUBLANES
    seg = s // n_seg
    seg_stride = seg + SEG_PAD
    piece = min(seg, LRU_CHUNK)
    n_chunks = s // LRU_CHUNK
    pad = SUBLANES

    zeros_pad = jnp.zeros((pad, c_blk), F32)
    xpad[pl.ds(0, pad), :] = zeros_pad
    xpad[pl.ds(pad + s, pad), :] = zeros_pad

    def copy_chunk(i, carry):
        t0 = pl.multiple_of(i * LRU_CHUNK, LRU_CHUNK)
        xpad[pl.ds(pad + t0, LRU_CHUNK), :] = lx_ref[0, pl.ds(t0, LRU_CHUNK), :]
        return carry

    lax.fori_loop(0, n_chunks, copy_chunk, 0)

    cw = cw_ref[...]
    conv_width = cw.shape[0]
    cb = cb_ref[...]
    bias = b_ref[0]
    quarter_c_log_sig = (0.25 * LRU_C) * jax.nn.log_sigmoid(lam_ref[0])
    w_cat = w_ref[0]

    def gates_chunk(i, carry):
        t0 = pl.multiple_of(i * LRU_CHUNK, LRU_CHUNK)
        c = cb
        left = conv_width // 2
        for k in range(conv_width):
            c = c + cw[k:k + 1, :] * xpad[pl.ds(t0 + (pad - left + k), LRU_CHUNK), :]
        z = jnp.dot(c.astype(BF16), w_cat, preferred_element_type=F32) + bias
        for d, (a_scr, b_scr) in enumerate(((af, bf), (ab, bb))):
            q = quarter_c_log_sig[:, d * c_blk:(d + 1) * c_blk]
            th_r = jnp.tanh(z[:, (2 * d) * c_blk:(2 * d + 1) * c_blk])
            th_i = jnp.tanh(z[:, (2 * d + 1) * c_blk:(2 * d + 2) * c_blk])
            t = jnp.tanh(q + q * th_r)
            inv = 1.0 / (1.0 - t)
            a = (1.0 + t) * inv
            bt = (jnp.sqrt(-t) * inv) * (c + c * th_i)
            for p in range(LRU_CHUNK // piece):
                t = t0 + p * piece
                dst = pl.ds((t // seg) * seg_stride + t % seg, piece)
                a_scr[dst, :] = a[p * piece:(p + 1) * piece]
                b_scr[dst, :] = bt[p * piece:(p + 1) * piece]
        return carry

    lax.fori_loop(0, n_chunks, gates_chunk, 0)

    def scan_step(i, carry):
        out = []
        for (a_scr, b_scr, o), (hs, ps) in zip(((af, bf, i), (ab, bb, seg - 1 - i)), carry):
            new_h, new_p = [], []
            for q in range(SCAN_GROUPS):
                rows = pl.ds(q * SUBLANES * seg_stride + o, SUBLANES, stride=seg_stride)
                a = a_scr[rows, :]
                h = a * hs[q] + b_scr[rows, :]
                p = a * ps[q]
                b_scr[rows, :] = h
                a_scr[rows, :] = p
                new_h.append(h)
                new_p.append(p)
            out.append((tuple(new_h), tuple(new_p)))
        return tuple(out)

    zero = jnp.zeros((SUBLANES, c_blk), F32)
    one = jnp.ones((SUBLANES, c_blk), F32)
    init = ((zero,) * SCAN_GROUPS, (one,) * SCAN_GROUPS)
    (hf_end, pf_end), (hb_end, pb_end) = lax.fori_loop(0, seg, scan_step, (init, init))

    row = lax.broadcasted_iota(jnp.int32, (SUBLANES, c_blk), 0)

    def entering(h_end, p_end, carry_in, first_row, shift):
        c = zero
        for _ in range(SUBLANES):
            c = jnp.where(row == first_row, carry_in, _shift_rows(h_end + p_end * c, shift))
        return c

    cf = [None] * SCAN_GROUPS
    cbk = [None] * SCAN_GROUPS
    carry_f = jnp.zeros((1, c_blk), F32)
    carry_b = jnp.zeros((1, c_blk), F32)
    for q in range(SCAN_GROUPS):
        cf[q] = entering(hf_end[q], pf_end[q], carry_f, 0, 1)
        carry_f = (hf_end[q] + pf_end[q] * cf[q])[SUBLANES - 1:SUBLANES, :]
        qb = SCAN_GROUPS - 1 - q
        cbk[qb] = entering(hb_end[qb], pb_end[qb], carry_b, SUBLANES - 1, -1)
        carry_b = (hb_end[qb] + pb_end[qb] * cbk[qb])[0:1, :]

    for j in range(n_seg):
        q, i = divmod(j, SUBLANES)
        src = pl.ds(j * seg_stride, seg)
        rows = pl.ds(j * seg, seg)
        hsum = ((bf[src, :] + af[src, :] * cf[q][i:i + 1, :])
                + (bb[src, :] + ab[src, :] * cbk[q][i:i + 1, :]))
        y_ref[0, rows, :] = (jax.nn.gelu(lg_ref[0, rows, :]) * hsum).astype(y_ref.dtype)


def _lru(lx, lg, conv_w, conv_b, w_cat, b_cat, lam_cat):
    bsz, s, d_lru = lx.shape
    c_blk = LANES
    n_blk = d_lru // c_blk
    scr = pltpu.VMEM((s + SCAN_GROUPS * SUBLANES * SEG_PAD, c_blk), F32)
    return pl.pallas_call(
        _lru_kernel,
        grid=(bsz, n_blk),
        in_specs=[
            pl.BlockSpec((1, s, c_blk), lambda b, c: (b, 0, c)),
            pl.BlockSpec((1, s, c_blk), lambda b, c: (b, 0, c)),
            pl.BlockSpec((4, c_blk), lambda b, c: (0, c)),
            pl.BlockSpec((1, c_blk), lambda b, c: (0, c)),
            pl.BlockSpec((1, c_blk, 4 * c_blk), lambda b, c: (c, 0, 0)),
            pl.BlockSpec((1, 1, 4 * c_blk), lambda b, c: (c, 0, 0)),
            pl.BlockSpec((1, 1, 2 * c_blk), lambda b, c: (c, 0, 0)),
        ],
        out_specs=pl.BlockSpec((1, s, c_blk), lambda b, c: (b, 0, c)),
        out_shape=jax.ShapeDtypeStruct((bsz, s, d_lru), BF16),
        scratch_shapes=[pltpu.VMEM((s + 2 * SUBLANES, c_blk), F32), scr, scr, scr, scr],
        compiler_params=_cparams(("parallel", "parallel")),
        name="lru",
    )(lx, lg, conv_w, conv_b, w_cat, b_cat, lam_cat)


def _fourier_kernel(f_ref, ka_ref, kb_ref, cc_ref, sc_ref, twc_ref, tws_ref, y_ref,
                    ar_scr, ai_scr, y_scr, *, scale):
    n1 = f_ref.shape[1]
    n_u = f_ref.shape[2]
    cols = f_ref.shape[4]
    r = SUBLANES * n1
    n2 = n_u * SUBLANES
    stage_stride = n1 + STAGE_PAD

    def stage_a(i, carry):
        us = (2 * i, 2 * i + 1)
        xu = jnp.concatenate([f_ref[0, :, u].reshape(r, cols) for u in us], axis=1)
        a = jnp.dot(ka_ref[...], xu.astype(BF16), preferred_element_type=F32)
        for j, u in enumerate(us):
            a_re = a[:r, j * cols:(j + 1) * cols]
            a_im = a[r:, j * cols:(j + 1) * cols]
            c = twc_ref[u]
            sn = tws_ref[u]
            ar_scr[u] = a_re * c - a_im * sn
            ai_scr[u] = a_re * sn + a_im * c
        return carry

    lax.fori_loop(0, n_u // 2, stage_a, 0)

    nb = FOURIER_K1_BATCH

    def stage_b(i, carry):
        rows = pl.ds(pl.multiple_of(i * (nb * SUBLANES), nb * SUBLANES), nb * SUBLANES)
        a_re = ar_scr[:, rows, :]
        a_im = ai_scr[:, rows, :]

        def rows_s2(a, j):
            return a[:, j * SUBLANES:(j + 1) * SUBLANES, :].reshape(n2, cols)

        st = jnp.concatenate(
            [jnp.concatenate([rows_s2(a_re, j), rows_s2(a_im, j)], axis=0) for j in range(nb)],
            axis=1).astype(BF16)
        x = jnp.dot(kb_ref[...], st, preferred_element_type=F32)
        x_re = jnp.concatenate([x[:n2, j * cols:(j + 1) * cols] for j in range(nb)], axis=0)
        x_im = jnp.concatenate([x[n2:, j * cols:(j + 1) * cols] for j in range(nb)], axis=0)
        y = (jnp.dot(x_re.astype(BF16), cc_ref[...], preferred_element_type=F32)
             + jnp.dot(x_im.astype(BF16), sc_ref[...], preferred_element_type=F32)) * scale
        for j in range(nb):
            y_scr[pl.ds(i * nb + j, n2, stride=stage_stride), :] = y[j * n2:(j + 1) * n2]
        return carry

    lax.fori_loop(0, n1 // nb, stage_b, 0)

    def copy_out(k2, carry):
        y_ref[0, pl.ds(pl.multiple_of(k2 * n1, n1), n1), :] = y_scr[pl.ds(k2 * stage_stride, n1), :]
        return carry

    lax.fori_loop(0, n2, copy_out, 0)


def _dft_tables(s, c_grp):
    n2 = FOURIER_N2
    n1 = s // n2
    n_u = n2 // SUBLANES

    def cos_sin(num, den):
        ang = (2.0 * np.pi / den) * (num % den).astype(np.float64)
        return np.cos(ang), np.sin(ang)

    def const(a, dtype=F32):
        return jnp.asarray(a.astype(np.float32)).astype(dtype)

    i1 = np.arange(n1, dtype=np.int64)
    c1, s1 = cos_sin(i1[:, None] * i1[None, :], n1)
    eye8 = np.eye(SUBLANES)
    ka = const(np.concatenate([np.kron(c1, eye8), np.kron(-s1, eye8)], axis=0), BF16)

    i2 = np.arange(n2, dtype=np.int64)
    c2, s2 = cos_sin(i2[:, None] * i2[None, :], n2)
    kb = const(np.block([[c2, s2], [-s2, c2]]), BF16)

    ic = np.arange(c_grp, dtype=np.int64)
    cc, sc = cos_sin(ic[:, None] * ic[None, :], c_grp)

    u = np.arange(n_u, dtype=np.int64)[:, None, None]
    k1 = np.arange(n1, dtype=np.int64)[None, :, None]
    v = np.arange(SUBLANES, dtype=np.int64)[None, None, :]
    tc, ts = cos_sin(k1 * (SUBLANES * u + v), s)
    shape = (n_u, n1 * SUBLANES, c_grp)
    twc = jnp.broadcast_to(const(tc.reshape(n_u, -1, 1)), shape)
    tws = jnp.broadcast_to(const(-ts.reshape(n_u, -1, 1)), shape)
    return ka, kb, const(cc, BF16), const(sc, BF16), twc, tws


def _fourier(four, n_groups):
    bsz, s, d_four = four.shape
    c_grp = d_four // n_groups
    n2 = FOURIER_N2
    n1 = s // n2
    n_u = n2 // SUBLANES
    r = SUBLANES * n1
    ka, kb, cc, sc, twc, tws = _dft_tables(s, c_grp)
    f5 = four.reshape(bsz, n1, n_u, SUBLANES, d_four)
    scale = 1.0 / math.sqrt(s * c_grp)
    const2 = lambda b, g: (0, 0)
    const3 = lambda b, g: (0, 0, 0)
    return pl.pallas_call(
        functools.partial(_fourier_kernel, scale=scale),
        grid=(bsz, n_groups),
        in_specs=[
            pl.BlockSpec((1, n1, n_u, SUBLANES, c_grp), lambda b, g: (b, 0, 0, 0, g)),
            pl.BlockSpec(ka.shape, const2),
            pl.BlockSpec(kb.shape, const2),
            pl.BlockSpec(cc.shape, const2),
            pl.BlockSpec(sc.shape, const2),
            pl.BlockSpec(twc.shape, const3),
            pl.BlockSpec(tws.shape, const3),
        ],
        out_specs=pl.BlockSpec((1, s, c_grp), lambda b, g: (b, 0, g)),
        out_shape=jax.ShapeDtypeStruct((bsz, s, d_four), F32),
        scratch_shapes=[pltpu.VMEM((n_u, r, c_grp), F32), pltpu.VMEM((n_u, r, c_grp), F32),
                        pltpu.VMEM((n2 * (n1 + STAGE_PAD), c_grp), F32)],
        compiler_params=_cparams(("parallel", "parallel")),
        name="fourier",
    )(f5, ka, kb, cc, sc, twc, tws)


def _outproj_kernel(x_ref, yl_ref, yf_ref, wo1_ref, wo2_ref, g2_ref, wr_ref,
                    x1_ref, h2_ref, aff_ref):
    x1 = (x_ref[0]
          + jnp.dot(yl_ref[0], wo1_ref[...], preferred_element_type=F32)
          + jnp.dot(yf_ref[0].astype(BF16), wo2_ref[...], preferred_element_type=F32))
    x1_ref[0] = x1
    h2 = _rms_scale(x1) * g2_ref[...]
    for j in range(h2.shape[1] // LANES):
        h2_ref[pl.ds(j, h2.shape[0], stride=SUBLANES), :] = h2[:, j * LANES:(j + 1) * LANES]
    def split(v):
        hi = v.astype(BF16)
        return hi, (v - hi.astype(F32)).astype(BF16)

    def dot_nt(a, bm):
        return lax.dot_general(a, bm, (((1,), (1,)), ((), ())), preferred_element_type=F32)

    w_hi, w_lo = split(wr_ref[...])
    h_hi, h_lo = split(h2)
    logits = dot_nt(w_hi, h_hi) + (dot_nt(w_lo, h_hi) + dot_nt(w_hi, h_lo))
    ex = jnp.exp(logits - jnp.max(logits, axis=0, keepdims=True))
    aff_ref[0] = ex / jnp.sum(ex, axis=0, keepdims=True)


def _outproj(x, y_lru, y_four, wo1, wo2, g2, wr_t):
    bsz, s, d = x.shape
    d_lru = y_lru.shape[-1]
    d_four = y_four.shape[-1]
    n_e = wr_t.shape[0]
    assert d == SUBLANES * LANES
    n_i = s // ROW_TILE
    tile = lambda w: pl.BlockSpec((1, ROW_TILE, w), lambda b, i: (b, i, 0))
    const = lambda shape: pl.BlockSpec(shape, lambda b, i: (0, 0))
    return pl.pallas_call(
        _outproj_kernel,
        grid=(bsz, n_i),
        in_specs=[tile(d), tile(d_lru), tile(d_four), const(wo1.shape), const(wo2.shape),
                  const(g2.shape), const(wr_t.shape)],
        out_specs=[tile(d),
                   pl.BlockSpec((ROW_TILE * SUBLANES, LANES), lambda b, i: (b * n_i + i, 0)),
                   pl.BlockSpec((1, n_e, ROW_TILE), lambda b, i: (b, 0, i))],
        out_shape=[
            jax.ShapeDtypeStruct((bsz, s, d), F32),
            jax.ShapeDtypeStruct((bsz * s * SUBLANES, LANES), F32),
            jax.ShapeDtypeStruct((bsz, n_e, s), F32),
        ],
        compiler_params=_cparams(("parallel", "parallel")),
        name="outproj",
    )(x, y_lru, y_four, wo1, wo2, g2, wr_t)


def _select_kernel(aff_ref, tri_ref, pos_ref, starts_ref, *, cap):
    v = aff_ref[0]
    n_e, s = v.shape
    cap_f = float(cap)

    def midpoint(lo, hi):
        mid = 0.5 * (lo + hi)
        return mid, (mid > lo) & (mid < hi)

    def cond(carry):
        _, active = midpoint(*carry)
        return jnp.max(active.astype(F32)) > 0.0

    def body(carry):
        lo, hi = carry
        mid, active = midpoint(lo, hi)
        cnt = jnp.sum((v >= mid).astype(F32), axis=1, keepdims=True)
        enough = cnt >= cap_f
        return (jnp.where(active & enough, mid, lo), jnp.where(active & (~enough), mid, hi))

    lo0 = jnp.zeros((n_e, 1), F32)
    hi0 = jnp.full((n_e, 1), 2.0, F32)
    thr, _ = lax.while_loop(cond, body, (lo0, hi0))

    above = v > thr
    tie = v == thr
    need = cap_f - jnp.sum(above.astype(F32), axis=1, keepdims=True)

    n_tiles = s // TOKEN_TILE
    tri = tri_ref[...]
    lane = lax.broadcasted_iota(jnp.int32, (n_e, LANES), 1)

    def prefix(mask_f, want_starts):
        run = jnp.zeros((n_e, 1), F32)
        starts = jnp.zeros((n_e, LANES), F32)
        pieces = []
        for t in range(n_tiles):
            m = mask_f[:, t * TOKEN_TILE:(t + 1) * TOKEN_TILE]
            incl = jnp.dot(m.astype(BF16), tri, preferred_element_type=F32)
            pieces.append(run + incl - m)
            if want_starts:
                starts = jnp.where(lane == t, run, starts)
            run = run + incl[:, TOKEN_TILE - 1:TOKEN_TILE]
        if want_starts:
            starts = jnp.where(lane == n_tiles, run, starts)
        return jnp.concatenate(pieces, axis=1), starts

    tie_rank, _ = prefix(tie.astype(F32), False)
    sel = above | (tie & (tie_rank < need))
    pos, starts = prefix(sel.astype(F32), True)
    pos_ref[0] = jnp.where(sel, pos, -1.0).astype(jnp.int32)
    starts_ref[0] = starts.astype(jnp.int32)


def _select(aff_t, cap):
    bsz, n_e, s = aff_t.shape
    idx = jnp.arange(TOKEN_TILE, dtype=jnp.int32)
    tri = (idx[:, None] <= idx[None, :]).astype(BF16)
    return pl.pallas_call(
        functools.partial(_select_kernel, cap=cap),
        grid=(bsz,),
        in_specs=[
            pl.BlockSpec((1, n_e, s), lambda b: (b, 0, 0)),
            pl.BlockSpec(tri.shape, lambda b: (0, 0)),
        ],
        out_specs=[
            pl.BlockSpec((1, n_e, s), lambda b: (b, 0, 0)),
            pl.BlockSpec((1, n_e, LANES), lambda b: (b, 0, 0)),
        ],
        out_shape=[
            jax.ShapeDtypeStruct((bsz, n_e, s), jnp.int32),
            jax.ShapeDtypeStruct((bsz, n_e, LANES), jnp.int32),
        ],
        compiler_params=_cparams(("parallel",)),
        name="select",
    )(aff_t, tri)


def _token_table(gate, first_token):
    n_tok, n_e = gate.shape
    lane = lax.broadcasted_iota(jnp.int32, (n_tok, LANES), 1)
    tok = first_token + lax.broadcasted_iota(jnp.int32, (n_tok, LANES), 0)
    table = jnp.where(lane == 0, tok // TOKEN_ID_BASE, jnp.where(lane == 1, tok % TOKEN_ID_BASE, 0))
    table = table.astype(F32)
    place_lane = lax.broadcasted_iota(jnp.int32, (n_e, LANES), 1)
    place_sub = lax.broadcasted_iota(jnp.int32, (n_e, LANES), 0)
    rest = gate
    for j in range(3):
        piece = rest.astype(BF16)
        rest = rest - piece.astype(F32)
        place = jnp.where(place_lane == GATE_COL0 + j * n_e + place_sub, 1.0, 0.0).astype(BF16)
        table = table + jnp.dot(piece, place, preferred_element_type=F32)
    return table.astype(BF16)


def _compact_kernel(starts_ref, pos_ref, gate_ref, idx_ref, gs_ref, r_scr, *, cap):
    b = pl.program_id(0)
    n_e = pos_ref.shape[1]
    n_k = pos_ref.shape[2]
    n_blocks = cap // SLOT_BLOCK
    wide_windows = -(-(TOKEN_TILE + SUBLANES) // COMPACT_WIN)
    r_scr[...] = jnp.zeros_like(r_scr)
    sub = lax.broadcasted_iota(jnp.int32, (COMPACT_WIN, TOKEN_TILE), 0)

    def tile(k, carry):
        rows = pl.ds(pl.multiple_of(k * TOKEN_TILE, TOKEN_TILE), TOKEN_TILE)
        vk = _token_table(gate_ref[0, rows, :], k * TOKEN_TILE)
        starts = []
        fits = None
        for e in range(n_e):
            base = (b * n_e + e) * LANES
            ws = (starts_ref[base + k] // SUBLANES) * SUBLANES
            ok = starts_ref[base + k + 1] - ws <= COMPACT_WIN
            fits = ok if fits is None else fits & ok
            starts.append(ws)

        def place(e, ws):
            hit = pos_ref[0, e, k] - ws == sub
            vals = jnp.dot(jnp.where(hit, 1.0, 0.0).astype(BF16), vk, preferred_element_type=F32)
            r_scr[e, pl.ds(pl.multiple_of(ws, SUBLANES), COMPACT_WIN), :] += vals

        @pl.when(fits)
        def _():
            for e in range(n_e):
                place(e, starts[e])

        @pl.when(jnp.logical_not(fits))
        def _():
            for e in range(n_e):
                for j in range(wide_windows):
                    place(e, starts[e] + j * COMPACT_WIN)

        return carry

    lax.fori_loop(0, n_k, tile, 0)

    lane = lax.broadcasted_iota(jnp.int32, (SUBLANES, LANES), 1)
    sub8 = lax.broadcasted_iota(jnp.int32, (SUBLANES, LANES), 0)
    id_rows = jnp.where(sub8 == 0, jnp.where(lane == 0, float(TOKEN_ID_BASE),
                                             jnp.where(lane == 1, 1.0, 0.0)), 0.0)

    def finish(e, carry):
        gate_lane = (lane >= GATE_COL0) & (lane < GATE_COL0 + 3 * n_e) & ((lane - GATE_COL0) % n_e == e)
        selector = (id_rows + jnp.where((sub8 == 1) & gate_lane, 1.0, 0.0)).astype(BF16)
        for m in range(n_blocks):
            blk = r_scr[e, pl.ds(m * SLOT_BLOCK, SLOT_BLOCK), :].astype(BF16)
            rows = lax.dot_general(selector, blk, (((1,), (1,)), ((), ())),
                                   preferred_element_type=F32)
            idx_ref[0, e, m] = rows[0:1, :].astype(jnp.int32)
            gs_ref[0, e, m] = rows[1:2, :]
        return carry

    lax.fori_loop(0, n_e, finish, 0)


def _compact(starts_flat, pos5, gate_c, cap):
    bsz, n_e, n_k = pos5.shape[:3]
    s = gate_c.shape[1]
    assert s <= TOKEN_ID_BASE * 256 and GATE_COL0 + 3 * n_e <= LANES
    n_blocks = cap // SLOT_BLOCK
    out_block = (1, n_e, n_blocks, 1, SLOT_BLOCK)
    out_spec = pl.BlockSpec(out_block, lambda b, st: (b, 0, 0, 0, 0))
    wide_rows = -(-(TOKEN_TILE + SUBLANES) // COMPACT_WIN) * COMPACT_WIN
    grid_spec = pltpu.PrefetchScalarGridSpec(
        num_scalar_prefetch=1,
        grid=(bsz,),
        in_specs=[pl.BlockSpec((1, n_e, n_k, 1, TOKEN_TILE), lambda b, st: (b, 0, 0, 0, 0)),
                  pl.BlockSpec((1, s, n_e), lambda b, st: (b, 0, 0))],
        out_specs=[out_spec, out_spec],
        scratch_shapes=[pltpu.VMEM((n_e, cap + wide_rows, LANES), F32)],
    )
    return pl.pallas_call(
        functools.partial(_compact_kernel, cap=cap),
        grid_spec=grid_spec,
        out_shape=[jax.ShapeDtypeStruct((bsz,) + out_block[1:], jnp.int32),
                   jax.ShapeDtypeStruct((bsz,) + out_block[1:], F32)],
        compiler_params=_cparams(("parallel",)),
        name="compact",
    )(starts_flat, pos5, gate_c)


def _moe_kernel(idx_ref, idx_next_ref, gs_ref, h2_hbm, wg_ref, wu_ref, wd_ref, eo_ref,
                xbuf, sem, xb_scr, acc_scr, *, seq_len, n_ff):
    b = pl.program_id(0)
    e = pl.program_id(1)
    f = pl.program_id(2)
    n_e = pl.num_programs(1)
    n_blocks = xb_scr.shape[0]
    cap = n_blocks * SLOT_BLOCK
    d = xb_scr.shape[2]
    step = b * n_e + e
    n_steps = pl.num_programs(0) * n_e
    cur = step % 2

    def token_copy(idx_smem, batch, buf, p):
        tok = idx_smem[0, 0, p]
        src = h2_hbm.at[pl.ds(pl.multiple_of((batch * seq_len + tok) * SUBLANES, SUBLANES), SUBLANES)]
        dst = xbuf.at[buf, pl.ds(pl.multiple_of(p * SUBLANES, SUBLANES), SUBLANES)]
        return pltpu.make_async_copy(src, dst, sem.at[buf])

    def request(idx_smem, batch, buf, first, count):
        def body(i, carry):
            for j in range(GATHER_UNROLL):
                token_copy(idx_smem, batch, buf, first + i * GATHER_UNROLL + j).start()
            return carry
        lax.fori_loop(0, count // GATHER_UNROLL, body, 0)

    @pl.when((step == 0) & (f == 0))
    def _():
        request(idx_ref, b, cur, 0, cap)

    next_batch = jnp.where(step + 1 < n_steps, step + 1, 0) // n_e
    per_block = cap // (n_ff * n_blocks)

    @pl.when(f == 0)
    def _():
        pltpu.make_async_copy(h2_hbm.at[pl.ds(0, cap * SUBLANES)], xbuf.at[cur], sem.at[cur]).wait()
        for m in range(n_blocks):
            first = m * SLOT_BLOCK * SUBLANES
            xm = jnp.concatenate(
                [xbuf[cur, pl.ds(first + j, SLOT_BLOCK, stride=SUBLANES), :] for j in range(d // LANES)],
                axis=1)
            xb_scr[m] = xm.astype(BF16)
        acc_scr[...] = jnp.zeros_like(acc_scr)

    wg = wg_ref[0].astype(BF16)
    wu = wu_ref[0].astype(BF16)
    wd = wd_ref[0].astype(BF16)
    for m in range(n_blocks):
        first = (f * n_blocks + m) * per_block
        for j in range(per_block):
            token_copy(idx_next_ref, next_batch, 1 - cur, first + j).start()
        xm = xb_scr[m]
        g = jnp.dot(xm, wg, preferred_element_type=F32)
        u = jnp.dot(xm, wu, preferred_element_type=F32)
        act = (jax.nn.silu(g) * u).astype(BF16)
        acc_scr[m] += jnp.dot(act, wd, preferred_element_type=F32)

    @pl.when((step == n_steps - 1) & (f == n_ff - 1))
    def _():
        pltpu.make_async_copy(h2_hbm.at[pl.ds(0, cap * SUBLANES)], xbuf.at[1 - cur],
                              sem.at[1 - cur]).wait()

    @pl.when(f == n_ff - 1)
    def _():
        eye = (lax.broadcasted_iota(jnp.int32, (SLOT_BLOCK, SLOT_BLOCK), 0)
               == lax.broadcasted_iota(jnp.int32, (SLOT_BLOCK, SLOT_BLOCK), 1))
        for m in range(n_blocks):
            gcol = jnp.sum(jnp.where(eye, gs_ref[0, m], 0.0), axis=1, keepdims=True)
            rows = pl.ds(m * SLOT_BLOCK, SLOT_BLOCK)
            eo_ref[0, 0, rows, :] = (gcol * acc_scr[m]).astype(eo_ref.dtype)


def _moe(idx, gs, h2_tiles, wg, wu, wd, bsz, s):
    n_e, d, d_ff = wg.shape
    n_blocks = gs.shape[1]
    cap = n_blocks * SLOT_BLOCK
    n_ff = d_ff // FF_CHUNK
    assert d_ff % FF_CHUNK == 0 and cap % (n_ff * n_blocks) == 0 and cap % GATHER_UNROLL == 0
    assert d == SUBLANES * LANES

    def next_step(b, e, f):
        return ((b * n_e + e + 1) % (bsz * n_e), 0, 0)

    smem = pltpu.SMEM
    return pl.pallas_call(
        functools.partial(_moe_kernel, seq_len=s, n_ff=n_ff),
        grid=(bsz, n_e, n_ff),
        in_specs=[
            pl.BlockSpec((1, 1, cap), lambda b, e, f: (b * n_e + e, 0, 0), memory_space=smem),
            pl.BlockSpec((1, 1, cap), next_step, memory_space=smem),
            pl.BlockSpec((1, n_blocks, 1, SLOT_BLOCK), lambda b, e, f: (b * n_e + e, 0, 0, 0)),
            pl.BlockSpec(memory_space=pl.ANY),
            pl.BlockSpec((1, d, FF_CHUNK), lambda b, e, f: (e, 0, f)),
            pl.BlockSpec((1, d, FF_CHUNK), lambda b, e, f: (e, 0, f)),
            pl.BlockSpec((1, FF_CHUNK, d), lambda b, e, f: (e, f, 0)),
        ],
        out_specs=pl.BlockSpec((1, 1, cap, d), lambda b, e, f: (b, e, 0, 0)),
        out_shape=jax.ShapeDtypeStruct((bsz, n_e, cap, d), BF16),
        scratch_shapes=[pltpu.VMEM((2, cap * SUBLANES, LANES), F32),
                        pltpu.SemaphoreType.DMA((2,)),
                        pltpu.VMEM((n_blocks, SLOT_BLOCK, d), BF16),
                        pltpu.VMEM((n_blocks, SLOT_BLOCK, d), F32)],
        compiler_params=_cparams(("arbitrary", "arbitrary", "arbitrary")),
        name="moe",
    )(idx, idx, gs, h2_tiles, wg, wu, wd)


def _combine_kernel(starts_ref, x1_ref, pos_ref, gf_ref, eo_ref, out_ref, wcat_scr, acc_scr,
                    *, cap):
    for i in range(x1_ref.shape[1] // TOKEN_TILE):
        _combine_tile(starts_ref, x1_ref, pos_ref, gf_ref, eo_ref, out_ref, wcat_scr, acc_scr,
                      pl.program_id(1) * (x1_ref.shape[1] // TOKEN_TILE) + i,
                      pl.ds(i * TOKEN_TILE, TOKEN_TILE), cap)


def _combine_tile(starts_ref, x1_ref, pos_ref, gf_ref, eo_ref, out_ref, wcat_scr, acc_scr,
                  k, rows, cap):
    b = pl.program_id(0)
    n_e = pos_ref.shape[2]
    pos = pos_ref[0, rows, :]
    expert_lane = lax.broadcasted_iota(jnp.int32, (1, n_e), 1)

    s0 = [starts_ref[(b * n_e + e) * LANES + k] for e in range(n_e)]
    s1 = [starts_ref[(b * n_e + e) * LANES + k + 1] for e in range(n_e)]

    def window_starts(rows):
        starts = [jnp.minimum(s // WIN_ALIGN, (cap - rows) // WIN_ALIGN) * WIN_ALIGN for s in s0]
        vec = jnp.zeros((1, n_e), jnp.int32)
        for e in range(n_e):
            vec = jnp.where(expert_lane == e, starts[e], vec)
        return starts, vec

    def onehot(rel, first, count, width):
        n_lanes = count * width
        lane = lax.broadcasted_iota(jnp.int32, (n_e, n_lanes), 1)
        sub = lax.broadcasted_iota(jnp.int32, (n_e, n_lanes), 0)
        expand = jnp.where(sub == first + lane // width, 1.0, 0.0).astype(BF16)
        spread = jnp.dot(jnp.clip(rel, -1, width).astype(F32).astype(BF16), expand,
                         preferred_element_type=F32)
        want = (lax.broadcasted_iota(jnp.int32, (TOKEN_TILE, n_lanes), 1) % width).astype(F32)
        return jnp.where(spread == want, 1.0, 0.0).astype(BF16)

    def finish(acc):
        out_ref[0, rows, :] = _rms_scale(acc) * gf_ref[...]

    narrow, narrow_vec = window_starts(NARROW_WIN)
    fits = None
    for e in range(n_e):
        ok = s1[e] - narrow[e] <= NARROW_WIN
        fits = ok if fits is None else fits & ok

    @pl.when(fits)
    def _():
        for e in range(n_e):
            src = pl.ds(pl.multiple_of(narrow[e], WIN_ALIGN), NARROW_WIN)
            wcat_scr[pl.ds(e * NARROW_WIN, NARROW_WIN), :] = eo_ref[0, e, src, :]
        hit = onehot(pos - narrow_vec, 0, n_e, NARROW_WIN)
        finish(x1_ref[0, rows, :] + jnp.dot(hit, wcat_scr[pl.ds(0, n_e * NARROW_WIN), :],
                                            preferred_element_type=F32))

    @pl.when(jnp.logical_not(fits))
    def _():
        wide, wide_vec = window_starts(WIN_ROWS)
        rel = pos - wide_vec
        group = wcat_scr.shape[0] // SLOT_BLOCK
        acc_scr[...] = x1_ref[0, rows, :]
        for g in range(n_e // group):
            for i in range(group):
                e = g * group + i
                src = pl.ds(pl.multiple_of(wide[e], WIN_ALIGN), SLOT_BLOCK)
                wcat_scr[pl.ds(i * SLOT_BLOCK, SLOT_BLOCK), :] = eo_ref[0, e, src, :]
            acc_scr[...] += jnp.dot(onehot(rel, g * group, group, SLOT_BLOCK), wcat_scr[...],
                                    preferred_element_type=F32)
        for e in range(n_e):
            src = pl.ds(pl.multiple_of(wide[e] + SLOT_BLOCK, WIN_ALIGN), WIN_ALIGN)
            wcat_scr[pl.ds(e * WIN_ALIGN, WIN_ALIGN), :] = eo_ref[0, e, src, :]
        tail = onehot(rel - SLOT_BLOCK, 0, n_e, WIN_ALIGN)
        finish(acc_scr[...] + jnp.dot(tail, wcat_scr[pl.ds(0, n_e * WIN_ALIGN), :],
                                      preferred_element_type=F32))


def _combine(starts_flat, x1, pos_c, eo, gf):
    bsz, s, d = x1.shape
    n_e = pos_c.shape[-1]
    cap = eo.shape[2]
    step_rows = COMBINE_TILES * TOKEN_TILE
    wide_group = 4
    assert n_e % wide_group == 0 and n_e * NARROW_WIN <= wide_group * SLOT_BLOCK
    assert s % step_rows == 0
    grid_spec = pltpu.PrefetchScalarGridSpec(
        num_scalar_prefetch=1,
        grid=(bsz, s // step_rows),
        in_specs=[
            pl.BlockSpec((1, step_rows, d), lambda b, k, st: (b, k, 0)),
            pl.BlockSpec((1, step_rows, n_e), lambda b, k, st: (b, k, 0)),
            pl.BlockSpec((1, d), lambda b, k, st: (0, 0)),
            pl.BlockSpec((1, n_e, cap, d), lambda b, k, st: (b, 0, 0, 0),
                         pipeline_mode=pl.Buffered(1)),
        ],
        out_specs=pl.BlockSpec((1, step_rows, d), lambda b, k, st: (b, k, 0)),
        scratch_shapes=[pltpu.VMEM((wide_group * SLOT_BLOCK, d), BF16),
                        pltpu.VMEM((TOKEN_TILE, d), F32)],
    )
    return pl.pallas_call(
        functools.partial(_combine_kernel, cap=cap),
        grid_spec=grid_spec,
        out_shape=jax.ShapeDtypeStruct((bsz, s, d), F32),
        compiler_params=_cparams(("parallel", "parallel")),
        name="combine",
    )(starts_flat, x1, pos_c, gf, eo)


def _block_diag(w):
    h, hd, _ = w.shape
    eye = jnp.eye(h, dtype=w.dtype)
    return (eye[:, None, :, None] * w[:, :, None, :]).reshape(h * hd, h * hd)


def _lru_params(wa_f, wx_f, wa_b, wx_b, ba_f, bx_f, ba_b, bx_b, lam_f, lam_b):
    d_lru = ba_f.shape[0]
    n_blk = d_lru // LANES
    mats = [0.5 * _block_diag(w) for w in (wa_f, wx_f, wa_b, wx_b)]
    ba_f, bx_f, ba_b, bx_b = (0.5 * v for v in (ba_f, bx_f, ba_b, bx_b))
    w_cat = jnp.stack([
        jnp.concatenate([m[c * LANES:(c + 1) * LANES, c * LANES:(c + 1) * LANES] for m in mats], axis=1)
        for c in range(n_blk)]).astype(BF16)
    b_cat = jnp.stack([
        jnp.concatenate([v[c * LANES:(c + 1) * LANES] for v in (ba_f, bx_f, ba_b, bx_b)])
        for c in range(n_blk)])[:, None, :]
    lam_cat = jnp.stack([
        jnp.concatenate([v[c * LANES:(c + 1) * LANES] for v in (lam_f, lam_b)])
        for c in range(n_blk)])[:, None, :]
    return w_cat, b_cat, lam_cat


def kernel(x, norm1_g, w_in, conv_w, conv_b, lru_wa_f, lru_ba_f, lru_wx_f, lru_bx_f, lru_lam_f,
           lru_wa_b, lru_ba_b, lru_wx_b, lru_bx_b, lru_lam_b, w_out, norm2_g, w_router,
           w_gate, w_up, w_down, normf_g):
    bsz, s, d = x.shape
    d_lru = conv_b.shape[0]
    d_four = w_in.shape[1] - 2 * d_lru
    n_e = w_router.shape[1]
    cap = CAPACITY_FACTOR * s // n_e
    assert s % LRU_CHUNK == 0 and s % (SCAN_GROUPS * SUBLANES * SUBLANES) == 0
    assert s % FOURIER_N2 == 0 and cap % SLOT_BLOCK == 0 and s // TOKEN_TILE < LANES
    assert cap >= WIN_ROWS and w_gate.shape[2] % FF_CHUNK == 0

    lx, lg, fo = _inproj(x.reshape(bsz * s, d), norm1_g[None, :], w_in.astype(BF16), d_lru, d_four)
    lx = lx.reshape(bsz, s, d_lru)
    lg = lg.reshape(bsz, s, d_lru)
    fo = fo.reshape(bsz, s, d_four)

    w_cat, b_cat, lam_cat = _lru_params(lru_wa_f, lru_wx_f, lru_wa_b, lru_wx_b,
                                        lru_ba_f, lru_bx_f, lru_ba_b, lru_bx_b,
                                        lru_lam_f, lru_lam_b)
    y_lru = _lru(lx, lg, conv_w, conv_b[None, :], w_cat, b_cat, lam_cat)
    y_four = _fourier(fo, FOURIER_GROUPS)

    w_out_bf = w_out.astype(BF16)
    x1, h2_tiles, aff_t = _outproj(x, y_lru, y_four, w_out_bf[:d_lru], w_out_bf[d_lru:],
                                   norm2_g[None, :], w_router.T)

    pos_r, starts = _select(aff_t, cap)
    starts_flat = starts.reshape(-1)
    pos_c = jnp.transpose(pos_r, (0, 2, 1))
    pos5 = pos_r.reshape(bsz, n_e, s // TOKEN_TILE, 1, TOKEN_TILE)
    idx, gs = _compact(starts_flat, pos5, jnp.transpose(aff_t, (0, 2, 1)), cap)
    eo = _moe(idx.reshape(bsz * n_e, 1, cap), gs.reshape((bsz * n_e,) + gs.shape[2:]), h2_tiles,
              w_gate, w_up, w_down, bsz, s)
    return _combine(starts_flat, x1, pos_c, eo, normf_g[None, :])
```

```python
import functools
import math

import jax
import jax.numpy as jnp
import numpy as np
from jax import lax
from jax.experimental import pallas as pl
from jax.experimental.pallas import tpu as pltpu

F32 = jnp.float32
BF16 = jnp.bfloat16

EPS = 1e-6
LRU_C = 8.0
CAPACITY_FACTOR = 2
FOURIER_GROUPS = 4

SUBLANES = 8
LANES = 128
MXU_DIM = 256
VMEM_LIMIT_BYTES = 56 * 1024 * 1024

ROW_TILE = 1024
LRU_CHUNK = 1024
SCAN_GROUPS = 4
SLOT_BLOCK = MXU_DIM
TOKEN_TILE = MXU_DIM
SEG_PAD = 4
FOURIER_N2 = LANES
FOURIER_K1_BATCH = 16
FOURIER_U_BATCH = 2
STAGE_PAD = 4
FF_CHUNK = 1024
GATHER_UNROLL = 8
COMPACT_WIN = 64
TOKEN_ID_BASE = 64
GATE_COL0 = 16
WIN_ALIGN = 16
WIN_ROWS = SLOT_BLOCK + WIN_ALIGN
NARROW_WIN = 64
COMBINE_TILES = 2


def _cparams(semantics):
    return pltpu.CompilerParams(dimension_semantics=semantics,
                                vmem_limit_bytes=VMEM_LIMIT_BYTES)


def _rms_scale(x):
    return x * lax.rsqrt(jnp.mean(x * x, axis=-1, keepdims=True) + EPS)


def _inproj_kernel(x_ref, g_ref, w_ref, lx_ref, lg_ref, fo_ref):
    h = _rms_scale(x_ref[...]) * g_ref[...]
    p = jnp.dot(h.astype(BF16), w_ref[...], preferred_element_type=F32)
    d = lx_ref.shape[-1]
    lx_ref[...] = p[:, :d]
    lg_ref[...] = p[:, d:2 * d]
    fo_ref[...] = p[:, 2 * d:]


def _inproj(x2, g, w_bf, d_lru, d_four):
    m, d = x2.shape
    n = w_bf.shape[1]
    return pl.pallas_call(
        _inproj_kernel,
        grid=(m // ROW_TILE,),
        in_specs=[
            pl.BlockSpec((ROW_TILE, d), lambda i: (i, 0)),
            pl.BlockSpec((1, d), lambda i: (0, 0)),
            pl.BlockSpec((d, n), lambda i: (0, 0)),
        ],
        out_specs=[
            pl.BlockSpec((ROW_TILE, d_lru), lambda i: (i, 0)),
            pl.BlockSpec((ROW_TILE, d_lru), lambda i: (i, 0)),
            pl.BlockSpec((ROW_TILE, d_four), lambda i: (i, 0)),
        ],
        out_shape=[
            jax.ShapeDtypeStruct((m, d_lru), F32),
            jax.ShapeDtypeStruct((m, d_lru), F32),
            jax.ShapeDtypeStruct((m, d_four), F32),
        ],
        compiler_params=_cparams(("parallel",)),
        name="inproj",
    )(x2, g, w_bf)


def _shift_rows(x, shift):
    n = x.shape[0]
    rows = lax.broadcasted_iota(jnp.int32, x.shape, 0)
    rolled = pltpu.roll(x, shift % n, axis=0)
    keep = (rows >= shift) if shift > 0 else (rows < n + shift)
    return jnp.where(keep, rolled, 0.0)


def _lru_kernel(lx_ref, lg_ref, cw_ref, cb_ref, w_ref, b_ref, lam_ref, y_ref,
                xpad, af, bf, ab, bb):
    s = lx_ref.shape[1]
    c_blk = lx_ref.shape[2]
    n_seg = SCAN_GROUPS * SUBLANES
    seg = s // n_seg
    seg_stride = seg + SEG_PAD
    piece = min(seg, LRU_CHUNK)
    n_chunks = s // LRU_CHUNK
    pad = SUBLANES

    zeros_pad = jnp.zeros((pad, c_blk), F32)
    xpad[pl.ds(0, pad), :] = zeros_pad
    xpad[pl.ds(pad + s, pad), :] = zeros_pad

    def copy_chunk(i, carry):
        t0 = pl.multiple_of(i * LRU_CHUNK, LRU_CHUNK)
        xpad[pl.ds(pad + t0, LRU_CHUNK), :] = lx_ref[0, pl.ds(t0, LRU_CHUNK), :]
        return carry

    lax.fori_loop(0, n_chunks, copy_chunk, 0)

    cw = cw_ref[...]
    conv_width = cw.shape[0]
    cb = cb_ref[...]
    bias = b_ref[0]
    quarter_c_log_sig = (0.25 * LRU_C) * jax.nn.log_sigmoid(lam_ref[0])
    w_cat = w_ref[0]

    def gates_chunk(i, carry):
        t0 = pl.multiple_of(i * LRU_CHUNK, LRU_CHUNK)
        c = cb
        left = conv_width // 2
        for k in range(conv_width):
            c = c + cw[k:k + 1, :] * xpad[pl.ds(t0 + (pad - left + k), LRU_CHUNK), :]
        z = jnp.dot(c.astype(BF16), w_cat, preferred_element_type=F32) + bias
        for d, (a_scr, b_scr) in enumerate(((af, bf), (ab, bb))):
            q = quarter_c_log_sig[:, d * c_blk:(d + 1) * c_blk]
            th_r = jnp.tanh(z[:, (2 * d) * c_blk:(2 * d + 1) * c_blk])
            th_i = jnp.tanh(z[:, (2 * d + 1) * c_blk:(2 * d + 2) * c_blk])
            t = jnp.tanh(q + q * th_r)
            inv = 1.0 / (1.0 - t)
            a = (1.0 + t) * inv
            bt = (jnp.sqrt(-t) * inv) * (c + c * th_i)
            for p in range(LRU_CHUNK // piece):
                t = t0 + p * piece
                dst = pl.ds((t // seg) * seg_stride + t % seg, piece)
                a_scr[dst, :] = a[p * piece:(p + 1) * piece]
                b_scr[dst, :] = bt[p * piece:(p + 1) * piece]
        return carry

    lax.fori_loop(0, n_chunks, gates_chunk, 0)

    def scan_step(i, carry):
        out = []
        for (a_scr, b_scr, o), (hs, ps) in zip(((af, bf, i), (ab, bb, seg - 1 - i)), carry):
            new_h, new_p = [], []
            for q in range(SCAN_GROUPS):
                rows = pl.ds(q * SUBLANES * seg_stride + o, SUBLANES, stride=seg_stride)
                a = a_scr[rows, :]
                h = a * hs[q] + b_scr[rows, :]
                p = a * ps[q]
                b_scr[rows, :] = h
                a_scr[rows, :] = p
                new_h.append(h)
                new_p.append(p)
            out.append((tuple(new_h), tuple(new_p)))
        return tuple(out)

    zero = jnp.zeros((SUBLANES, c_blk), F32)
    one = jnp.ones((SUBLANES, c_blk), F32)
    init = ((zero,) * SCAN_GROUPS, (one,) * SCAN_GROUPS)
    (hf_end, pf_end), (hb_end, pb_end) = lax.fori_loop(0, seg, scan_step, (init, init))

    row = lax.broadcasted_iota(jnp.int32, (SUBLANES, c_blk), 0)

    def entering(h_end, p_end, carry_in, first_row, shift):
        c = zero
        for _ in range(SUBLANES):
            c = jnp.where(row == first_row, carry_in, _shift_rows(h_end + p_end * c, shift))
        return c

    cf = [None] * SCAN_GROUPS
    cbk = [None] * SCAN_GROUPS
    carry_f = jnp.zeros((1, c_blk), F32)
    carry_b = jnp.zeros((1, c_blk), F32)
    for q in range(SCAN_GROUPS):
        cf[q] = entering(hf_end[q], pf_end[q], carry_f, 0, 1)
        carry_f = (hf_end[q] + pf_end[q] * cf[q])[SUBLANES - 1:SUBLANES, :]
        qb = SCAN_GROUPS - 1 - q
        cbk[qb] = entering(hb_end[qb], pb_end[qb], carry_b, SUBLANES - 1, -1)
        carry_b = (hb_end[qb] + pb_end[qb] * cbk[qb])[0:1, :]

    for j in range(n_seg):
        q, i = divmod(j, SUBLANES)
        src = pl.ds(j * seg_stride, seg)
        rows = pl.ds(j * seg, seg)
        hsum = ((bf[src, :] + af[src, :] * cf[q][i:i + 1, :])
                + (bb[src, :] + ab[src, :] * cbk[q][i:i + 1, :]))
        y_ref[0, rows, :] = (jax.nn.gelu(lg_ref[0, rows, :]) * hsum).astype(y_ref.dtype)


def _lru(lx, lg, conv_w, conv_b, w_cat, b_cat, lam_cat):
    bsz, s, d_lru = lx.shape
    c_blk = LANES
    n_blk = d_lru // c_blk
    scr = pltpu.VMEM((s + SCAN_GROUPS * SUBLANES * SEG_PAD, c_blk), F32)
    return pl.pallas_call(
        _lru_kernel,
        grid=(bsz, n_blk),
        in_specs=[
            pl.BlockSpec((1, s, c_blk), lambda b, c: (b, 0, c)),
            pl.BlockSpec((1, s, c_blk), lambda b, c: (b, 0, c)),
            pl.BlockSpec((4, c_blk), lambda b, c: (0, c)),
            pl.BlockSpec((1, c_blk), lambda b, c: (0, c)),
            pl.BlockSpec((1, c_blk, 4 * c_blk), lambda b, c: (c, 0, 0)),
            pl.BlockSpec((1, 1, 4 * c_blk), lambda b, c: (c, 0, 0)),
            pl.BlockSpec((1, 1, 2 * c_blk), lambda b, c: (c, 0, 0)),
        ],
        out_specs=pl.BlockSpec((1, s, c_blk), lambda b, c: (b, 0, c)),
        out_shape=jax.ShapeDtypeStruct((bsz, s, d_lru), BF16),
        scratch_shapes=[pltpu.VMEM((s + 2 * SUBLANES, c_blk), F32), scr, scr, scr, scr],
        compiler_params=_cparams(("parallel", "parallel")),
        name="lru",
    )(lx, lg, conv_w, conv_b, w_cat, b_cat, lam_cat)


def _fourier_kernel(f_ref, ka_ref, kb_ref, cc_ref, sc_ref, twc_ref, tws_ref, y_ref,
                    ar_scr, ai_scr, y_scr, *, scale):
    n1 = f_ref.shape[1]
    n_u = f_ref.shape[2]
    cols = f_ref.shape[4]
    r = SUBLANES * n1
    n2 = n_u * SUBLANES
    stage_stride = n1 + STAGE_PAD

    def stage_a(i, carry):
        us = tuple(FOURIER_U_BATCH * i + j for j in range(FOURIER_U_BATCH))
        xu = jnp.concatenate([f_ref[0, :, u].reshape(r, cols) for u in us], axis=1)
        a = jnp.dot(ka_ref[...], xu.astype(BF16), preferred_element_type=F32)
        for j, u in enumerate(us):
            a_re = a[:r, j * cols:(j + 1) * cols]
            a_im = a[r:, j * cols:(j + 1) * cols]
            c = twc_ref[u]
            sn = tws_ref[u]
            ar_scr[u] = a_re * c - a_im * sn
            ai_scr[u] = a_re * sn + a_im * c
        return carry

    lax.fori_loop(0, n_u // FOURIER_U_BATCH, stage_a, 0)

    nb = FOURIER_K1_BATCH

    def stage_b(i, carry):
        rows = pl.ds(pl.multiple_of(i * (nb * SUBLANES), nb * SUBLANES), nb * SUBLANES)
        a_re = ar_scr[:, rows, :]
        a_im = ai_scr[:, rows, :]

        def rows_s2(a, j):
            return a[:, j * SUBLANES:(j + 1) * SUBLANES, :].reshape(n2, cols)

        st = jnp.concatenate(
            [jnp.concatenate([rows_s2(a_re, j), rows_s2(a_im, j)], axis=0) for j in range(nb)],
            axis=1).astype(BF16)
        x = jnp.dot(kb_ref[...], st, preferred_element_type=F32)
        x_re = jnp.concatenate([x[:n2, j * cols:(j + 1) * cols] for j in range(nb)], axis=0)
        x_im = jnp.concatenate([x[n2:, j * cols:(j + 1) * cols] for j in range(nb)], axis=0)
        y = (jnp.dot(x_re.astype(BF16), cc_ref[...], preferred_element_type=F32)
             + jnp.dot(x_im.astype(BF16), sc_ref[...], preferred_element_type=F32)) * scale
        for j in range(nb):
            y_scr[pl.ds(i * nb + j, n2, stride=stage_stride), :] = y[j * n2:(j + 1) * n2]
        return carry

    lax.fori_loop(0, n1 // nb, stage_b, 0)

    def copy_out(k2, carry):
        y_ref[0, pl.ds(pl.multiple_of(k2 * n1, n1), n1), :] = y_scr[pl.ds(k2 * stage_stride, n1), :]
        return carry

    lax.fori_loop(0, n2, copy_out, 0)


def _dft_tables(s, c_grp):
    n2 = FOURIER_N2
    n1 = s // n2
    n_u = n2 // SUBLANES

    def cos_sin(num, den):
        ang = (2.0 * np.pi / den) * (num % den).astype(np.float64)
        return np.cos(ang), np.sin(ang)

    def const(a, dtype=F32):
        return jnp.asarray(a.astype(np.float32)).astype(dtype)

    i1 = np.arange(n1, dtype=np.int64)
    c1, s1 = cos_sin(i1[:, None] * i1[None, :], n1)
    eye8 = np.eye(SUBLANES)
    ka = const(np.concatenate([np.kron(c1, eye8), np.kron(-s1, eye8)], axis=0), BF16)

    i2 = np.arange(n2, dtype=np.int64)
    c2, s2 = cos_sin(i2[:, None] * i2[None, :], n2)
    kb = const(np.block([[c2, s2], [-s2, c2]]), BF16)

    ic = np.arange(c_grp, dtype=np.int64)
    cc, sc = cos_sin(ic[:, None] * ic[None, :], c_grp)

    u = np.arange(n_u, dtype=np.int64)[:, None, None]
    k1 = np.arange(n1, dtype=np.int64)[None, :, None]
    v = np.arange(SUBLANES, dtype=np.int64)[None, None, :]
    tc, ts = cos_sin(k1 * (SUBLANES * u + v), s)
    shape = (n_u, n1 * SUBLANES, c_grp)
    twc = jnp.broadcast_to(const(tc.reshape(n_u, -1, 1)), shape)
    tws = jnp.broadcast_to(const(-ts.reshape(n_u, -1, 1)), shape)
    return ka, kb, const(cc, BF16), const(sc, BF16), twc, tws


def _fourier(four, n_groups):
    bsz, s, d_four = four.shape
    c_grp = d_four // n_groups
    n2 = FOURIER_N2
    n1 = s // n2
    n_u = n2 // SUBLANES
    r = SUBLANES * n1
    ka, kb, cc, sc, twc, tws = _dft_tables(s, c_grp)
    f5 = four.reshape(bsz, n1, n_u, SUBLANES, d_four)
    scale = 1.0 / math.sqrt(s * c_grp)
    const2 = lambda b, g: (0, 0)
    const3 = lambda b, g: (0, 0, 0)
    return pl.pallas_call(
        functools.partial(_fourier_kernel, scale=scale),
        grid=(bsz, n_groups),
        in_specs=[
            pl.BlockSpec((1, n1, n_u, SUBLANES, c_grp), lambda b, g: (b, 0, 0, 0, g)),
            pl.BlockSpec(ka.shape, const2),
            pl.BlockSpec(kb.shape, const2),
            pl.BlockSpec(cc.shape, const2),
            pl.BlockSpec(sc.shape, const2),
            pl.BlockSpec(twc.shape, const3),
            pl.BlockSpec(tws.shape, const3),
        ],
        out_specs=pl.BlockSpec((1, s, c_grp), lambda b, g: (b, 0, g)),
        out_shape=jax.ShapeDtypeStruct((bsz, s, d_four), F32),
        scratch_shapes=[pltpu.VMEM((n_u, r, c_grp), F32), pltpu.VMEM((n_u, r, c_grp), F32),
                        pltpu.VMEM((n2 * (n1 + STAGE_PAD), c_grp), F32)],
        compiler_params=_cparams(("parallel", "parallel")),
        name="fourier",
    )(f5, ka, kb, cc, sc, twc, tws)


def _outproj_kernel(x_ref, yl_ref, yf_ref, wo1_ref, wo2_ref, g2_ref, wr_ref,
                    x1_ref, h2_ref, aff_ref):
    x1 = (x_ref[0]
          + jnp.dot(yl_ref[0], wo1_ref[...], preferred_element_type=F32)
          + jnp.dot(yf_ref[0].astype(BF16), wo2_ref[...], preferred_element_type=F32))
    x1_ref[0] = x1
    h2 = _rms_scale(x1) * g2_ref[...]
    for j in range(h2.shape[1] // LANES):
        h2_ref[pl.ds(j, h2.shape[0], stride=SUBLANES), :] = h2[:, j * LANES:(j + 1) * LANES]
    def split(v):
        hi = v.astype(BF16)
        return hi, (v - hi.astype(F32)).astype(BF16)

    def dot_nt(a, bm):
        return lax.dot_general(a, bm, (((1,), (1,)), ((), ())), preferred_element_type=F32)

    w_hi, w_lo = split(wr_ref[...])
    h_hi, h_lo = split(h2)
    logits = dot_nt(w_hi, h_hi) + (dot_nt(w_lo, h_hi) + dot_nt(w_hi, h_lo))
    ex = jnp.exp(logits - jnp.max(logits, axis=0, keepdims=True))
    aff_ref[0] = ex / jnp.sum(ex, axis=0, keepdims=True)


def _outproj(x, y_lru, y_four, wo1, wo2, g2, wr_t):
    bsz, s, d = x.shape
    d_lru = y_lru.shape[-1]
    d_four = y_four.shape[-1]
    n_e = wr_t.shape[0]
    assert d == SUBLANES * LANES
    n_i = s // ROW_TILE
    tile = lambda w: pl.BlockSpec((1, ROW_TILE, w), lambda b, i: (b, i, 0))
    const = lambda shape: pl.BlockSpec(shape, lambda b, i: (0, 0))
    return pl.pallas_call(
        _outproj_kernel,
        grid=(bsz, n_i),
        in_specs=[tile(d), tile(d_lru), tile(d_four), const(wo1.shape), const(wo2.shape),
                  const(g2.shape), const(wr_t.shape)],
        out_specs=[tile(d),
                   pl.BlockSpec((ROW_TILE * SUBLANES, LANES), lambda b, i: (b * n_i + i, 0)),
                   pl.BlockSpec((1, n_e, ROW_TILE), lambda b, i: (b, 0, i))],
        out_shape=[
            jax.ShapeDtypeStruct((bsz, s, d), F32),
            jax.ShapeDtypeStruct((bsz * s * SUBLANES, LANES), F32),
            jax.ShapeDtypeStruct((bsz, n_e, s), F32),
        ],
        compiler_params=_cparams(("parallel", "parallel")),
        name="outproj",
    )(x, y_lru, y_four, wo1, wo2, g2, wr_t)


def _select_kernel(aff_ref, tri_ref, pos_ref, starts_ref, *, cap):
    v = aff_ref[0]
    n_e, s = v.shape
    cap_f = float(cap)

    def midpoint(lo, hi):
        mid = 0.5 * (lo + hi)
        return mid, (mid > lo) & (mid < hi)

    def cond(carry):
        _, active = midpoint(*carry)
        return jnp.max(active.astype(F32)) > 0.0

    def body(carry):
        lo, hi = carry
        mid, active = midpoint(lo, hi)
        cnt = jnp.sum((v >= mid).astype(F32), axis=1, keepdims=True)
        enough = cnt >= cap_f
        return (jnp.where(active & enough, mid, lo), jnp.where(active & (~enough), mid, hi))

    lo0 = jnp.zeros((n_e, 1), F32)
    hi0 = jnp.full((n_e, 1), 2.0, F32)
    thr, _ = lax.while_loop(cond, body, (lo0, hi0))

    above = v > thr
    tie = v == thr
    need = cap_f - jnp.sum(above.astype(F32), axis=1, keepdims=True)

    n_tiles = s // TOKEN_TILE
    tri = tri_ref[...]
    lane = lax.broadcasted_iota(jnp.int32, (n_e, LANES), 1)

    def prefix(mask_f, want_starts):
        run = jnp.zeros((n_e, 1), F32)
        starts = jnp.zeros((n_e, LANES), F32)
        pieces = []
        for t in range(n_tiles):
            m = mask_f[:, t * TOKEN_TILE:(t + 1) * TOKEN_TILE]
            incl = jnp.dot(m.astype(BF16), tri, preferred_element_type=F32)
            pieces.append(run + incl - m)
            if want_starts:
                starts = jnp.where(lane == t, run, starts)
            run = run + incl[:, TOKEN_TILE - 1:TOKEN_TILE]
        if want_starts:
            starts = jnp.where(lane == n_tiles, run, starts)
        return jnp.concatenate(pieces, axis=1), starts

    tie_rank, _ = prefix(tie.astype(F32), False)
    sel = above | (tie & (tie_rank < need))
    pos, starts = prefix(sel.astype(F32), True)
    pos_ref[0] = jnp.where(sel, pos, -1.0).astype(jnp.int32)
    starts_ref[0] = starts.astype(jnp.int32)


def _select(aff_t, cap):
    bsz, n_e, s = aff_t.shape
    idx = jnp.arange(TOKEN_TILE, dtype=jnp.int32)
    tri = (idx[:, None] <= idx[None, :]).astype(BF16)
    return pl.pallas_call(
        functools.partial(_select_kernel, cap=cap),
        grid=(bsz,),
        in_specs=[
            pl.BlockSpec((1, n_e, s), lambda b: (b, 0, 0)),
            pl.BlockSpec(tri.shape, lambda b: (0, 0)),
        ],
        out_specs=[
            pl.BlockSpec((1, n_e, s), lambda b: (b, 0, 0)),
            pl.BlockSpec((1, n_e, LANES), lambda b: (b, 0, 0)),
        ],
        out_shape=[
            jax.ShapeDtypeStruct((bsz, n_e, s), jnp.int32),
            jax.ShapeDtypeStruct((bsz, n_e, LANES), jnp.int32),
        ],
        compiler_params=_cparams(("parallel",)),
        name="select",
    )(aff_t, tri)


def _token_table(gate, first_token):
    n_tok, n_e = gate.shape
    lane = lax.broadcasted_iota(jnp.int32, (n_tok, LANES), 1)
    tok = first_token + lax.broadcasted_iota(jnp.int32, (n_tok, LANES), 0)
    table = jnp.where(lane == 0, tok // TOKEN_ID_BASE, jnp.where(lane == 1, tok % TOKEN_ID_BASE, 0))
    table = table.astype(F32)
    place_lane = lax.broadcasted_iota(jnp.int32, (n_e, LANES), 1)
    place_sub = lax.broadcasted_iota(jnp.int32, (n_e, LANES), 0)
    rest = gate
    for j in range(3):
        piece = rest.astype(BF16)
        rest = rest - piece.astype(F32)
        place = jnp.where(place_lane == GATE_COL0 + j * n_e + place_sub, 1.0, 0.0).astype(BF16)
        table = table + jnp.dot(piece, place, preferred_element_type=F32)
    return table.astype(BF16)


def _compact_kernel(starts_ref, pos_ref, gate_ref, idx_ref, gs_ref, r_scr, *, cap):
    b = pl.program_id(0)
    n_e = pos_ref.shape[1]
    n_k = pos_ref.shape[2]
    n_blocks = cap // SLOT_BLOCK
    wide_windows = -(-(TOKEN_TILE + SUBLANES) // COMPACT_WIN)
    r_scr[...] = jnp.zeros_like(r_scr)
    sub = lax.broadcasted_iota(jnp.int32, (COMPACT_WIN, TOKEN_TILE), 0)

    def tile(k, carry):
        rows = pl.ds(pl.multiple_of(k * TOKEN_TILE, TOKEN_TILE), TOKEN_TILE)
        vk = _token_table(gate_ref[0, rows, :], k * TOKEN_TILE)
        starts = []
        fits = None
        for e in range(n_e):
            base = (b * n_e + e) * LANES
            ws = (starts_ref[base + k] // SUBLANES) * SUBLANES
            ok = starts_ref[base + k + 1] - ws <= COMPACT_WIN
            fits = ok if fits is None else fits & ok
            starts.append(ws)

        def place(e, ws):
            hit = pos_ref[0, e, k] - ws == sub
            vals = jnp.dot(jnp.where(hit, 1.0, 0.0).astype(BF16), vk, preferred_element_type=F32)
            r_scr[e, pl.ds(pl.multiple_of(ws, SUBLANES), COMPACT_WIN), :] += vals

        @pl.when(fits)
        def _():
            for e in range(n_e):
                place(e, starts[e])

        @pl.when(jnp.logical_not(fits))
        def _():
            for e in range(n_e):
                for j in range(wide_windows):
                    place(e, starts[e] + j * COMPACT_WIN)

        return carry

    lax.fori_loop(0, n_k, tile, 0)

    lane = lax.broadcasted_iota(jnp.int32, (SUBLANES, LANES), 1)
    sub8 = lax.broadcasted_iota(jnp.int32, (SUBLANES, LANES), 0)
    id_rows = jnp.where(sub8 == 0, jnp.where(lane == 0, float(TOKEN_ID_BASE),
                                             jnp.where(lane == 1, 1.0, 0.0)), 0.0)

    def finish(e, carry):
        gate_lane = (lane >= GATE_COL0) & (lane < GATE_COL0 + 3 * n_e) & ((lane - GATE_COL0) % n_e == e)
        selector = (id_rows + jnp.where((sub8 == 1) & gate_lane, 1.0, 0.0)).astype(BF16)
        for m in range(n_blocks):
            blk = r_scr[e, pl.ds(m * SLOT_BLOCK, SLOT_BLOCK), :].astype(BF16)
            rows = lax.dot_general(selector, blk, (((1,), (1,)), ((), ())),
                                   preferred_element_type=F32)
            idx_ref[0, e, m] = rows[0:1, :].astype(jnp.int32)
            gs_ref[0, e, m] = rows[1:2, :]
        return carry

    lax.fori_loop(0, n_e, finish, 0)


def _compact(starts_flat, pos5, gate_c, cap):
    bsz, n_e, n_k = pos5.shape[:3]
    s = gate_c.shape[1]
    assert s <= TOKEN_ID_BASE * 256 and GATE_COL0 + 3 * n_e <= LANES
    n_blocks = cap // SLOT_BLOCK
    out_block = (1, n_e, n_blocks, 1, SLOT_BLOCK)
    out_spec = pl.BlockSpec(out_block, lambda b, st: (b, 0, 0, 0, 0))
    wide_rows = -(-(TOKEN_TILE + SUBLANES) // COMPACT_WIN) * COMPACT_WIN
    grid_spec = pltpu.PrefetchScalarGridSpec(
        num_scalar_prefetch=1,
        grid=(bsz,),
        in_specs=[pl.BlockSpec((1, n_e, n_k, 1, TOKEN_TILE), lambda b, st: (b, 0, 0, 0, 0)),
                  pl.BlockSpec((1, s, n_e), lambda b, st: (b, 0, 0))],
        out_specs=[out_spec, out_spec],
        scratch_shapes=[pltpu.VMEM((n_e, cap + wide_rows, LANES), F32)],
    )
    return pl.pallas_call(
        functools.partial(_compact_kernel, cap=cap),
        grid_spec=grid_spec,
        out_shape=[jax.ShapeDtypeStruct((bsz,) + out_block[1:], jnp.int32),
                   jax.ShapeDtypeStruct((bsz,) + out_block[1:], F32)],
        compiler_params=_cparams(("parallel",)),
        name="compact",
    )(starts_flat, pos5, gate_c)


def _moe_kernel(idx_ref, idx_next_ref, gs_ref, h2_hbm, wg_ref, wu_ref, wd_ref, eo_ref,
                xbuf, sem, xb_scr, acc_scr, *, seq_len, n_ff):
    b = pl.program_id(0)
    e = pl.program_id(1)
    f = pl.program_id(2)
    n_e = pl.num_programs(1)
    n_blocks = xb_scr.shape[0]
    cap = n_blocks * SLOT_BLOCK
    d = xb_scr.shape[2]
    step = b * n_e + e
    n_steps = pl.num_programs(0) * n_e
    cur = step % 2

    def token_copy(idx_smem, batch, buf, p):
        tok = idx_smem[0, 0, p]
        src = h2_hbm.at[pl.ds(pl.multiple_of((batch * seq_len + tok) * SUBLANES, SUBLANES), SUBLANES)]
        dst = xbuf.at[buf, pl.ds(pl.multiple_of(p * SUBLANES, SUBLANES), SUBLANES)]
        return pltpu.make_async_copy(src, dst, sem.at[buf])

    def request(idx_smem, batch, buf, first, count):
        def body(i, carry):
            for j in range(GATHER_UNROLL):
                token_copy(idx_smem, batch, buf, first + i * GATHER_UNROLL + j).start()
            return carry
        lax.fori_loop(0, count // GATHER_UNROLL, body, 0)

    @pl.when((step == 0) & (f == 0))
    def _():
        request(idx_ref, b, cur, 0, cap)

    next_batch = jnp.where(step + 1 < n_steps, step + 1, 0) // n_e
    per_block = cap // (n_ff * n_blocks)

    @pl.when(f == 0)
    def _():
        pltpu.make_async_copy(h2_hbm.at[pl.ds(0, cap * SUBLANES)], xbuf.at[cur], sem.at[cur]).wait()
        for m in range(n_blocks):
            first = m * SLOT_BLOCK * SUBLANES
            xm = jnp.concatenate(
                [xbuf[cur, pl.ds(first + j, SLOT_BLOCK, stride=SUBLANES), :] for j in range(d // LANES)],
                axis=1)
            xb_scr[m] = xm.astype(BF16)
        acc_scr[...] = jnp.zeros_like(acc_scr)

    wg = wg_ref[0].astype(BF16)
    wu = wu_ref[0].astype(BF16)
    wd = wd_ref[0].astype(BF16)
    for m in range(n_blocks):
        first = (f * n_blocks + m) * per_block
        for j in range(per_block):
            token_copy(idx_next_ref, next_batch, 1 - cur, first + j).start()
        xm = xb_scr[m]
        g = jnp.dot(xm, wg, preferred_element_type=F32)
        u = jnp.dot(xm, wu, preferred_element_type=F32)
        act = (jax.nn.silu(g) * u).astype(BF16)
        acc_scr[m] += jnp.dot(act, wd, preferred_element_type=F32)

    @pl.when((step == n_steps - 1) & (f == n_ff - 1))
    def _():
        pltpu.make_async_copy(h2_hbm.at[pl.ds(0, cap * SUBLANES)], xbuf.at[1 - cur],
                              sem.at[1 - cur]).wait()

    @pl.when(f == n_ff - 1)
    def _():
        eye = (lax.broadcasted_iota(jnp.int32, (SLOT_BLOCK, SLOT_BLOCK), 0)
               == lax.broadcasted_iota(jnp.int32, (SLOT_BLOCK, SLOT_BLOCK), 1))
        for m in range(n_blocks):
            gcol = jnp.sum(jnp.where(eye, gs_ref[0, m], 0.0), axis=1, keepdims=True)
            rows = pl.ds(m * SLOT_BLOCK, SLOT_BLOCK)
            eo_ref[0, 0, rows, :] = (gcol * acc_scr[m]).astype(eo_ref.dtype)


def _moe(idx, gs, h2_tiles, wg, wu, wd, bsz, s):
    n_e, d, d_ff = wg.shape
    n_blocks = gs.shape[1]
    cap = n_blocks * SLOT_BLOCK
    n_ff = d_ff // FF_CHUNK
    assert d_ff % FF_CHUNK == 0 and cap % (n_ff * n_blocks) == 0 and cap % GATHER_UNROLL == 0
    assert d == SUBLANES * LANES

    def next_step(b, e, f):
        return ((b * n_e + e + 1) % (bsz * n_e), 0, 0)

    smem = pltpu.SMEM
    return pl.pallas_call(
        functools.partial(_moe_kernel, seq_len=s, n_ff=n_ff),
        grid=(bsz, n_e, n_ff),
        in_specs=[
            pl.BlockSpec((1, 1, cap), lambda b, e, f: (b * n_e + e, 0, 0), memory_space=smem),
            pl.BlockSpec((1, 1, cap), next_step, memory_space=smem),
            pl.BlockSpec((1, n_blocks, 1, SLOT_BLOCK), lambda b, e, f: (b * n_e + e, 0, 0, 0)),
            pl.BlockSpec(memory_space=pl.ANY),
            pl.BlockSpec((1, d, FF_CHUNK), lambda b, e, f: (e, 0, f)),
            pl.BlockSpec((1, d, FF_CHUNK), lambda b, e, f: (e, 0, f)),
            pl.BlockSpec((1, FF_CHUNK, d), lambda b, e, f: (e, f, 0)),
        ],
        out_specs=pl.BlockSpec((1, 1, cap, d), lambda b, e, f: (b, e, 0, 0)),
        out_shape=jax.ShapeDtypeStruct((bsz, n_e, cap, d), BF16),
        scratch_shapes=[pltpu.VMEM((2, cap * SUBLANES, LANES), F32),
                        pltpu.SemaphoreType.DMA((2,)),
                        pltpu.VMEM((n_blocks, SLOT_BLOCK, d), BF16),
                        pltpu.VMEM((n_blocks, SLOT_BLOCK, d), F32)],
        compiler_params=_cparams(("arbitrary", "arbitrary", "arbitrary")),
        name="moe",
    )(idx, idx, gs, h2_tiles, wg, wu, wd)


def _combine_kernel(starts_ref, x1_ref, pos_ref, gf_ref, eo_ref, out_ref, wcat_scr, acc_scr,
                    *, cap):
    for i in range(x1_ref.shape[1] // TOKEN_TILE):
        _combine_tile(starts_ref, x1_ref, pos_ref, gf_ref, eo_ref, out_ref, wcat_scr, acc_scr,
                      pl.program_id(1) * (x1_ref.shape[1] // TOKEN_TILE) + i,
                      pl.ds(i * TOKEN_TILE, TOKEN_TILE), cap)


def _combine_tile(starts_ref, x1_ref, pos_ref, gf_ref, eo_ref, out_ref, wcat_scr, acc_scr,
                  k, rows, cap):
    b = pl.program_id(0)
    n_e = pos_ref.shape[2]
    pos = pos_ref[0, rows, :]
    expert_lane = lax.broadcasted_iota(jnp.int32, (1, n_e), 1)

    s0 = [starts_ref[(b * n_e + e) * LANES + k] for e in range(n_e)]
    s1 = [starts_ref[(b * n_e + e) * LANES + k + 1] for e in range(n_e)]

    def window_starts(rows):
        starts = [jnp.minimum(s // WIN_ALIGN, (cap - rows) // WIN_ALIGN) * WIN_ALIGN for s in s0]
        vec = jnp.zeros((1, n_e), jnp.int32)
        for e in range(n_e):
            vec = jnp.where(expert_lane == e, starts[e], vec)
        return starts, vec

    def onehot(rel, first, count, width):
        n_lanes = count * width
        lane = lax.broadcasted_iota(jnp.int32, (n_e, n_lanes), 1)
        sub = lax.broadcasted_iota(jnp.int32, (n_e, n_lanes), 0)
        expand = jnp.where(sub == first + lane // width, 1.0, 0.0).astype(BF16)
        spread = jnp.dot(jnp.clip(rel, -1, width).astype(F32).astype(BF16), expand,
                         preferred_element_type=F32)
        want = (lax.broadcasted_iota(jnp.int32, (TOKEN_TILE, n_lanes), 1) % width).astype(F32)
        return jnp.where(spread == want, 1.0, 0.0).astype(BF16)

    def finish(acc):
        out_ref[0, rows, :] = _rms_scale(acc) * gf_ref[...]

    narrow, narrow_vec = window_starts(NARROW_WIN)
    fits = None
    for e in range(n_e):
        ok = s1[e] - narrow[e] <= NARROW_WIN
        fits = ok if fits is None else fits & ok

    @pl.when(fits)
    def _():
        for e in range(n_e):
            src = pl.ds(pl.multiple_of(narrow[e], WIN_ALIGN), NARROW_WIN)
            wcat_scr[pl.ds(e * NARROW_WIN, NARROW_WIN), :] = eo_ref[0, e, src, :]
        hit = onehot(pos - narrow_vec, 0, n_e, NARROW_WIN)
        finish(x1_ref[0, rows, :] + jnp.dot(hit, wcat_scr[pl.ds(0, n_e * NARROW_WIN), :],
                                            preferred_element_type=F32))

    @pl.when(jnp.logical_not(fits))
    def _():
        wide, wide_vec = window_starts(WIN_ROWS)
        rel = pos - wide_vec
        group = wcat_scr.shape[0] // SLOT_BLOCK
        acc_scr[...] = x1_ref[0, rows, :]
        for g in range(n_e // group):
            for i in range(group):
                e = g * group + i
                src = pl.ds(pl.multiple_of(wide[e], WIN_ALIGN), SLOT_BLOCK)
                wcat_scr[pl.ds(i * SLOT_BLOCK, SLOT_BLOCK), :] = eo_ref[0, e, src, :]
            acc_scr[...] += jnp.dot(onehot(rel, g * group, group, SLOT_BLOCK), wcat_scr[...],
                                    preferred_element_type=F32)
        for e in range(n_e):
            src = pl.ds(pl.multiple_of(wide[e] + SLOT_BLOCK, WIN_ALIGN), WIN_ALIGN)
            wcat_scr[pl.ds(e * WIN_ALIGN, WIN_ALIGN), :] = eo_ref[0, e, src, :]
        tail = onehot(rel - SLOT_BLOCK, 0, n_e, WIN_ALIGN)
        finish(acc_scr[...] + jnp.dot(tail, wcat_scr[pl.ds(0, n_e * WIN_ALIGN), :],
                                      preferred_element_type=F32))


def _combine(starts_flat, x1, pos_c, eo, gf):
    bsz, s, d = x1.shape
    n_e = pos_c.shape[-1]
    cap = eo.shape[2]
    step_rows = COMBINE_TILES * TOKEN_TILE
    wide_group = 4
    assert n_e % wide_group == 0 and n_e * NARROW_WIN <= wide_group * SLOT_BLOCK
    assert s % step_rows == 0
    grid_spec = pltpu.PrefetchScalarGridSpec(
        num_scalar_prefetch=1,
        grid=(bsz, s // step_rows),
        in_specs=[
            pl.BlockSpec((1, step_rows, d), lambda b, k, st: (b, k, 0)),
            pl.BlockSpec((1, step_rows, n_e), lambda b, k, st: (b, k, 0)),
            pl.BlockSpec((1, d), lambda b, k, st: (0, 0)),
            pl.BlockSpec((1, n_e, cap, d), lambda b, k, st: (b, 0, 0, 0),
                         pipeline_mode=pl.Buffered(1)),
        ],
        out_specs=pl.BlockSpec((1, step_rows, d), lambda b, k, st: (b, k, 0)),
        scratch_shapes=[pltpu.VMEM((wide_group * SLOT_BLOCK, d), BF16),
                        pltpu.VMEM((TOKEN_TILE, d), F32)],
    )
    return pl.pallas_call(
        functools.partial(_combine_kernel, cap=cap),
        grid_spec=grid_spec,
        out_shape=jax.ShapeDtypeStruct((bsz, s, d), F32),
        compiler_params=_cparams(("parallel", "parallel")),
        name="combine",
    )(starts_flat, x1, pos_c, gf, eo)


def _block_diag(w):
    h, hd, _ = w.shape
    eye = jnp.eye(h, dtype=w.dtype)
    return (eye[:, None, :, None] * w[:, :, None, :]).reshape(h * hd, h * hd)


def _lru_params(wa_f, wx_f, wa_b, wx_b, ba_f, bx_f, ba_b, bx_b, lam_f, lam_b):
    d_lru = ba_f.shape[0]
    n_blk = d_lru // LANES
    mats = [0.5 * _block_diag(w) for w in (wa_f, wx_f, wa_b, wx_b)]
    ba_f, bx_f, ba_b, bx_b = (0.5 * v for v in (ba_f, bx_f, ba_b, bx_b))
    w_cat = jnp.stack([
        jnp.concatenate([m[c * LANES:(c + 1) * LANES, c * LANES:(c + 1) * LANES] for m in mats], axis=1)
        for c in range(n_blk)]).astype(BF16)
    b_cat = jnp.stack([
        jnp.concatenate([v[c * LANES:(c + 1) * LANES] for v in (ba_f, bx_f, ba_b, bx_b)])
        for c in range(n_blk)])[:, None, :]
    lam_cat = jnp.stack([
        jnp.concatenate([v[c * LANES:(c + 1) * LANES] for v in (lam_f, lam_b)])
        for c in range(n_blk)])[:, None, :]
    return w_cat, b_cat, lam_cat


def kernel(x, norm1_g, w_in, conv_w, conv_b, lru_wa_f, lru_ba_f, lru_wx_f, lru_bx_f, lru_lam_f,
           lru_wa_b, lru_ba_b, lru_wx_b, lru_bx_b, lru_lam_b, w_out, norm2_g, w_router,
           w_gate, w_up, w_down, normf_g):
    bsz, s, d = x.shape
    d_lru = conv_b.shape[0]
    d_four = w_in.shape[1] - 2 * d_lru
    n_e = w_router.shape[1]
    cap = CAPACITY_FACTOR * s // n_e
    assert s % LRU_CHUNK == 0 and s % (SCAN_GROUPS * SUBLANES * SUBLANES) == 0
    assert s % FOURIER_N2 == 0 and cap % SLOT_BLOCK == 0 and s // TOKEN_TILE < LANES
    assert cap >= WIN_ROWS and w_gate.shape[2] % FF_CHUNK == 0

    lx, lg, fo = _inproj(x.reshape(bsz * s, d), norm1_g[None, :], w_in.astype(BF16), d_lru, d_four)
    lx = lx.reshape(bsz, s, d_lru)
    lg = lg.reshape(bsz, s, d_lru)
    fo = fo.reshape(bsz, s, d_four)

    w_cat, b_cat, lam_cat = _lru_params(lru_wa_f, lru_wx_f, lru_wa_b, lru_wx_b,
                                        lru_ba_f, lru_bx_f, lru_ba_b, lru_bx_b,
                                        lru_lam_f, lru_lam_b)
    y_lru = _lru(lx, lg, conv_w, conv_b[None, :], w_cat, b_cat, lam_cat)
    y_four = _fourier(fo, FOURIER_GROUPS)

    w_out_bf = w_out.astype(BF16)
    x1, h2_tiles, aff_t = _outproj(x, y_lru, y_four, w_out_bf[:d_lru], w_out_bf[d_lru:],
                                   norm2_g[None, :], w_router.T)

    pos_r, starts = _select(aff_t, cap)
    starts_flat = starts.reshape(-1)
    pos_c = jnp.transpose(pos_r, (0, 2, 1))
    pos5 = pos_r.reshape(bsz, n_e, s // TOKEN_TILE, 1, TOKEN_TILE)
    idx, gs = _compact(starts_flat, pos5, jnp.transpose(aff_t, (0, 2, 1)), cap)
    eo = _moe(idx.reshape(bsz * n_e, 1, cap), gs.reshape((bsz * n_e,) + gs.shape[2:]), h2_tiles,
              w_gate, w_up, w_down, bsz, s)
    return _combine(starts_flat, x1, pos_c, eo, normf_g[None, :])
```

```python
import functools
import math

import jax
import jax.numpy as jnp
import numpy as np
from jax import lax
from jax.experimental import pallas as pl
from jax.experimental.pallas import tpu as pltpu

F32 = jnp.float32
BF16 = jnp.bfloat16

EPS = 1e-6
LRU_C = 8.0
CAPACITY_FACTOR = 2
FOURIER_GROUPS = 4

SUBLANES = 8
LANES = 128
MXU_DIM = 256
VMEM_LIMIT_BYTES = 56 * 1024 * 1024

ROW_TILE = 1024
LRU_CHUNK = 2048
SCAN_GROUPS = 4
SLOT_BLOCK = MXU_DIM
TOKEN_TILE = MXU_DIM
SEG_PAD = 4
FOURIER_N2 = LANES
FOURIER_K1_BATCH = 32
FOURIER_U_BATCH = 2
STAGE_PAD = 4
FF_CHUNK = 1024
GATHER_UNROLL = 8
COMPACT_WIN = 64
TOKEN_ID_BASE = 64
GATE_COL0 = 16
WIN_ALIGN = 16
WIN_ROWS = SLOT_BLOCK + WIN_ALIGN
NARROW_WIN = 64
COMBINE_TILES = 2


def _cparams(semantics):
    return pltpu.CompilerParams(dimension_semantics=semantics,
                                vmem_limit_bytes=VMEM_LIMIT_BYTES)


def _rms_scale(x):
    return x * lax.rsqrt(jnp.mean(x * x, axis=-1, keepdims=True) + EPS)


def _inproj_kernel(x_ref, g_ref, w_ref, lx_ref, lg_ref, fo_ref):
    h = _rms_scale(x_ref[...]) * g_ref[...]
    p = jnp.dot(h.astype(BF16), w_ref[...], preferred_element_type=F32)
    d = lx_ref.shape[-1]
    lx_ref[...] = p[:, :d]
    lg_ref[...] = p[:, d:2 * d]
    fo_ref[...] = p[:, 2 * d:]


def _inproj(x2, g, w_bf, d_lru, d_four):
    m, d = x2.shape
    n = w_bf.shape[1]
    return pl.pallas_call(
        _inproj_kernel,
        grid=(m // ROW_TILE,),
        in_specs=[
            pl.BlockSpec((ROW_TILE, d), lambda i: (i, 0)),
            pl.BlockSpec((1, d), lambda i: (0, 0)),
            pl.BlockSpec((d, n), lambda i: (0, 0)),
        ],
        out_specs=[
            pl.BlockSpec((ROW_TILE, d_lru), lambda i: (i, 0)),
            pl.BlockSpec((ROW_TILE, d_lru), lambda i: (i, 0)),
            pl.BlockSpec((ROW_TILE, d_four), lambda i: (i, 0)),
        ],
        out_shape=[
            jax.ShapeDtypeStruct((m, d_lru), F32),
            jax.ShapeDtypeStruct((m, d_lru), F32),
            jax.ShapeDtypeStruct((m, d_four), F32),
        ],
        compiler_params=_cparams(("parallel",)),
        name="inproj",
    )(x2, g, w_bf)


def _shift_rows(x, shift):
    n = x.shape[0]
    rows = lax.broadcasted_iota(jnp.int32, x.shape, 0)
    rolled = pltpu.roll(x, shift % n, axis=0)
    keep = (rows >= shift) if shift > 0 else (rows < n + shift)
    return jnp.where(keep, rolled, 0.0)


def _lru_kernel(lx_ref, lg_ref, cw_ref, cb_ref, w_ref, b_ref, lam_ref, y_ref,
                xpad, af, bf, ab, bb):
    s = lx_ref.shape[1]
    c_blk = lx_ref.shape[2]
    n_seg = SCAN_GROUPS * SUBLANES
    seg = s // n_seg
    seg_stride = seg + SEG_PAD
    piece = min(seg, LRU_CHUNK)
    n_chunks = s // LRU_CHUNK
    pad = SUBLANES

    zeros_pad = jnp.zeros((pad, c_blk), F32)
    xpad[pl.ds(0, pad), :] = zeros_pad
    xpad[pl.ds(pad + s, pad), :] = zeros_pad

    def copy_chunk(i, carry):
        t0 = pl.multiple_of(i * LRU_CHUNK, LRU_CHUNK)
        xpad[pl.ds(pad + t0, LRU_CHUNK), :] = lx_ref[0, pl.ds(t0, LRU_CHUNK), :]
        return carry

    lax.fori_loop(0, n_chunks, copy_chunk, 0)

    cw = cw_ref[...]
    conv_width = cw.shape[0]
    cb = cb_ref[...]
    bias = b_ref[0]
    quarter_c_log_sig = (0.25 * LRU_C) * jax.nn.log_sigmoid(lam_ref[0])
    w_cat = w_ref[0]

    def gates_chunk(i, carry):
        t0 = pl.multiple_of(i * LRU_CHUNK, LRU_CHUNK)
        c = cb
        left = conv_width // 2
        for k in range(conv_width):
            c = c + cw[k:k + 1, :] * xpad[pl.ds(t0 + (pad - left + k), LRU_CHUNK), :]
        z = jnp.dot(c.astype(BF16), w_cat, preferred_element_type=F32) + bias
        for d, (a_scr, b_scr) in enumerate(((af, bf), (ab, bb))):
            q = quarter_c_log_sig[:, d * c_blk:(d + 1) * c_blk]
            th_r = jnp.tanh(z[:, (2 * d) * c_blk:(2 * d + 1) * c_blk])
            th_i = jnp.tanh(z[:, (2 * d + 1) * c_blk:(2 * d + 2) * c_blk])
            t = jnp.tanh(q + q * th_r)
            inv = 1.0 / (1.0 - t)
            a = (1.0 + t) * inv
            bt = (jnp.sqrt(-t) * inv) * (c + c * th_i)
            for p in range(LRU_CHUNK // piece):
                t = t0 + p * piece
                dst = pl.ds((t // seg) * seg_stride + t % seg, piece)
                a_scr[dst, :] = a[p * piece:(p + 1) * piece]
                b_scr[dst, :] = bt[p * piece:(p + 1) * piece]
        return carry

    lax.fori_loop(0, n_chunks, gates_chunk, 0)

    def scan_step(i, carry):
        out = []
        for (a_scr, b_scr, o), (hs, ps) in zip(((af, bf, i), (ab, bb, seg - 1 - i)), carry):
            new_h, new_p = [], []
            for q in range(SCAN_GROUPS):
                rows = pl.ds(q * SUBLANES * seg_stride + o, SUBLANES, stride=seg_stride)
                a = a_scr[rows, :]
                h = a * hs[q] + b_scr[rows, :]
                p = a * ps[q]
                b_scr[rows, :] = h
                a_scr[rows, :] = p
                new_h.append(h)
                new_p.append(p)
            out.append((tuple(new_h), tuple(new_p)))
        return tuple(out)

    zero = jnp.zeros((SUBLANES, c_blk), F32)
    one = jnp.ones((SUBLANES, c_blk), F32)
    init = ((zero,) * SCAN_GROUPS, (one,) * SCAN_GROUPS)
    (hf_end, pf_end), (hb_end, pb_end) = lax.fori_loop(0, seg, scan_step, (init, init))

    row = lax.broadcasted_iota(jnp.int32, (SUBLANES, c_blk), 0)

    def entering(h_end, p_end, carry_in, first_row, shift):
        c = zero
        for _ in range(SUBLANES):
            c = jnp.where(row == first_row, carry_in, _shift_rows(h_end + p_end * c, shift))
        return c

    cf = [None] * SCAN_GROUPS
    cbk = [None] * SCAN_GROUPS
    carry_f = jnp.zeros((1, c_blk), F32)
    carry_b = jnp.zeros((1, c_blk), F32)
    for q in range(SCAN_GROUPS):
        cf[q] = entering(hf_end[q], pf_end[q], carry_f, 0, 1)
        carry_f = (hf_end[q] + pf_end[q] * cf[q])[SUBLANES - 1:SUBLANES, :]
        qb = SCAN_GROUPS - 1 - q
        cbk[qb] = entering(hb_end[qb], pb_end[qb], carry_b, SUBLANES - 1, -1)
        carry_b = (hb_end[qb] + pb_end[qb] * cbk[qb])[0:1, :]

    for j in range(n_seg):
        q, i = divmod(j, SUBLANES)
        src = pl.ds(j * seg_stride, seg)
        rows = pl.ds(j * seg, seg)
        hsum = ((bf[src, :] + af[src, :] * cf[q][i:i + 1, :])
                + (bb[src, :] + ab[src, :] * cbk[q][i:i + 1, :]))
        y_ref[0, rows, :] = (jax.nn.gelu(lg_ref[0, rows, :]) * hsum).astype(y_ref.dtype)


def _lru(lx, lg, conv_w, conv_b, w_cat, b_cat, lam_cat):
    bsz, s, d_lru = lx.shape
    c_blk = LANES
    n_blk = d_lru // c_blk
    scr = pltpu.VMEM((s + SCAN_GROUPS * SUBLANES * SEG_PAD, c_blk), F32)
    return pl.pallas_call(
        _lru_kernel,
        grid=(bsz, n_blk),
        in_specs=[
            pl.BlockSpec((1, s, c_blk), lambda b, c: (b, 0, c)),
            pl.BlockSpec((1, s, c_blk), lambda b, c: (b, 0, c)),
            pl.BlockSpec((4, c_blk), lambda b, c: (0, c)),
            pl.BlockSpec((1, c_blk), lambda b, c: (0, c)),
            pl.BlockSpec((1, c_blk, 4 * c_blk), lambda b, c: (c, 0, 0)),
            pl.BlockSpec((1, 1, 4 * c_blk), lambda b, c: (c, 0, 0)),
            pl.BlockSpec((1, 1, 2 * c_blk), lambda b, c: (c, 0, 0)),
        ],
        out_specs=pl.BlockSpec((1, s, c_blk), lambda b, c: (b, 0, c)),
        out_shape=jax.ShapeDtypeStruct((bsz, s, d_lru), BF16),
        scratch_shapes=[pltpu.VMEM((s + 2 * SUBLANES, c_blk), F32), scr, scr, scr, scr],
        compiler_params=_cparams(("parallel", "parallel")),
        name="lru",
    )(lx, lg, conv_w, conv_b, w_cat, b_cat, lam_cat)


def _fourier_kernel(f_ref, ka_ref, kb_ref, cc_ref, sc_ref, twc_ref, tws_ref, y_ref,
                    ar_scr, ai_scr, y_scr, *, scale):
    n1 = f_ref.shape[1]
    n_u = f_ref.shape[2]
    cols = f_ref.shape[4]
    r = SUBLANES * n1
    n2 = n_u * SUBLANES
    stage_stride = n1 + STAGE_PAD

    def stage_a(i, carry):
        us = tuple(FOURIER_U_BATCH * i + j for j in range(FOURIER_U_BATCH))
        xu = jnp.concatenate([f_ref[0, :, u].reshape(r, cols) for u in us], axis=1)
        a = jnp.dot(ka_ref[...], xu.astype(BF16), preferred_element_type=F32)
        for j, u in enumerate(us):
            a_re = a[:r, j * cols:(j + 1) * cols]
            a_im = a[r:, j * cols:(j + 1) * cols]
            c = twc_ref[u]
            sn = tws_ref[u]
            ar_scr[u] = a_re * c - a_im * sn
            ai_scr[u] = a_re * sn + a_im * c
        return carry

    lax.fori_loop(0, n_u // FOURIER_U_BATCH, stage_a, 0)

    nb = FOURIER_K1_BATCH

    def stage_b(i, carry):
        rows = pl.ds(pl.multiple_of(i * (nb * SUBLANES), nb * SUBLANES), nb * SUBLANES)
        a_re = ar_scr[:, rows, :]
        a_im = ai_scr[:, rows, :]

        def rows_s2(a, j):
            return a[:, j * SUBLANES:(j + 1) * SUBLANES, :].reshape(n2, cols)

        st = jnp.concatenate(
            [jnp.concatenate([rows_s2(a_re, j), rows_s2(a_im, j)], axis=0) for j in range(nb)],
            axis=1).astype(BF16)
        x = jnp.dot(kb_ref[...], st, preferred_element_type=F32)
        x_re = jnp.concatenate([x[:n2, j * cols:(j + 1) * cols] for j in range(nb)], axis=0)
        x_im = jnp.concatenate([x[n2:, j * cols:(j + 1) * cols] for j in range(nb)], axis=0)
        y = (jnp.dot(x_re.astype(BF16), cc_ref[...], preferred_element_type=F32)
             + jnp.dot(x_im.astype(BF16), sc_ref[...], preferred_element_type=F32)) * scale
        for j in range(nb):
            y_scr[pl.ds(i * nb + j, n2, stride=stage_stride), :] = y[j * n2:(j + 1) * n2]
        return carry

    lax.fori_loop(0, n1 // nb, stage_b, 0)

    def copy_out(k2, carry):
        y_ref[0, pl.ds(pl.multiple_of(k2 * n1, n1), n1), :] = y_scr[pl.ds(k2 * stage_stride, n1), :]
        return carry

    lax.fori_loop(0, n2, copy_out, 0)


def _dft_tables(s, c_grp):
    n2 = FOURIER_N2
    n1 = s // n2
    n_u = n2 // SUBLANES

    def cos_sin(num, den):
        ang = (2.0 * np.pi / den) * (num % den).astype(np.float64)
        return np.cos(ang), np.sin(ang)

    def const(a, dtype=F32):
        return jnp.asarray(a.astype(np.float32)).astype(dtype)

    i1 = np.arange(n1, dtype=np.int64)
    c1, s1 = cos_sin(i1[:, None] * i1[None, :], n1)
    eye8 = np.eye(SUBLANES)
    ka = const(np.concatenate([np.kron(c1, eye8), np.kron(-s1, eye8)], axis=0), BF16)

    i2 = np.arange(n2, dtype=np.int64)
    c2, s2 = cos_sin(i2[:, None] * i2[None, :], n2)
    kb = const(np.block([[c2, s2], [-s2, c2]]), BF16)

    ic = np.arange(c_grp, dtype=np.int64)
    cc, sc = cos_sin(ic[:, None] * ic[None, :], c_grp)

    u = np.arange(n_u, dtype=np.int64)[:, None, None]
    k1 = np.arange(n1, dtype=np.int64)[None, :, None]
    v = np.arange(SUBLANES, dtype=np.int64)[None, None, :]
    tc, ts = cos_sin(k1 * (SUBLANES * u + v), s)
    shape = (n_u, n1 * SUBLANES, c_grp)
    twc = jnp.broadcast_to(const(tc.reshape(n_u, -1, 1)), shape)
    tws = jnp.broadcast_to(const(-ts.reshape(n_u, -1, 1)), shape)
    return ka, kb, const(cc, BF16), const(sc, BF16), twc, tws


def _fourier(four, n_groups):
    bsz, s, d_four = four.shape
    c_grp = d_four // n_groups
    n2 = FOURIER_N2
    n1 = s // n2
    n_u = n2 // SUBLANES
    r = SUBLANES * n1
    ka, kb, cc, sc, twc, tws = _dft_tables(s, c_grp)
    f5 = four.reshape(bsz, n1, n_u, SUBLANES, d_four)
    scale = 1.0 / math.sqrt(s * c_grp)
    const2 = lambda b, g: (0, 0)
    const3 = lambda b, g: (0, 0, 0)
    return pl.pallas_call(
        functools.partial(_fourier_kernel, scale=scale),
        grid=(bsz, n_groups),
        in_specs=[
            pl.BlockSpec((1, n1, n_u, SUBLANES, c_grp), lambda b, g: (b, 0, 0, 0, g)),
            pl.BlockSpec(ka.shape, const2),
            pl.BlockSpec(kb.shape, const2),
            pl.BlockSpec(cc.shape, const2),
            pl.BlockSpec(sc.shape, const2),
            pl.BlockSpec(twc.shape, const3),
            pl.BlockSpec(tws.shape, const3),
        ],
        out_specs=pl.BlockSpec((1, s, c_grp), lambda b, g: (b, 0, g)),
        out_shape=jax.ShapeDtypeStruct((bsz, s, d_four), F32),
        scratch_shapes=[pltpu.VMEM((n_u, r, c_grp), F32), pltpu.VMEM((n_u, r, c_grp), F32),
                        pltpu.VMEM((n2 * (n1 + STAGE_PAD), c_grp), F32)],
        compiler_params=_cparams(("parallel", "parallel")),
        name="fourier",
    )(f5, ka, kb, cc, sc, twc, tws)


def _outproj_kernel(x_ref, yl_ref, yf_ref, wo1_ref, wo2_ref, g2_ref, wr_ref,
                    x1_ref, h2_ref, aff_ref):
    x1 = (x_ref[0]
          + jnp.dot(yl_ref[0], wo1_ref[...], preferred_element_type=F32)
          + jnp.dot(yf_ref[0].astype(BF16), wo2_ref[...], preferred_element_type=F32))
    x1_ref[0] = x1
    h2 = _rms_scale(x1) * g2_ref[...]
    for j in range(h2.shape[1] // LANES):
        h2_ref[pl.ds(j, h2.shape[0], stride=SUBLANES), :] = h2[:, j * LANES:(j + 1) * LANES]
    def split(v):
        hi = v.astype(BF16)
        return hi, (v - hi.astype(F32)).astype(BF16)

    def dot_nt(a, bm):
        return lax.dot_general(a, bm, (((1,), (1,)), ((), ())), preferred_element_type=F32)

    w_hi, w_lo = split(wr_ref[...])
    h_hi, h_lo = split(h2)
    logits = dot_nt(w_hi, h_hi) + (dot_nt(w_lo, h_hi) + dot_nt(w_hi, h_lo))
    ex = jnp.exp(logits - jnp.max(logits, axis=0, keepdims=True))
    aff_ref[0] = ex / jnp.sum(ex, axis=0, keepdims=True)


def _outproj(x, y_lru, y_four, wo1, wo2, g2, wr_t):
    bsz, s, d = x.shape
    d_lru = y_lru.shape[-1]
    d_four = y_four.shape[-1]
    n_e = wr_t.shape[0]
    assert d == SUBLANES * LANES
    n_i = s // ROW_TILE
    tile = lambda w: pl.BlockSpec((1, ROW_TILE, w), lambda b, i: (b, i, 0))
    const = lambda shape: pl.BlockSpec(shape, lambda b, i: (0, 0))
    return pl.pallas_call(
        _outproj_kernel,
        grid=(bsz, n_i),
        in_specs=[tile(d), tile(d_lru), tile(d_four), const(wo1.shape), const(wo2.shape),
                  const(g2.shape), const(wr_t.shape)],
        out_specs=[tile(d),
                   pl.BlockSpec((ROW_TILE * SUBLANES, LANES), lambda b, i: (b * n_i + i, 0)),
                   pl.BlockSpec((1, n_e, ROW_TILE), lambda b, i: (b, 0, i))],
        out_shape=[
            jax.ShapeDtypeStruct((bsz, s, d), F32),
            jax.ShapeDtypeStruct((bsz * s * SUBLANES, LANES), F32),
            jax.ShapeDtypeStruct((bsz, n_e, s), F32),
        ],
        compiler_params=_cparams(("parallel", "parallel")),
        name="outproj",
    )(x, y_lru, y_four, wo1, wo2, g2, wr_t)


def _select_kernel(aff_ref, tri_ref, pos_ref, starts_ref, *, cap):
    v = aff_ref[0]
    n_e, s = v.shape
    cap_f = float(cap)

    def midpoint(lo, hi):
        mid = 0.5 * (lo + hi)
        return mid, (mid > lo) & (mid < hi)

    def cond(carry):
        _, active = midpoint(*carry)
        return jnp.max(active.astype(F32)) > 0.0

    def body(carry):
        lo, hi = carry
        mid, active = midpoint(lo, hi)
        cnt = jnp.sum((v >= mid).astype(F32), axis=1, keepdims=True)
        enough = cnt >= cap_f
        return (jnp.where(active & enough, mid, lo), jnp.where(active & (~enough), mid, hi))

    lo0 = jnp.zeros((n_e, 1), F32)
    hi0 = jnp.full((n_e, 1), 2.0, F32)
    thr, _ = lax.while_loop(cond, body, (lo0, hi0))

    above = v > thr
    tie = v == thr
    need = cap_f - jnp.sum(above.astype(F32), axis=1, keepdims=True)

    n_tiles = s // TOKEN_TILE
    tri = tri_ref[...]
    lane = lax.broadcasted_iota(jnp.int32, (n_e, LANES), 1)

    def prefix(mask_f, want_starts):
        run = jnp.zeros((n_e, 1), F32)
        starts = jnp.zeros((n_e, LANES), F32)
        pieces = []
        for t in range(n_tiles):
            m = mask_f[:, t * TOKEN_TILE:(t + 1) * TOKEN_TILE]
            incl = jnp.dot(m.astype(BF16), tri, preferred_element_type=F32)
            pieces.append(run + incl - m)
            if want_starts:
                starts = jnp.where(lane == t, run, starts)
            run = run + incl[:, TOKEN_TILE - 1:TOKEN_TILE]
        if want_starts:
            starts = jnp.where(lane == n_tiles, run, starts)
        return jnp.concatenate(pieces, axis=1), starts

    tie_rank, _ = prefix(tie.astype(F32), False)
    sel = above | (tie & (tie_rank < need))
    pos, starts = prefix(sel.astype(F32), True)
    pos_ref[0] = jnp.where(sel, pos, -1.0).astype(jnp.int32)
    starts_ref[0] = starts.astype(jnp.int32)


def _select(aff_t, cap):
    bsz, n_e, s = aff_t.shape
    idx = jnp.arange(TOKEN_TILE, dtype=jnp.int32)
    tri = (idx[:, None] <= idx[None, :]).astype(BF16)
    return pl.pallas_call(
        functools.partial(_select_kernel, cap=cap),
        grid=(bsz,),
        in_specs=[
            pl.BlockSpec((1, n_e, s), lambda b: (b, 0, 0)),
            pl.BlockSpec(tri.shape, lambda b: (0, 0)),
        ],
        out_specs=[
            pl.BlockSpec((1, n_e, s), lambda b: (b, 0, 0)),
            pl.BlockSpec((1, n_e, LANES), lambda b: (b, 0, 0)),
        ],
        out_shape=[
            jax.ShapeDtypeStruct((bsz, n_e, s), jnp.int32),
            jax.ShapeDtypeStruct((bsz, n_e, LANES), jnp.int32),
        ],
        compiler_params=_cparams(("parallel",)),
        name="select",
    )(aff_t, tri)


def _token_table(gate, first_token):
    n_tok, n_e = gate.shape
    lane = lax.broadcasted_iota(jnp.int32, (n_tok, LANES), 1)
    tok = first_token + lax.broadcasted_iota(jnp.int32, (n_tok, LANES), 0)
    table = jnp.where(lane == 0, tok // TOKEN_ID_BASE, jnp.where(lane == 1, tok % TOKEN_ID_BASE, 0))
    table = table.astype(F32)
    place_lane = lax.broadcasted_iota(jnp.int32, (n_e, LANES), 1)
    place_sub = lax.broadcasted_iota(jnp.int32, (n_e, LANES), 0)
    rest = gate
    for j in range(3):
        piece = rest.astype(BF16)
        rest = rest - piece.astype(F32)
        place = jnp.where(place_lane == GATE_COL0 + j * n_e + place_sub, 1.0, 0.0).astype(BF16)
        table = table + jnp.dot(piece, place, preferred_element_type=F32)
    return table.astype(BF16)


def _compact_kernel(starts_ref, pos_ref, gate_ref, idx_ref, gs_ref, r_scr, *, cap):
    b = pl.program_id(0)
    n_e = pos_ref.shape[1]
    n_k = pos_ref.shape[2]
    n_blocks = cap // SLOT_BLOCK
    wide_windows = -(-(TOKEN_TILE + SUBLANES) // COMPACT_WIN)
    r_scr[...] = jnp.zeros_like(r_scr)
    sub = lax.broadcasted_iota(jnp.int32, (COMPACT_WIN, TOKEN_TILE), 0)

    def tile(k, carry):
        rows = pl.ds(pl.multiple_of(k * TOKEN_TILE, TOKEN_TILE), TOKEN_TILE)
        vk = _token_table(gate_ref[0, rows, :], k * TOKEN_TILE)
        starts = []
        fits = None
        for e in range(n_e):
            base = (b * n_e + e) * LANES
            ws = (starts_ref[base + k] // SUBLANES) * SUBLANES
            ok = starts_ref[base + k + 1] - ws <= COMPACT_WIN
            fits = ok if fits is None else fits & ok
            starts.append(ws)

        def place(e, ws):
            hit = pos_ref[0, e, k] - ws == sub
            vals = jnp.dot(jnp.where(hit, 1.0, 0.0).astype(BF16), vk, preferred_element_type=F32)
            r_scr[e, pl.ds(pl.multiple_of(ws, SUBLANES), COMPACT_WIN), :] += vals

        @pl.when(fits)
        def _():
            for e in range(n_e):
                place(e, starts[e])

        @pl.when(jnp.logical_not(fits))
        def _():
            for e in range(n_e):
                for j in range(wide_windows):
                    place(e, starts[e] + j * COMPACT_WIN)

        return carry

    lax.fori_loop(0, n_k, tile, 0)

    lane = lax.broadcasted_iota(jnp.int32, (SUBLANES, LANES), 1)
    sub8 = lax.broadcasted_iota(jnp.int32, (SUBLANES, LANES), 0)
    id_rows = jnp.where(sub8 == 0, jnp.where(lane == 0, float(TOKEN_ID_BASE),
                                             jnp.where(lane == 1, 1.0, 0.0)), 0.0)

    def finish(e, carry):
        gate_lane = (lane >= GATE_COL0) & (lane < GATE_COL0 + 3 * n_e) & ((lane - GATE_COL0) % n_e == e)
        selector = (id_rows + jnp.where((sub8 == 1) & gate_lane, 1.0, 0.0)).astype(BF16)
        for m in range(n_blocks):
            blk = r_scr[e, pl.ds(m * SLOT_BLOCK, SLOT_BLOCK), :].astype(BF16)
            rows = lax.dot_general(selector, blk, (((1,), (1,)), ((), ())),
                                   preferred_element_type=F32)
            idx_ref[0, e, m] = rows[0:1, :].astype(jnp.int32)
            gs_ref[0, e, m] = rows[1:2, :]
        return carry

    lax.fori_loop(0, n_e, finish, 0)


def _compact(starts_flat, pos5, gate_c, cap):
    bsz, n_e, n_k = pos5.shape[:3]
    s = gate_c.shape[1]
    assert s <= TOKEN_ID_BASE * 256 and GATE_COL0 + 3 * n_e <= LANES
    n_blocks = cap // SLOT_BLOCK
    out_block = (1, n_e, n_blocks, 1, SLOT_BLOCK)
    out_spec = pl.BlockSpec(out_block, lambda b, st: (b, 0, 0, 0, 0))
    wide_rows = -(-(TOKEN_TILE + SUBLANES) // COMPACT_WIN) * COMPACT_WIN
    grid_spec = pltpu.PrefetchScalarGridSpec(
        num_scalar_prefetch=1,
        grid=(bsz,),
        in_specs=[pl.BlockSpec((1, n_e, n_k, 1, TOKEN_TILE), lambda b, st: (b, 0, 0, 0, 0)),
                  pl.BlockSpec((1, s, n_e), lambda b, st: (b, 0, 0))],
        out_specs=[out_spec, out_spec],
        scratch_shapes=[pltpu.VMEM((n_e, cap + wide_rows, LANES), F32)],
    )
    return pl.pallas_call(
        functools.partial(_compact_kernel, cap=cap),
        grid_spec=grid_spec,
        out_shape=[jax.ShapeDtypeStruct((bsz,) + out_block[1:], jnp.int32),
                   jax.ShapeDtypeStruct((bsz,) + out_block[1:], F32)],
        compiler_params=_cparams(("parallel",)),
        name="compact",
    )(starts_flat, pos5, gate_c)


def _moe_kernel(idx_ref, idx_next_ref, gs_ref, h2_hbm, wg_ref, wu_ref, wd_ref, eo_ref,
                xbuf, sem, xb_scr, acc_scr, *, seq_len, n_ff):
    b = pl.program_id(0)
    e = pl.program_id(1)
    f = pl.program_id(2)
    n_e = pl.num_programs(1)
    n_blocks = xb_scr.shape[0]
    cap = n_blocks * SLOT_BLOCK
    d = xb_scr.shape[2]
    step = b * n_e + e
    n_steps = pl.num_programs(0) * n_e
    cur = step % 2

    def token_copy(idx_smem, batch, buf, p):
        tok = idx_smem[0, 0, p]
        src = h2_hbm.at[pl.ds(pl.multiple_of((batch * seq_len + tok) * SUBLANES, SUBLANES), SUBLANES)]
        dst = xbuf.at[buf, pl.ds(pl.multiple_of(p * SUBLANES, SUBLANES), SUBLANES)]
        return pltpu.make_async_copy(src, dst, sem.at[buf])

    def request(idx_smem, batch, buf, first, count):
        def body(i, carry):
            for j in range(GATHER_UNROLL):
                token_copy(idx_smem, batch, buf, first + i * GATHER_UNROLL + j).start()
            return carry
        lax.fori_loop(0, count // GATHER_UNROLL, body, 0)

    @pl.when((step == 0) & (f == 0))
    def _():
        request(idx_ref, b, cur, 0, cap)

    next_batch = jnp.where(step + 1 < n_steps, step + 1, 0) // n_e
    per_block = cap // (n_ff * n_blocks)

    @pl.when(f == 0)
    def _():
        pltpu.make_async_copy(h2_hbm.at[pl.ds(0, cap * SUBLANES)], xbuf.at[cur], sem.at[cur]).wait()
        for m in range(n_blocks):
            first = m * SLOT_BLOCK * SUBLANES
            xm = jnp.concatenate(
                [xbuf[cur, pl.ds(first + j, SLOT_BLOCK, stride=SUBLANES), :] for j in range(d // LANES)],
                axis=1)
            xb_scr[m] = xm.astype(BF16)
        acc_scr[...] = jnp.zeros_like(acc_scr)

    wg = wg_ref[0].astype(BF16)
    wu = wu_ref[0].astype(BF16)
    wd = wd_ref[0].astype(BF16)
    for m in range(n_blocks):
        first = (f * n_blocks + m) * per_block
        for j in range(per_block):
            token_copy(idx_next_ref, next_batch, 1 - cur, first + j).start()
        xm = xb_scr[m]
        g = jnp.dot(xm, wg, preferred_element_type=F32)
        u = jnp.dot(xm, wu, preferred_element_type=F32)
        act = (jax.nn.silu(g) * u).astype(BF16)
        acc_scr[m] += jnp.dot(act, wd, preferred_element_type=F32)

    @pl.when((step == n_steps - 1) & (f == n_ff - 1))
    def _():
        pltpu.make_async_copy(h2_hbm.at[pl.ds(0, cap * SUBLANES)], xbuf.at[1 - cur],
                              sem.at[1 - cur]).wait()

    @pl.when(f == n_ff - 1)
    def _():
        eye = (lax.broadcasted_iota(jnp.int32, (SLOT_BLOCK, SLOT_BLOCK), 0)
               == lax.broadcasted_iota(jnp.int32, (SLOT_BLOCK, SLOT_BLOCK), 1))
        for m in range(n_blocks):
            gcol = jnp.sum(jnp.where(eye, gs_ref[0, m], 0.0), axis=1, keepdims=True)
            rows = pl.ds(m * SLOT_BLOCK, SLOT_BLOCK)
            eo_ref[0, 0, rows, :] = (gcol * acc_scr[m]).astype(eo_ref.dtype)


def _moe(idx, gs, h2_tiles, wg, wu, wd, bsz, s):
    n_e, d, d_ff = wg.shape
    n_blocks = gs.shape[1]
    cap = n_blocks * SLOT_BLOCK
    n_ff = d_ff // FF_CHUNK
    assert d_ff % FF_CHUNK == 0 and cap % (n_ff * n_blocks) == 0 and cap % GATHER_UNROLL == 0
    assert d == SUBLANES * LANES

    def next_step(b, e, f):
        return ((b * n_e + e + 1) % (bsz * n_e), 0, 0)

    smem = pltpu.SMEM
    return pl.pallas_call(
        functools.partial(_moe_kernel, seq_len=s, n_ff=n_ff),
        grid=(bsz, n_e, n_ff),
        in_specs=[
            pl.BlockSpec((1, 1, cap), lambda b, e, f: (b * n_e + e, 0, 0), memory_space=smem),
            pl.BlockSpec((1, 1, cap), next_step, memory_space=smem),
            pl.BlockSpec((1, n_blocks, 1, SLOT_BLOCK), lambda b, e, f: (b * n_e + e, 0, 0, 0)),
            pl.BlockSpec(memory_space=pl.ANY),
            pl.BlockSpec((1, d, FF_CHUNK), lambda b, e, f: (e, 0, f)),
            pl.BlockSpec((1, d, FF_CHUNK), lambda b, e, f: (e, 0, f)),
            pl.BlockSpec((1, FF_CHUNK, d), lambda b, e, f: (e, f, 0)),
        ],
        out_specs=pl.BlockSpec((1, 1, cap, d), lambda b, e, f: (b, e, 0, 0)),
        out_shape=jax.ShapeDtypeStruct((bsz, n_e, cap, d), BF16),
        scratch_shapes=[pltpu.VMEM((2, cap * SUBLANES, LANES), F32),
                        pltpu.SemaphoreType.DMA((2,)),
                        pltpu.VMEM((n_blocks, SLOT_BLOCK, d), BF16),
                        pltpu.VMEM((n_blocks, SLOT_BLOCK, d), F32)],
        compiler_params=_cparams(("arbitrary", "arbitrary", "arbitrary")),
        name="moe",
    )(idx, idx, gs, h2_tiles, wg, wu, wd)


def _combine_kernel(starts_ref, x1_ref, pos_ref, gf_ref, eo_ref, out_ref, wcat_scr, acc_scr,
                    *, cap):
    for i in range(x1_ref.shape[1] // TOKEN_TILE):
        _combine_tile(starts_ref, x1_ref, pos_ref, gf_ref, eo_ref, out_ref, wcat_scr, acc_scr,
                      pl.program_id(1) * (x1_ref.shape[1] // TOKEN_TILE) + i,
                      pl.ds(i * TOKEN_TILE, TOKEN_TILE), cap)


def _combine_tile(starts_ref, x1_ref, pos_ref, gf_ref, eo_ref, out_ref, wcat_scr, acc_scr,
                  k, rows, cap):
    b = pl.program_id(0)
    n_e = pos_ref.shape[2]
    pos = pos_ref[0, rows, :]
    expert_lane = lax.broadcasted_iota(jnp.int32, (1, n_e), 1)

    s0 = [starts_ref[(b * n_e + e) * LANES + k] for e in range(n_e)]
    s1 = [starts_ref[(b * n_e + e) * LANES + k + 1] for e in range(n_e)]

    def window_starts(rows):
        starts = [jnp.minimum(s // WIN_ALIGN, (cap - rows) // WIN_ALIGN) * WIN_ALIGN for s in s0]
        vec = jnp.zeros((1, n_e), jnp.int32)
        for e in range(n_e):
            vec = jnp.where(expert_lane == e, starts[e], vec)
        return starts, vec

    def onehot(rel, first, count, width):
        n_lanes = count * width
        lane = lax.broadcasted_iota(jnp.int32, (n_e, n_lanes), 1)
        sub = lax.broadcasted_iota(jnp.int32, (n_e, n_lanes), 0)
        expand = jnp.where(sub == first + lane // width, 1.0, 0.0).astype(BF16)
        spread = jnp.dot(jnp.clip(rel, -1, width).astype(F32).astype(BF16), expand,
                         preferred_element_type=F32)
        want = (lax.broadcasted_iota(jnp.int32, (TOKEN_TILE, n_lanes), 1) % width).astype(F32)
        return jnp.where(spread == want, 1.0, 0.0).astype(BF16)

    def finish(acc):
        out_ref[0, rows, :] = _rms_scale(acc) * gf_ref[...]

    narrow, narrow_vec = window_starts(NARROW_WIN)
    fits = None
    for e in range(n_e):
        ok = s1[e] - narrow[e] <= NARROW_WIN
        fits = ok if fits is None else fits & ok

    @pl.when(fits)
    def _():
        for e in range(n_e):
            src = pl.ds(pl.multiple_of(narrow[e], WIN_ALIGN), NARROW_WIN)
            wcat_scr[pl.ds(e * NARROW_WIN, NARROW_WIN), :] = eo_ref[0, e, src, :]
        hit = onehot(pos - narrow_vec, 0, n_e, NARROW_WIN)
        finish(x1_ref[0, rows, :] + jnp.dot(hit, wcat_scr[pl.ds(0, n_e * NARROW_WIN), :],
                                            preferred_element_type=F32))

    @pl.when(jnp.logical_not(fits))
    def _():
        wide, wide_vec = window_starts(WIN_ROWS)
        rel = pos - wide_vec
        group = wcat_scr.shape[0] // SLOT_BLOCK
        acc_scr[...] = x1_ref[0, rows, :]
        for g in range(n_e // group):
            for i in range(group):
                e = g * group + i
                src = pl.ds(pl.multiple_of(wide[e], WIN_ALIGN), SLOT_BLOCK)
                wcat_scr[pl.ds(i * SLOT_BLOCK, SLOT_BLOCK), :] = eo_ref[0, e, src, :]
            acc_scr[...] += jnp.dot(onehot(rel, g * group, group, SLOT_BLOCK), wcat_scr[...],
                                    preferred_element_type=F32)
        for e in range(n_e):
            src = pl.ds(pl.multiple_of(wide[e] + SLOT_BLOCK, WIN_ALIGN), WIN_ALIGN)
            wcat_scr[pl.ds(e * WIN_ALIGN, WIN_ALIGN), :] = eo_ref[0, e, src, :]
        tail = onehot(rel - SLOT_BLOCK, 0, n_e, WIN_ALIGN)
        finish(acc_scr[...] + jnp.dot(tail, wcat_scr[pl.ds(0, n_e * WIN_ALIGN), :],
                                      preferred_element_type=F32))


def _combine(starts_flat, x1, pos_c, eo, gf):
    bsz, s, d = x1.shape
    n_e = pos_c.shape[-1]
    cap = eo.shape[2]
    step_rows = COMBINE_TILES * TOKEN_TILE
    wide_group = 4
    assert n_e % wide_group == 0 and n_e * NARROW_WIN <= wide_group * SLOT_BLOCK
    assert s % step_rows == 0
    grid_spec = pltpu.PrefetchScalarGridSpec(
        num_scalar_prefetch=1,
        grid=(bsz, s // step_rows),
        in_specs=[
            pl.BlockSpec((1, step_rows, d), lambda b, k, st: (b, k, 0)),
            pl.BlockSpec((1, step_rows, n_e), lambda b, k, st: (b, k, 0)),
            pl.BlockSpec((1, d), lambda b, k, st: (0, 0)),
            pl.BlockSpec((1, n_e, cap, d), lambda b, k, st: (b, 0, 0, 0),
                         pipeline_mode=pl.Buffered(1)),
        ],
        out_specs=pl.BlockSpec((1, step_rows, d), lambda b, k, st: (b, k, 0)),
        scratch_shapes=[pltpu.VMEM((wide_group * SLOT_BLOCK, d), BF16),
                        pltpu.VMEM((TOKEN_TILE, d), F32)],
    )
    return pl.pallas_call(
        functools.partial(_combine_kernel, cap=cap),
        grid_spec=grid_spec,
        out_shape=jax.ShapeDtypeStruct((bsz, s, d), F32),
        compiler_params=_cparams(("parallel", "parallel")),
        name="combine",
    )(starts_flat, x1, pos_c, gf, eo)


def _block_diag(w):
    h, hd, _ = w.shape
    eye = jnp.eye(h, dtype=w.dtype)
    return (eye[:, None, :, None] * w[:, :, None, :]).reshape(h * hd, h * hd)


def _lru_params(wa_f, wx_f, wa_b, wx_b, ba_f, bx_f, ba_b, bx_b, lam_f, lam_b):
    d_lru = ba_f.shape[0]
    n_blk = d_lru // LANES
    mats = [0.5 * _block_diag(w) for w in (wa_f, wx_f, wa_b, wx_b)]
    ba_f, bx_f, ba_b, bx_b = (0.5 * v for v in (ba_f, bx_f, ba_b, bx_b))
    w_cat = jnp.stack([
        jnp.concatenate([m[c * LANES:(c + 1) * LANES, c * LANES:(c + 1) * LANES] for m in mats], axis=1)
        for c in range(n_blk)]).astype(BF16)
    b_cat = jnp.stack([
        jnp.concatenate([v[c * LANES:(c + 1) * LANES] for v in (ba_f, bx_f, ba_b, bx_b)])
        for c in range(n_blk)])[:, None, :]
    lam_cat = jnp.stack([
        jnp.concatenate([v[c * LANES:(c + 1) * LANES] for v in (lam_f, lam_b)])
        for c in range(n_blk)])[:, None, :]
    return w_cat, b_cat, lam_cat


def kernel(x, norm1_g, w_in, conv_w, conv_b, lru_wa_f, lru_ba_f, lru_wx_f, lru_bx_f, lru_lam_f,
           lru_wa_b, lru_ba_b, lru_wx_b, lru_bx_b, lru_lam_b, w_out, norm2_g, w_router,
           w_gate, w_up, w_down, normf_g):
    bsz, s, d = x.shape
    d_lru = conv_b.shape[0]
    d_four = w_in.shape[1] - 2 * d_lru
    n_e = w_router.shape[1]
    cap = CAPACITY_FACTOR * s // n_e
    assert s % LRU_CHUNK == 0 and s % (SCAN_GROUPS * SUBLANES * SUBLANES) == 0
    assert s % FOURIER_N2 == 0 and cap % SLOT_BLOCK == 0 and s // TOKEN_TILE < LANES
    assert cap >= WIN_ROWS and w_gate.shape[2] % FF_CHUNK == 0

    lx, lg, fo = _inproj(x.reshape(bsz * s, d), norm1_g[None, :], w_in.astype(BF16), d_lru, d_four)
    lx = lx.reshape(bsz, s, d_lru)
    lg = lg.reshape(bsz, s, d_lru)
    fo = fo.reshape(bsz, s, d_four)

    w_cat, b_cat, lam_cat = _lru_params(lru_wa_f, lru_wx_f, lru_wa_b, lru_wx_b,
                                        lru_ba_f, lru_bx_f, lru_ba_b, lru_bx_b,
                                        lru_lam_f, lru_lam_b)
    y_lru = _lru(lx, lg, conv_w, conv_b[None, :], w_cat, b_cat, lam_cat)
    y_four = _fourier(fo, FOURIER_GROUPS)

    w_out_bf = w_out.astype(BF16)
    x1, h2_tiles, aff_t = _outproj(x, y_lru, y_four, w_out_bf[:d_lru], w_out_bf[d_lru:],
                                   norm2_g[None, :], w_router.T)

    pos_r, starts = _select(aff_t, cap)
    starts_flat = starts.reshape(-1)
    pos_c = jnp.transpose(pos_r, (0, 2, 1))
    pos5 = pos_r.reshape(bsz, n_e, s // TOKEN_TILE, 1, TOKEN_TILE)
    idx, gs = _compact(starts_flat, pos5, jnp.transpose(aff_t, (0, 2, 1)), cap)
    eo = _moe(idx.reshape(bsz * n_e, 1, cap), gs.reshape((bsz * n_e,) + gs.shape[2:]), h2_tiles,
              w_gate, w_up, w_down, bsz, s)
    return _combine(starts_flat, x1, pos_c, eo, normf_g[None, :])
```

```python
import functools
import math

import jax
import jax.numpy as jnp
import numpy as np
from jax import lax
from jax.experimental import pallas as pl
from jax.experimental.pallas import tpu as pltpu

F32 = jnp.float32
BF16 = jnp.bfloat16

EPS = 1e-6
LRU_C = 8.0
CAPACITY_FACTOR = 2
FOURIER_GROUPS = 4

SUBLANES = 8
LANES = 128
MXU_DIM = 256
VMEM_LIMIT_BYTES = 56 * 1024 * 1024

ROW_TILE = 1024
LRU_CHUNK = 2048
SCAN_GROUPS = 4
SLOT_BLOCK = MXU_DIM
TOKEN_TILE = MXU_DIM
SEG_PAD = 4
FOURIER_N2 = LANES
FOURIER_K1_BATCH = 32
FOURIER_U_BATCH = 2
STAGE_PAD = 4
FF_CHUNK = 1024
GATHER_UNROLL = 8
COMPACT_WIN = 64
TOKEN_ID_BASE = 64
GATE_COL0 = 16
WIN_ALIGN = 16
WIN_ROWS = SLOT_BLOCK + WIN_ALIGN
NARROW_WIN = 64
COMBINE_TILES = 2


def _cparams(semantics):
    return pltpu.CompilerParams(dimension_semantics=semantics,
                                vmem_limit_bytes=VMEM_LIMIT_BYTES)


def _rms_scale(x):
    return x * lax.rsqrt(jnp.mean(x * x, axis=-1, keepdims=True) + EPS)


def _inproj_kernel(x_ref, g_ref, w_ref, lx_ref, lg_ref, fo_ref):
    h = _rms_scale(x_ref[...]) * g_ref[...]
    p = jnp.dot(h.astype(BF16), w_ref[...], preferred_element_type=F32)
    d = lx_ref.shape[-1]
    lx_ref[...] = p[:, :d]
    lg_ref[...] = p[:, d:2 * d]
    fo_ref[...] = p[:, 2 * d:]


def _inproj(x2, g, w_bf, d_lru, d_four):
    m, d = x2.shape
    n = w_bf.shape[1]
    return pl.pallas_call(
        _inproj_kernel,
        grid=(m // ROW_TILE,),
        in_specs=[
            pl.BlockSpec((ROW_TILE, d), lambda i: (i, 0)),
            pl.BlockSpec((1, d), lambda i: (0, 0)),
            pl.BlockSpec((d, n), lambda i: (0, 0)),
        ],
        out_specs=[
            pl.BlockSpec((ROW_TILE, d_lru), lambda i: (i, 0)),
            pl.BlockSpec((ROW_TILE, d_lru), lambda i: (i, 0)),
            pl.BlockSpec((ROW_TILE, d_four), lambda i: (i, 0)),
        ],
        out_shape=[
            jax.ShapeDtypeStruct((m, d_lru), F32),
            jax.ShapeDtypeStruct((m, d_lru), F32),
            jax.ShapeDtypeStruct((m, d_four), F32),
        ],
        compiler_params=_cparams(("parallel",)),
        name="inproj",
    )(x2, g, w_bf)


def _shift_rows(x, shift):
    n = x.shape[0]
    rows = lax.broadcasted_iota(jnp.int32, x.shape, 0)
    rolled = pltpu.roll(x, shift % n, axis=0)
    keep = (rows >= shift) if shift > 0 else (rows < n + shift)
    return jnp.where(keep, rolled, 0.0)


def _lru_kernel(lx_ref, lg_ref, cw_ref, cb_ref, w_ref, b_ref, lam_ref, y_ref,
                xpad, af, bf, ab, bb):
    s = lx_ref.shape[1]
    c_blk = lx_ref.shape[2]
    n_seg = SCAN_GROUPS * SUBLANES
    seg = s // n_seg
    seg_stride = seg + SEG_PAD
    piece = min(seg, LRU_CHUNK)
    n_chunks = s // LRU_CHUNK
    pad = SUBLANES

    zeros_pad = jnp.zeros((pad, c_blk), F32)
    xpad[pl.ds(0, pad), :] = zeros_pad
    xpad[pl.ds(pad + s, pad), :] = zeros_pad

    def copy_chunk(i, carry):
        t0 = pl.multiple_of(i * LRU_CHUNK, LRU_CHUNK)
        xpad[pl.ds(pad + t0, LRU_CHUNK), :] = lx_ref[0, pl.ds(t0, LRU_CHUNK), :]
        return carry

    lax.fori_loop(0, n_chunks, copy_chunk, 0)

    cw = cw_ref[...]
    conv_width = cw.shape[0]
    cb = cb_ref[...]
    bias = b_ref[0]
    quarter_c_log_sig = (0.25 * LRU_C) * jax.nn.log_sigmoid(lam_ref[0])
    w_cat = w_ref[0]

    def gates_chunk(i, carry):
        t0 = pl.multiple_of(i * LRU_CHUNK, LRU_CHUNK)
        c = cb
        left = conv_width // 2
        for k in range(conv_width):
            c = c + cw[k:k + 1, :] * xpad[pl.ds(t0 + (pad - left + k), LRU_CHUNK), :]
        z = jnp.dot(c.astype(BF16), w_cat, preferred_element_type=F32) + bias
        for d, (a_scr, b_scr) in enumerate(((af, bf), (ab, bb))):
            q = quarter_c_log_sig[:, d * c_blk:(d + 1) * c_blk]
            th_r = jnp.tanh(z[:, (2 * d) * c_blk:(2 * d + 1) * c_blk])
            th_i = jnp.tanh(z[:, (2 * d + 1) * c_blk:(2 * d + 2) * c_blk])
            t = jnp.tanh(q + q * th_r)
            inv = 1.0 / (1.0 - t)
            a = (1.0 + t) * inv
            bt = (jnp.sqrt(-t) * inv) * (c + c * th_i)
            for p in range(LRU_CHUNK // piece):
                t = t0 + p * piece
                dst = pl.ds((t // seg) * seg_stride + t % seg, piece)
                a_scr[dst, :] = a[p * piece:(p + 1) * piece]
                b_scr[dst, :] = bt[p * piece:(p + 1) * piece]
        return carry

    lax.fori_loop(0, n_chunks, gates_chunk, 0)

    def scan_step(i, carry):
        out = []
        for (a_scr, b_scr, o), (hs, ps) in zip(((af, bf, i), (ab, bb, seg - 1 - i)), carry):
            new_h, new_p = [], []
            for q in range(SCAN_GROUPS):
                rows = pl.ds(q * SUBLANES * seg_stride + o, SUBLANES, stride=seg_stride)
                a = a_scr[rows, :]
                h = a * hs[q] + b_scr[rows, :]
                p = a * ps[q]
                b_scr[rows, :] = h
                a_scr[rows, :] = p
                new_h.append(h)
                new_p.append(p)
            out.append((tuple(new_h), tuple(new_p)))
        return tuple(out)

    zero = jnp.zeros((SUBLANES, c_blk), F32)
    one = jnp.ones((SUBLANES, c_blk), F32)
    init = ((zero,) * SCAN_GROUPS, (one,) * SCAN_GROUPS)
    (hf_end, pf_end), (hb_end, pb_end) = lax.fori_loop(0, seg, scan_step, (init, init))

    row = lax.broadcasted_iota(jnp.int32, (SUBLANES, c_blk), 0)

    def entering(h_end, p_end, carry_in, first_row, shift):
        c = zero
        for _ in range(SUBLANES):
            c = jnp.where(row == first_row, carry_in, _shift_rows(h_end + p_end * c, shift))
        return c

    cf = [None] * SCAN_GROUPS
    cbk = [None] * SCAN_GROUPS
    carry_f = jnp.zeros((1, c_blk), F32)
    carry_b = jnp.zeros((1, c_blk), F32)
    for q in range(SCAN_GROUPS):
        cf[q] = entering(hf_end[q], pf_end[q], carry_f, 0, 1)
        carry_f = (hf_end[q] + pf_end[q] * cf[q])[SUBLANES - 1:SUBLANES, :]
        qb = SCAN_GROUPS - 1 - q
        cbk[qb] = entering(hb_end[qb], pb_end[qb], carry_b, SUBLANES - 1, -1)
        carry_b = (hb_end[qb] + pb_end[qb] * cbk[qb])[0:1, :]

    for j in range(n_seg):
        q, i = divmod(j, SUBLANES)
        src = pl.ds(j * seg_stride, seg)
        rows = pl.ds(j * seg, seg)
        hsum = ((bf[src, :] + af[src, :] * cf[q][i:i + 1, :])
                + (bb[src, :] + ab[src, :] * cbk[q][i:i + 1, :]))
        y_ref[0, rows, :] = (jax.nn.gelu(lg_ref[0, rows, :]) * hsum).astype(y_ref.dtype)


def _lru(lx, lg, conv_w, conv_b, w_cat, b_cat, lam_cat):
    bsz, s, d_lru = lx.shape
    c_blk = LANES
    n_blk = d_lru // c_blk
    scr = pltpu.VMEM((s + SCAN_GROUPS * SUBLANES * SEG_PAD, c_blk), F32)
    return pl.pallas_call(
        _lru_kernel,
        grid=(bsz, n_blk),
        in_specs=[
            pl.BlockSpec((1, s, c_blk), lambda b, c: (b, 0, c)),
            pl.BlockSpec((1, s, c_blk), lambda b, c: (b, 0, c)),
            pl.BlockSpec((4, c_blk), lambda b, c: (0, c)),
            pl.BlockSpec((1, c_blk), lambda b, c: (0, c)),
            pl.BlockSpec((1, c_blk, 4 * c_blk), lambda b, c: (c, 0, 0)),
            pl.BlockSpec((1, 1, 4 * c_blk), lambda b, c: (c, 0, 0)),
            pl.BlockSpec((1, 1, 2 * c_blk), lambda b, c: (c, 0, 0)),
        ],
        out_specs=pl.BlockSpec((1, s, c_blk), lambda b, c: (b, 0, c)),
        out_shape=jax.ShapeDtypeStruct((bsz, s, d_lru), BF16),
        scratch_shapes=[pltpu.VMEM((s + 2 * SUBLANES, c_blk), F32), scr, scr, scr, scr],
        compiler_params=_cparams(("parallel", "parallel")),
        name="lru",
    )(lx, lg, conv_w, conv_b, w_cat, b_cat, lam_cat)


def _fourier_kernel(f_ref, ka_ref, kb_ref, cc_ref, sc_ref, twc_ref, tws_ref, y_ref,
                    ar_scr, ai_scr, y_scr, *, scale):
    n1 = f_ref.shape[1]
    n_u = f_ref.shape[2]
    cols = f_ref.shape[4]
    r = SUBLANES * n1
    n2 = n_u * SUBLANES
    stage_stride = n1 + STAGE_PAD

    def stage_a(i, carry):
        us = tuple(FOURIER_U_BATCH * i + j for j in range(FOURIER_U_BATCH))
        xu = jnp.concatenate([f_ref[0, :, u].reshape(r, cols) for u in us], axis=1)
        a = jnp.dot(ka_ref[...], xu.astype(BF16), preferred_element_type=F32)
        for j, u in enumerate(us):
            a_re = a[:r, j * cols:(j + 1) * cols]
            a_im = a[r:, j * cols:(j + 1) * cols]
            c = twc_ref[u]
            sn = tws_ref[u]
            ar_scr[u] = a_re * c - a_im * sn
            ai_scr[u] = a_re * sn + a_im * c
        return carry

    lax.fori_loop(0, n_u // FOURIER_U_BATCH, stage_a, 0)

    nb = FOURIER_K1_BATCH

    def stage_b(i, carry):
        rows = pl.ds(pl.multiple_of(i * (nb * SUBLANES), nb * SUBLANES), nb * SUBLANES)
        a_re = ar_scr[:, rows, :]
        a_im = ai_scr[:, rows, :]

        def rows_s2(a, j):
            return a[:, j * SUBLANES:(j + 1) * SUBLANES, :].reshape(n2, cols)

        st = jnp.concatenate(
            [jnp.concatenate([rows_s2(a_re, j), rows_s2(a_im, j)], axis=0) for j in range(nb)],
            axis=1).astype(BF16)
        x = jnp.dot(kb_ref[...], st, preferred_element_type=F32)
        x_re = jnp.concatenate([x[:n2, j * cols:(j + 1) * cols] for j in range(nb)], axis=0)
        x_im = jnp.concatenate([x[n2:, j * cols:(j + 1) * cols] for j in range(nb)], axis=0)
        y = (jnp.dot(x_re.astype(BF16), cc_ref[...], preferred_element_type=F32)
             + jnp.dot(x_im.astype(BF16), sc_ref[...], preferred_element_type=F32)) * scale
        for j in range(nb):
            y_scr[pl.ds(i * nb + j, n2, stride=stage_stride), :] = y[j * n2:(j + 1) * n2]
        return carry

    lax.fori_loop(0, n1 // nb, stage_b, 0)

    def copy_out(k2, carry):
        y_ref[0, pl.ds(pl.multiple_of(k2 * n1, n1), n1), :] = y_scr[pl.ds(k2 * stage_stride, n1), :]
        return carry

    lax.fori_loop(0, n2, copy_out, 0)


def _dft_tables(s, c_grp):
    n2 = FOURIER_N2
    n1 = s // n2
    n_u = n2 // SUBLANES

    def cos_sin(num, den):
        ang = (2.0 * np.pi / den) * (num % den).astype(np.float64)
        return np.cos(ang), np.sin(ang)

    def const(a, dtype=F32):
        return jnp.asarray(a.astype(np.float32)).astype(dtype)

    i1 = np.arange(n1, dtype=np.int64)
    c1, s1 = cos_sin(i1[:, None] * i1[None, :], n1)
    eye8 = np.eye(SUBLANES)
    ka = const(np.concatenate([np.kron(c1, eye8), np.kron(-s1, eye8)], axis=0), BF16)

    i2 = np.arange(n2, dtype=np.int64)
    c2, s2 = cos_sin(i2[:, None] * i2[None, :], n2)
    kb = const(np.block([[c2, s2], [-s2, c2]]), BF16)

    ic = np.arange(c_grp, dtype=np.int64)
    cc, sc = cos_sin(ic[:, None] * ic[None, :], c_grp)

    u = np.arange(n_u, dtype=np.int64)[:, None, None]
    k1 = np.arange(n1, dtype=np.int64)[None, :, None]
    v = np.arange(SUBLANES, dtype=np.int64)[None, None, :]
    tc, ts = cos_sin(k1 * (SUBLANES * u + v), s)
    shape = (n_u, n1 * SUBLANES, c_grp)
    twc = jnp.broadcast_to(const(tc.reshape(n_u, -1, 1)), shape)
    tws = jnp.broadcast_to(const(-ts.reshape(n_u, -1, 1)), shape)
    return ka, kb, const(cc, BF16), const(sc, BF16), twc, tws


def _fourier(four, n_groups):
    bsz, s, d_four = four.shape
    c_grp = d_four // n_groups
    n2 = FOURIER_N2
    n1 = s // n2
    n_u = n2 // SUBLANES
    r = SUBLANES * n1
    ka, kb, cc, sc, twc, tws = _dft_tables(s, c_grp)
    f5 = four.reshape(bsz, n1, n_u, SUBLANES, d_four)
    scale = 1.0 / math.sqrt(s * c_grp)
    const2 = lambda b, g: (0, 0)
    const3 = lambda b, g: (0, 0, 0)
    return pl.pallas_call(
        functools.partial(_fourier_kernel, scale=scale),
        grid=(bsz, n_groups),
        in_specs=[
            pl.BlockSpec((1, n1, n_u, SUBLANES, c_grp), lambda b, g: (b, 0, 0, 0, g)),
            pl.BlockSpec(ka.shape, const2),
            pl.BlockSpec(kb.shape, const2),
            pl.BlockSpec(cc.shape, const2),
            pl.BlockSpec(sc.shape, const2),
            pl.BlockSpec(twc.shape, const3),
            pl.BlockSpec(tws.shape, const3),
        ],
        out_specs=pl.BlockSpec((1, s, c_grp), lambda b, g: (b, 0, g)),
        out_shape=jax.ShapeDtypeStruct((bsz, s, d_four), F32),
        scratch_shapes=[pltpu.VMEM((n_u, r, c_grp), F32), pltpu.VMEM((n_u, r, c_grp), F32),
                        pltpu.VMEM((n2 * (n1 + STAGE_PAD), c_grp), F32)],
        compiler_params=_cparams(("parallel", "parallel")),
        name="fourier",
    )(f5, ka, kb, cc, sc, twc, tws)


def _outproj_kernel(x_ref, yl_ref, yf_ref, wo1_ref, wo2_ref, g2_ref, wr_ref,
                    x1_ref, h2_ref, aff_ref):
    x1 = (x_ref[0]
          + jnp.dot(yl_ref[0], wo1_ref[...], preferred_element_type=F32)
          + jnp.dot(yf_ref[0].astype(BF16), wo2_ref[...], preferred_element_type=F32))
    x1_ref[0] = x1
    h2 = _rms_scale(x1) * g2_ref[...]
    for j in range(h2.shape[1] // LANES):
        h2_ref[pl.ds(j, h2.shape[0], stride=SUBLANES), :] = h2[:, j * LANES:(j + 1) * LANES]
    def split(v):
        hi = v.astype(BF16)
        return hi, (v - hi.astype(F32)).astype(BF16)

    def dot_nt(a, bm):
        return lax.dot_general(a, bm, (((1,), (1,)), ((), ())), preferred_element_type=F32)

    w_hi, w_lo = split(wr_ref[...])
    h_hi, h_lo = split(h2)
    logits = dot_nt(w_hi, h_hi) + (dot_nt(w_lo, h_hi) + dot_nt(w_hi, h_lo))
    ex = jnp.exp(logits - jnp.max(logits, axis=0, keepdims=True))
    aff_ref[0] = ex / jnp.sum(ex, axis=0, keepdims=True)


def _outproj(x, y_lru, y_four, wo1, wo2, g2, wr_t):
    bsz, s, d = x.shape
    d_lru = y_lru.shape[-1]
    d_four = y_four.shape[-1]
    n_e = wr_t.shape[0]
    assert d == SUBLANES * LANES
    n_i = s // ROW_TILE
    tile = lambda w: pl.BlockSpec((1, ROW_TILE, w), lambda b, i: (b, i, 0))
    const = lambda shape: pl.BlockSpec(shape, lambda b, i: (0, 0))
    return pl.pallas_call(
        _outproj_kernel,
        grid=(bsz, n_i),
        in_specs=[tile(d), tile(d_lru), tile(d_four), const(wo1.shape), const(wo2.shape),
                  const(g2.shape), const(wr_t.shape)],
        out_specs=[tile(d),
                   pl.BlockSpec((ROW_TILE * SUBLANES, LANES), lambda b, i: (b * n_i + i, 0)),
                   pl.BlockSpec((1, n_e, ROW_TILE), lambda b, i: (b, 0, i))],
        out_shape=[
            jax.ShapeDtypeStruct((bsz, s, d), F32),
            jax.ShapeDtypeStruct((bsz * s * SUBLANES, LANES), F32),
            jax.ShapeDtypeStruct((bsz, n_e, s), F32),
        ],
        compiler_params=_cparams(("parallel", "parallel")),
        name="outproj",
    )(x, y_lru, y_four, wo1, wo2, g2, wr_t)


def _select_kernel(aff_ref, tri_ref, pos_ref, starts_ref, *, cap):
    v = aff_ref[0]
    n_e, s = v.shape
    cap_f = float(cap)

    def midpoint(lo, hi):
        mid = 0.5 * (lo + hi)
        return mid, (mid > lo) & (mid < hi)

    def cond(carry):
        _, active = midpoint(*carry)
        return jnp.max(active.astype(F32)) > 0.0

    def body(carry):
        lo, hi = carry
        mid, active = midpoint(lo, hi)
        cnt = jnp.sum((v >= mid).astype(F32), axis=1, keepdims=True)
        enough = cnt >= cap_f
        return (jnp.where(active & enough, mid, lo), jnp.where(active & (~enough), mid, hi))

    lo0 = jnp.zeros((n_e, 1), F32)
    hi0 = jnp.full((n_e, 1), 2.0, F32)
    thr, _ = lax.while_loop(cond, body, (lo0, hi0))

    above = v > thr
    tie = v == thr
    need = cap_f - jnp.sum(above.astype(F32), axis=1, keepdims=True)

    n_tiles = s // TOKEN_TILE
    tri = tri_ref[...]
    lane = lax.broadcasted_iota(jnp.int32, (n_e, LANES), 1)

    def prefix(mask_f, want_starts):
        run = jnp.zeros((n_e, 1), F32)
        starts = jnp.zeros((n_e, LANES), F32)
        pieces = []
        for t in range(n_tiles):
            m = mask_f[:, t * TOKEN_TILE:(t + 1) * TOKEN_TILE]
            incl = jnp.dot(m.astype(BF16), tri, preferred_element_type=F32)
            pieces.append(run + incl - m)
            if want_starts:
                starts = jnp.where(lane == t, run, starts)
            run = run + incl[:, TOKEN_TILE - 1:TOKEN_TILE]
        if want_starts:
            starts = jnp.where(lane == n_tiles, run, starts)
        return jnp.concatenate(pieces, axis=1), starts

    tie_rank, _ = prefix(tie.astype(F32), False)
    sel = above | (tie & (tie_rank < need))
    pos, starts = prefix(sel.astype(F32), True)
    pos_ref[0] = jnp.where(sel, pos, -1.0).astype(jnp.int32)
    starts_ref[0] = starts.astype(jnp.int32)


def _select(aff_t, cap):
    bsz, n_e, s = aff_t.shape
    idx = jnp.arange(TOKEN_TILE, dtype=jnp.int32)
    tri = (idx[:, None] <= idx[None, :]).astype(BF16)
    return pl.pallas_call(
        functools.partial(_select_kernel, cap=cap),
        grid=(bsz,),
        in_specs=[
            pl.BlockSpec((1, n_e, s), lambda b: (b, 0, 0)),
            pl.BlockSpec(tri.shape, lambda b: (0, 0)),
        ],
        out_specs=[
            pl.BlockSpec((1, n_e, s), lambda b: (b, 0, 0)),
            pl.BlockSpec((1, n_e, LANES), lambda b: (b, 0, 0)),
        ],
        out_shape=[
            jax.ShapeDtypeStruct((bsz, n_e, s), jnp.int32),
            jax.ShapeDtypeStruct((bsz, n_e, LANES), jnp.int32),
        ],
        compiler_params=_cparams(("parallel",)),
        name="select",
    )(aff_t, tri)


def _token_table(gate, first_token):
    n_tok, n_e = gate.shape
    lane = lax.broadcasted_iota(jnp.int32, (n_tok, LANES), 1)
    tok = first_token + lax.broadcasted_iota(jnp.int32, (n_tok, LANES), 0)
    table = jnp.where(lane == 0, tok // TOKEN_ID_BASE, jnp.where(lane == 1, tok % TOKEN_ID_BASE, 0))
    table = table.astype(F32)
    place_lane = lax.broadcasted_iota(jnp.int32, (n_e, LANES), 1)
    place_sub = lax.broadcasted_iota(jnp.int32, (n_e, LANES), 0)
    rest = gate
    for j in range(3):
        piece = rest.astype(BF16)
        rest = rest - piece.astype(F32)
        place = jnp.where(place_lane == GATE_COL0 + j * n_e + place_sub, 1.0, 0.0).astype(BF16)
        table = table + jnp.dot(piece, place, preferred_element_type=F32)
    return table.astype(BF16)


def _compact_kernel(starts_ref, pos_ref, gate_ref, idx_ref, gs_ref, r_scr, *, cap):
    b = pl.program_id(0)
    n_e = pos_ref.shape[1]
    n_k = pos_ref.shape[2]
    n_blocks = cap // SLOT_BLOCK
    wide_windows = -(-(TOKEN_TILE + SUBLANES) // COMPACT_WIN)
    r_scr[...] = jnp.zeros_like(r_scr)
    sub = lax.broadcasted_iota(jnp.int32, (COMPACT_WIN, TOKEN_TILE), 0)

    def tile(k, carry):
        rows = pl.ds(pl.multiple_of(k * TOKEN_TILE, TOKEN_TILE), TOKEN_TILE)
        vk = _token_table(gate_ref[0, rows, :], k * TOKEN_TILE)
        starts = []
        fits = None
        for e in range(n_e):
            base = (b * n_e + e) * LANES
            ws = (starts_ref[base + k] // SUBLANES) * SUBLANES
            ok = starts_ref[base + k + 1] - ws <= COMPACT_WIN
            fits = ok if fits is None else fits & ok
            starts.append(ws)

        def place(e, ws):
            hit = pos_ref[0, e, k] - ws == sub
            vals = jnp.dot(jnp.where(hit, 1.0, 0.0).astype(BF16), vk, preferred_element_type=F32)
            r_scr[e, pl.ds(pl.multiple_of(ws, SUBLANES), COMPACT_WIN), :] += vals

        @pl.when(fits)
        def _():
            for e in range(n_e):
                place(e, starts[e])

        @pl.when(jnp.logical_not(fits))
        def _():
            for e in range(n_e):
                for j in range(wide_windows):
                    place(e, starts[e] + j * COMPACT_WIN)

        return carry

    lax.fori_loop(0, n_k, tile, 0)

    lane = lax.broadcasted_iota(jnp.int32, (SUBLANES, LANES), 1)
    sub8 = lax.broadcasted_iota(jnp.int32, (SUBLANES, LANES), 0)
    id_rows = jnp.where(sub8 == 0, jnp.where(lane == 0, float(TOKEN_ID_BASE),
                                             jnp.where(lane == 1, 1.0, 0.0)), 0.0)

    def finish(e, carry):
        gate_lane = (lane >= GATE_COL0) & (lane < GATE_COL0 + 3 * n_e) & ((lane - GATE_COL0) % n_e == e)
        selector = (id_rows + jnp.where((sub8 == 1) & gate_lane, 1.0, 0.0)).astype(BF16)
        for m in range(n_blocks):
            blk = r_scr[e, pl.ds(m * SLOT_BLOCK, SLOT_BLOCK), :].astype(BF16)
            rows = lax.dot_general(selector, blk, (((1,), (1,)), ((), ())),
                                   preferred_element_type=F32)
            idx_ref[0, e, m] = rows[0:1, :].astype(jnp.int32)
            gs_ref[0, e, m] = rows[1:2, :]
        return carry

    lax.fori_loop(0, n_e, finish, 0)


def _compact(starts_flat, pos5, gate_c, cap):
    bsz, n_e, n_k = pos5.shape[:3]
    s = gate_c.shape[1]
    assert s <= TOKEN_ID_BASE * 256 and GATE_COL0 + 3 * n_e <= LANES
    n_blocks = cap // SLOT_BLOCK
    out_block = (1, n_e, n_blocks, 1, SLOT_BLOCK)
    out_spec = pl.BlockSpec(out_block, lambda b, st: (b, 0, 0, 0, 0))
    wide_rows = -(-(TOKEN_TILE + SUBLANES) // COMPACT_WIN) * COMPACT_WIN
    grid_spec = pltpu.PrefetchScalarGridSpec(
        num_scalar_prefetch=1,
        grid=(bsz,),
        in_specs=[pl.BlockSpec((1, n_e, n_k, 1, TOKEN_TILE), lambda b, st: (b, 0, 0, 0, 0)),
                  pl.BlockSpec((1, s, n_e), lambda b, st: (b, 0, 0))],
        out_specs=[out_spec, out_spec],
        scratch_shapes=[pltpu.VMEM((n_e, cap + wide_rows, LANES), F32)],
    )
    return pl.pallas_call(
        functools.partial(_compact_kernel, cap=cap),
        grid_spec=grid_spec,
        out_shape=[jax.ShapeDtypeStruct((bsz,) + out_block[1:], jnp.int32),
                   jax.ShapeDtypeStruct((bsz,) + out_block[1:], F32)],
        compiler_params=_cparams(("parallel",)),
        name="compact",
    )(starts_flat, pos5, gate_c)


def _moe_kernel(idx_ref, idx_next_ref, gs_ref, h2_hbm, wg_ref, wu_ref, wd_ref, eo_ref,
                xbuf, sem, xb_scr, acc_scr, *, seq_len, n_ff):
    b = pl.program_id(0)
    e = pl.program_id(1)
    f = pl.program_id(2)
    n_e = pl.num_programs(1)
    n_blocks = xb_scr.shape[0]
    cap = n_blocks * SLOT_BLOCK
    d = xb_scr.shape[2]
    step = b * n_e + e
    n_steps = pl.num_programs(0) * n_e
    cur = step % 2

    def token_copy(idx_smem, batch, buf, p):
        tok = idx_smem[0, 0, p]
        src = h2_hbm.at[pl.ds(pl.multiple_of((batch * seq_len + tok) * SUBLANES, SUBLANES), SUBLANES)]
        dst = xbuf.at[buf, pl.ds(pl.multiple_of(p * SUBLANES, SUBLANES), SUBLANES)]
        return pltpu.make_async_copy(src, dst, sem.at[buf])

    def request(idx_smem, batch, buf, first, count):
        def body(i, carry):
            for j in range(GATHER_UNROLL):
                token_copy(idx_smem, batch, buf, first + i * GATHER_UNROLL + j).start()
            return carry
        lax.fori_loop(0, count // GATHER_UNROLL, body, 0)

    @pl.when((step == 0) & (f == 0))
    def _():
        request(idx_ref, b, cur, 0, cap)

    next_batch = jnp.where(step + 1 < n_steps, step + 1, 0) // n_e
    per_block = cap // (n_ff * n_blocks)

    @pl.when(f == 0)
    def _():
        pltpu.make_async_copy(h2_hbm.at[pl.ds(0, cap * SUBLANES)], xbuf.at[cur], sem.at[cur]).wait()
        for m in range(n_blocks):
            first = m * SLOT_BLOCK * SUBLANES
            xm = jnp.concatenate(
                [xbuf[cur, pl.ds(first + j, SLOT_BLOCK, stride=SUBLANES), :] for j in range(d // LANES)],
                axis=1)
            xb_scr[m] = xm.astype(BF16)

    def ffn_chunk(first_chunk, last_chunk):
        wg = wg_ref[0].astype(BF16)
        wu = wu_ref[0].astype(BF16)
        wd = wd_ref[0].astype(BF16)
        if last_chunk:
            eye = (lax.broadcasted_iota(jnp.int32, (SLOT_BLOCK, SLOT_BLOCK), 0)
                   == lax.broadcasted_iota(jnp.int32, (SLOT_BLOCK, SLOT_BLOCK), 1))
        for m in range(n_blocks):
            first = (f * n_blocks + m) * per_block
            for j in range(per_block):
                token_copy(idx_next_ref, next_batch, 1 - cur, first + j).start()
            xm = xb_scr[m]
            g = jnp.dot(xm, wg, preferred_element_type=F32)
            u = jnp.dot(xm, wu, preferred_element_type=F32)
            act = (jax.nn.silu(g) * u).astype(BF16)
            part = jnp.dot(act, wd, preferred_element_type=F32)
            if not first_chunk:
                part = acc_scr[m] + part
            if last_chunk:
                gcol = jnp.sum(jnp.where(eye, gs_ref[0, m], 0.0), axis=1, keepdims=True)
                eo_ref[0, 0, pl.ds(m * SLOT_BLOCK, SLOT_BLOCK), :] = (gcol * part).astype(eo_ref.dtype)
            else:
                acc_scr[m] = part

    if n_ff == 1:
        ffn_chunk(True, True)
    else:
        pl.when(f == 0)(lambda: ffn_chunk(True, False))
        pl.when(f == n_ff - 1)(lambda: ffn_chunk(False, True))
        if n_ff > 2:
            pl.when((f > 0) & (f < n_ff - 1))(lambda: ffn_chunk(False, False))

    @pl.when((step == n_steps - 1) & (f == n_ff - 1))
    def _():
        pltpu.make_async_copy(h2_hbm.at[pl.ds(0, cap * SUBLANES)], xbuf.at[1 - cur],
                              sem.at[1 - cur]).wait()


def _moe(idx, gs, h2_tiles, wg, wu, wd, bsz, s):
    n_e, d, d_ff = wg.shape
    n_blocks = gs.shape[1]
    cap = n_blocks * SLOT_BLOCK
    n_ff = d_ff // FF_CHUNK
    assert d_ff % FF_CHUNK == 0 and cap % (n_ff * n_blocks) == 0 and cap % GATHER_UNROLL == 0
    assert d == SUBLANES * LANES

    def next_step(b, e, f):
        return ((b * n_e + e + 1) % (bsz * n_e), 0, 0)

    smem = pltpu.SMEM
    return pl.pallas_call(
        functools.partial(_moe_kernel, seq_len=s, n_ff=n_ff),
        grid=(bsz, n_e, n_ff),
        in_specs=[
            pl.BlockSpec((1, 1, cap), lambda b, e, f: (b * n_e + e, 0, 0), memory_space=smem),
            pl.BlockSpec((1, 1, cap), next_step, memory_space=smem),
            pl.BlockSpec((1, n_blocks, 1, SLOT_BLOCK), lambda b, e, f: (b * n_e + e, 0, 0, 0)),
            pl.BlockSpec(memory_space=pl.ANY),
            pl.BlockSpec((1, d, FF_CHUNK), lambda b, e, f: (e, 0, f)),
            pl.BlockSpec((1, d, FF_CHUNK), lambda b, e, f: (e, 0, f)),
            pl.BlockSpec((1, FF_CHUNK, d), lambda b, e, f: (e, f, 0)),
        ],
        out_specs=pl.BlockSpec((1, 1, cap, d), lambda b, e, f: (b, e, 0, 0)),
        out_shape=jax.ShapeDtypeStruct((bsz, n_e, cap, d), BF16),
        scratch_shapes=[pltpu.VMEM((2, cap * SUBLANES, LANES), F32),
                        pltpu.SemaphoreType.DMA((2,)),
                        pltpu.VMEM((n_blocks, SLOT_BLOCK, d), BF16),
                        pltpu.VMEM((n_blocks, SLOT_BLOCK, d), F32)],
        compiler_params=_cparams(("arbitrary", "arbitrary", "arbitrary")),
        name="moe",
    )(idx, idx, gs, h2_tiles, wg, wu, wd)


def _combine_kernel(starts_ref, x1_ref, pos_ref, gf_ref, eo_ref, out_ref, wcat_scr, acc_scr,
                    *, cap):
    for i in range(x1_ref.shape[1] // TOKEN_TILE):
        _combine_tile(starts_ref, x1_ref, pos_ref, gf_ref, eo_ref, out_ref, wcat_scr, acc_scr,
                      pl.program_id(1) * (x1_ref.shape[1] // TOKEN_TILE) + i,
                      pl.ds(i * TOKEN_TILE, TOKEN_TILE), cap)


def _combine_tile(starts_ref, x1_ref, pos_ref, gf_ref, eo_ref, out_ref, wcat_scr, acc_scr,
                  k, rows, cap):
    b = pl.program_id(0)
    n_e = pos_ref.shape[2]
    pos = pos_ref[0, rows, :]
    expert_lane = lax.broadcasted_iota(jnp.int32, (1, n_e), 1)

    s0 = [starts_ref[(b * n_e + e) * LANES + k] for e in range(n_e)]
    s1 = [starts_ref[(b * n_e + e) * LANES + k + 1] for e in range(n_e)]

    def window_starts(rows):
        starts = [jnp.minimum(s // WIN_ALIGN, (cap - rows) // WIN_ALIGN) * WIN_ALIGN for s in s0]
        vec = jnp.zeros((1, n_e), jnp.int32)
        for e in range(n_e):
            vec = jnp.where(expert_lane == e, starts[e], vec)
        return starts, vec

    def onehot(rel, first, count, width):
        n_lanes = count * width
        lane = lax.broadcasted_iota(jnp.int32, (n_e, n_lanes), 1)
        sub = lax.broadcasted_iota(jnp.int32, (n_e, n_lanes), 0)
        expand = jnp.where(sub == first + lane // width, 1.0, 0.0).astype(BF16)
        spread = jnp.dot(jnp.clip(rel, -1, width).astype(F32).astype(BF16), expand,
                         preferred_element_type=F32)
        want = (lax.broadcasted_iota(jnp.int32, (TOKEN_TILE, n_lanes), 1) % width).astype(F32)
        return jnp.where(spread == want, 1.0, 0.0).astype(BF16)

    def finish(acc):
        out_ref[0, rows, :] = _rms_scale(acc) * gf_ref[...]

    narrow, narrow_vec = window_starts(NARROW_WIN)
    fits = None
    for e in range(n_e):
        ok = s1[e] - narrow[e] <= NARROW_WIN
        fits = ok if fits is None else fits & ok

    @pl.when(fits)
    def _():
        for e in range(n_e):
            src = pl.ds(pl.multiple_of(narrow[e], WIN_ALIGN), NARROW_WIN)
            wcat_scr[pl.ds(e * NARROW_WIN, NARROW_WIN), :] = eo_ref[0, e, src, :]
        hit = onehot(pos - narrow_vec, 0, n_e, NARROW_WIN)
        finish(x1_ref[0, rows, :] + jnp.dot(hit, wcat_scr[pl.ds(0, n_e * NARROW_WIN), :],
                                            preferred_element_type=F32))

    @pl.when(jnp.logical_not(fits))
    def _():
        wide, wide_vec = window_starts(WIN_ROWS)
        rel = pos - wide_vec
        group = wcat_scr.shape[0] // SLOT_BLOCK
        acc_scr[...] = x1_ref[0, rows, :]
        for g in range(n_e // group):
            for i in range(group):
                e = g * group + i
                src = pl.ds(pl.multiple_of(wide[e], WIN_ALIGN), SLOT_BLOCK)
                wcat_scr[pl.ds(i * SLOT_BLOCK, SLOT_BLOCK), :] = eo_ref[0, e, src, :]
            acc_scr[...] += jnp.dot(onehot(rel, g * group, group, SLOT_BLOCK), wcat_scr[...],
                                    preferred_element_type=F32)
        for e in range(n_e):
            src = pl.ds(pl.multiple_of(wide[e] + SLOT_BLOCK, WIN_ALIGN), WIN_ALIGN)
            wcat_scr[pl.ds(e * WIN_ALIGN, WIN_ALIGN), :] = eo_ref[0, e, src, :]
        tail = onehot(rel - SLOT_BLOCK, 0, n_e, WIN_ALIGN)
        finish(acc_scr[...] + jnp.dot(tail, wcat_scr[pl.ds(0, n_e * WIN_ALIGN), :],
                                      preferred_element_type=F32))


def _combine(starts_flat, x1, pos_c, eo, gf):
    bsz, s, d = x1.shape
    n_e = pos_c.shape[-1]
    cap = eo.shape[2]
    step_rows = COMBINE_TILES * TOKEN_TILE
    wide_group = 4
    assert n_e % wide_group == 0 and n_e * NARROW_WIN <= wide_group * SLOT_BLOCK
    assert s % step_rows == 0
    grid_spec = pltpu.PrefetchScalarGridSpec(
        num_scalar_prefetch=1,
        grid=(bsz, s // step_rows),
        in_specs=[
            pl.BlockSpec((1, step_rows, d), lambda b, k, st: (b, k, 0)),
            pl.BlockSpec((1, step_rows, n_e), lambda b, k, st: (b, k, 0)),
            pl.BlockSpec((1, d), lambda b, k, st: (0, 0)),
            pl.BlockSpec((1, n_e, cap, d), lambda b, k, st: (b, 0, 0, 0),
                         pipeline_mode=pl.Buffered(1)),
        ],
        out_specs=pl.BlockSpec((1, step_rows, d), lambda b, k, st: (b, k, 0)),
        scratch_shapes=[pltpu.VMEM((wide_group * SLOT_BLOCK, d), BF16),
                        pltpu.VMEM((TOKEN_TILE, d), F32)],
    )
    return pl.pallas_call(
        functools.partial(_combine_kernel, cap=cap),
        grid_spec=grid_spec,
        out_shape=jax.ShapeDtypeStruct((bsz, s, d), F32),
        compiler_params=_cparams(("parallel", "parallel")),
        name="combine",
    )(starts_flat, x1, pos_c, gf, eo)


def _block_diag(w):
    h, hd, _ = w.shape
    eye = jnp.eye(h, dtype=w.dtype)
    return (eye[:, None, :, None] * w[:, :, None, :]).reshape(h * hd, h * hd)


def _lru_params(wa_f, wx_f, wa_b, wx_b, ba_f, bx_f, ba_b, bx_b, lam_f, lam_b):
    d_lru = ba_f.shape[0]
    n_blk = d_lru // LANES
    mats = [0.5 * _block_diag(w) for w in (wa_f, wx_f, wa_b, wx_b)]
    ba_f, bx_f, ba_b, bx_b = (0.5 * v for v in (ba_f, bx_f, ba_b, bx_b))
    w_cat = jnp.stack([
        jnp.concatenate([m[c * LANES:(c + 1) * LANES, c * LANES:(c + 1) * LANES] for m in mats], axis=1)
        for c in range(n_blk)]).astype(BF16)
    b_cat = jnp.stack([
        jnp.concatenate([v[c * LANES:(c + 1) * LANES] for v in (ba_f, bx_f, ba_b, bx_b)])
        for c in range(n_blk)])[:, None, :]
    lam_cat = jnp.stack([
        jnp.concatenate([v[c * LANES:(c + 1) * LANES] for v in (lam_f, lam_b)])
        for c in range(n_blk)])[:, None, :]
    return w_cat, b_cat, lam_cat


def kernel(x, norm1_g, w_in, conv_w, conv_b, lru_wa_f, lru_ba_f, lru_wx_f, lru_bx_f, lru_lam_f,
           lru_wa_b, lru_ba_b, lru_wx_b, lru_bx_b, lru_lam_b, w_out, norm2_g, w_router,
           w_gate, w_up, w_down, normf_g):
    bsz, s, d = x.shape
    d_lru = conv_b.shape[0]
    d_four = w_in.shape[1] - 2 * d_lru
    n_e = w_router.shape[1]
    cap = CAPACITY_FACTOR * s // n_e
    assert s % LRU_CHUNK == 0 and s % (SCAN_GROUPS * SUBLANES * SUBLANES) == 0
    assert s % FOURIER_N2 == 0 and cap % SLOT_BLOCK == 0 and s // TOKEN_TILE < LANES
    assert cap >= WIN_ROWS and w_gate.shape[2] % FF_CHUNK == 0

    lx, lg, fo = _inproj(x.reshape(bsz * s, d), norm1_g[None, :], w_in.astype(BF16), d_lru, d_four)
    lx = lx.reshape(bsz, s, d_lru)
    lg = lg.reshape(bsz, s, d_lru)
    fo = fo.reshape(bsz, s, d_four)

    w_cat, b_cat, lam_cat = _lru_params(lru_wa_f, lru_wx_f, lru_wa_b, lru_wx_b,
                                        lru_ba_f, lru_bx_f, lru_ba_b, lru_bx_b,
                                        lru_lam_f, lru_lam_b)
    y_lru = _lru(lx, lg, conv_w, conv_b[None, :], w_cat, b_cat, lam_cat)
    y_four = _fourier(fo, FOURIER_GROUPS)

    w_out_bf = w_out.astype(BF16)
    x1, h2_tiles, aff_t = _outproj(x, y_lru, y_four, w_out_bf[:d_lru], w_out_bf[d_lru:],
                                   norm2_g[None, :], w_router.T)

    pos_r, starts = _select(aff_t, cap)
    starts_flat = starts.reshape(-1)
    pos_c = jnp.transpose(pos_r, (0, 2, 1))
    pos5 = pos_r.reshape(bsz, n_e, s // TOKEN_TILE, 1, TOKEN_TILE)
    idx, gs = _compact(starts_flat, pos5, jnp.transpose(aff_t, (0, 2, 1)), cap)
    eo = _moe(idx.reshape(bsz * n_e, 1, cap), gs.reshape((bsz * n_e,) + gs.shape[2:]), h2_tiles,
              w_gate, w_up, w_down, bsz, s)
    return _combine(starts_flat, x1, pos_c, eo, normf_g[None, :])
```

```python
import functools
import math

import jax
import jax.numpy as jnp
import numpy as np
from jax import lax
from jax.experimental import pallas as pl
from jax.experimental.pallas import tpu as pltpu

F32 = jnp.float32
BF16 = jnp.bfloat16

EPS = 1e-6
LRU_C = 8.0
CAPACITY_FACTOR = 2
FOURIER_GROUPS = 4

SUBLANES = 8
LANES = 128
MXU_DIM = 256
VMEM_LIMIT_BYTES = 56 * 1024 * 1024

ROW_TILE = 1024
LRU_CHUNK = 2048
SCAN_GROUPS = 4
SLOT_BLOCK = MXU_DIM
TOKEN_TILE = MXU_DIM
SEG_PAD = 4
FOURIER_N2 = LANES
FOURIER_K1_BATCH = 32
FOURIER_U_BATCH = 2
STAGE_PAD = 4
FF_CHUNK = 1024
GATHER_UNROLL = 8
COMPACT_WIN = 64
TOKEN_ID_BASE = 64
GATE_COL0 = 16
WIN_ALIGN = 16
WIN_ROWS = SLOT_BLOCK + WIN_ALIGN
NARROW_WIN = 64
COMBINE_TILES = 4


def _cparams(semantics):
    return pltpu.CompilerParams(dimension_semantics=semantics,
                                vmem_limit_bytes=VMEM_LIMIT_BYTES)


def _rms_scale(x):
    return x * lax.rsqrt(jnp.mean(x * x, axis=-1, keepdims=True) + EPS)


def _inproj_kernel(x_ref, g_ref, w_ref, lx_ref, lg_ref, fo_ref):
    h = _rms_scale(x_ref[...]) * g_ref[...]
    p = jnp.dot(h.astype(BF16), w_ref[...], preferred_element_type=F32)
    d = lx_ref.shape[-1]
    lx_ref[...] = p[:, :d]
    lg_ref[...] = p[:, d:2 * d]
    fo_ref[...] = p[:, 2 * d:]


def _inproj(x2, g, w_bf, d_lru, d_four):
    m, d = x2.shape
    n = w_bf.shape[1]
    return pl.pallas_call(
        _inproj_kernel,
        grid=(m // ROW_TILE,),
        in_specs=[
            pl.BlockSpec((ROW_TILE, d), lambda i: (i, 0)),
            pl.BlockSpec((1, d), lambda i: (0, 0)),
            pl.BlockSpec((d, n), lambda i: (0, 0)),
        ],
        out_specs=[
            pl.BlockSpec((ROW_TILE, d_lru), lambda i: (i, 0)),
            pl.BlockSpec((ROW_TILE, d_lru), lambda i: (i, 0)),
            pl.BlockSpec((ROW_TILE, d_four), lambda i: (i, 0)),
        ],
        out_shape=[
            jax.ShapeDtypeStruct((m, d_lru), F32),
            jax.ShapeDtypeStruct((m, d_lru), F32),
            jax.ShapeDtypeStruct((m, d_four), F32),
        ],
        compiler_params=_cparams(("parallel",)),
        name="inproj",
    )(x2, g, w_bf)


def _shift_rows(x, shift):
    n = x.shape[0]
    rows = lax.broadcasted_iota(jnp.int32, x.shape, 0)
    rolled = pltpu.roll(x, shift % n, axis=0)
    keep = (rows >= shift) if shift > 0 else (rows < n + shift)
    return jnp.where(keep, rolled, 0.0)


def _lru_kernel(lx_ref, lg_ref, cw_ref, cb_ref, w_ref, b_ref, lam_ref, y_ref,
                xpad, af, bf, ab, bb):
    s = lx_ref.shape[1]
    c_blk = lx_ref.shape[2]
    n_seg = SCAN_GROUPS * SUBLANES
    seg = s // n_seg
    seg_stride = seg + SEG_PAD
    piece = min(seg, LRU_CHUNK)
    n_chunks = s // LRU_CHUNK
    pad = SUBLANES

    zeros_pad = jnp.zeros((pad, c_blk), F32)
    xpad[pl.ds(0, pad), :] = zeros_pad
    xpad[pl.ds(pad + s, pad), :] = zeros_pad

    def copy_chunk(i, carry):
        t0 = pl.multiple_of(i * LRU_CHUNK, LRU_CHUNK)
        xpad[pl.ds(pad + t0, LRU_CHUNK), :] = lx_ref[0, pl.ds(t0, LRU_CHUNK), :]
        return carry

    lax.fori_loop(0, n_chunks, copy_chunk, 0)

    cw = cw_ref[...]
    conv_width = cw.shape[0]
    cb = cb_ref[...]
    bias = b_ref[0]
    quarter_c_log_sig = (0.25 * LRU_C) * jax.nn.log_sigmoid(lam_ref[0])
    w_cat = w_ref[0]

    def gates_chunk(i, carry):
        t0 = pl.multiple_of(i * LRU_CHUNK, LRU_CHUNK)
        c = cb
        left = conv_width // 2
        for k in range(conv_width):
            c = c + cw[k:k + 1, :] * xpad[pl.ds(t0 + (pad - left + k), LRU_CHUNK), :]
        z = jnp.dot(c.astype(BF16), w_cat, preferred_element_type=F32) + bias
        for d, (a_scr, b_scr) in enumerate(((af, bf), (ab, bb))):
            q = quarter_c_log_sig[:, d * c_blk:(d + 1) * c_blk]
            th_r = jnp.tanh(z[:, (2 * d) * c_blk:(2 * d + 1) * c_blk])
            th_i = jnp.tanh(z[:, (2 * d + 1) * c_blk:(2 * d + 2) * c_blk])
            t = jnp.tanh(q + q * th_r)
            inv = 1.0 / (1.0 - t)
            a = (1.0 + t) * inv
            bt = (jnp.sqrt(-t) * inv) * (c + c * th_i)
            for p in range(LRU_CHUNK // piece):
                t = t0 + p * piece
                dst = pl.ds((t // seg) * seg_stride + t % seg, piece)
                a_scr[dst, :] = a[p * piece:(p + 1) * piece]
                b_scr[dst, :] = bt[p * piece:(p + 1) * piece]
        return carry

    lax.fori_loop(0, n_chunks, gates_chunk, 0)

    def scan_step(i, carry):
        out = []
        for (a_scr, b_scr, o), (hs, ps) in zip(((af, bf, i), (ab, bb, seg - 1 - i)), carry):
            new_h, new_p = [], []
            for q in range(SCAN_GROUPS):
                rows = pl.ds(q * SUBLANES * seg_stride + o, SUBLANES, stride=seg_stride)
                a = a_scr[rows, :]
                h = a * hs[q] + b_scr[rows, :]
                p = a * ps[q]
                b_scr[rows, :] = h
                a_scr[rows, :] = p
                new_h.append(h)
                new_p.append(p)
            out.append((tuple(new_h), tuple(new_p)))
        return tuple(out)

    zero = jnp.zeros((SUBLANES, c_blk), F32)
    one = jnp.ones((SUBLANES, c_blk), F32)
    init = ((zero,) * SCAN_GROUPS, (one,) * SCAN_GROUPS)
    (hf_end, pf_end), (hb_end, pb_end) = lax.fori_loop(0, seg, scan_step, (init, init))

    row = lax.broadcasted_iota(jnp.int32, (SUBLANES, c_blk), 0)

    def entering(h_end, p_end, carry_in, first_row, shift):
        c = zero
        for _ in range(SUBLANES):
            c = jnp.where(row == first_row, carry_in, _shift_rows(h_end + p_end * c, shift))
        return c

    cf = [None] * SCAN_GROUPS
    cbk = [None] * SCAN_GROUPS
    carry_f = jnp.zeros((1, c_blk), F32)
    carry_b = jnp.zeros((1, c_blk), F32)
    for q in range(SCAN_GROUPS):
        cf[q] = entering(hf_end[q], pf_end[q], carry_f, 0, 1)
        carry_f = (hf_end[q] + pf_end[q] * cf[q])[SUBLANES - 1:SUBLANES, :]
        qb = SCAN_GROUPS - 1 - q
        cbk[qb] = entering(hb_end[qb], pb_end[qb], carry_b, SUBLANES - 1, -1)
        carry_b = (hb_end[qb] + pb_end[qb] * cbk[qb])[0:1, :]

    for j in range(n_seg):
        q, i = divmod(j, SUBLANES)
        src = pl.ds(j * seg_stride, seg)
        rows = pl.ds(j * seg, seg)
        hsum = ((bf[src, :] + af[src, :] * cf[q][i:i + 1, :])
                + (bb[src, :] + ab[src, :] * cbk[q][i:i + 1, :]))
        y_ref[0, rows, :] = (jax.nn.gelu(lg_ref[0, rows, :]) * hsum).astype(y_ref.dtype)


def _lru(lx, lg, conv_w, conv_b, w_cat, b_cat, lam_cat):
    bsz, s, d_lru = lx.shape
    c_blk = LANES
    n_blk = d_lru // c_blk
    scr = pltpu.VMEM((s + SCAN_GROUPS * SUBLANES * SEG_PAD, c_blk), F32)
    return pl.pallas_call(
        _lru_kernel,
        grid=(bsz, n_blk),
        in_specs=[
            pl.BlockSpec((1, s, c_blk), lambda b, c: (b, 0, c)),
            pl.BlockSpec((1, s, c_blk), lambda b, c: (b, 0, c)),
            pl.BlockSpec((4, c_blk), lambda b, c: (0, c)),
            pl.BlockSpec((1, c_blk), lambda b, c: (0, c)),
            pl.BlockSpec((1, c_blk, 4 * c_blk), lambda b, c: (c, 0, 0)),
            pl.BlockSpec((1, 1, 4 * c_blk), lambda b, c: (c, 0, 0)),
            pl.BlockSpec((1, 1, 2 * c_blk), lambda b, c: (c, 0, 0)),
        ],
        out_specs=pl.BlockSpec((1, s, c_blk), lambda b, c: (b, 0, c)),
        out_shape=jax.ShapeDtypeStruct((bsz, s, d_lru), BF16),
        scratch_shapes=[pltpu.VMEM((s + 2 * SUBLANES, c_blk), F32), scr, scr, scr, scr],
        compiler_params=_cparams(("parallel", "parallel")),
        name="lru",
    )(lx, lg, conv_w, conv_b, w_cat, b_cat, lam_cat)


def _fourier_kernel(f_ref, ka_ref, kb_ref, cc_ref, sc_ref, twc_ref, tws_ref, y_ref,
                    ar_scr, ai_scr, y_scr, *, scale):
    n1 = f_ref.shape[1]
    n_u = f_ref.shape[2]
    cols = f_ref.shape[4]
    r = SUBLANES * n1
    n2 = n_u * SUBLANES
    stage_stride = n1 + STAGE_PAD

    def stage_a(i, carry):
        us = tuple(FOURIER_U_BATCH * i + j for j in range(FOURIER_U_BATCH))
        xu = jnp.concatenate([f_ref[0, :, u].reshape(r, cols) for u in us], axis=1)
        a = jnp.dot(ka_ref[...], xu.astype(BF16), preferred_element_type=F32)
        for j, u in enumerate(us):
            a_re = a[:r, j * cols:(j + 1) * cols]
            a_im = a[r:, j * cols:(j + 1) * cols]
            c = twc_ref[u]
            sn = tws_ref[u]
            ar_scr[u] = a_re * c - a_im * sn
            ai_scr[u] = a_re * sn + a_im * c
        return carry

    lax.fori_loop(0, n_u // FOURIER_U_BATCH, stage_a, 0)

    nb = FOURIER_K1_BATCH

    def stage_b(i, carry):
        rows = pl.ds(pl.multiple_of(i * (nb * SUBLANES), nb * SUBLANES), nb * SUBLANES)
        a_re = ar_scr[:, rows, :]
        a_im = ai_scr[:, rows, :]

        def rows_s2(a, j):
            return a[:, j * SUBLANES:(j + 1) * SUBLANES, :].reshape(n2, cols)

        st = jnp.concatenate(
            [jnp.concatenate([rows_s2(a_re, j), rows_s2(a_im, j)], axis=0) for j in range(nb)],
            axis=1).astype(BF16)
        x = jnp.dot(kb_ref[...], st, preferred_element_type=F32)
        x_re = jnp.concatenate([x[:n2, j * cols:(j + 1) * cols] for j in range(nb)], axis=0)
        x_im = jnp.concatenate([x[n2:, j * cols:(j + 1) * cols] for j in range(nb)], axis=0)
        y = (jnp.dot(x_re.astype(BF16), cc_ref[...], preferred_element_type=F32)
             + jnp.dot(x_im.astype(BF16), sc_ref[...], preferred_element_type=F32)) * scale
        for j in range(nb):
            y_scr[pl.ds(i * nb + j, n2, stride=stage_stride), :] = y[j * n2:(j + 1) * n2]
        return carry

    lax.fori_loop(0, n1 // nb, stage_b, 0)

    def copy_out(k2, carry):
        y_ref[0, pl.ds(pl.multiple_of(k2 * n1, n1), n1), :] = y_scr[pl.ds(k2 * stage_stride, n1), :]
        return carry

    lax.fori_loop(0, n2, copy_out, 0)


def _dft_tables(s, c_grp):
    n2 = FOURIER_N2
    n1 = s // n2
    n_u = n2 // SUBLANES

    def cos_sin(num, den):
        ang = (2.0 * np.pi / den) * (num % den).astype(np.float64)
        return np.cos(ang), np.sin(ang)

    def const(a, dtype=F32):
        return jnp.asarray(a.astype(np.float32)).astype(dtype)

    i1 = np.arange(n1, dtype=np.int64)
    c1, s1 = cos_sin(i1[:, None] * i1[None, :], n1)
    eye8 = np.eye(SUBLANES)
    ka = const(np.concatenate([np.kron(c1, eye8), np.kron(-s1, eye8)], axis=0), BF16)

    i2 = np.arange(n2, dtype=np.int64)
    c2, s2 = cos_sin(i2[:, None] * i2[None, :], n2)
    kb = const(np.block([[c2, s2], [-s2, c2]]), BF16)

    ic = np.arange(c_grp, dtype=np.int64)
    cc, sc = cos_sin(ic[:, None] * ic[None, :], c_grp)

    u = np.arange(n_u, dtype=np.int64)[:, None, None]
    k1 = np.arange(n1, dtype=np.int64)[None, :, None]
    v = np.arange(SUBLANES, dtype=np.int64)[None, None, :]
    tc, ts = cos_sin(k1 * (SUBLANES * u + v), s)
    shape = (n_u, n1 * SUBLANES, c_grp)
    twc = jnp.broadcast_to(const(tc.reshape(n_u, -1, 1)), shape)
    tws = jnp.broadcast_to(const(-ts.reshape(n_u, -1, 1)), shape)
    return ka, kb, const(cc, BF16), const(sc, BF16), twc, tws


def _fourier(four, n_groups):
    bsz, s, d_four = four.shape
    c_grp = d_four // n_groups
    n2 = FOURIER_N2
    n1 = s // n2
    n_u = n2 // SUBLANES
    r = SUBLANES * n1
    ka, kb, cc, sc, twc, tws = _dft_tables(s, c_grp)
    f5 = four.reshape(bsz, n1, n_u, SUBLANES, d_four)
    scale = 1.0 / math.sqrt(s * c_grp)
    const2 = lambda b, g: (0, 0)
    const3 = lambda b, g: (0, 0, 0)
    return pl.pallas_call(
        functools.partial(_fourier_kernel, scale=scale),
        grid=(bsz, n_groups),
        in_specs=[
            pl.BlockSpec((1, n1, n_u, SUBLANES, c_grp), lambda b, g: (b, 0, 0, 0, g)),
            pl.BlockSpec(ka.shape, const2),
            pl.BlockSpec(kb.shape, const2),
            pl.BlockSpec(cc.shape, const2),
            pl.BlockSpec(sc.shape, const2),
            pl.BlockSpec(twc.shape, const3),
            pl.BlockSpec(tws.shape, const3),
        ],
        out_specs=pl.BlockSpec((1, s, c_grp), lambda b, g: (b, 0, g)),
        out_shape=jax.ShapeDtypeStruct((bsz, s, d_four), F32),
        scratch_shapes=[pltpu.VMEM((n_u, r, c_grp), F32), pltpu.VMEM((n_u, r, c_grp), F32),
                        pltpu.VMEM((n2 * (n1 + STAGE_PAD), c_grp), F32)],
        compiler_params=_cparams(("parallel", "parallel")),
        name="fourier",
    )(f5, ka, kb, cc, sc, twc, tws)


def _outproj_kernel(x_ref, yl_ref, yf_ref, wo1_ref, wo2_ref, g2_ref, wr_ref,
                    x1_ref, h2_ref, aff_ref):
    x1 = (x_ref[0]
          + jnp.dot(yl_ref[0], wo1_ref[...], preferred_element_type=F32)
          + jnp.dot(yf_ref[0].astype(BF16), wo2_ref[...], preferred_element_type=F32))
    x1_ref[0] = x1
    h2 = _rms_scale(x1) * g2_ref[...]
    for j in range(h2.shape[1] // LANES):
        h2_ref[pl.ds(j, h2.shape[0], stride=SUBLANES), :] = h2[:, j * LANES:(j + 1) * LANES]
    def split(v):
        hi = v.astype(BF16)
        return hi, (v - hi.astype(F32)).astype(BF16)

    def dot_nt(a, bm):
        return lax.dot_general(a, bm, (((1,), (1,)), ((), ())), preferred_element_type=F32)

    w_hi, w_lo = split(wr_ref[...])
    h_hi, h_lo = split(h2)
    logits = dot_nt(w_hi, h_hi) + (dot_nt(w_lo, h_hi) + dot_nt(w_hi, h_lo))
    ex = jnp.exp(logits - jnp.max(logits, axis=0, keepdims=True))
    aff_ref[0] = ex / jnp.sum(ex, axis=0, keepdims=True)


def _outproj(x, y_lru, y_four, wo1, wo2, g2, wr_t):
    bsz, s, d = x.shape
    d_lru = y_lru.shape[-1]
    d_four = y_four.shape[-1]
    n_e = wr_t.shape[0]
    assert d == SUBLANES * LANES
    n_i = s // ROW_TILE
    tile = lambda w: pl.BlockSpec((1, ROW_TILE, w), lambda b, i: (b, i, 0))
    const = lambda shape: pl.BlockSpec(shape, lambda b, i: (0, 0))
    return pl.pallas_call(
        _outproj_kernel,
        grid=(bsz, n_i),
        in_specs=[tile(d), tile(d_lru), tile(d_four), const(wo1.shape), const(wo2.shape),
                  const(g2.shape), const(wr_t.shape)],
        out_specs=[tile(d),
                   pl.BlockSpec((ROW_TILE * SUBLANES, LANES), lambda b, i: (b * n_i + i, 0)),
                   pl.BlockSpec((1, n_e, ROW_TILE), lambda b, i: (b, 0, i))],
        out_shape=[
            jax.ShapeDtypeStruct((bsz, s, d), F32),
            jax.ShapeDtypeStruct((bsz * s * SUBLANES, LANES), F32),
            jax.ShapeDtypeStruct((bsz, n_e, s), F32),
        ],
        compiler_params=_cparams(("parallel", "parallel")),
        name="outproj",
    )(x, y_lru, y_four, wo1, wo2, g2, wr_t)


def _select_kernel(aff_ref, tri_ref, pos_ref, starts_ref, *, cap):
    v = aff_ref[0]
    n_e, s = v.shape
    cap_f = float(cap)

    def midpoint(lo, hi):
        mid = 0.5 * (lo + hi)
        return mid, (mid > lo) & (mid < hi)

    def cond(carry):
        _, active = midpoint(*carry)
        return jnp.max(active.astype(F32)) > 0.0

    def body(carry):
        lo, hi = carry
        mid, active = midpoint(lo, hi)
        cnt = jnp.sum((v >= mid).astype(F32), axis=1, keepdims=True)
        enough = cnt >= cap_f
        return (jnp.where(active & enough, mid, lo), jnp.where(active & (~enough), mid, hi))

    lo0 = jnp.zeros((n_e, 1), F32)
    hi0 = jnp.full((n_e, 1), 2.0, F32)
    thr, _ = lax.while_loop(cond, body, (lo0, hi0))

    above = v > thr
    tie = v == thr
    need = cap_f - jnp.sum(above.astype(F32), axis=1, keepdims=True)

    n_tiles = s // TOKEN_TILE
    tri = tri_ref[...]
    lane = lax.broadcasted_iota(jnp.int32, (n_e, LANES), 1)

    def prefix(mask_f, want_starts):
        run = jnp.zeros((n_e, 1), F32)
        starts = jnp.zeros((n_e, LANES), F32)
        pieces = []
        for t in range(n_tiles):
            m = mask_f[:, t * TOKEN_TILE:(t + 1) * TOKEN_TILE]
            incl = jnp.dot(m.astype(BF16), tri, preferred_element_type=F32)
            pieces.append(run + incl - m)
            if want_starts:
                starts = jnp.where(lane == t, run, starts)
            run = run + incl[:, TOKEN_TILE - 1:TOKEN_TILE]
        if want_starts:
            starts = jnp.where(lane == n_tiles, run, starts)
        return jnp.concatenate(pieces, axis=1), starts

    tie_rank, _ = prefix(tie.astype(F32), False)
    sel = above | (tie & (tie_rank < need))
    pos, starts = prefix(sel.astype(F32), True)
    pos_ref[0] = jnp.where(sel, pos, -1.0).astype(jnp.int32)
    starts_ref[0] = starts.astype(jnp.int32)


def _select(aff_t, cap):
    bsz, n_e, s = aff_t.shape
    idx = jnp.arange(TOKEN_TILE, dtype=jnp.int32)
    tri = (idx[:, None] <= idx[None, :]).astype(BF16)
    return pl.pallas_call(
        functools.partial(_select_kernel, cap=cap),
        grid=(bsz,),
        in_specs=[
            pl.BlockSpec((1, n_e, s), lambda b: (b, 0, 0)),
            pl.BlockSpec(tri.shape, lambda b: (0, 0)),
        ],
        out_specs=[
            pl.BlockSpec((1, n_e, s), lambda b: (b, 0, 0)),
            pl.BlockSpec((1, n_e, LANES), lambda b: (b, 0, 0)),
        ],
        out_shape=[
            jax.ShapeDtypeStruct((bsz, n_e, s), jnp.int32),
            jax.ShapeDtypeStruct((bsz, n_e, LANES), jnp.int32),
        ],
        compiler_params=_cparams(("parallel",)),
        name="select",
    )(aff_t, tri)


def _token_table(gate, first_token):
    n_tok, n_e = gate.shape
    lane = lax.broadcasted_iota(jnp.int32, (n_tok, LANES), 1)
    tok = first_token + lax.broadcasted_iota(jnp.int32, (n_tok, LANES), 0)
    table = jnp.where(lane == 0, tok // TOKEN_ID_BASE, jnp.where(lane == 1, tok % TOKEN_ID_BASE, 0))
    table = table.astype(F32)
    place_lane = lax.broadcasted_iota(jnp.int32, (n_e, LANES), 1)
    place_sub = lax.broadcasted_iota(jnp.int32, (n_e, LANES), 0)
    rest = gate
    for j in range(3):
        piece = rest.astype(BF16)
        rest = rest - piece.astype(F32)
        place = jnp.where(place_lane == GATE_COL0 + j * n_e + place_sub, 1.0, 0.0).astype(BF16)
        table = table + jnp.dot(piece, place, preferred_element_type=F32)
    return table.astype(BF16)


def _compact_kernel(starts_ref, pos_ref, gate_ref, idx_ref, gs_ref, r_scr, *, cap):
    b = pl.program_id(0)
    n_e = pos_ref.shape[1]
    n_k = pos_ref.shape[2]
    n_blocks = cap // SLOT_BLOCK
    wide_windows = -(-(TOKEN_TILE + SUBLANES) // COMPACT_WIN)
    r_scr[...] = jnp.zeros_like(r_scr)
    sub = lax.broadcasted_iota(jnp.int32, (COMPACT_WIN, TOKEN_TILE), 0)

    def tile(k, carry):
        rows = pl.ds(pl.multiple_of(k * TOKEN_TILE, TOKEN_TILE), TOKEN_TILE)
        vk = _token_table(gate_ref[0, rows, :], k * TOKEN_TILE)
        starts = []
        fits = None
        for e in range(n_e):
            base = (b * n_e + e) * LANES
            ws = (starts_ref[base + k] // SUBLANES) * SUBLANES
            ok = starts_ref[base + k + 1] - ws <= COMPACT_WIN
            fits = ok if fits is None else fits & ok
            starts.append(ws)

        def place(e, ws):
            hit = pos_ref[0, e, k] - ws == sub
            vals = jnp.dot(jnp.where(hit, 1.0, 0.0).astype(BF16), vk, preferred_element_type=F32)
            r_scr[e, pl.ds(pl.multiple_of(ws, SUBLANES), COMPACT_WIN), :] += vals

        @pl.when(fits)
        def _():
            for e in range(n_e):
                place(e, starts[e])

        @pl.when(jnp.logical_not(fits))
        def _():
            for e in range(n_e):
                for j in range(wide_windows):
                    place(e, starts[e] + j * COMPACT_WIN)

        return carry

    lax.fori_loop(0, n_k, tile, 0)

    lane = lax.broadcasted_iota(jnp.int32, (SUBLANES, LANES), 1)
    sub8 = lax.broadcasted_iota(jnp.int32, (SUBLANES, LANES), 0)
    id_rows = jnp.where(sub8 == 0, jnp.where(lane == 0, float(TOKEN_ID_BASE),
                                             jnp.where(lane == 1, 1.0, 0.0)), 0.0)

    def finish(e, carry):
        gate_lane = (lane >= GATE_COL0) & (lane < GATE_COL0 + 3 * n_e) & ((lane - GATE_COL0) % n_e == e)
        selector = (id_rows + jnp.where((sub8 == 1) & gate_lane, 1.0, 0.0)).astype(BF16)
        for m in range(n_blocks):
            blk = r_scr[e, pl.ds(m * SLOT_BLOCK, SLOT_BLOCK), :].astype(BF16)
            rows = lax.dot_general(selector, blk, (((1,), (1,)), ((), ())),
                                   preferred_element_type=F32)
            idx_ref[0, e, m] = rows[0:1, :].astype(jnp.int32)
            gs_ref[0, e, m] = rows[1:2, :]
        return carry

    lax.fori_loop(0, n_e, finish, 0)


def _compact(starts_flat, pos5, gate_c, cap):
    bsz, n_e, n_k = pos5.shape[:3]
    s = gate_c.shape[1]
    assert s <= TOKEN_ID_BASE * 256 and GATE_COL0 + 3 * n_e <= LANES
    n_blocks = cap // SLOT_BLOCK
    out_block = (1, n_e, n_blocks, 1, SLOT_BLOCK)
    out_spec = pl.BlockSpec(out_block, lambda b, st: (b, 0, 0, 0, 0))
    wide_rows = -(-(TOKEN_TILE + SUBLANES) // COMPACT_WIN) * COMPACT_WIN
    grid_spec = pltpu.PrefetchScalarGridSpec(
        num_scalar_prefetch=1,
        grid=(bsz,),
        in_specs=[pl.BlockSpec((1, n_e, n_k, 1, TOKEN_TILE), lambda b, st: (b, 0, 0, 0, 0)),
                  pl.BlockSpec((1, s, n_e), lambda b, st: (b, 0, 0))],
        out_specs=[out_spec, out_spec],
        scratch_shapes=[pltpu.VMEM((n_e, cap + wide_rows, LANES), F32)],
    )
    return pl.pallas_call(
        functools.partial(_compact_kernel, cap=cap),
        grid_spec=grid_spec,
        out_shape=[jax.ShapeDtypeStruct((bsz,) + out_block[1:], jnp.int32),
                   jax.ShapeDtypeStruct((bsz,) + out_block[1:], F32)],
        compiler_params=_cparams(("parallel",)),
        name="compact",
    )(starts_flat, pos5, gate_c)


def _moe_kernel(idx_ref, idx_next_ref, gs_ref, h2_hbm, wg_ref, wu_ref, wd_ref, eo_ref,
                xbuf, sem, xb_scr, acc_scr, *, seq_len, n_ff):
    b = pl.program_id(0)
    e = pl.program_id(1)
    f = pl.program_id(2)
    n_e = pl.num_programs(1)
    n_blocks = xb_scr.shape[0]
    cap = n_blocks * SLOT_BLOCK
    d = xb_scr.shape[2]
    step = b * n_e + e
    n_steps = pl.num_programs(0) * n_e
    cur = step % 2

    def token_copy(idx_smem, batch, buf, p):
        tok = idx_smem[0, 0, p]
        src = h2_hbm.at[pl.ds(pl.multiple_of((batch * seq_len + tok) * SUBLANES, SUBLANES), SUBLANES)]
        dst = xbuf.at[buf, pl.ds(pl.multiple_of(p * SUBLANES, SUBLANES), SUBLANES)]
        return pltpu.make_async_copy(src, dst, sem.at[buf])

    def request(idx_smem, batch, buf, first, count):
        def body(i, carry):
            for j in range(GATHER_UNROLL):
                token_copy(idx_smem, batch, buf, first + i * GATHER_UNROLL + j).start()
            return carry
        lax.fori_loop(0, count // GATHER_UNROLL, body, 0)

    @pl.when((step == 0) & (f == 0))
    def _():
        request(idx_ref, b, cur, 0, cap)

    next_batch = jnp.where(step + 1 < n_steps, step + 1, 0) // n_e
    per_block = cap // (n_ff * n_blocks)

    @pl.when(f == 0)
    def _():
        pltpu.make_async_copy(h2_hbm.at[pl.ds(0, cap * SUBLANES)], xbuf.at[cur], sem.at[cur]).wait()
        for m in range(n_blocks):
            first = m * SLOT_BLOCK * SUBLANES
            xm = jnp.concatenate(
                [xbuf[cur, pl.ds(first + j, SLOT_BLOCK, stride=SUBLANES), :] for j in range(d // LANES)],
                axis=1)
            xb_scr[m] = xm.astype(BF16)
        acc_scr[...] = jnp.zeros_like(acc_scr)

    wg = wg_ref[0].astype(BF16)
    wu = wu_ref[0].astype(BF16)
    wd = wd_ref[0].astype(BF16)
    for m in range(n_blocks):
        first = (f * n_blocks + m) * per_block
        for j in range(per_block):
            token_copy(idx_next_ref, next_batch, 1 - cur, first + j).start()
        xm = xb_scr[m]
        g = jnp.dot(xm, wg, preferred_element_type=F32)
        u = jnp.dot(xm, wu, preferred_element_type=F32)
        act = (jax.nn.silu(g) * u).astype(BF16)
        acc_scr[m] += jnp.dot(act, wd, preferred_element_type=F32)

    @pl.when((step == n_steps - 1) & (f == n_ff - 1))
    def _():
        pltpu.make_async_copy(h2_hbm.at[pl.ds(0, cap * SUBLANES)], xbuf.at[1 - cur],
                              sem.at[1 - cur]).wait()

    @pl.when(f == n_ff - 1)
    def _():
        eye = (lax.broadcasted_iota(jnp.int32, (SLOT_BLOCK, SLOT_BLOCK), 0)
               == lax.broadcasted_iota(jnp.int32, (SLOT_BLOCK, SLOT_BLOCK), 1))
        for m in range(n_blocks):
            gcol = jnp.sum(jnp.where(eye, gs_ref[0, m], 0.0), axis=1, keepdims=True)
            rows = pl.ds(m * SLOT_BLOCK, SLOT_BLOCK)
            eo_ref[0, 0, rows, :] = (gcol * acc_scr[m]).astype(eo_ref.dtype)


def _moe(idx, gs, h2_tiles, wg, wu, wd, bsz, s):
    n_e, d, d_ff = wg.shape
    n_blocks = gs.shape[1]
    cap = n_blocks * SLOT_BLOCK
    n_ff = d_ff // FF_CHUNK
    assert d_ff % FF_CHUNK == 0 and cap % (n_ff * n_blocks) == 0 and cap % GATHER_UNROLL == 0
    assert d == SUBLANES * LANES

    def next_step(b, e, f):
        return ((b * n_e + e + 1) % (bsz * n_e), 0, 0)

    smem = pltpu.SMEM
    return pl.pallas_call(
        functools.partial(_moe_kernel, seq_len=s, n_ff=n_ff),
        grid=(bsz, n_e, n_ff),
        in_specs=[
            pl.BlockSpec((1, 1, cap), lambda b, e, f: (b * n_e + e, 0, 0), memory_space=smem),
            pl.BlockSpec((1, 1, cap), next_step, memory_space=smem),
            pl.BlockSpec((1, n_blocks, 1, SLOT_BLOCK), lambda b, e, f: (b * n_e + e, 0, 0, 0)),
            pl.BlockSpec(memory_space=pl.ANY),
            pl.BlockSpec((1, d, FF_CHUNK), lambda b, e, f: (e, 0, f)),
            pl.BlockSpec((1, d, FF_CHUNK), lambda b, e, f: (e, 0, f)),
            pl.BlockSpec((1, FF_CHUNK, d), lambda b, e, f: (e, f, 0)),
        ],
        out_specs=pl.BlockSpec((1, 1, cap, d), lambda b, e, f: (b, e, 0, 0)),
        out_shape=jax.ShapeDtypeStruct((bsz, n_e, cap, d), BF16),
        scratch_shapes=[pltpu.VMEM((2, cap * SUBLANES, LANES), F32),
                        pltpu.SemaphoreType.DMA((2,)),
                        pltpu.VMEM((n_blocks, SLOT_BLOCK, d), BF16),
                        pltpu.VMEM((n_blocks, SLOT_BLOCK, d), F32)],
        compiler_params=_cparams(("arbitrary", "arbitrary", "arbitrary")),
        name="moe",
    )(idx, idx, gs, h2_tiles, wg, wu, wd)


def _combine_kernel(starts_ref, x1_ref, pos_ref, gf_ref, eo_ref, out_ref, wcat_scr, acc_scr,
                    *, cap):
    n_tiles = x1_ref.shape[1] // TOKEN_TILE

    def tile(i, carry):
        _combine_tile(starts_ref, x1_ref, pos_ref, gf_ref, eo_ref, out_ref, wcat_scr, acc_scr,
                      pl.program_id(1) * n_tiles + i,
                      pl.ds(pl.multiple_of(i * TOKEN_TILE, TOKEN_TILE), TOKEN_TILE), cap)
        return carry

    lax.fori_loop(0, n_tiles, tile, 0)


def _combine_tile(starts_ref, x1_ref, pos_ref, gf_ref, eo_ref, out_ref, wcat_scr, acc_scr,
                  k, rows, cap):
    b = pl.program_id(0)
    n_e = pos_ref.shape[2]
    pos = pos_ref[0, rows, :]
    expert_lane = lax.broadcasted_iota(jnp.int32, (1, n_e), 1)

    s0 = [starts_ref[(b * n_e + e) * LANES + k] for e in range(n_e)]
    s1 = [starts_ref[(b * n_e + e) * LANES + k + 1] for e in range(n_e)]

    def window_starts(rows):
        starts = [jnp.minimum(s // WIN_ALIGN, (cap - rows) // WIN_ALIGN) * WIN_ALIGN for s in s0]
        vec = jnp.zeros((1, n_e), jnp.int32)
        for e in range(n_e):
            vec = jnp.where(expert_lane == e, starts[e], vec)
        return starts, vec

    def onehot(rel, first, count, width):
        n_lanes = count * width
        lane = lax.broadcasted_iota(jnp.int32, (n_e, n_lanes), 1)
        sub = lax.broadcasted_iota(jnp.int32, (n_e, n_lanes), 0)
        expand = jnp.where(sub == first + lane // width, 1.0, 0.0).astype(BF16)
        spread = jnp.dot(jnp.clip(rel, -1, width).astype(F32).astype(BF16), expand,
                         preferred_element_type=F32)
        want = (lax.broadcasted_iota(jnp.int32, (TOKEN_TILE, n_lanes), 1) % width).astype(F32)
        return jnp.where(spread == want, 1.0, 0.0).astype(BF16)

    def finish(acc):
        out_ref[0, rows, :] = _rms_scale(acc) * gf_ref[...]

    narrow, narrow_vec = window_starts(NARROW_WIN)
    fits = None
    for e in range(n_e):
        ok = s1[e] - narrow[e] <= NARROW_WIN
        fits = ok if fits is None else fits & ok

    @pl.when(fits)
    def _():
        for e in range(n_e):
            src = pl.ds(pl.multiple_of(narrow[e], WIN_ALIGN), NARROW_WIN)
            wcat_scr[pl.ds(e * NARROW_WIN, NARROW_WIN), :] = eo_ref[0, e, src, :]
        hit = onehot(pos - narrow_vec, 0, n_e, NARROW_WIN)
        finish(x1_ref[0, rows, :] + jnp.dot(hit, wcat_scr[pl.ds(0, n_e * NARROW_WIN), :],
                                            preferred_element_type=F32))

    @pl.when(jnp.logical_not(fits))
    def _():
        wide, wide_vec = window_starts(WIN_ROWS)
        rel = pos - wide_vec
        group = wcat_scr.shape[0] // SLOT_BLOCK
        acc_scr[...] = x1_ref[0, rows, :]
        for g in range(n_e // group):
            for i in range(group):
                e = g * group + i
                src = pl.ds(pl.multiple_of(wide[e], WIN_ALIGN), SLOT_BLOCK)
                wcat_scr[pl.ds(i * SLOT_BLOCK, SLOT_BLOCK), :] = eo_ref[0, e, src, :]
            acc_scr[...] += jnp.dot(onehot(rel, g * group, group, SLOT_BLOCK), wcat_scr[...],
                                    preferred_element_type=F32)
        for e in range(n_e):
            src = pl.ds(pl.multiple_of(wide[e] + SLOT_BLOCK, WIN_ALIGN), WIN_ALIGN)
            wcat_scr[pl.ds(e * WIN_ALIGN, WIN_ALIGN), :] = eo_ref[0, e, src, :]
        tail = onehot(rel - SLOT_BLOCK, 0, n_e, WIN_ALIGN)
        finish(acc_scr[...] + jnp.dot(tail, wcat_scr[pl.ds(0, n_e * WIN_ALIGN), :],
                                      preferred_element_type=F32))


def _combine(starts_flat, x1, pos_c, eo, gf):
    bsz, s, d = x1.shape
    n_e = pos_c.shape[-1]
    cap = eo.shape[2]
    step_rows = COMBINE_TILES * TOKEN_TILE
    wide_group = 4
    assert n_e % wide_group == 0 and n_e * NARROW_WIN <= wide_group * SLOT_BLOCK
    assert s % step_rows == 0
    grid_spec = pltpu.PrefetchScalarGridSpec(
        num_scalar_prefetch=1,
        grid=(bsz, s // step_rows),
        in_specs=[
            pl.BlockSpec((1, step_rows, d), lambda b, k, st: (b, k, 0)),
            pl.BlockSpec((1, step_rows, n_e), lambda b, k, st: (b, k, 0)),
            pl.BlockSpec((1, d), lambda b, k, st: (0, 0)),
            pl.BlockSpec((1, n_e, cap, d), lambda b, k, st: (b, 0, 0, 0),
                         pipeline_mode=pl.Buffered(1)),
        ],
        out_specs=pl.BlockSpec((1, step_rows, d), lambda b, k, st: (b, k, 0)),
        scratch_shapes=[pltpu.VMEM((wide_group * SLOT_BLOCK, d), BF16),
                        pltpu.VMEM((TOKEN_TILE, d), F32)],
    )
    return pl.pallas_call(
        functools.partial(_combine_kernel, cap=cap),
        grid_spec=grid_spec,
        out_shape=jax.ShapeDtypeStruct((bsz, s, d), F32),
        compiler_params=_cparams(("parallel", "parallel")),
        name="combine",
    )(starts_flat, x1, pos_c, gf, eo)


def _block_diag(w):
    h, hd, _ = w.shape
    eye = jnp.eye(h, dtype=w.dtype)
    return (eye[:, None, :, None] * w[:, :, None, :]).reshape(h * hd, h * hd)


def _lru_params(wa_f, wx_f, wa_b, wx_b, ba_f, bx_f, ba_b, bx_b, lam_f, lam_b):
    d_lru = ba_f.shape[0]
    n_blk = d_lru // LANES
    mats = [0.5 * _block_diag(w) for w in (wa_f, wx_f, wa_b, wx_b)]
    ba_f, bx_f, ba_b, bx_b = (0.5 * v for v in (ba_f, bx_f, ba_b, bx_b))
    w_cat = jnp.stack([
        jnp.concatenate([m[c * LANES:(c + 1) * LANES, c * LANES:(c + 1) * LANES] for m in mats], axis=1)
        for c in range(n_blk)]).astype(BF16)
    b_cat = jnp.stack([
        jnp.concatenate([v[c * LANES:(c + 1) * LANES] for v in (ba_f, bx_f, ba_b, bx_b)])
        for c in range(n_blk)])[:, None, :]
    lam_cat = jnp.stack([
        jnp.concatenate([v[c * LANES:(c + 1) * LANES] for v in (lam_f, lam_b)])
        for c in range(n_blk)])[:, None, :]
    return w_cat, b_cat, lam_cat


def kernel(x, norm1_g, w_in, conv_w, conv_b, lru_wa_f, lru_ba_f, lru_wx_f, lru_bx_f, lru_lam_f,
           lru_wa_b, lru_ba_b, lru_wx_b, lru_bx_b, lru_lam_b, w_out, norm2_g, w_router,
           w_gate, w_up, w_down, normf_g):
    bsz, s, d = x.shape
    d_lru = conv_b.shape[0]
    d_four = w_in.shape[1] - 2 * d_lru
    n_e = w_router.shape[1]
    cap = CAPACITY_FACTOR * s // n_e
    assert s % LRU_CHUNK == 0 and s % (SCAN_GROUPS * SUBLANES * SUBLANES) == 0
    assert s % FOURIER_N2 == 0 and cap % SLOT_BLOCK == 0 and s // TOKEN_TILE < LANES
    assert cap >= WIN_ROWS and w_gate.shape[2] % FF_CHUNK == 0

    lx, lg, fo = _inproj(x.reshape(bsz * s, d), norm1_g[None, :], w_in.astype(BF16), d_lru, d_four)
    lx = lx.reshape(bsz, s, d_lru)
    lg = lg.reshape(bsz, s, d_lru)
    fo = fo.reshape(bsz, s, d_four)

    w_cat, b_cat, lam_cat = _lru_params(lru_wa_f, lru_wx_f, lru_wa_b, lru_wx_b,
                                        lru_ba_f, lru_bx_f, lru_ba_b, lru_bx_b,
                                        lru_lam_f, lru_lam_b)
    y_lru = _lru(lx, lg, conv_w, conv_b[None, :], w_cat, b_cat, lam_cat)
    y_four = _fourier(fo, FOURIER_GROUPS)

    w_out_bf = w_out.astype(BF16)
    x1, h2_tiles, aff_t = _outproj(x, y_lru, y_four, w_out_bf[:d_lru], w_out_bf[d_lru:],
                                   norm2_g[None, :], w_router.T)

    pos_r, starts = _select(aff_t, cap)
    starts_flat = starts.reshape(-1)
    pos_c = jnp.transpose(pos_r, (0, 2, 1))
    pos5 = pos_r.reshape(bsz, n_e, s // TOKEN_TILE, 1, TOKEN_TILE)
    idx, gs = _compact(starts_flat, pos5, jnp.transpose(aff_t, (0, 2, 1)), cap)
    eo = _moe(idx.reshape(bsz * n_e, 1, cap), gs.reshape((bsz * n_e,) + gs.shape[2:]), h2_tiles,
              w_gate, w_up, w_down, bsz, s)
    return _combine(starts_flat, x1, pos_c, eo, normf_g[None, :])
```

```python
import functools
import math

import jax
import jax.numpy as jnp
import numpy as np
from jax import lax
from jax.experimental import pallas as pl
from jax.experimental.pallas import tpu as pltpu

F32 = jnp.float32
BF16 = jnp.bfloat16

EPS = 1e-6
LRU_C = 8.0
CAPACITY_FACTOR = 2
FOURIER_GROUPS = 4

SUBLANES = 8
LANES = 128
MXU_DIM = 256
VMEM_LIMIT_BYTES = 56 * 1024 * 1024

ROW_TILE = 1024
LRU_CHUNK = 2048
SCAN_GROUPS = 4
SCAN_UNROLL = 4
SLOT_BLOCK = MXU_DIM
TOKEN_TILE = MXU_DIM
SEG_PAD = 4
FOURIER_N2 = LANES
FOURIER_K1_BATCH = 32
FOURIER_U_BATCH = 2
STAGE_PAD = 4
COPY_UNROLL = 4
FF_CHUNK = 1024
GATHER_UNROLL = 8
COMPACT_WIN = 64
TOKEN_ID_BASE = 64
GATE_COL0 = 16
WIN_ALIGN = 16
WIN_ROWS = SLOT_BLOCK + WIN_ALIGN
NARROW_WIN = 64
COMBINE_TILES = 4


def _cparams(semantics):
    return pltpu.CompilerParams(dimension_semantics=semantics,
                                vmem_limit_bytes=VMEM_LIMIT_BYTES)


def _rms_scale(x):
    return x * lax.rsqrt(jnp.mean(x * x, axis=-1, keepdims=True) + EPS)


def _inproj_kernel(x_ref, g_ref, w_ref, lx_ref, lg_ref, fo_ref):
    h = _rms_scale(x_ref[...]) * g_ref[...]
    p = jnp.dot(h.astype(BF16), w_ref[...], preferred_element_type=F32)
    d = lx_ref.shape[-1]
    lx_ref[...] = p[:, :d]
    lg_ref[...] = p[:, d:2 * d]
    fo_ref[...] = p[:, 2 * d:]


def _inproj(x2, g, w_bf, d_lru, d_four):
    m, d = x2.shape
    n = w_bf.shape[1]
    return pl.pallas_call(
        _inproj_kernel,
        grid=(m // ROW_TILE,),
        in_specs=[
            pl.BlockSpec((ROW_TILE, d), lambda i: (i, 0)),
            pl.BlockSpec((1, d), lambda i: (0, 0)),
            pl.BlockSpec((d, n), lambda i: (0, 0)),
        ],
        out_specs=[
            pl.BlockSpec((ROW_TILE, d_lru), lambda i: (i, 0)),
            pl.BlockSpec((ROW_TILE, d_lru), lambda i: (i, 0)),
            pl.BlockSpec((ROW_TILE, d_four), lambda i: (i, 0)),
        ],
        out_shape=[
            jax.ShapeDtypeStruct((m, d_lru), F32),
            jax.ShapeDtypeStruct((m, d_lru), F32),
            jax.ShapeDtypeStruct((m, d_four), F32),
        ],
        compiler_params=_cparams(("parallel",)),
        name="inproj",
    )(x2, g, w_bf)


def _shift_rows(x, shift):
    n = x.shape[0]
    rows = lax.broadcasted_iota(jnp.int32, x.shape, 0)
    rolled = pltpu.roll(x, shift % n, axis=0)
    keep = (rows >= shift) if shift > 0 else (rows < n + shift)
    return jnp.where(keep, rolled, 0.0)


def _lru_kernel(lx_ref, lg_ref, cw_ref, cb_ref, w_ref, b_ref, lam_ref, y_ref,
                xpad, af, bf, ab, bb):
    s = lx_ref.shape[1]
    c_blk = lx_ref.shape[2]
    n_seg = SCAN_GROUPS * SUBLANES
    seg = s // n_seg
    seg_stride = seg + SEG_PAD
    piece = min(seg, LRU_CHUNK)
    n_chunks = s // LRU_CHUNK
    pad = SUBLANES

    zeros_pad = jnp.zeros((pad, c_blk), F32)
    xpad[pl.ds(0, pad), :] = zeros_pad
    xpad[pl.ds(pad + s, pad), :] = zeros_pad

    def copy_chunk(i, carry):
        t0 = pl.multiple_of(i * LRU_CHUNK, LRU_CHUNK)
        xpad[pl.ds(pad + t0, LRU_CHUNK), :] = lx_ref[0, pl.ds(t0, LRU_CHUNK), :]
        return carry

    lax.fori_loop(0, n_chunks, copy_chunk, 0)

    cw = cw_ref[...]
    conv_width = cw.shape[0]
    cb = cb_ref[...]
    bias = b_ref[0]
    quarter_c_log_sig = (0.25 * LRU_C) * jax.nn.log_sigmoid(lam_ref[0])
    w_cat = w_ref[0]

    def gates_chunk(i, carry):
        t0 = pl.multiple_of(i * LRU_CHUNK, LRU_CHUNK)
        c = cb
        left = conv_width // 2
        for k in range(conv_width):
            c = c + cw[k:k + 1, :] * xpad[pl.ds(t0 + (pad - left + k), LRU_CHUNK), :]
        z = jnp.dot(c.astype(BF16), w_cat, preferred_element_type=F32) + bias
        for d, (a_scr, b_scr) in enumerate(((af, bf), (ab, bb))):
            q = quarter_c_log_sig[:, d * c_blk:(d + 1) * c_blk]
            th_r = jnp.tanh(z[:, (2 * d) * c_blk:(2 * d + 1) * c_blk])
            th_i = jnp.tanh(z[:, (2 * d + 1) * c_blk:(2 * d + 2) * c_blk])
            t = jnp.tanh(q + q * th_r)
            inv = 1.0 / (1.0 - t)
            a = (1.0 + t) * inv
            bt = (jnp.sqrt(-t) * inv) * (c + c * th_i)
            for p in range(LRU_CHUNK // piece):
                t = t0 + p * piece
                dst = pl.ds((t // seg) * seg_stride + t % seg, piece)
                a_scr[dst, :] = a[p * piece:(p + 1) * piece]
                b_scr[dst, :] = bt[p * piece:(p + 1) * piece]
        return carry

    lax.fori_loop(0, n_chunks, gates_chunk, 0)

    def scan_step(i, carry):
        def rows(o, q):
            return pl.ds(q * SUBLANES * seg_stride + o, SUBLANES, stride=seg_stride)

        offsets = [(i * SCAN_UNROLL + k, seg - 1 - (i * SCAN_UNROLL + k)) for k in range(SCAN_UNROLL)]
        scratch = ((af, bf), (ab, bb))
        loaded = [[[(scratch[d][0][rows(o[d], q), :], scratch[d][1][rows(o[d], q), :])
                    for q in range(SCAN_GROUPS)] for d in range(2)] for o in offsets]
        state = [[list(hs), list(ps)] for hs, ps in carry]
        results = []
        for k, o in enumerate(offsets):
            for d in range(2):
                for q in range(SCAN_GROUPS):
                    a, b = loaded[k][d][q]
                    state[d][0][q] = a * state[d][0][q] + b
                    state[d][1][q] = a * state[d][1][q]
                    results.append((d, o[d], q, state[d][0][q], state[d][1][q]))
        for d, o, q, h, p in results:
            scratch[d][1][rows(o, q), :] = h
            scratch[d][0][rows(o, q), :] = p
        return tuple((tuple(hs), tuple(ps)) for hs, ps in state)

    zero = jnp.zeros((SUBLANES, c_blk), F32)
    one = jnp.ones((SUBLANES, c_blk), F32)
    init = ((zero,) * SCAN_GROUPS, (one,) * SCAN_GROUPS)
    (hf_end, pf_end), (hb_end, pb_end) = lax.fori_loop(0, seg // SCAN_UNROLL, scan_step, (init, init))

    row = lax.broadcasted_iota(jnp.int32, (SUBLANES, c_blk), 0)

    def entering(h_end, p_end, carry_in, first_row, shift):
        c = zero
        for _ in range(SUBLANES):
            c = jnp.where(row == first_row, carry_in, _shift_rows(h_end + p_end * c, shift))
        return c

    cf = [None] * SCAN_GROUPS
    cbk = [None] * SCAN_GROUPS
    carry_f = jnp.zeros((1, c_blk), F32)
    carry_b = jnp.zeros((1, c_blk), F32)
    for q in range(SCAN_GROUPS):
        cf[q] = entering(hf_end[q], pf_end[q], carry_f, 0, 1)
        carry_f = (hf_end[q] + pf_end[q] * cf[q])[SUBLANES - 1:SUBLANES, :]
        qb = SCAN_GROUPS - 1 - q
        cbk[qb] = entering(hb_end[qb], pb_end[qb], carry_b, SUBLANES - 1, -1)
        carry_b = (hb_end[qb] + pb_end[qb] * cbk[qb])[0:1, :]

    for j in range(n_seg):
        q, i = divmod(j, SUBLANES)
        src = pl.ds(j * seg_stride, seg)
        rows = pl.ds(j * seg, seg)
        hsum = ((bf[src, :] + af[src, :] * cf[q][i:i + 1, :])
                + (bb[src, :] + ab[src, :] * cbk[q][i:i + 1, :]))
        y_ref[0, rows, :] = (jax.nn.gelu(lg_ref[0, rows, :]) * hsum).astype(y_ref.dtype)


def _lru(lx, lg, conv_w, conv_b, w_cat, b_cat, lam_cat):
    bsz, s, d_lru = lx.shape
    c_blk = LANES
    n_blk = d_lru // c_blk
    scr = pltpu.VMEM((s + SCAN_GROUPS * SUBLANES * SEG_PAD, c_blk), F32)
    return pl.pallas_call(
        _lru_kernel,
        grid=(bsz, n_blk),
        in_specs=[
            pl.BlockSpec((1, s, c_blk), lambda b, c: (b, 0, c)),
            pl.BlockSpec((1, s, c_blk), lambda b, c: (b, 0, c)),
            pl.BlockSpec((4, c_blk), lambda b, c: (0, c)),
            pl.BlockSpec((1, c_blk), lambda b, c: (0, c)),
            pl.BlockSpec((1, c_blk, 4 * c_blk), lambda b, c: (c, 0, 0)),
            pl.BlockSpec((1, 1, 4 * c_blk), lambda b, c: (c, 0, 0)),
            pl.BlockSpec((1, 1, 2 * c_blk), lambda b, c: (c, 0, 0)),
        ],
        out_specs=pl.BlockSpec((1, s, c_blk), lambda b, c: (b, 0, c)),
        out_shape=jax.ShapeDtypeStruct((bsz, s, d_lru), BF16),
        scratch_shapes=[pltpu.VMEM((s + 2 * SUBLANES, c_blk), F32), scr, scr, scr, scr],
        compiler_params=_cparams(("parallel", "parallel")),
        name="lru",
    )(lx, lg, conv_w, conv_b, w_cat, b_cat, lam_cat)


def _fourier_kernel(f_ref, ka_ref, kb_ref, cc_ref, sc_ref, twc_ref, tws_ref, y_ref,
                    ar_scr, ai_scr, y_scr, *, scale):
    n1 = f_ref.shape[1]
    n_u = f_ref.shape[2]
    cols = f_ref.shape[4]
    r = SUBLANES * n1
    n2 = n_u * SUBLANES
    stage_stride = n1 + STAGE_PAD

    def stage_a(i, carry):
        us = tuple(FOURIER_U_BATCH * i + j for j in range(FOURIER_U_BATCH))
        xu = jnp.concatenate([f_ref[0, :, u].reshape(r, cols) for u in us], axis=1)
        a = jnp.dot(ka_ref[...], xu.astype(BF16), preferred_element_type=F32)
        for j, u in enumerate(us):
            a_re = a[:r, j * cols:(j + 1) * cols]
            a_im = a[r:, j * cols:(j + 1) * cols]
            c = twc_ref[u]
            sn = tws_ref[u]
            ar_scr[u] = a_re * c - a_im * sn
            ai_scr[u] = a_re * sn + a_im * c
        return carry

    lax.fori_loop(0, n_u // FOURIER_U_BATCH, stage_a, 0)

    nb = FOURIER_K1_BATCH

    def stage_b(i, carry):
        rows = pl.ds(pl.multiple_of(i * (nb * SUBLANES), nb * SUBLANES), nb * SUBLANES)
        a_re = ar_scr[:, rows, :]
        a_im = ai_scr[:, rows, :]

        def rows_s2(a, j):
            return a[:, j * SUBLANES:(j + 1) * SUBLANES, :].reshape(n2, cols)

        st = jnp.concatenate(
            [jnp.concatenate([rows_s2(a_re, j), rows_s2(a_im, j)], axis=0) for j in range(nb)],
            axis=1).astype(BF16)
        x = jnp.dot(kb_ref[...], st, preferred_element_type=F32)
        x_re = jnp.concatenate([x[:n2, j * cols:(j + 1) * cols] for j in range(nb)], axis=0)
        x_im = jnp.concatenate([x[n2:, j * cols:(j + 1) * cols] for j in range(nb)], axis=0)
        y = (jnp.dot(x_re.astype(BF16), cc_ref[...], preferred_element_type=F32)
             + jnp.dot(x_im.astype(BF16), sc_ref[...], preferred_element_type=F32)) * scale
        for j in range(nb):
            y_scr[pl.ds(i * nb + j, n2, stride=stage_stride), :] = y[j * n2:(j + 1) * n2]
        return carry

    lax.fori_loop(0, n1 // nb, stage_b, 0)

    def copy_out(i, carry):
        blocks = [y_scr[pl.ds((i * COPY_UNROLL + j) * stage_stride, n1), :] for j in range(COPY_UNROLL)]
        for j in range(COPY_UNROLL):
            k2 = i * COPY_UNROLL + j
            y_ref[0, pl.ds(pl.multiple_of(k2 * n1, n1), n1), :] = blocks[j]
        return carry

    lax.fori_loop(0, n2 // COPY_UNROLL, copy_out, 0)


def _dft_tables(s, c_grp):
    n2 = FOURIER_N2
    n1 = s // n2
    n_u = n2 // SUBLANES

    def cos_sin(num, den):
        ang = (2.0 * np.pi / den) * (num % den).astype(np.float64)
        return np.cos(ang), np.sin(ang)

    def const(a, dtype=F32):
        return jnp.asarray(a.astype(np.float32)).astype(dtype)

    i1 = np.arange(n1, dtype=np.int64)
    c1, s1 = cos_sin(i1[:, None] * i1[None, :], n1)
    eye8 = np.eye(SUBLANES)
    ka = const(np.concatenate([np.kron(c1, eye8), np.kron(-s1, eye8)], axis=0), BF16)

    i2 = np.arange(n2, dtype=np.int64)
    c2, s2 = cos_sin(i2[:, None] * i2[None, :], n2)
    kb = const(np.block([[c2, s2], [-s2, c2]]), BF16)

    ic = np.arange(c_grp, dtype=np.int64)
    cc, sc = cos_sin(ic[:, None] * ic[None, :], c_grp)

    u = np.arange(n_u, dtype=np.int64)[:, None, None]
    k1 = np.arange(n1, dtype=np.int64)[None, :, None]
    v = np.arange(SUBLANES, dtype=np.int64)[None, None, :]
    tc, ts = cos_sin(k1 * (SUBLANES * u + v), s)
    shape = (n_u, n1 * SUBLANES, c_grp)
    twc = jnp.broadcast_to(const(tc.reshape(n_u, -1, 1)), shape)
    tws = jnp.broadcast_to(const(-ts.reshape(n_u, -1, 1)), shape)
    return ka, kb, const(cc, BF16), const(sc, BF16), twc, tws


def _fourier(four, n_groups):
    bsz, s, d_four = four.shape
    c_grp = d_four // n_groups
    n2 = FOURIER_N2
    n1 = s // n2
    n_u = n2 // SUBLANES
    r = SUBLANES * n1
    ka, kb, cc, sc, twc, tws = _dft_tables(s, c_grp)
    f5 = four.reshape(bsz, n1, n_u, SUBLANES, d_four)
    scale = 1.0 / math.sqrt(s * c_grp)
    const2 = lambda b, g: (0, 0)
    const3 = lambda b, g: (0, 0, 0)
    return pl.pallas_call(
        functools.partial(_fourier_kernel, scale=scale),
        grid=(bsz, n_groups),
        in_specs=[
            pl.BlockSpec((1, n1, n_u, SUBLANES, c_grp), lambda b, g: (b, 0, 0, 0, g)),
            pl.BlockSpec(ka.shape, const2),
            pl.BlockSpec(kb.shape, const2),
            pl.BlockSpec(cc.shape, const2),
            pl.BlockSpec(sc.shape, const2),
            pl.BlockSpec(twc.shape, const3),
            pl.BlockSpec(tws.shape, const3),
        ],
        out_specs=pl.BlockSpec((1, s, c_grp), lambda b, g: (b, 0, g)),
        out_shape=jax.ShapeDtypeStruct((bsz, s, d_four), F32),
        scratch_shapes=[pltpu.VMEM((n_u, r, c_grp), F32), pltpu.VMEM((n_u, r, c_grp), F32),
                        pltpu.VMEM((n2 * (n1 + STAGE_PAD), c_grp), F32)],
        compiler_params=_cparams(("parallel", "parallel")),
        name="fourier",
    )(f5, ka, kb, cc, sc, twc, tws)


def _outproj_kernel(x_ref, yl_ref, yf_ref, wo1_ref, wo2_ref, g2_ref, wr_ref,
                    x1_ref, h2_ref, aff_ref):
    x1 = (x_ref[0]
          + jnp.dot(yl_ref[0], wo1_ref[...], preferred_element_type=F32)
          + jnp.dot(yf_ref[0].astype(BF16), wo2_ref[...], preferred_element_type=F32))
    x1_ref[0] = x1
    h2 = _rms_scale(x1) * g2_ref[...]
    for j in range(h2.shape[1] // LANES):
        h2_ref[pl.ds(j, h2.shape[0], stride=SUBLANES), :] = h2[:, j * LANES:(j + 1) * LANES]
    def split(v):
        hi = v.astype(BF16)
        return hi, (v - hi.astype(F32)).astype(BF16)

    def dot_nt(a, bm):
        return lax.dot_general(a, bm, (((1,), (1,)), ((), ())), preferred_element_type=F32)

    w_hi, w_lo = split(wr_ref[...])
    h_hi, h_lo = split(h2)
    logits = dot_nt(w_hi, h_hi) + (dot_nt(w_lo, h_hi) + dot_nt(w_hi, h_lo))
    ex = jnp.exp(logits - jnp.max(logits, axis=0, keepdims=True))
    aff_ref[0] = ex / jnp.sum(ex, axis=0, keepdims=True)


def _outproj(x, y_lru, y_four, wo1, wo2, g2, wr_t):
    bsz, s, d = x.shape
    d_lru = y_lru.shape[-1]
    d_four = y_four.shape[-1]
    n_e = wr_t.shape[0]
    assert d == SUBLANES * LANES
    n_i = s // ROW_TILE
    tile = lambda w: pl.BlockSpec((1, ROW_TILE, w), lambda b, i: (b, i, 0))
    const = lambda shape: pl.BlockSpec(shape, lambda b, i: (0, 0))
    return pl.pallas_call(
        _outproj_kernel,
        grid=(bsz, n_i),
        in_specs=[tile(d), tile(d_lru), tile(d_four), const(wo1.shape), const(wo2.shape),
                  const(g2.shape), const(wr_t.shape)],
        out_specs=[tile(d),
                   pl.BlockSpec((ROW_TILE * SUBLANES, LANES), lambda b, i: (b * n_i + i, 0)),
                   pl.BlockSpec((1, n_e, ROW_TILE), lambda b, i: (b, 0, i))],
        out_shape=[
            jax.ShapeDtypeStruct((bsz, s, d), F32),
            jax.ShapeDtypeStruct((bsz * s * SUBLANES, LANES), F32),
            jax.ShapeDtypeStruct((bsz, n_e, s), F32),
        ],
        compiler_params=_cparams(("parallel", "parallel")),
        name="outproj",
    )(x, y_lru, y_four, wo1, wo2, g2, wr_t)


def _select_kernel(aff_ref, tri_ref, pos_ref, starts_ref, *, cap):
    v = aff_ref[0]
    n_e, s = v.shape
    cap_f = float(cap)

    def midpoint(lo, hi):
        mid = 0.5 * (lo + hi)
        return mid, (mid > lo) & (mid < hi)

    def cond(carry):
        _, active = midpoint(*carry)
        return jnp.max(active.astype(F32)) > 0.0

    def body(carry):
        lo, hi = carry
        mid, active = midpoint(lo, hi)
        cnt = jnp.sum((v >= mid).astype(F32), axis=1, keepdims=True)
        enough = cnt >= cap_f
        return (jnp.where(active & enough, mid, lo), jnp.where(active & (~enough), mid, hi))

    lo0 = jnp.zeros((n_e, 1), F32)
    hi0 = jnp.full((n_e, 1), 2.0, F32)
    thr, _ = lax.while_loop(cond, body, (lo0, hi0))

    above = v > thr
    tie = v == thr
    need = cap_f - jnp.sum(above.astype(F32), axis=1, keepdims=True)

    n_tiles = s // TOKEN_TILE
    tri = tri_ref[...]
    lane = lax.broadcasted_iota(jnp.int32, (n_e, LANES), 1)

    def prefix(mask_f, want_starts):
        run = jnp.zeros((n_e, 1), F32)
        starts = jnp.zeros((n_e, LANES), F32)
        pieces = []
        for t in range(n_tiles):
            m = mask_f[:, t * TOKEN_TILE:(t + 1) * TOKEN_TILE]
            incl = jnp.dot(m.astype(BF16), tri, preferred_element_type=F32)
            pieces.append(run + incl - m)
            if want_starts:
                starts = jnp.where(lane == t, run, starts)
            run = run + incl[:, TOKEN_TILE - 1:TOKEN_TILE]
        if want_starts:
            starts = jnp.where(lane == n_tiles, run, starts)
        return jnp.concatenate(pieces, axis=1), starts

    tie_rank, _ = prefix(tie.astype(F32), False)
    sel = above | (tie & (tie_rank < need))
    pos, starts = prefix(sel.astype(F32), True)
    pos_ref[0] = jnp.where(sel, pos, -1.0).astype(jnp.int32)
    starts_ref[0] = starts.astype(jnp.int32)


def _select(aff_t, cap):
    bsz, n_e, s = aff_t.shape
    idx = jnp.arange(TOKEN_TILE, dtype=jnp.int32)
    tri = (idx[:, None] <= idx[None, :]).astype(BF16)
    return pl.pallas_call(
        functools.partial(_select_kernel, cap=cap),
        grid=(bsz,),
        in_specs=[
            pl.BlockSpec((1, n_e, s), lambda b: (b, 0, 0)),
            pl.BlockSpec(tri.shape, lambda b: (0, 0)),
        ],
        out_specs=[
            pl.BlockSpec((1, n_e, s), lambda b: (b, 0, 0)),
            pl.BlockSpec((1, n_e, LANES), lambda b: (b, 0, 0)),
        ],
        out_shape=[
            jax.ShapeDtypeStruct((bsz, n_e, s), jnp.int32),
            jax.ShapeDtypeStruct((bsz, n_e, LANES), jnp.int32),
        ],
        compiler_params=_cparams(("parallel",)),
        name="select",
    )(aff_t, tri)


def _token_table(gate, first_token):
    n_tok, n_e = gate.shape
    lane = lax.broadcasted_iota(jnp.int32, (n_tok, LANES), 1)
    tok = first_token + lax.broadcasted_iota(jnp.int32, (n_tok, LANES), 0)
    table = jnp.where(lane == 0, tok // TOKEN_ID_BASE, jnp.where(lane == 1, tok % TOKEN_ID_BASE, 0))
    table = table.astype(F32)
    place_lane = lax.broadcasted_iota(jnp.int32, (n_e, LANES), 1)
    place_sub = lax.broadcasted_iota(jnp.int32, (n_e, LANES), 0)
    rest = gate
    for j in range(3):
        piece = rest.astype(BF16)
        rest = rest - piece.astype(F32)
        place = jnp.where(place_lane == GATE_COL0 + j * n_e + place_sub, 1.0, 0.0).astype(BF16)
        table = table + jnp.dot(piece, place, preferred_element_type=F32)
    return table.astype(BF16)


def _compact_kernel(starts_ref, pos_ref, gate_ref, idx_ref, gs_ref, r_scr, *, cap):
    b = pl.program_id(0)
    n_e = pos_ref.shape[1]
    n_k = pos_ref.shape[2]
    n_blocks = cap // SLOT_BLOCK
    wide_windows = -(-(TOKEN_TILE + SUBLANES) // COMPACT_WIN)
    r_scr[...] = jnp.zeros_like(r_scr)
    sub = lax.broadcasted_iota(jnp.int32, (COMPACT_WIN, TOKEN_TILE), 0)

    def tile(k, carry):
        rows = pl.ds(pl.multiple_of(k * TOKEN_TILE, TOKEN_TILE), TOKEN_TILE)
        vk = _token_table(gate_ref[0, rows, :], k * TOKEN_TILE)
        starts = []
        fits = None
        for e in range(n_e):
            base = (b * n_e + e) * LANES
            ws = (starts_ref[base + k] // SUBLANES) * SUBLANES
            ok = starts_ref[base + k + 1] - ws <= COMPACT_WIN
            fits = ok if fits is None else fits & ok
            starts.append(ws)

        def place(e, ws):
            hit = pos_ref[0, e, k] - ws == sub
            vals = jnp.dot(jnp.where(hit, 1.0, 0.0).astype(BF16), vk, preferred_element_type=F32)
            r_scr[e, pl.ds(pl.multiple_of(ws, SUBLANES), COMPACT_WIN), :] += vals

        @pl.when(fits)
        def _():
            for e in range(n_e):
                place(e, starts[e])

        @pl.when(jnp.logical_not(fits))
        def _():
            for e in range(n_e):
                for j in range(wide_windows):
                    place(e, starts[e] + j * COMPACT_WIN)

        return carry

    lax.fori_loop(0, n_k, tile, 0)

    lane = lax.broadcasted_iota(jnp.int32, (SUBLANES, LANES), 1)
    sub8 = lax.broadcasted_iota(jnp.int32, (SUBLANES, LANES), 0)
    id_rows = jnp.where(sub8 == 0, jnp.where(lane == 0, float(TOKEN_ID_BASE),
                                             jnp.where(lane == 1, 1.0, 0.0)), 0.0)

    def finish(e, carry):
        gate_lane = (lane >= GATE_COL0) & (lane < GATE_COL0 + 3 * n_e) & ((lane - GATE_COL0) % n_e == e)
        selector = (id_rows + jnp.where((sub8 == 1) & gate_lane, 1.0, 0.0)).astype(BF16)
        for m in range(n_blocks):
            blk = r_scr[e, pl.ds(m * SLOT_BLOCK, SLOT_BLOCK), :].astype(BF16)
            rows = lax.dot_general(selector, blk, (((1,), (1,)), ((), ())),
                                   preferred_element_type=F32)
            idx_ref[0, e, m] = rows[0:1, :].astype(jnp.int32)
            gs_ref[0, e, m] = rows[1:2, :]
        return carry

    lax.fori_loop(0, n_e, finish, 0)


def _compact(starts_flat, pos5, gate_c, cap):
    bsz, n_e, n_k = pos5.shape[:3]
    s = gate_c.shape[1]
    assert s <= TOKEN_ID_BASE * 256 and GATE_COL0 + 3 * n_e <= LANES
    n_blocks = cap // SLOT_BLOCK
    out_block = (1, n_e, n_blocks, 1, SLOT_BLOCK)
    out_spec = pl.BlockSpec(out_block, lambda b, st: (b, 0, 0, 0, 0))
    wide_rows = -(-(TOKEN_TILE + SUBLANES) // COMPACT_WIN) * COMPACT_WIN
    grid_spec = pltpu.PrefetchScalarGridSpec(
        num_scalar_prefetch=1,
        grid=(bsz,),
        in_specs=[pl.BlockSpec((1, n_e, n_k, 1, TOKEN_TILE), lambda b, st: (b, 0, 0, 0, 0)),
                  pl.BlockSpec((1, s, n_e), lambda b, st: (b, 0, 0))],
        out_specs=[out_spec, out_spec],
        scratch_shapes=[pltpu.VMEM((n_e, cap + wide_rows, LANES), F32)],
    )
    return pl.pallas_call(
        functools.partial(_compact_kernel, cap=cap),
        grid_spec=grid_spec,
        out_shape=[jax.ShapeDtypeStruct((bsz,) + out_block[1:], jnp.int32),
                   jax.ShapeDtypeStruct((bsz,) + out_block[1:], F32)],
        compiler_params=_cparams(("parallel",)),
        name="compact",
    )(starts_flat, pos5, gate_c)


def _moe_kernel(idx_ref, idx_next_ref, gs_ref, h2_hbm, wg_ref, wu_ref, wd_ref, eo_ref,
                xbuf, sem, xb_scr, acc_scr, *, seq_len, n_ff):
    b = pl.program_id(0)
    e = pl.program_id(1)
    f = pl.program_id(2)
    n_e = pl.num_programs(1)
    n_blocks = xb_scr.shape[0]
    cap = n_blocks * SLOT_BLOCK
    d = xb_scr.shape[2]
    step = b * n_e + e
    n_steps = pl.num_programs(0) * n_e
    cur = step % 2

    def token_copy(idx_smem, batch, buf, p):
        tok = idx_smem[0, 0, p]
        src = h2_hbm.at[pl.ds(pl.multiple_of((batch * seq_len + tok) * SUBLANES, SUBLANES), SUBLANES)]
        dst = xbuf.at[buf, pl.ds(pl.multiple_of(p * SUBLANES, SUBLANES), SUBLANES)]
        return pltpu.make_async_copy(src, dst, sem.at[buf])

    def request(idx_smem, batch, buf, first, count):
        def body(i, carry):
            for j in range(GATHER_UNROLL):
                token_copy(idx_smem, batch, buf, first + i * GATHER_UNROLL + j).start()
            return carry
        lax.fori_loop(0, count // GATHER_UNROLL, body, 0)

    @pl.when((step == 0) & (f == 0))
    def _():
        request(idx_ref, b, cur, 0, cap)

    next_batch = jnp.where(step + 1 < n_steps, step + 1, 0) // n_e
    per_block = cap // (n_ff * n_blocks)

    @pl.when(f == 0)
    def _():
        pltpu.make_async_copy(h2_hbm.at[pl.ds(0, cap * SUBLANES)], xbuf.at[cur], sem.at[cur]).wait()
        for m in range(n_blocks):
            first = m * SLOT_BLOCK * SUBLANES
            xm = jnp.concatenate(
                [xbuf[cur, pl.ds(first + j, SLOT_BLOCK, stride=SUBLANES), :] for j in range(d // LANES)],
                axis=1)
            xb_scr[m] = xm.astype(BF16)
        acc_scr[...] = jnp.zeros_like(acc_scr)

    wg = wg_ref[0].astype(BF16)
    wu = wu_ref[0].astype(BF16)
    wd = wd_ref[0].astype(BF16)
    for m in range(n_blocks):
        first = (f * n_blocks + m) * per_block
        for j in range(per_block):
            token_copy(idx_next_ref, next_batch, 1 - cur, first + j).start()
        xm = xb_scr[m]
        g = jnp.dot(xm, wg, preferred_element_type=F32)
        u = jnp.dot(xm, wu, preferred_element_type=F32)
        act = (jax.nn.silu(g) * u).astype(BF16)
        acc_scr[m] += jnp.dot(act, wd, preferred_element_type=F32)

    @pl.when((step == n_steps - 1) & (f == n_ff - 1))
    def _():
        pltpu.make_async_copy(h2_hbm.at[pl.ds(0, cap * SUBLANES)], xbuf.at[1 - cur],
                              sem.at[1 - cur]).wait()

    @pl.when(f == n_ff - 1)
    def _():
        eye = (lax.broadcasted_iota(jnp.int32, (SLOT_BLOCK, SLOT_BLOCK), 0)
               == lax.broadcasted_iota(jnp.int32, (SLOT_BLOCK, SLOT_BLOCK), 1))
        for m in range(n_blocks):
            gcol = jnp.sum(jnp.where(eye, gs_ref[0, m], 0.0), axis=1, keepdims=True)
            rows = pl.ds(m * SLOT_BLOCK, SLOT_BLOCK)
            eo_ref[0, 0, rows, :] = (gcol * acc_scr[m]).astype(eo_ref.dtype)


def _moe(idx, gs, h2_tiles, wg, wu, wd, bsz, s):
    n_e, d, d_ff = wg.shape
    n_blocks = gs.shape[1]
    cap = n_blocks * SLOT_BLOCK
    n_ff = d_ff // FF_CHUNK
    assert d_ff % FF_CHUNK == 0 and cap % (n_ff * n_blocks) == 0 and cap % GATHER_UNROLL == 0
    assert d == SUBLANES * LANES

    def next_step(b, e, f):
        return ((b * n_e + e + 1) % (bsz * n_e), 0, 0)

    smem = pltpu.SMEM
    return pl.pallas_call(
        functools.partial(_moe_kernel, seq_len=s, n_ff=n_ff),
        grid=(bsz, n_e, n_ff),
        in_specs=[
            pl.BlockSpec((1, 1, cap), lambda b, e, f: (b * n_e + e, 0, 0), memory_space=smem),
            pl.BlockSpec((1, 1, cap), next_step, memory_space=smem),
            pl.BlockSpec((1, n_blocks, 1, SLOT_BLOCK), lambda b, e, f: (b * n_e + e, 0, 0, 0)),
            pl.BlockSpec(memory_space=pl.ANY),
            pl.BlockSpec((1, d, FF_CHUNK), lambda b, e, f: (e, 0, f)),
            pl.BlockSpec((1, d, FF_CHUNK), lambda b, e, f: (e, 0, f)),
            pl.BlockSpec((1, FF_CHUNK, d), lambda b, e, f: (e, f, 0)),
        ],
        out_specs=pl.BlockSpec((1, 1, cap, d), lambda b, e, f: (b, e, 0, 0)),
        out_shape=jax.ShapeDtypeStruct((bsz, n_e, cap, d), BF16),
        scratch_shapes=[pltpu.VMEM((2, cap * SUBLANES, LANES), F32),
                        pltpu.SemaphoreType.DMA((2,)),
                        pltpu.VMEM((n_blocks, SLOT_BLOCK, d), BF16),
                        pltpu.VMEM((n_blocks, SLOT_BLOCK, d), F32)],
        compiler_params=_cparams(("arbitrary", "arbitrary", "arbitrary")),
        name="moe",
    )(idx, idx, gs, h2_tiles, wg, wu, wd)


def _combine_kernel(starts_ref, x1_ref, pos_ref, gf_ref, eo_ref, out_ref, wcat_scr, acc_scr,
                    *, cap):
    n_tiles = x1_ref.shape[1] // TOKEN_TILE

    def tile(i, carry):
        _combine_tile(starts_ref, x1_ref, pos_ref, gf_ref, eo_ref, out_ref, wcat_scr, acc_scr,
                      pl.program_id(1) * n_tiles + i,
                      pl.ds(pl.multiple_of(i * TOKEN_TILE, TOKEN_TILE), TOKEN_TILE), cap)
        return carry

    lax.fori_loop(0, n_tiles, tile, 0)


def _combine_tile(starts_ref, x1_ref, pos_ref, gf_ref, eo_ref, out_ref, wcat_scr, acc_scr,
                  k, rows, cap):
    b = pl.program_id(0)
    n_e = pos_ref.shape[2]
    pos = pos_ref[0, rows, :]
    expert_lane = lax.broadcasted_iota(jnp.int32, (1, n_e), 1)

    s0 = [starts_ref[(b * n_e + e) * LANES + k] for e in range(n_e)]
    s1 = [starts_ref[(b * n_e + e) * LANES + k + 1] for e in range(n_e)]

    def window_starts(rows):
        starts = [jnp.minimum(s // WIN_ALIGN, (cap - rows) // WIN_ALIGN) * WIN_ALIGN for s in s0]
        vec = jnp.zeros((1, n_e), jnp.int32)
        for e in range(n_e):
            vec = jnp.where(expert_lane == e, starts[e], vec)
        return starts, vec

    def onehot(rel, first, count, width):
        n_lanes = count * width
        lane = lax.broadcasted_iota(jnp.int32, (n_e, n_lanes), 1)
        sub = lax.broadcasted_iota(jnp.int32, (n_e, n_lanes), 0)
        expand = jnp.where(sub == first + lane // width, 1.0, 0.0).astype(BF16)
        spread = jnp.dot(jnp.clip(rel, -1, width).astype(F32).astype(BF16), expand,
                         preferred_element_type=F32)
        want = (lax.broadcasted_iota(jnp.int32, (TOKEN_TILE, n_lanes), 1) % width).astype(F32)
        return jnp.where(spread == want, 1.0, 0.0).astype(BF16)

    def finish(acc):
        out_ref[0, rows, :] = _rms_scale(acc) * gf_ref[...]

    narrow, narrow_vec = window_starts(NARROW_WIN)
    fits = None
    for e in range(n_e):
        ok = s1[e] - narrow[e] <= NARROW_WIN
        fits = ok if fits is None else fits & ok

    @pl.when(fits)
    def _():
        for e in range(n_e):
            src = pl.ds(pl.multiple_of(narrow[e], WIN_ALIGN), NARROW_WIN)
            wcat_scr[pl.ds(e * NARROW_WIN, NARROW_WIN), :] = eo_ref[0, e, src, :]
        hit = onehot(pos - narrow_vec, 0, n_e, NARROW_WIN)
        finish(x1_ref[0, rows, :] + jnp.dot(hit, wcat_scr[pl.ds(0, n_e * NARROW_WIN), :],
                                            preferred_element_type=F32))

    @pl.when(jnp.logical_not(fits))
    def _():
        wide, wide_vec = window_starts(WIN_ROWS)
        rel = pos - wide_vec
        group = wcat_scr.shape[0] // SLOT_BLOCK
        acc_scr[...] = x1_ref[0, rows, :]
        for g in range(n_e // group):
            for i in range(group):
                e = g * group + i
                src = pl.ds(pl.multiple_of(wide[e], WIN_ALIGN), SLOT_BLOCK)
                wcat_scr[pl.ds(i * SLOT_BLOCK, SLOT_BLOCK), :] = eo_ref[0, e, src, :]
            acc_scr[...] += jnp.dot(onehot(rel, g * group, group, SLOT_BLOCK), wcat_scr[...],
                                    preferred_element_type=F32)
        for e in range(n_e):
            src = pl.ds(pl.multiple_of(wide[e] + SLOT_BLOCK, WIN_ALIGN), WIN_ALIGN)
            wcat_scr[pl.ds(e * WIN_ALIGN, WIN_ALIGN), :] = eo_ref[0, e, src, :]
        tail = onehot(rel - SLOT_BLOCK, 0, n_e, WIN_ALIGN)
        finish(acc_scr[...] + jnp.dot(tail, wcat_scr[pl.ds(0, n_e * WIN_ALIGN), :],
                                      preferred_element_type=F32))


def _combine(starts_flat, x1, pos_c, eo, gf):
    bsz, s, d = x1.shape
    n_e = pos_c.shape[-1]
    cap = eo.shape[2]
    step_rows = COMBINE_TILES * TOKEN_TILE
    wide_group = 4
    assert n_e % wide_group == 0 and n_e * NARROW_WIN <= wide_group * SLOT_BLOCK
    assert s % step_rows == 0
    grid_spec = pltpu.PrefetchScalarGridSpec(
        num_scalar_prefetch=1,
        grid=(bsz, s // step_rows),
        in_specs=[
            pl.BlockSpec((1, step_rows, d), lambda b, k, st: (b, k, 0)),
            pl.BlockSpec((1, step_rows, n_e), lambda b, k, st: (b, k, 0)),
            pl.BlockSpec((1, d), lambda b, k, st: (0, 0)),
            pl.BlockSpec((1, n_e, cap, d), lambda b, k, st: (b, 0, 0, 0),
                         pipeline_mode=pl.Buffered(1)),
        ],
        out_specs=pl.BlockSpec((1, step_rows, d), lambda b, k, st: (b, k, 0)),
        scratch_shapes=[pltpu.VMEM((wide_group * SLOT_BLOCK, d), BF16),
                        pltpu.VMEM((TOKEN_TILE, d), F32)],
    )
    return pl.pallas_call(
        functools.partial(_combine_kernel, cap=cap),
        grid_spec=grid_spec,
        out_shape=jax.ShapeDtypeStruct((bsz, s, d), F32),
        compiler_params=_cparams(("parallel", "parallel")),
        name="combine",
    )(starts_flat, x1, pos_c, gf, eo)


def _block_diag(w):
    h, hd, _ = w.shape
    eye = jnp.eye(h, dtype=w.dtype)
    return (eye[:, None, :, None] * w[:, :, None, :]).reshape(h * hd, h * hd)


def _lru_params(wa_f, wx_f, wa_b, wx_b, ba_f, bx_f, ba_b, bx_b, lam_f, lam_b):
    d_lru = ba_f.shape[0]
    n_blk = d_lru // LANES
    mats = [0.5 * _block_diag(w) for w in (wa_f, wx_f, wa_b, wx_b)]
    ba_f, bx_f, ba_b, bx_b = (0.5 * v for v in (ba_f, bx_f, ba_b, bx_b))
    w_cat = jnp.stack([
        jnp.concatenate([m[c * LANES:(c + 1) * LANES, c * LANES:(c + 1) * LANES] for m in mats], axis=1)
        for c in range(n_blk)]).astype(BF16)
    b_cat = jnp.stack([
        jnp.concatenate([v[c * LANES:(c + 1) * LANES] for v in (ba_f, bx_f, ba_b, bx_b)])
        for c in range(n_blk)])[:, None, :]
    lam_cat = jnp.stack([
        jnp.concatenate([v[c * LANES:(c + 1) * LANES] for v in (lam_f, lam_b)])
        for c in range(n_blk)])[:, None, :]
    return w_cat, b_cat, lam_cat


def kernel(x, norm1_g, w_in, conv_w, conv_b, lru_wa_f, lru_ba_f, lru_wx_f, lru_bx_f, lru_lam_f,
           lru_wa_b, lru_ba_b, lru_wx_b, lru_bx_b, lru_lam_b, w_out, norm2_g, w_router,
           w_gate, w_up, w_down, normf_g):
    bsz, s, d = x.shape
    d_lru = conv_b.shape[0]
    d_four = w_in.shape[1] - 2 * d_lru
    n_e = w_router.shape[1]
    cap = CAPACITY_FACTOR * s // n_e
    assert s % LRU_CHUNK == 0 and s % (SCAN_GROUPS * SUBLANES * SUBLANES) == 0
    assert s % FOURIER_N2 == 0 and cap % SLOT_BLOCK == 0 and s // TOKEN_TILE < LANES
    assert cap >= WIN_ROWS and w_gate.shape[2] % FF_CHUNK == 0

    lx, lg, fo = _inproj(x.reshape(bsz * s, d), norm1_g[None, :], w_in.astype(BF16), d_lru, d_four)
    lx = lx.reshape(bsz, s, d_lru)
    lg = lg.reshape(bsz, s, d_lru)
    fo = fo.reshape(bsz, s, d_four)

    w_cat, b_cat, lam_cat = _lru_params(lru_wa_f, lru_wx_f, lru_wa_b, lru_wx_b,
                                        lru_ba_f, lru_bx_f, lru_ba_b, lru_bx_b,
                                        lru_lam_f, lru_lam_b)
    y_lru = _lru(lx, lg, conv_w, conv_b[None, :], w_cat, b_cat, lam_cat)
    y_four = _fourier(fo, FOURIER_GROUPS)

    w_out_bf = w_out.astype(BF16)
    x1, h2_tiles, aff_t = _outproj(x, y_lru, y_four, w_out_bf[:d_lru], w_out_bf[d_lru:],
                                   norm2_g[None, :], w_router.T)

    pos_r, starts = _select(aff_t, cap)
    starts_flat = starts.reshape(-1)
    pos_c = jnp.transpose(pos_r, (0, 2, 1))
    pos5 = pos_r.reshape(bsz, n_e, s // TOKEN_TILE, 1, TOKEN_TILE)
    idx, gs = _compact(starts_flat, pos5, jnp.transpose(aff_t, (0, 2, 1)), cap)
    eo = _moe(idx.reshape(bsz * n_e, 1, cap), gs.reshape((bsz * n_e,) + gs.shape[2:]), h2_tiles,
              w_gate, w_up, w_down, bsz, s)
    return _combine(starts_flat, x1, pos_c, eo, normf_g[None, :])
```

```python
import functools
import math

import jax
import jax.numpy as jnp
import numpy as np
from jax import lax
from jax.experimental import pallas as pl
from jax.experimental.pallas import tpu as pltpu

F32 = jnp.float32
BF16 = jnp.bfloat16

EPS = 1e-6
LRU_C = 8.0
CAPACITY_FACTOR = 2
FOURIER_GROUPS = 4

SUBLANES = 8
LANES = 128
MXU_DIM = 256
VMEM_LIMIT_BYTES = 56 * 1024 * 1024

ROW_TILE = 1024
LRU_CHUNK = 2048
SCAN_GROUPS = 4
SCAN_UNROLL = 4
SLOT_BLOCK = MXU_DIM
TOKEN_TILE = MXU_DIM
SEG_PAD = 4
FOURIER_N2 = LANES
FOURIER_K1_BATCH = 32
FOURIER_U_BATCH = 2
STAGE_PAD = 4
COPY_UNROLL = 4
FF_CHUNK = 1024
GATHER_UNROLL = 8
COMPACT_WIN = 64
TOKEN_ID_BASE = 64
GATE_COL0 = 16
WIN_ALIGN = 16
WIN_ROWS = SLOT_BLOCK + WIN_ALIGN
NARROW_WIN = 64
COMBINE_TILES = 4


def _cparams(semantics):
    return pltpu.CompilerParams(dimension_semantics=semantics,
                                vmem_limit_bytes=VMEM_LIMIT_BYTES)


def _rms_scale(x):
    return x * lax.rsqrt(jnp.mean(x * x, axis=-1, keepdims=True) + EPS)


def _inproj_kernel(x_ref, g_ref, w_ref, lx_ref, lg_ref, fo_ref):
    h = _rms_scale(x_ref[...]) * g_ref[...]
    p = jnp.dot(h.astype(BF16), w_ref[...], preferred_element_type=F32)
    d = lx_ref.shape[-1]
    lx_ref[...] = p[:, :d]
    lg_ref[...] = p[:, d:2 * d]
    fo_ref[...] = p[:, 2 * d:]


def _inproj(x2, g, w_bf, d_lru, d_four):
    m, d = x2.shape
    n = w_bf.shape[1]
    return pl.pallas_call(
        _inproj_kernel,
        grid=(m // ROW_TILE,),
        in_specs=[
            pl.BlockSpec((ROW_TILE, d), lambda i: (i, 0)),
            pl.BlockSpec((1, d), lambda i: (0, 0)),
            pl.BlockSpec((d, n), lambda i: (0, 0)),
        ],
        out_specs=[
            pl.BlockSpec((ROW_TILE, d_lru), lambda i: (i, 0)),
            pl.BlockSpec((ROW_TILE, d_lru), lambda i: (i, 0)),
            pl.BlockSpec((ROW_TILE, d_four), lambda i: (i, 0)),
        ],
        out_shape=[
            jax.ShapeDtypeStruct((m, d_lru), F32),
            jax.ShapeDtypeStruct((m, d_lru), F32),
            jax.ShapeDtypeStruct((m, d_four), F32),
        ],
        compiler_params=_cparams(("parallel",)),
        name="inproj",
    )(x2, g, w_bf)


def _shift_rows(x, shift):
    n = x.shape[0]
    rows = lax.broadcasted_iota(jnp.int32, x.shape, 0)
    rolled = pltpu.roll(x, shift % n, axis=0)
    keep = (rows >= shift) if shift > 0 else (rows < n + shift)
    return jnp.where(keep, rolled, 0.0)


def _lru_kernel(lx_ref, lg_ref, cw_ref, cb_ref, w_ref, b_ref, lam_ref, y_ref,
                xpad, af, bf, ab, bb):
    s = lx_ref.shape[1]
    c_blk = lx_ref.shape[2]
    n_seg = SCAN_GROUPS * SUBLANES
    seg = s // n_seg
    seg_stride = seg + SEG_PAD
    piece = min(seg, LRU_CHUNK)
    n_chunks = s // LRU_CHUNK
    pad = SUBLANES

    zeros_pad = jnp.zeros((pad, c_blk), F32)
    xpad[pl.ds(0, pad), :] = zeros_pad
    xpad[pl.ds(pad + s, pad), :] = zeros_pad

    def copy_chunk(i, carry):
        t0 = pl.multiple_of(i * LRU_CHUNK, LRU_CHUNK)
        xpad[pl.ds(pad + t0, LRU_CHUNK), :] = lx_ref[0, pl.ds(t0, LRU_CHUNK), :]
        return carry

    lax.fori_loop(0, n_chunks, copy_chunk, 0)

    cw = cw_ref[...]
    conv_width = cw.shape[0]
    cb = cb_ref[...]
    bias = b_ref[0]
    quarter_c_log_sig = (0.25 * LRU_C) * jax.nn.log_sigmoid(lam_ref[0])
    w_cat = w_ref[0]

    def gates_chunk(i, carry):
        t0 = pl.multiple_of(i * LRU_CHUNK, LRU_CHUNK)
        c = cb
        left = conv_width // 2
        for k in range(conv_width):
            c = c + cw[k:k + 1, :] * xpad[pl.ds(t0 + (pad - left + k), LRU_CHUNK), :]
        z = jnp.dot(c.astype(BF16), w_cat, preferred_element_type=F32) + bias
        for d, (a_scr, b_scr) in enumerate(((af, bf), (ab, bb))):
            q = quarter_c_log_sig[:, d * c_blk:(d + 1) * c_blk]
            th_r = jnp.tanh(z[:, (2 * d) * c_blk:(2 * d + 1) * c_blk])
            th_i = jnp.tanh(z[:, (2 * d + 1) * c_blk:(2 * d + 2) * c_blk])
            t = jnp.tanh(q + q * th_r)
            inv = 1.0 / (1.0 - t)
            a = (1.0 + t) * inv
            bt = (jnp.sqrt(-t) * inv) * (c + c * th_i)
            for p in range(LRU_CHUNK // piece):
                t = t0 + p * piece
                dst = pl.ds((t // seg) * seg_stride + t % seg, piece)
                a_scr[dst, :] = a[p * piece:(p + 1) * piece]
                b_scr[dst, :] = bt[p * piece:(p + 1) * piece]
        return carry

    lax.fori_loop(0, n_chunks, gates_chunk, 0)

    def scan_step(i, carry):
        def rows(o, q):
            return pl.ds(q * SUBLANES * seg_stride + o, SUBLANES, stride=seg_stride)

        offsets = [(i * SCAN_UNROLL + k, seg - 1 - (i * SCAN_UNROLL + k)) for k in range(SCAN_UNROLL)]
        scratch = ((af, bf), (ab, bb))
        loaded = [[[(scratch[d][0][rows(o[d], q), :], scratch[d][1][rows(o[d], q), :])
                    for q in range(SCAN_GROUPS)] for d in range(2)] for o in offsets]
        state = [[list(hs), list(ps)] for hs, ps in carry]
        results = []
        for k, o in enumerate(offsets):
            for d in range(2):
                for q in range(SCAN_GROUPS):
                    a, b = loaded[k][d][q]
                    state[d][0][q] = a * state[d][0][q] + b
                    state[d][1][q] = a * state[d][1][q]
                    results.append((d, o[d], q, state[d][0][q], state[d][1][q]))
        for d, o, q, h, p in results:
            scratch[d][1][rows(o, q), :] = h
            scratch[d][0][rows(o, q), :] = p
        return tuple((tuple(hs), tuple(ps)) for hs, ps in state)

    zero = jnp.zeros((SUBLANES, c_blk), F32)
    one = jnp.ones((SUBLANES, c_blk), F32)
    init = ((zero,) * SCAN_GROUPS, (one,) * SCAN_GROUPS)
    (hf_end, pf_end), (hb_end, pb_end) = lax.fori_loop(0, seg // SCAN_UNROLL, scan_step, (init, init))

    row = lax.broadcasted_iota(jnp.int32, (SUBLANES, c_blk), 0)

    def entering(h_end, p_end, carry_in, first_row, shift):
        c = zero
        for _ in range(SUBLANES):
            c = jnp.where(row == first_row, carry_in, _shift_rows(h_end + p_end * c, shift))
        return c

    cf = [None] * SCAN_GROUPS
    cbk = [None] * SCAN_GROUPS
    carry_f = jnp.zeros((1, c_blk), F32)
    carry_b = jnp.zeros((1, c_blk), F32)
    for q in range(SCAN_GROUPS):
        cf[q] = entering(hf_end[q], pf_end[q], carry_f, 0, 1)
        carry_f = (hf_end[q] + pf_end[q] * cf[q])[SUBLANES - 1:SUBLANES, :]
        qb = SCAN_GROUPS - 1 - q
        cbk[qb] = entering(hb_end[qb], pb_end[qb], carry_b, SUBLANES - 1, -1)
        carry_b = (hb_end[qb] + pb_end[qb] * cbk[qb])[0:1, :]

    for j in range(n_seg):
        q, i = divmod(j, SUBLANES)
        src = pl.ds(j * seg_stride, seg)
        rows = pl.ds(j * seg, seg)
        hsum = ((bf[src, :] + af[src, :] * cf[q][i:i + 1, :])
                + (bb[src, :] + ab[src, :] * cbk[q][i:i + 1, :]))
        y_ref[0, rows, :] = (jax.nn.gelu(lg_ref[0, rows, :]) * hsum).astype(y_ref.dtype)


def _lru(lx, lg, conv_w, conv_b, w_cat, b_cat, lam_cat):
    bsz, s, d_lru = lx.shape
    c_blk = LANES
    n_blk = d_lru // c_blk
    scr = pltpu.VMEM((s + SCAN_GROUPS * SUBLANES * SEG_PAD, c_blk), F32)
    return pl.pallas_call(
        _lru_kernel,
        grid=(bsz, n_blk),
        in_specs=[
            pl.BlockSpec((1, s, c_blk), lambda b, c: (b, 0, c)),
            pl.BlockSpec((1, s, c_blk), lambda b, c: (b, 0, c)),
            pl.BlockSpec((4, c_blk), lambda b, c: (0, c)),
            pl.BlockSpec((1, c_blk), lambda b, c: (0, c)),
            pl.BlockSpec((1, c_blk, 4 * c_blk), lambda b, c: (c, 0, 0)),
            pl.BlockSpec((1, 1, 4 * c_blk), lambda b, c: (c, 0, 0)),
            pl.BlockSpec((1, 1, 2 * c_blk), lambda b, c: (c, 0, 0)),
        ],
        out_specs=pl.BlockSpec((1, s, c_blk), lambda b, c: (b, 0, c)),
        out_shape=jax.ShapeDtypeStruct((bsz, s, d_lru), BF16),
        scratch_shapes=[pltpu.VMEM((s + 2 * SUBLANES, c_blk), F32), scr, scr, scr, scr],
        compiler_params=_cparams(("parallel", "parallel")),
        name="lru",
    )(lx, lg, conv_w, conv_b, w_cat, b_cat, lam_cat)


def _fourier_kernel(f_ref, ka_ref, kb_ref, cc_ref, sc_ref, twc_ref, tws_ref, y_ref,
                    ar_scr, ai_scr, y_scr, *, scale):
    n1 = f_ref.shape[1]
    n_u = f_ref.shape[2]
    cols = f_ref.shape[4]
    r = SUBLANES * n1
    n2 = n_u * SUBLANES
    stage_stride = n1 + STAGE_PAD

    def stage_a(i, carry):
        us = tuple(FOURIER_U_BATCH * i + j for j in range(FOURIER_U_BATCH))
        xu = jnp.concatenate([f_ref[0, :, u].reshape(r, cols) for u in us], axis=1)
        a = jnp.dot(ka_ref[...], xu.astype(BF16), preferred_element_type=F32)
        for j, u in enumerate(us):
            a_re = a[:r, j * cols:(j + 1) * cols]
            a_im = a[r:, j * cols:(j + 1) * cols]
            c = twc_ref[u]
            sn = tws_ref[u]
            ar_scr[u] = a_re * c - a_im * sn
            ai_scr[u] = a_re * sn + a_im * c
        return carry

    lax.fori_loop(0, n_u // FOURIER_U_BATCH, stage_a, 0)

    nb = FOURIER_K1_BATCH

    def stage_b(i, carry):
        rows = pl.ds(pl.multiple_of(i * (nb * SUBLANES), nb * SUBLANES), nb * SUBLANES)
        a_re = ar_scr[:, rows, :]
        a_im = ai_scr[:, rows, :]

        def rows_s2(a, j):
            return a[:, j * SUBLANES:(j + 1) * SUBLANES, :].reshape(n2, cols)

        st = jnp.concatenate(
            [jnp.concatenate([rows_s2(a_re, j), rows_s2(a_im, j)], axis=0) for j in range(nb)],
            axis=1).astype(BF16)
        x = jnp.dot(kb_ref[...], st, preferred_element_type=F32)
        x_re = jnp.concatenate([x[:n2, j * cols:(j + 1) * cols] for j in range(nb)], axis=0)
        x_im = jnp.concatenate([x[n2:, j * cols:(j + 1) * cols] for j in range(nb)], axis=0)
        y = (jnp.dot(x_re.astype(BF16), cc_ref[...], preferred_element_type=F32)
             + jnp.dot(x_im.astype(BF16), sc_ref[...], preferred_element_type=F32)) * scale
        for j in range(nb):
            y_scr[pl.ds(i * nb + j, n2, stride=stage_stride), :] = y[j * n2:(j + 1) * n2]
        return carry

    lax.fori_loop(0, n1 // nb, stage_b, 0)

    def copy_out(i, carry):
        blocks = [y_scr[pl.ds((i * COPY_UNROLL + j) * stage_stride, n1), :] for j in range(COPY_UNROLL)]
        for j in range(COPY_UNROLL):
            k2 = i * COPY_UNROLL + j
            y_ref[0, pl.ds(pl.multiple_of(k2 * n1, n1), n1), :] = blocks[j]
        return carry

    lax.fori_loop(0, n2 // COPY_UNROLL, copy_out, 0)


def _dft_tables(s, c_grp):
    n2 = FOURIER_N2
    n1 = s // n2
    n_u = n2 // SUBLANES

    def cos_sin(num, den):
        ang = (2.0 * np.pi / den) * (num % den).astype(np.float64)
        return np.cos(ang), np.sin(ang)

    def const(a, dtype=F32):
        return jnp.asarray(a.astype(np.float32)).astype(dtype)

    i1 = np.arange(n1, dtype=np.int64)
    c1, s1 = cos_sin(i1[:, None] * i1[None, :], n1)
    eye8 = np.eye(SUBLANES)
    ka = const(np.concatenate([np.kron(c1, eye8), np.kron(-s1, eye8)], axis=0), BF16)

    i2 = np.arange(n2, dtype=np.int64)
    c2, s2 = cos_sin(i2[:, None] * i2[None, :], n2)
    kb = const(np.block([[c2, s2], [-s2, c2]]), BF16)

    ic = np.arange(c_grp, dtype=np.int64)
    cc, sc = cos_sin(ic[:, None] * ic[None, :], c_grp)

    u = np.arange(n_u, dtype=np.int64)[:, None, None]
    k1 = np.arange(n1, dtype=np.int64)[None, :, None]
    v = np.arange(SUBLANES, dtype=np.int64)[None, None, :]
    tc, ts = cos_sin(k1 * (SUBLANES * u + v), s)
    shape = (n_u, n1 * SUBLANES, c_grp)
    twc = jnp.broadcast_to(const(tc.reshape(n_u, -1, 1)), shape)
    tws = jnp.broadcast_to(const(-ts.reshape(n_u, -1, 1)), shape)
    return ka, kb, const(cc, BF16), const(sc, BF16), twc, tws


def _fourier(four, n_groups):
    bsz, s, d_four = four.shape
    c_grp = d_four // n_groups
    n2 = FOURIER_N2
    n1 = s // n2
    n_u = n2 // SUBLANES
    r = SUBLANES * n1
    ka, kb, cc, sc, twc, tws = _dft_tables(s, c_grp)
    f5 = four.reshape(bsz, n1, n_u, SUBLANES, d_four)
    scale = 1.0 / math.sqrt(s * c_grp)
    const2 = lambda b, g: (0, 0)
    const3 = lambda b, g: (0, 0, 0)
    return pl.pallas_call(
        functools.partial(_fourier_kernel, scale=scale),
        grid=(bsz, n_groups),
        in_specs=[
            pl.BlockSpec((1, n1, n_u, SUBLANES, c_grp), lambda b, g: (b, 0, 0, 0, g)),
            pl.BlockSpec(ka.shape, const2),
            pl.BlockSpec(kb.shape, const2),
            pl.BlockSpec(cc.shape, const2),
            pl.BlockSpec(sc.shape, const2),
            pl.BlockSpec(twc.shape, const3),
            pl.BlockSpec(tws.shape, const3),
        ],
        out_specs=pl.BlockSpec((1, s, c_grp), lambda b, g: (b, 0, g)),
        out_shape=jax.ShapeDtypeStruct((bsz, s, d_four), F32),
        scratch_shapes=[pltpu.VMEM((n_u, r, c_grp), F32), pltpu.VMEM((n_u, r, c_grp), F32),
                        pltpu.VMEM((n2 * (n1 + STAGE_PAD), c_grp), F32)],
        compiler_params=_cparams(("parallel", "parallel")),
        name="fourier",
    )(f5, ka, kb, cc, sc, twc, tws)


def _outproj_kernel(x_ref, yl_ref, yf_ref, wo1_ref, wo2_ref, g2_ref, wr_ref,
                    x1_ref, h2_ref, aff_ref):
    x1 = (x_ref[0]
          + jnp.dot(yl_ref[0], wo1_ref[...], preferred_element_type=F32)
          + jnp.dot(yf_ref[0].astype(BF16), wo2_ref[...], preferred_element_type=F32))
    x1_ref[0] = x1
    h2 = _rms_scale(x1) * g2_ref[...]
    for j in range(h2.shape[1] // LANES):
        h2_ref[pl.ds(j, h2.shape[0], stride=SUBLANES), :] = h2[:, j * LANES:(j + 1) * LANES]
    def split(v):
        hi = v.astype(BF16)
        return hi, (v - hi.astype(F32)).astype(BF16)

    def dot_nt(a, bm):
        return lax.dot_general(a, bm, (((1,), (1,)), ((), ())), preferred_element_type=F32)

    w_hi, w_lo = split(wr_ref[...])
    h_hi, h_lo = split(h2)
    logits = dot_nt(w_hi, h_hi) + (dot_nt(w_lo, h_hi) + dot_nt(w_hi, h_lo))
    ex = jnp.exp(logits - jnp.max(logits, axis=0, keepdims=True))
    aff_ref[0] = ex / jnp.sum(ex, axis=0, keepdims=True)


def _outproj(x, y_lru, y_four, wo1, wo2, g2, wr_t):
    bsz, s, d = x.shape
    d_lru = y_lru.shape[-1]
    d_four = y_four.shape[-1]
    n_e = wr_t.shape[0]
    assert d == SUBLANES * LANES
    n_i = s // ROW_TILE
    tile = lambda w: pl.BlockSpec((1, ROW_TILE, w), lambda b, i: (b, i, 0))
    const = lambda shape: pl.BlockSpec(shape, lambda b, i: (0, 0))
    return pl.pallas_call(
        _outproj_kernel,
        grid=(bsz, n_i),
        in_specs=[tile(d), tile(d_lru), tile(d_four), const(wo1.shape), const(wo2.shape),
                  const(g2.shape), const(wr_t.shape)],
        out_specs=[tile(d),
                   pl.BlockSpec((ROW_TILE * SUBLANES, LANES), lambda b, i: (b * n_i + i, 0)),
                   pl.BlockSpec((1, n_e, ROW_TILE), lambda b, i: (b, 0, i))],
        out_shape=[
            jax.ShapeDtypeStruct((bsz, s, d), F32),
            jax.ShapeDtypeStruct((bsz * s * SUBLANES, LANES), F32),
            jax.ShapeDtypeStruct((bsz, n_e, s), F32),
        ],
        compiler_params=_cparams(("parallel", "parallel")),
        name="outproj",
    )(x, y_lru, y_four, wo1, wo2, g2, wr_t)


def _select_kernel(aff_ref, tri_ref, pos_ref, starts_ref, *, cap):
    v = aff_ref[0]
    n_e, s = v.shape
    cap_f = float(cap)

    def midpoint(lo, hi):
        mid = 0.5 * (lo + hi)
        return mid, (mid > lo) & (mid < hi)

    def cond(carry):
        _, active = midpoint(*carry)
        return jnp.max(active.astype(F32)) > 0.0

    def body(carry):
        lo, hi = carry
        mid, active = midpoint(lo, hi)
        cnt = jnp.sum((v >= mid).astype(F32), axis=1, keepdims=True)
        enough = cnt >= cap_f
        return (jnp.where(active & enough, mid, lo), jnp.where(active & (~enough), mid, hi))

    lo0 = jnp.zeros((n_e, 1), F32)
    hi0 = jnp.full((n_e, 1), 2.0, F32)
    thr, _ = lax.while_loop(cond, body, (lo0, hi0))

    above = v > thr
    tie = v == thr
    need = cap_f - jnp.sum(above.astype(F32), axis=1, keepdims=True)

    n_tiles = s // TOKEN_TILE
    tri = tri_ref[...]
    lane = lax.broadcasted_iota(jnp.int32, (n_e, LANES), 1)

    def prefix(mask_f, want_starts):
        run = jnp.zeros((n_e, 1), F32)
        starts = jnp.zeros((n_e, LANES), F32)
        pieces = []
        for t in range(n_tiles):
            m = mask_f[:, t * TOKEN_TILE:(t + 1) * TOKEN_TILE]
            incl = jnp.dot(m.astype(BF16), tri, preferred_element_type=F32)
            pieces.append(run + incl - m)
            if want_starts:
                starts = jnp.where(lane == t, run, starts)
            run = run + incl[:, TOKEN_TILE - 1:TOKEN_TILE]
        if want_starts:
            starts = jnp.where(lane == n_tiles, run, starts)
        return jnp.concatenate(pieces, axis=1), starts

    tie_rank, _ = prefix(tie.astype(F32), False)
    sel = above | (tie & (tie_rank < need))
    pos, starts = prefix(sel.astype(F32), True)
    pos_ref[0] = jnp.where(sel, pos, -1.0).astype(jnp.int32)
    starts_ref[0] = starts.astype(jnp.int32)


def _select(aff_t, cap):
    bsz, n_e, s = aff_t.shape
    idx = jnp.arange(TOKEN_TILE, dtype=jnp.int32)
    tri = (idx[:, None] <= idx[None, :]).astype(BF16)
    return pl.pallas_call(
        functools.partial(_select_kernel, cap=cap),
        grid=(bsz,),
        in_specs=[
            pl.BlockSpec((1, n_e, s), lambda b: (b, 0, 0)),
            pl.BlockSpec(tri.shape, lambda b: (0, 0)),
        ],
        out_specs=[
            pl.BlockSpec((1, n_e, s), lambda b: (b, 0, 0)),
            pl.BlockSpec((1, n_e, LANES), lambda b: (b, 0, 0)),
        ],
        out_shape=[
            jax.ShapeDtypeStruct((bsz, n_e, s), jnp.int32),
            jax.ShapeDtypeStruct((bsz, n_e, LANES), jnp.int32),
        ],
        compiler_params=_cparams(("parallel",)),
        name="select",
    )(aff_t, tri)


def _token_table(gate, first_token):
    n_tok, n_e = gate.shape
    lane = lax.broadcasted_iota(jnp.int32, (n_tok, LANES), 1)
    tok = first_token + lax.broadcasted_iota(jnp.int32, (n_tok, LANES), 0)
    table = jnp.where(lane == 0, tok // TOKEN_ID_BASE, jnp.where(lane == 1, tok % TOKEN_ID_BASE, 0))
    table = table.astype(F32)
    place_lane = lax.broadcasted_iota(jnp.int32, (n_e, LANES), 1)
    place_sub = lax.broadcasted_iota(jnp.int32, (n_e, LANES), 0)
    rest = gate
    for j in range(3):
        piece = rest.astype(BF16)
        rest = rest - piece.astype(F32)
        place = jnp.where(place_lane == GATE_COL0 + j * n_e + place_sub, 1.0, 0.0).astype(BF16)
        table = table + jnp.dot(piece, place, preferred_element_type=F32)
    return table.astype(BF16)


def _compact_kernel(starts_ref, pos_ref, gate_ref, idx_ref, gs_ref, r_scr, *, cap):
    b = pl.program_id(0)
    n_e = pos_ref.shape[1]
    n_k = pos_ref.shape[2] // TOKEN_TILE
    n_blocks = cap // SLOT_BLOCK
    wide_windows = -(-(TOKEN_TILE + SUBLANES) // COMPACT_WIN)
    r_scr[...] = jnp.zeros_like(r_scr)
    sub = lax.broadcasted_iota(jnp.int32, (COMPACT_WIN, TOKEN_TILE), 0)

    def tile(k, carry):
        tokens = pl.ds(pl.multiple_of(k * TOKEN_TILE, TOKEN_TILE), TOKEN_TILE)
        vk = _token_table(gate_ref[0, tokens, :], k * TOKEN_TILE)
        starts = []
        fits = None
        for e in range(n_e):
            base = (b * n_e + e) * LANES
            ws = (starts_ref[base + k] // SUBLANES) * SUBLANES
            ok = starts_ref[base + k + 1] - ws <= COMPACT_WIN
            fits = ok if fits is None else fits & ok
            starts.append(ws)

        def place(e, ws):
            hit = pos_ref[0, pl.ds(e, 1), tokens] - ws == sub
            vals = jnp.dot(jnp.where(hit, 1.0, 0.0).astype(BF16), vk, preferred_element_type=F32)
            r_scr[e, pl.ds(pl.multiple_of(ws, SUBLANES), COMPACT_WIN), :] += vals

        @pl.when(fits)
        def _():
            for e in range(n_e):
                place(e, starts[e])

        @pl.when(jnp.logical_not(fits))
        def _():
            for e in range(n_e):
                for j in range(wide_windows):
                    place(e, starts[e] + j * COMPACT_WIN)

        return carry

    lax.fori_loop(0, n_k, tile, 0)

    lane = lax.broadcasted_iota(jnp.int32, (SUBLANES, LANES), 1)
    sub8 = lax.broadcasted_iota(jnp.int32, (SUBLANES, LANES), 0)
    id_rows = jnp.where(sub8 == 0, jnp.where(lane == 0, float(TOKEN_ID_BASE),
                                             jnp.where(lane == 1, 1.0, 0.0)), 0.0)

    def finish(e, carry):
        gate_lane = (lane >= GATE_COL0) & (lane < GATE_COL0 + 3 * n_e) & ((lane - GATE_COL0) % n_e == e)
        selector = (id_rows + jnp.where((sub8 == 1) & gate_lane, 1.0, 0.0)).astype(BF16)
        for m in range(n_blocks):
            blk = r_scr[e, pl.ds(m * SLOT_BLOCK, SLOT_BLOCK), :].astype(BF16)
            rows = lax.dot_general(selector, blk, (((1,), (1,)), ((), ())),
                                   preferred_element_type=F32)
            idx_ref[0, e, m] = rows[0:1, :].astype(jnp.int32)
            gs_ref[0, e, m] = rows[1:2, :]
        return carry

    lax.fori_loop(0, n_e, finish, 0)


def _compact(starts_flat, pos_r, gate_c, cap):
    bsz, n_e, s = pos_r.shape
    assert s <= TOKEN_ID_BASE * 256 and GATE_COL0 + 3 * n_e <= LANES
    n_blocks = cap // SLOT_BLOCK
    out_block = (1, n_e, n_blocks, 1, SLOT_BLOCK)
    out_spec = pl.BlockSpec(out_block, lambda b, st: (b, 0, 0, 0, 0))
    wide_rows = -(-(TOKEN_TILE + SUBLANES) // COMPACT_WIN) * COMPACT_WIN
    grid_spec = pltpu.PrefetchScalarGridSpec(
        num_scalar_prefetch=1,
        grid=(bsz,),
        in_specs=[pl.BlockSpec((1, n_e, s), lambda b, st: (b, 0, 0)),
                  pl.BlockSpec((1, s, n_e), lambda b, st: (b, 0, 0))],
        out_specs=[out_spec, out_spec],
        scratch_shapes=[pltpu.VMEM((n_e, cap + wide_rows, LANES), F32)],
    )
    return pl.pallas_call(
        functools.partial(_compact_kernel, cap=cap),
        grid_spec=grid_spec,
        out_shape=[jax.ShapeDtypeStruct((bsz,) + out_block[1:], jnp.int32),
                   jax.ShapeDtypeStruct((bsz,) + out_block[1:], F32)],
        compiler_params=_cparams(("parallel",)),
        name="compact",
    )(starts_flat, pos_r, gate_c)


def _moe_kernel(idx_ref, idx_next_ref, gs_ref, h2_hbm, wg_ref, wu_ref, wd_ref, eo_ref,
                xbuf, sem, xb_scr, acc_scr, *, seq_len, n_ff):
    b = pl.program_id(0)
    e = pl.program_id(1)
    f = pl.program_id(2)
    n_e = pl.num_programs(1)
    n_blocks = xb_scr.shape[0]
    cap = n_blocks * SLOT_BLOCK
    d = xb_scr.shape[2]
    step = b * n_e + e
    n_steps = pl.num_programs(0) * n_e
    cur = step % 2

    def token_copy(idx_smem, batch, buf, p):
        tok = idx_smem[0, 0, p]
        src = h2_hbm.at[pl.ds(pl.multiple_of((batch * seq_len + tok) * SUBLANES, SUBLANES), SUBLANES)]
        dst = xbuf.at[buf, pl.ds(pl.multiple_of(p * SUBLANES, SUBLANES), SUBLANES)]
        return pltpu.make_async_copy(src, dst, sem.at[buf])

    def request(idx_smem, batch, buf, first, count):
        def body(i, carry):
            for j in range(GATHER_UNROLL):
                token_copy(idx_smem, batch, buf, first + i * GATHER_UNROLL + j).start()
            return carry
        lax.fori_loop(0, count // GATHER_UNROLL, body, 0)

    @pl.when((step == 0) & (f == 0))
    def _():
        request(idx_ref, b, cur, 0, cap)

    next_batch = jnp.where(step + 1 < n_steps, step + 1, 0) // n_e
    per_block = cap // (n_ff * n_blocks)

    @pl.when(f == 0)
    def _():
        pltpu.make_async_copy(h2_hbm.at[pl.ds(0, cap * SUBLANES)], xbuf.at[cur], sem.at[cur]).wait()
        for m in range(n_blocks):
            first = m * SLOT_BLOCK * SUBLANES
            xm = jnp.concatenate(
                [xbuf[cur, pl.ds(first + j, SLOT_BLOCK, stride=SUBLANES), :] for j in range(d // LANES)],
                axis=1)
            xb_scr[m] = xm.astype(BF16)
        acc_scr[...] = jnp.zeros_like(acc_scr)

    wg = wg_ref[0].astype(BF16)
    wu = wu_ref[0].astype(BF16)
    wd = wd_ref[0].astype(BF16)
    for m in range(n_blocks):
        first = (f * n_blocks + m) * per_block
        for j in range(per_block):
            token_copy(idx_next_ref, next_batch, 1 - cur, first + j).start()
        xm = xb_scr[m]
        g = jnp.dot(xm, wg, preferred_element_type=F32)
        u = jnp.dot(xm, wu, preferred_element_type=F32)
        act = (jax.nn.silu(g) * u).astype(BF16)
        acc_scr[m] += jnp.dot(act, wd, preferred_element_type=F32)

    @pl.when((step == n_steps - 1) & (f == n_ff - 1))
    def _():
        pltpu.make_async_copy(h2_hbm.at[pl.ds(0, cap * SUBLANES)], xbuf.at[1 - cur],
                              sem.at[1 - cur]).wait()

    @pl.when(f == n_ff - 1)
    def _():
        eye = (lax.broadcasted_iota(jnp.int32, (SLOT_BLOCK, SLOT_BLOCK), 0)
               == lax.broadcasted_iota(jnp.int32, (SLOT_BLOCK, SLOT_BLOCK), 1))
        for m in range(n_blocks):
            gcol = jnp.sum(jnp.where(eye, gs_ref[0, m], 0.0), axis=1, keepdims=True)
            rows = pl.ds(m * SLOT_BLOCK, SLOT_BLOCK)
            eo_ref[0, 0, rows, :] = (gcol * acc_scr[m]).astype(eo_ref.dtype)


def _moe(idx, gs, h2_tiles, wg, wu, wd, bsz, s):
    n_e, d, d_ff = wg.shape
    n_blocks = gs.shape[1]
    cap = n_blocks * SLOT_BLOCK
    n_ff = d_ff // FF_CHUNK
    assert d_ff % FF_CHUNK == 0 and cap % (n_ff * n_blocks) == 0 and cap % GATHER_UNROLL == 0
    assert d == SUBLANES * LANES

    def next_step(b, e, f):
        return ((b * n_e + e + 1) % (bsz * n_e), 0, 0)

    smem = pltpu.SMEM
    return pl.pallas_call(
        functools.partial(_moe_kernel, seq_len=s, n_ff=n_ff),
        grid=(bsz, n_e, n_ff),
        in_specs=[
            pl.BlockSpec((1, 1, cap), lambda b, e, f: (b * n_e + e, 0, 0), memory_space=smem),
            pl.BlockSpec((1, 1, cap), next_step, memory_space=smem),
            pl.BlockSpec((1, n_blocks, 1, SLOT_BLOCK), lambda b, e, f: (b * n_e + e, 0, 0, 0)),
            pl.BlockSpec(memory_space=pl.ANY),
            pl.BlockSpec((1, d, FF_CHUNK), lambda b, e, f: (e, 0, f)),
            pl.BlockSpec((1, d, FF_CHUNK), lambda b, e, f: (e, 0, f)),
            pl.BlockSpec((1, FF_CHUNK, d), lambda b, e, f: (e, f, 0)),
        ],
        out_specs=pl.BlockSpec((1, 1, cap, d), lambda b, e, f: (b, e, 0, 0)),
        out_shape=jax.ShapeDtypeStruct((bsz, n_e, cap, d), BF16),
        scratch_shapes=[pltpu.VMEM((2, cap * SUBLANES, LANES), F32),
                        pltpu.SemaphoreType.DMA((2,)),
                        pltpu.VMEM((n_blocks, SLOT_BLOCK, d), BF16),
                        pltpu.VMEM((n_blocks, SLOT_BLOCK, d), F32)],
        compiler_params=_cparams(("arbitrary", "arbitrary", "arbitrary")),
        name="moe",
    )(idx, idx, gs, h2_tiles, wg, wu, wd)


def _combine_kernel(starts_ref, x1_ref, pos_ref, gf_ref, eo_ref, out_ref, wcat_scr, acc_scr,
                    *, cap):
    n_tiles = x1_ref.shape[1] // TOKEN_TILE

    def tile(i, carry):
        _combine_tile(starts_ref, x1_ref, pos_ref, gf_ref, eo_ref, out_ref, wcat_scr, acc_scr,
                      pl.program_id(1) * n_tiles + i,
                      pl.ds(pl.multiple_of(i * TOKEN_TILE, TOKEN_TILE), TOKEN_TILE), cap)
        return carry

    lax.fori_loop(0, n_tiles, tile, 0)


def _combine_tile(starts_ref, x1_ref, pos_ref, gf_ref, eo_ref, out_ref, wcat_scr, acc_scr,
                  k, rows, cap):
    b = pl.program_id(0)
    n_e = pos_ref.shape[2]
    pos = pos_ref[0, rows, :]
    expert_lane = lax.broadcasted_iota(jnp.int32, (1, n_e), 1)

    s0 = [starts_ref[(b * n_e + e) * LANES + k] for e in range(n_e)]
    s1 = [starts_ref[(b * n_e + e) * LANES + k + 1] for e in range(n_e)]

    def window_starts(rows):
        starts = [jnp.minimum(s // WIN_ALIGN, (cap - rows) // WIN_ALIGN) * WIN_ALIGN for s in s0]
        vec = jnp.zeros((1, n_e), jnp.int32)
        for e in range(n_e):
            vec = jnp.where(expert_lane == e, starts[e], vec)
        return starts, vec

    def onehot(rel, first, count, width):
        n_lanes = count * width
        lane = lax.broadcasted_iota(jnp.int32, (n_e, n_lanes), 1)
        sub = lax.broadcasted_iota(jnp.int32, (n_e, n_lanes), 0)
        expand = jnp.where(sub == first + lane // width, 1.0, 0.0).astype(BF16)
        spread = jnp.dot(jnp.clip(rel, -1, width).astype(F32).astype(BF16), expand,
                         preferred_element_type=F32)
        want = (lax.broadcasted_iota(jnp.int32, (TOKEN_TILE, n_lanes), 1) % width).astype(F32)
        return jnp.where(spread == want, 1.0, 0.0).astype(BF16)

    def finish(acc):
        out_ref[0, rows, :] = _rms_scale(acc) * gf_ref[...]

    narrow, narrow_vec = window_starts(NARROW_WIN)
    fits = None
    for e in range(n_e):
        ok = s1[e] - narrow[e] <= NARROW_WIN
        fits = ok if fits is None else fits & ok

    @pl.when(fits)
    def _():
        for e in range(n_e):
            src = pl.ds(pl.multiple_of(narrow[e], WIN_ALIGN), NARROW_WIN)
            wcat_scr[pl.ds(e * NARROW_WIN, NARROW_WIN), :] = eo_ref[0, e, src, :]
        hit = onehot(pos - narrow_vec, 0, n_e, NARROW_WIN)
        finish(x1_ref[0, rows, :] + jnp.dot(hit, wcat_scr[pl.ds(0, n_e * NARROW_WIN), :],
                                            preferred_element_type=F32))

    @pl.when(jnp.logical_not(fits))
    def _():
        wide, wide_vec = window_starts(WIN_ROWS)
        rel = pos - wide_vec
        group = wcat_scr.shape[0] // SLOT_BLOCK
        acc_scr[...] = x1_ref[0, rows, :]
        for g in range(n_e // group):
            for i in range(group):
                e = g * group + i
                src = pl.ds(pl.multiple_of(wide[e], WIN_ALIGN), SLOT_BLOCK)
                wcat_scr[pl.ds(i * SLOT_BLOCK, SLOT_BLOCK), :] = eo_ref[0, e, src, :]
            acc_scr[...] += jnp.dot(onehot(rel, g * group, group, SLOT_BLOCK), wcat_scr[...],
                                    preferred_element_type=F32)
        for e in range(n_e):
            src = pl.ds(pl.multiple_of(wide[e] + SLOT_BLOCK, WIN_ALIGN), WIN_ALIGN)
            wcat_scr[pl.ds(e * WIN_ALIGN, WIN_ALIGN), :] = eo_ref[0, e, src, :]
        tail = onehot(rel - SLOT_BLOCK, 0, n_e, WIN_ALIGN)
        finish(acc_scr[...] + jnp.dot(tail, wcat_scr[pl.ds(0, n_e * WIN_ALIGN), :],
                                      preferred_element_type=F32))


def _combine(starts_flat, x1, pos_c, eo, gf):
    bsz, s, d = x1.shape
    n_e = pos_c.shape[-1]
    cap = eo.shape[2]
    step_rows = COMBINE_TILES * TOKEN_TILE
    wide_group = 4
    assert n_e % wide_group == 0 and n_e * NARROW_WIN <= wide_group * SLOT_BLOCK
    assert s % step_rows == 0
    grid_spec = pltpu.PrefetchScalarGridSpec(
        num_scalar_prefetch=1,
        grid=(bsz, s // step_rows),
        in_specs=[
            pl.BlockSpec((1, step_rows, d), lambda b, k, st: (b, k, 0)),
            pl.BlockSpec((1, step_rows, n_e), lambda b, k, st: (b, k, 0)),
            pl.BlockSpec((1, d), lambda b, k, st: (0, 0)),
            pl.BlockSpec((1, n_e, cap, d), lambda b, k, st: (b, 0, 0, 0),
                         pipeline_mode=pl.Buffered(1)),
        ],
        out_specs=pl.BlockSpec((1, step_rows, d), lambda b, k, st: (b, k, 0)),
        scratch_shapes=[pltpu.VMEM((wide_group * SLOT_BLOCK, d), BF16),
                        pltpu.VMEM((TOKEN_TILE, d), F32)],
    )
    return pl.pallas_call(
        functools.partial(_combine_kernel, cap=cap),
        grid_spec=grid_spec,
        out_shape=jax.ShapeDtypeStruct((bsz, s, d), F32),
        compiler_params=_cparams(("parallel", "parallel")),
        name="combine",
    )(starts_flat, x1, pos_c, gf, eo)


def _block_diag(w):
    h, hd, _ = w.shape
    eye = jnp.eye(h, dtype=w.dtype)
    return (eye[:, None, :, None] * w[:, :, None, :]).reshape(h * hd, h * hd)


def _lru_params(wa_f, wx_f, wa_b, wx_b, ba_f, bx_f, ba_b, bx_b, lam_f, lam_b):
    d_lru = ba_f.shape[0]
    n_blk = d_lru // LANES
    mats = [0.5 * _block_diag(w) for w in (wa_f, wx_f, wa_b, wx_b)]
    ba_f, bx_f, ba_b, bx_b = (0.5 * v for v in (ba_f, bx_f, ba_b, bx_b))
    w_cat = jnp.stack([
        jnp.concatenate([m[c * LANES:(c + 1) * LANES, c * LANES:(c + 1) * LANES] for m in mats], axis=1)
        for c in range(n_blk)]).astype(BF16)
    b_cat = jnp.stack([
        jnp.concatenate([v[c * LANES:(c + 1) * LANES] for v in (ba_f, bx_f, ba_b, bx_b)])
        for c in range(n_blk)])[:, None, :]
    lam_cat = jnp.stack([
        jnp.concatenate([v[c * LANES:(c + 1) * LANES] for v in (lam_f, lam_b)])
        for c in range(n_blk)])[:, None, :]
    return w_cat, b_cat, lam_cat


def kernel(x, norm1_g, w_in, conv_w, conv_b, lru_wa_f, lru_ba_f, lru_wx_f, lru_bx_f, lru_lam_f,
           lru_wa_b, lru_ba_b, lru_wx_b, lru_bx_b, lru_lam_b, w_out, norm2_g, w_router,
           w_gate, w_up, w_down, normf_g):
    bsz, s, d = x.shape
    d_lru = conv_b.shape[0]
    d_four = w_in.shape[1] - 2 * d_lru
    n_e = w_router.shape[1]
    cap = CAPACITY_FACTOR * s // n_e
    assert s % LRU_CHUNK == 0 and s % (SCAN_GROUPS * SUBLANES * SUBLANES) == 0
    assert s % FOURIER_N2 == 0 and cap % SLOT_BLOCK == 0 and s // TOKEN_TILE < LANES
    assert cap >= WIN_ROWS and w_gate.shape[2] % FF_CHUNK == 0

    lx, lg, fo = _inproj(x.reshape(bsz * s, d), norm1_g[None, :], w_in.astype(BF16), d_lru, d_four)
    lx = lx.reshape(bsz, s, d_lru)
    lg = lg.reshape(bsz, s, d_lru)
    fo = fo.reshape(bsz, s, d_four)

    w_cat, b_cat, lam_cat = _lru_params(lru_wa_f, lru_wx_f, lru_wa_b, lru_wx_b,
                                        lru_ba_f, lru_bx_f, lru_ba_b, lru_bx_b,
                                        lru_lam_f, lru_lam_b)
    y_lru = _lru(lx, lg, conv_w, conv_b[None, :], w_cat, b_cat, lam_cat)
    y_four = _fourier(fo, FOURIER_GROUPS)

    w_out_bf = w_out.astype(BF16)
    x1, h2_tiles, aff_t = _outproj(x, y_lru, y_four, w_out_bf[:d_lru], w_out_bf[d_lru:],
                                   norm2_g[None, :], w_router.T)

    pos_r, starts = _select(aff_t, cap)
    starts_flat = starts.reshape(-1)
    pos_c = jnp.transpose(pos_r, (0, 2, 1))
    idx, gs = _compact(starts_flat, pos_r, jnp.transpose(aff_t, (0, 2, 1)), cap)
    eo = _moe(idx.reshape(bsz * n_e, 1, cap), gs.reshape((bsz * n_e,) + gs.shape[2:]), h2_tiles,
              w_gate, w_up, w_down, bsz, s)
    return _combine(starts_flat, x1, pos_c, eo, normf_g[None, :])
```

```python
import functools
import math

import jax
import jax.numpy as jnp
import numpy as np
from jax import lax
from jax.experimental import pallas as pl
from jax.experimental.pallas import tpu as pltpu

F32 = jnp.float32
BF16 = jnp.bfloat16

EPS = 1e-6
LRU_C = 8.0
CAPACITY_FACTOR = 2
FOURIER_GROUPS = 4

SUBLANES = 8
LANES = 128
MXU_DIM = 256
VMEM_LIMIT_BYTES = 56 * 1024 * 1024

ROW_TILE = 1024
INPROJ_ROWS = 2048
LRU_CHUNK = 2048
SCAN_GROUPS = 4
SCAN_UNROLL = 4
SLOT_BLOCK = MXU_DIM
TOKEN_TILE = MXU_DIM
SEG_PAD = 4
FOURIER_N2 = LANES
FOURIER_K1_BATCH = 32
FOURIER_U_BATCH = 2
STAGE_PAD = 4
COPY_UNROLL = 4
FF_CHUNK = 1024
GATHER_UNROLL = 8
GATHER_PRIORITY = 1
COMPACT_WIN = 64
TOKEN_ID_BASE = 64
GATE_COL0 = 16
WIN_ALIGN = 16
WIN_ROWS = SLOT_BLOCK + WIN_ALIGN
NARROW_WIN = 64
COMBINE_TILES = 4


def _cparams(semantics):
    return pltpu.CompilerParams(dimension_semantics=semantics,
                                vmem_limit_bytes=VMEM_LIMIT_BYTES)


def _rms_scale(x):
    return x * lax.rsqrt(jnp.mean(x * x, axis=-1, keepdims=True) + EPS)


def _inproj_kernel(x_ref, g_ref, w_ref, lx_ref, lg_ref, fo_ref):
    h = _rms_scale(x_ref[...]) * g_ref[...]
    p = jnp.dot(h.astype(BF16), w_ref[...], preferred_element_type=F32)
    d = lx_ref.shape[-1]
    lx_ref[...] = p[:, :d]
    lg_ref[...] = p[:, d:2 * d]
    fo_ref[...] = p[:, 2 * d:]


def _inproj(x2, g, w_bf, d_lru, d_four):
    m, d = x2.shape
    n = w_bf.shape[1]
    return pl.pallas_call(
        _inproj_kernel,
        grid=(m // INPROJ_ROWS,),
        in_specs=[
            pl.BlockSpec((INPROJ_ROWS, d), lambda i: (i, 0)),
            pl.BlockSpec((1, d), lambda i: (0, 0)),
            pl.BlockSpec((d, n), lambda i: (0, 0)),
        ],
        out_specs=[
            pl.BlockSpec((INPROJ_ROWS, d_lru), lambda i: (i, 0)),
            pl.BlockSpec((INPROJ_ROWS, d_lru), lambda i: (i, 0)),
            pl.BlockSpec((INPROJ_ROWS, d_four), lambda i: (i, 0)),
        ],
        out_shape=[
            jax.ShapeDtypeStruct((m, d_lru), F32),
            jax.ShapeDtypeStruct((m, d_lru), F32),
            jax.ShapeDtypeStruct((m, d_four), F32),
        ],
        compiler_params=_cparams(("parallel",)),
        name="inproj",
    )(x2, g, w_bf)


def _shift_rows(x, shift):
    n = x.shape[0]
    rows = lax.broadcasted_iota(jnp.int32, x.shape, 0)
    rolled = pltpu.roll(x, shift % n, axis=0)
    keep = (rows >= shift) if shift > 0 else (rows < n + shift)
    return jnp.where(keep, rolled, 0.0)


def _lru_kernel(lx_ref, lg_ref, cw_ref, cb_ref, w_ref, b_ref, lam_ref, y_ref,
                xpad, af, bf, ab, bb):
    s = lx_ref.shape[1]
    c_blk = lx_ref.shape[2]
    n_seg = SCAN_GROUPS * SUBLANES
    seg = s // n_seg
    seg_stride = seg + SEG_PAD
    piece = min(seg, LRU_CHUNK)
    n_chunks = s // LRU_CHUNK
    pad = SUBLANES

    zeros_pad = jnp.zeros((pad, c_blk), F32)
    xpad[pl.ds(0, pad), :] = zeros_pad
    xpad[pl.ds(pad + s, pad), :] = zeros_pad

    def copy_chunk(i, carry):
        t0 = pl.multiple_of(i * LRU_CHUNK, LRU_CHUNK)
        xpad[pl.ds(pad + t0, LRU_CHUNK), :] = lx_ref[0, pl.ds(t0, LRU_CHUNK), :]
        return carry

    lax.fori_loop(0, n_chunks, copy_chunk, 0)

    cw = cw_ref[...]
    conv_width = cw.shape[0]
    cb = cb_ref[...]
    bias = b_ref[0]
    quarter_c_log_sig = (0.25 * LRU_C) * jax.nn.log_sigmoid(lam_ref[0])
    w_cat = w_ref[0]

    def gates_chunk(i, carry):
        t0 = pl.multiple_of(i * LRU_CHUNK, LRU_CHUNK)
        c = cb
        left = conv_width // 2
        for k in range(conv_width):
            c = c + cw[k:k + 1, :] * xpad[pl.ds(t0 + (pad - left + k), LRU_CHUNK), :]
        z = jnp.dot(c.astype(BF16), w_cat, preferred_element_type=F32) + bias
        for d, (a_scr, b_scr) in enumerate(((af, bf), (ab, bb))):
            q = quarter_c_log_sig[:, d * c_blk:(d + 1) * c_blk]
            th_r = jnp.tanh(z[:, (2 * d) * c_blk:(2 * d + 1) * c_blk])
            th_i = jnp.tanh(z[:, (2 * d + 1) * c_blk:(2 * d + 2) * c_blk])
            t = jnp.tanh(q + q * th_r)
            inv = 1.0 / (1.0 - t)
            a = (1.0 + t) * inv
            bt = (jnp.sqrt(-t) * inv) * (c + c * th_i)
            for p in range(LRU_CHUNK // piece):
                t = t0 + p * piece
                dst = pl.ds((t // seg) * seg_stride + t % seg, piece)
                a_scr[dst, :] = a[p * piece:(p + 1) * piece]
                b_scr[dst, :] = bt[p * piece:(p + 1) * piece]
        return carry

    lax.fori_loop(0, n_chunks, gates_chunk, 0)

    def scan_step(i, carry):
        def rows(o, q):
            return pl.ds(q * SUBLANES * seg_stride + o, SUBLANES, stride=seg_stride)

        offsets = [(i * SCAN_UNROLL + k, seg - 1 - (i * SCAN_UNROLL + k)) for k in range(SCAN_UNROLL)]
        scratch = ((af, bf), (ab, bb))
        loaded = [[[(scratch[d][0][rows(o[d], q), :], scratch[d][1][rows(o[d], q), :])
                    for q in range(SCAN_GROUPS)] for d in range(2)] for o in offsets]
        state = [[list(hs), list(ps)] for hs, ps in carry]
        results = []
        for k, o in enumerate(offsets):
            for d in range(2):
                for q in range(SCAN_GROUPS):
                    a, b = loaded[k][d][q]
                    state[d][0][q] = a * state[d][0][q] + b
                    state[d][1][q] = a * state[d][1][q]
                    results.append((d, o[d], q, state[d][0][q], state[d][1][q]))
        for d, o, q, h, p in results:
            scratch[d][1][rows(o, q), :] = h
            scratch[d][0][rows(o, q), :] = p
        return tuple((tuple(hs), tuple(ps)) for hs, ps in state)

    zero = jnp.zeros((SUBLANES, c_blk), F32)
    one = jnp.ones((SUBLANES, c_blk), F32)
    init = ((zero,) * SCAN_GROUPS, (one,) * SCAN_GROUPS)
    (hf_end, pf_end), (hb_end, pb_end) = lax.fori_loop(0, seg // SCAN_UNROLL, scan_step, (init, init))

    row = lax.broadcasted_iota(jnp.int32, (SUBLANES, c_blk), 0)

    def entering(h_end, p_end, carry_in, first_row, shift):
        c = zero
        for _ in range(SUBLANES):
            c = jnp.where(row == first_row, carry_in, _shift_rows(h_end + p_end * c, shift))
        return c

    cf = [None] * SCAN_GROUPS
    cbk = [None] * SCAN_GROUPS
    carry_f = jnp.zeros((1, c_blk), F32)
    carry_b = jnp.zeros((1, c_blk), F32)
    for q in range(SCAN_GROUPS):
        cf[q] = entering(hf_end[q], pf_end[q], carry_f, 0, 1)
        carry_f = (hf_end[q] + pf_end[q] * cf[q])[SUBLANES - 1:SUBLANES, :]
        qb = SCAN_GROUPS - 1 - q
        cbk[qb] = entering(hb_end[qb], pb_end[qb], carry_b, SUBLANES - 1, -1)
        carry_b = (hb_end[qb] + pb_end[qb] * cbk[qb])[0:1, :]

    for j in range(n_seg):
        q, i = divmod(j, SUBLANES)
        src = pl.ds(j * seg_stride, seg)
        rows = pl.ds(j * seg, seg)
        hsum = ((bf[src, :] + af[src, :] * cf[q][i:i + 1, :])
                + (bb[src, :] + ab[src, :] * cbk[q][i:i + 1, :]))
        y_ref[0, rows, :] = (jax.nn.gelu(lg_ref[0, rows, :]) * hsum).astype(y_ref.dtype)


def _lru(lx, lg, conv_w, conv_b, w_cat, b_cat, lam_cat):
    bsz, s, d_lru = lx.shape
    c_blk = LANES
    n_blk = d_lru // c_blk
    scr = pltpu.VMEM((s + SCAN_GROUPS * SUBLANES * SEG_PAD, c_blk), F32)
    return pl.pallas_call(
        _lru_kernel,
        grid=(bsz, n_blk),
        in_specs=[
            pl.BlockSpec((1, s, c_blk), lambda b, c: (b, 0, c)),
            pl.BlockSpec((1, s, c_blk), lambda b, c: (b, 0, c)),
            pl.BlockSpec((4, c_blk), lambda b, c: (0, c)),
            pl.BlockSpec((1, c_blk), lambda b, c: (0, c)),
            pl.BlockSpec((1, c_blk, 4 * c_blk), lambda b, c: (c, 0, 0)),
            pl.BlockSpec((1, 1, 4 * c_blk), lambda b, c: (c, 0, 0)),
            pl.BlockSpec((1, 1, 2 * c_blk), lambda b, c: (c, 0, 0)),
        ],
        out_specs=pl.BlockSpec((1, s, c_blk), lambda b, c: (b, 0, c)),
        out_shape=jax.ShapeDtypeStruct((bsz, s, d_lru), BF16),
        scratch_shapes=[pltpu.VMEM((s + 2 * SUBLANES, c_blk), F32), scr, scr, scr, scr],
        compiler_params=_cparams(("parallel", "parallel")),
        name="lru",
    )(lx, lg, conv_w, conv_b, w_cat, b_cat, lam_cat)


def _fourier_kernel(f_ref, ka_ref, kb_ref, cc_ref, sc_ref, twc_ref, tws_ref, y_ref,
                    ar_scr, ai_scr, y_scr, *, scale):
    n1 = f_ref.shape[1]
    n_u = f_ref.shape[2]
    cols = f_ref.shape[4]
    r = SUBLANES * n1
    n2 = n_u * SUBLANES
    stage_stride = n1 + STAGE_PAD

    def stage_a(i, carry):
        us = tuple(FOURIER_U_BATCH * i + j for j in range(FOURIER_U_BATCH))
        xu = jnp.concatenate([f_ref[0, :, u].reshape(r, cols) for u in us], axis=1)
        a = jnp.dot(ka_ref[...], xu.astype(BF16), preferred_element_type=F32)
        for j, u in enumerate(us):
            a_re = a[:r, j * cols:(j + 1) * cols]
            a_im = a[r:, j * cols:(j + 1) * cols]
            c = twc_ref[u]
            sn = tws_ref[u]
            ar_scr[u] = a_re * c - a_im * sn
            ai_scr[u] = a_re * sn + a_im * c
        return carry

    lax.fori_loop(0, n_u // FOURIER_U_BATCH, stage_a, 0)

    nb = FOURIER_K1_BATCH

    def stage_b(i, carry):
        rows = pl.ds(pl.multiple_of(i * (nb * SUBLANES), nb * SUBLANES), nb * SUBLANES)
        a_re = ar_scr[:, rows, :]
        a_im = ai_scr[:, rows, :]

        def rows_s2(a, j):
            return a[:, j * SUBLANES:(j + 1) * SUBLANES, :].reshape(n2, cols)

        st = jnp.concatenate(
            [jnp.concatenate([rows_s2(a_re, j), rows_s2(a_im, j)], axis=0) for j in range(nb)],
            axis=1).astype(BF16)
        x = jnp.dot(kb_ref[...], st, preferred_element_type=F32)
        x_re = jnp.concatenate([x[:n2, j * cols:(j + 1) * cols] for j in range(nb)], axis=0)
        x_im = jnp.concatenate([x[n2:, j * cols:(j + 1) * cols] for j in range(nb)], axis=0)
        y = (jnp.dot(x_re.astype(BF16), cc_ref[...], preferred_element_type=F32)
             + jnp.dot(x_im.astype(BF16), sc_ref[...], preferred_element_type=F32)) * scale
        for j in range(nb):
            y_scr[pl.ds(i * nb + j, n2, stride=stage_stride), :] = y[j * n2:(j + 1) * n2]
        return carry

    lax.fori_loop(0, n1 // nb, stage_b, 0)

    def copy_out(i, carry):
        blocks = [y_scr[pl.ds((i * COPY_UNROLL + j) * stage_stride, n1), :] for j in range(COPY_UNROLL)]
        for j in range(COPY_UNROLL):
            k2 = i * COPY_UNROLL + j
            y_ref[0, pl.ds(pl.multiple_of(k2 * n1, n1), n1), :] = blocks[j]
        return carry

    lax.fori_loop(0, n2 // COPY_UNROLL, copy_out, 0)


def _dft_tables(s, c_grp):
    n2 = FOURIER_N2
    n1 = s // n2
    n_u = n2 // SUBLANES

    def cos_sin(num, den):
        ang = (2.0 * np.pi / den) * (num % den).astype(np.float64)
        return np.cos(ang), np.sin(ang)

    def const(a, dtype=F32):
        return jnp.asarray(a.astype(np.float32)).astype(dtype)

    i1 = np.arange(n1, dtype=np.int64)
    c1, s1 = cos_sin(i1[:, None] * i1[None, :], n1)
    eye8 = np.eye(SUBLANES)
    ka = const(np.concatenate([np.kron(c1, eye8), np.kron(-s1, eye8)], axis=0), BF16)

    i2 = np.arange(n2, dtype=np.int64)
    c2, s2 = cos_sin(i2[:, None] * i2[None, :], n2)
    kb = const(np.block([[c2, s2], [-s2, c2]]), BF16)

    ic = np.arange(c_grp, dtype=np.int64)
    cc, sc = cos_sin(ic[:, None] * ic[None, :], c_grp)

    u = np.arange(n_u, dtype=np.int64)[:, None, None]
    k1 = np.arange(n1, dtype=np.int64)[None, :, None]
    v = np.arange(SUBLANES, dtype=np.int64)[None, None, :]
    tc, ts = cos_sin(k1 * (SUBLANES * u + v), s)
    shape = (n_u, n1 * SUBLANES, c_grp)
    twc = jnp.broadcast_to(const(tc.reshape(n_u, -1, 1)), shape)
    tws = jnp.broadcast_to(const(-ts.reshape(n_u, -1, 1)), shape)
    return ka, kb, const(cc, BF16), const(sc, BF16), twc, tws


def _fourier(four, n_groups):
    bsz, s, d_four = four.shape
    c_grp = d_four // n_groups
    n2 = FOURIER_N2
    n1 = s // n2
    n_u = n2 // SUBLANES
    r = SUBLANES * n1
    ka, kb, cc, sc, twc, tws = _dft_tables(s, c_grp)
    f5 = four.reshape(bsz, n1, n_u, SUBLANES, d_four)
    scale = 1.0 / math.sqrt(s * c_grp)
    const2 = lambda b, g: (0, 0)
    const3 = lambda b, g: (0, 0, 0)
    return pl.pallas_call(
        functools.partial(_fourier_kernel, scale=scale),
        grid=(bsz, n_groups),
        in_specs=[
            pl.BlockSpec((1, n1, n_u, SUBLANES, c_grp), lambda b, g: (b, 0, 0, 0, g)),
            pl.BlockSpec(ka.shape, const2),
            pl.BlockSpec(kb.shape, const2),
            pl.BlockSpec(cc.shape, const2),
            pl.BlockSpec(sc.shape, const2),
            pl.BlockSpec(twc.shape, const3),
            pl.BlockSpec(tws.shape, const3),
        ],
        out_specs=pl.BlockSpec((1, s, c_grp), lambda b, g: (b, 0, g)),
        out_shape=jax.ShapeDtypeStruct((bsz, s, d_four), F32),
        scratch_shapes=[pltpu.VMEM((n_u, r, c_grp), F32), pltpu.VMEM((n_u, r, c_grp), F32),
                        pltpu.VMEM((n2 * (n1 + STAGE_PAD), c_grp), F32)],
        compiler_params=_cparams(("parallel", "parallel")),
        name="fourier",
    )(f5, ka, kb, cc, sc, twc, tws)


def _outproj_kernel(x_ref, yl_ref, yf_ref, wo1_ref, wo2_ref, g2_ref, wr_ref,
                    x1_ref, h2_ref, aff_ref):
    x1 = (x_ref[0]
          + jnp.dot(yl_ref[0], wo1_ref[...], preferred_element_type=F32)
          + jnp.dot(yf_ref[0].astype(BF16), wo2_ref[...], preferred_element_type=F32))
    x1_ref[0] = x1
    h2 = _rms_scale(x1) * g2_ref[...]
    for j in range(h2.shape[1] // LANES):
        h2_ref[pl.ds(j, h2.shape[0], stride=SUBLANES), :] = h2[:, j * LANES:(j + 1) * LANES]
    def split(v):
        hi = v.astype(BF16)
        return hi, (v - hi.astype(F32)).astype(BF16)

    def dot_nt(a, bm):
        return lax.dot_general(a, bm, (((1,), (1,)), ((), ())), preferred_element_type=F32)

    w_hi, w_lo = split(wr_ref[...])
    h_hi, h_lo = split(h2)
    logits = dot_nt(w_hi, h_hi) + (dot_nt(w_lo, h_hi) + dot_nt(w_hi, h_lo))
    ex = jnp.exp(logits - jnp.max(logits, axis=0, keepdims=True))
    aff_ref[0] = ex / jnp.sum(ex, axis=0, keepdims=True)


def _outproj(x, y_lru, y_four, wo1, wo2, g2, wr_t):
    bsz, s, d = x.shape
    d_lru = y_lru.shape[-1]
    d_four = y_four.shape[-1]
    n_e = wr_t.shape[0]
    assert d == SUBLANES * LANES
    n_i = s // ROW_TILE
    tile = lambda w: pl.BlockSpec((1, ROW_TILE, w), lambda b, i: (b, i, 0))
    const = lambda shape: pl.BlockSpec(shape, lambda b, i: (0, 0))
    return pl.pallas_call(
        _outproj_kernel,
        grid=(bsz, n_i),
        in_specs=[tile(d), tile(d_lru), tile(d_four), const(wo1.shape), const(wo2.shape),
                  const(g2.shape), const(wr_t.shape)],
        out_specs=[tile(d),
                   pl.BlockSpec((ROW_TILE * SUBLANES, LANES), lambda b, i: (b * n_i + i, 0)),
                   pl.BlockSpec((1, n_e, ROW_TILE), lambda b, i: (b, 0, i))],
        out_shape=[
            jax.ShapeDtypeStruct((bsz, s, d), F32),
            jax.ShapeDtypeStruct((bsz * s * SUBLANES, LANES), F32),
            jax.ShapeDtypeStruct((bsz, n_e, s), F32),
        ],
        compiler_params=_cparams(("parallel", "parallel")),
        name="outproj",
    )(x, y_lru, y_four, wo1, wo2, g2, wr_t)


def _select_kernel(aff_ref, tri_ref, pos_ref, starts_ref, *, cap):
    v = aff_ref[0]
    n_e, s = v.shape
    cap_f = float(cap)

    def midpoint(lo, hi):
        mid = 0.5 * (lo + hi)
        return mid, (mid > lo) & (mid < hi)

    def cond(carry):
        _, active = midpoint(*carry)
        return jnp.max(active.astype(F32)) > 0.0

    def body(carry):
        lo, hi = carry
        mid, active = midpoint(lo, hi)
        cnt = jnp.sum((v >= mid).astype(F32), axis=1, keepdims=True)
        enough = cnt >= cap_f
        return (jnp.where(active & enough, mid, lo), jnp.where(active & (~enough), mid, hi))

    lo0 = jnp.zeros((n_e, 1), F32)
    hi0 = jnp.full((n_e, 1), 2.0, F32)
    thr, _ = lax.while_loop(cond, body, (lo0, hi0))

    above = v > thr
    tie = v == thr
    need = cap_f - jnp.sum(above.astype(F32), axis=1, keepdims=True)

    n_tiles = s // TOKEN_TILE
    tri = tri_ref[...]
    lane = lax.broadcasted_iota(jnp.int32, (n_e, LANES), 1)

    def prefix(mask_f, want_starts):
        run = jnp.zeros((n_e, 1), F32)
        starts = jnp.zeros((n_e, LANES), F32)
        pieces = []
        for t in range(n_tiles):
            m = mask_f[:, t * TOKEN_TILE:(t + 1) * TOKEN_TILE]
            incl = jnp.dot(m.astype(BF16), tri, preferred_element_type=F32)
            pieces.append(run + incl - m)
            if want_starts:
                starts = jnp.where(lane == t, run, starts)
            run = run + incl[:, TOKEN_TILE - 1:TOKEN_TILE]
        if want_starts:
            starts = jnp.where(lane == n_tiles, run, starts)
        return jnp.concatenate(pieces, axis=1), starts

    tie_rank, _ = prefix(tie.astype(F32), False)
    sel = above | (tie & (tie_rank < need))
    pos, starts = prefix(sel.astype(F32), True)
    pos_ref[0] = jnp.where(sel, pos, -1.0).astype(jnp.int32)
    starts_ref[0] = starts.astype(jnp.int32)


def _select(aff_t, cap):
    bsz, n_e, s = aff_t.shape
    idx = jnp.arange(TOKEN_TILE, dtype=jnp.int32)
    tri = (idx[:, None] <= idx[None, :]).astype(BF16)
    return pl.pallas_call(
        functools.partial(_select_kernel, cap=cap),
        grid=(bsz,),
        in_specs=[
            pl.BlockSpec((1, n_e, s), lambda b: (b, 0, 0)),
            pl.BlockSpec(tri.shape, lambda b: (0, 0)),
        ],
        out_specs=[
            pl.BlockSpec((1, n_e, s), lambda b: (b, 0, 0)),
            pl.BlockSpec((1, n_e, LANES), lambda b: (b, 0, 0)),
        ],
        out_shape=[
            jax.ShapeDtypeStruct((bsz, n_e, s), jnp.int32),
            jax.ShapeDtypeStruct((bsz, n_e, LANES), jnp.int32),
        ],
        compiler_params=_cparams(("parallel",)),
        name="select",
    )(aff_t, tri)


def _token_table(gate, first_token):
    n_e, n_tok = gate.shape
    lane = lax.broadcasted_iota(jnp.int32, (n_tok, LANES), 1)
    tok = first_token + lax.broadcasted_iota(jnp.int32, (n_tok, LANES), 0)
    table = jnp.where(lane == 0, tok // TOKEN_ID_BASE, jnp.where(lane == 1, tok % TOKEN_ID_BASE, 0))
    table = table.astype(F32)
    place_lane = lax.broadcasted_iota(jnp.int32, (n_e, LANES), 1)
    place_sub = lax.broadcasted_iota(jnp.int32, (n_e, LANES), 0)
    rest = gate
    for j in range(3):
        piece = rest.astype(BF16)
        rest = rest - piece.astype(F32)
        place = jnp.where(place_lane == GATE_COL0 + j * n_e + place_sub, 1.0, 0.0).astype(BF16)
        table = table + lax.dot_general(piece, place, (((0,), (0,)), ((), ())),
                                        preferred_element_type=F32)
    return table.astype(BF16)


def _compact_kernel(starts_ref, pos_ref, gate_ref, idx_ref, gs_ref, r_scr, *, cap):
    b = pl.program_id(0)
    n_e = pos_ref.shape[1]
    n_k = pos_ref.shape[2] // TOKEN_TILE
    n_blocks = cap // SLOT_BLOCK
    wide_windows = -(-(TOKEN_TILE + SUBLANES) // COMPACT_WIN)
    r_scr[...] = jnp.zeros_like(r_scr)
    sub = lax.broadcasted_iota(jnp.int32, (COMPACT_WIN, TOKEN_TILE), 0)

    def tile(k, carry):
        tokens = pl.ds(pl.multiple_of(k * TOKEN_TILE, TOKEN_TILE), TOKEN_TILE)
        vk = _token_table(gate_ref[0, :, tokens], k * TOKEN_TILE)
        starts = []
        fits = None
        for e in range(n_e):
            base = (b * n_e + e) * LANES
            ws = (starts_ref[base + k] // SUBLANES) * SUBLANES
            ok = starts_ref[base + k + 1] - ws <= COMPACT_WIN
            fits = ok if fits is None else fits & ok
            starts.append(ws)

        def place(e, ws):
            hit = pos_ref[0, pl.ds(e, 1), tokens] - ws == sub
            vals = jnp.dot(jnp.where(hit, 1.0, 0.0).astype(BF16), vk, preferred_element_type=F32)
            r_scr[e, pl.ds(pl.multiple_of(ws, SUBLANES), COMPACT_WIN), :] += vals

        @pl.when(fits)
        def _():
            for e in range(n_e):
                place(e, starts[e])

        @pl.when(jnp.logical_not(fits))
        def _():
            for e in range(n_e):
                for j in range(wide_windows):
                    place(e, starts[e] + j * COMPACT_WIN)

        return carry

    lax.fori_loop(0, n_k, tile, 0)

    lane = lax.broadcasted_iota(jnp.int32, (SUBLANES, LANES), 1)
    sub8 = lax.broadcasted_iota(jnp.int32, (SUBLANES, LANES), 0)
    id_rows = jnp.where(sub8 == 0, jnp.where(lane == 0, float(TOKEN_ID_BASE),
                                             jnp.where(lane == 1, 1.0, 0.0)), 0.0)

    def finish(e, carry):
        gate_lane = (lane >= GATE_COL0) & (lane < GATE_COL0 + 3 * n_e) & ((lane - GATE_COL0) % n_e == e)
        selector = (id_rows + jnp.where((sub8 == 1) & gate_lane, 1.0, 0.0)).astype(BF16)
        for m in range(n_blocks):
            blk = r_scr[e, pl.ds(m * SLOT_BLOCK, SLOT_BLOCK), :].astype(BF16)
            rows = lax.dot_general(selector, blk, (((1,), (1,)), ((), ())),
                                   preferred_element_type=F32)
            idx_ref[0, e, m] = rows[0:1, :].astype(jnp.int32)
            gs_ref[0, e, m] = rows[1:2, :]
        return carry

    lax.fori_loop(0, n_e, finish, 0)


def _compact(starts_flat, pos_r, gate_r, cap):
    bsz, n_e, s = pos_r.shape
    assert s <= TOKEN_ID_BASE * 256 and GATE_COL0 + 3 * n_e <= LANES
    n_blocks = cap // SLOT_BLOCK
    out_block = (1, n_e, n_blocks, 1, SLOT_BLOCK)
    out_spec = pl.BlockSpec(out_block, lambda b, st: (b, 0, 0, 0, 0))
    wide_rows = -(-(TOKEN_TILE + SUBLANES) // COMPACT_WIN) * COMPACT_WIN
    grid_spec = pltpu.PrefetchScalarGridSpec(
        num_scalar_prefetch=1,
        grid=(bsz,),
        in_specs=[pl.BlockSpec((1, n_e, s), lambda b, st: (b, 0, 0)),
                  pl.BlockSpec((1, n_e, s), lambda b, st: (b, 0, 0))],
        out_specs=[out_spec, out_spec],
        scratch_shapes=[pltpu.VMEM((n_e, cap + wide_rows, LANES), F32)],
    )
    return pl.pallas_call(
        functools.partial(_compact_kernel, cap=cap),
        grid_spec=grid_spec,
        out_shape=[jax.ShapeDtypeStruct((bsz,) + out_block[1:], jnp.int32),
                   jax.ShapeDtypeStruct((bsz,) + out_block[1:], F32)],
        compiler_params=_cparams(("parallel",)),
        name="compact",
    )(starts_flat, pos_r, gate_r)


def _moe_kernel(idx_ref, idx_next_ref, gs_ref, h2_hbm, wg_ref, wu_ref, wd_ref, eo_ref,
                xbuf, sem, xb_scr, acc_scr, *, seq_len, n_ff):
    b = pl.program_id(0)
    e = pl.program_id(1)
    f = pl.program_id(2)
    n_e = pl.num_programs(1)
    n_blocks = xb_scr.shape[0]
    cap = n_blocks * SLOT_BLOCK
    d = xb_scr.shape[2]
    step = b * n_e + e
    n_steps = pl.num_programs(0) * n_e
    cur = step % 2

    def token_copy(idx_smem, batch, buf, p):
        tok = idx_smem[0, 0, p]
        src = h2_hbm.at[pl.ds(pl.multiple_of((batch * seq_len + tok) * SUBLANES, SUBLANES), SUBLANES)]
        dst = xbuf.at[buf, pl.ds(pl.multiple_of(p * SUBLANES, SUBLANES), SUBLANES)]
        return pltpu.make_async_copy(src, dst, sem.at[buf])

    def request(idx_smem, batch, buf, first, count):
        def body(i, carry):
            for j in range(GATHER_UNROLL):
                token_copy(idx_smem, batch, buf, first + i * GATHER_UNROLL + j).start(priority=GATHER_PRIORITY)
            return carry
        lax.fori_loop(0, count // GATHER_UNROLL, body, 0)

    @pl.when((step == 0) & (f == 0))
    def _():
        request(idx_ref, b, cur, 0, cap)

    next_batch = jnp.where(step + 1 < n_steps, step + 1, 0) // n_e
    per_block = cap // (n_ff * n_blocks)

    @pl.when(f == 0)
    def _():
        pltpu.make_async_copy(h2_hbm.at[pl.ds(0, cap * SUBLANES)], xbuf.at[cur], sem.at[cur]).wait()
        for m in range(n_blocks):
            first = m * SLOT_BLOCK * SUBLANES
            xm = jnp.concatenate(
                [xbuf[cur, pl.ds(first + j, SLOT_BLOCK, stride=SUBLANES), :] for j in range(d // LANES)],
                axis=1)
            xb_scr[m] = xm.astype(BF16)
        acc_scr[...] = jnp.zeros_like(acc_scr)

    wg = wg_ref[0].astype(BF16)
    wu = wu_ref[0].astype(BF16)
    wd = wd_ref[0].astype(BF16)
    for m in range(n_blocks):
        first = (f * n_blocks + m) * per_block
        for j in range(per_block):
            token_copy(idx_next_ref, next_batch, 1 - cur, first + j).start(priority=GATHER_PRIORITY)
        xm = xb_scr[m]
        g = jnp.dot(xm, wg, preferred_element_type=F32)
        u = jnp.dot(xm, wu, preferred_element_type=F32)
        act = (jax.nn.silu(g) * u).astype(BF16)
        acc_scr[m] += jnp.dot(act, wd, preferred_element_type=F32)

    @pl.when((step == n_steps - 1) & (f == n_ff - 1))
    def _():
        pltpu.make_async_copy(h2_hbm.at[pl.ds(0, cap * SUBLANES)], xbuf.at[1 - cur],
                              sem.at[1 - cur]).wait()

    @pl.when(f == n_ff - 1)
    def _():
        eye = (lax.broadcasted_iota(jnp.int32, (SLOT_BLOCK, SLOT_BLOCK), 0)
               == lax.broadcasted_iota(jnp.int32, (SLOT_BLOCK, SLOT_BLOCK), 1))
        for m in range(n_blocks):
            gcol = jnp.sum(jnp.where(eye, gs_ref[0, m], 0.0), axis=1, keepdims=True)
            rows = pl.ds(m * SLOT_BLOCK, SLOT_BLOCK)
            eo_ref[0, 0, rows, :] = (gcol * acc_scr[m]).astype(eo_ref.dtype)


def _moe(idx, gs, h2_tiles, wg, wu, wd, bsz, s):
    n_e, d, d_ff = wg.shape
    n_blocks = gs.shape[1]
    cap = n_blocks * SLOT_BLOCK
    n_ff = d_ff // FF_CHUNK
    assert d_ff % FF_CHUNK == 0 and cap % (n_ff * n_blocks) == 0 and cap % GATHER_UNROLL == 0
    assert d == SUBLANES * LANES

    def next_step(b, e, f):
        return ((b * n_e + e + 1) % (bsz * n_e), 0, 0)

    smem = pltpu.SMEM
    return pl.pallas_call(
        functools.partial(_moe_kernel, seq_len=s, n_ff=n_ff),
        grid=(bsz, n_e, n_ff),
        in_specs=[
            pl.BlockSpec((1, 1, cap), lambda b, e, f: (b * n_e + e, 0, 0), memory_space=smem),
            pl.BlockSpec((1, 1, cap), next_step, memory_space=smem),
            pl.BlockSpec((1, n_blocks, 1, SLOT_BLOCK), lambda b, e, f: (b * n_e + e, 0, 0, 0)),
            pl.BlockSpec(memory_space=pl.ANY),
            pl.BlockSpec((1, d, FF_CHUNK), lambda b, e, f: (e, 0, f)),
            pl.BlockSpec((1, d, FF_CHUNK), lambda b, e, f: (e, 0, f)),
            pl.BlockSpec((1, FF_CHUNK, d), lambda b, e, f: (e, f, 0)),
        ],
        out_specs=pl.BlockSpec((1, 1, cap, d), lambda b, e, f: (b, e, 0, 0)),
        out_shape=jax.ShapeDtypeStruct((bsz, n_e, cap, d), BF16),
        scratch_shapes=[pltpu.VMEM((2, cap * SUBLANES, LANES), F32),
                        pltpu.SemaphoreType.DMA((2,)),
                        pltpu.VMEM((n_blocks, SLOT_BLOCK, d), BF16),
                        pltpu.VMEM((n_blocks, SLOT_BLOCK, d), F32)],
        compiler_params=_cparams(("arbitrary", "arbitrary", "arbitrary")),
        name="moe",
    )(idx, idx, gs, h2_tiles, wg, wu, wd)


def _combine_kernel(starts_ref, x1_ref, pos_ref, gf_ref, eo_ref, out_ref, wcat_scr, acc_scr,
                    *, cap):
    n_tiles = x1_ref.shape[1] // TOKEN_TILE

    def tile(i, carry):
        _combine_tile(starts_ref, x1_ref, pos_ref, gf_ref, eo_ref, out_ref, wcat_scr, acc_scr,
                      pl.program_id(1) * n_tiles + i,
                      pl.ds(pl.multiple_of(i * TOKEN_TILE, TOKEN_TILE), TOKEN_TILE), cap)
        return carry

    lax.fori_loop(0, n_tiles, tile, 0)


def _combine_tile(starts_ref, x1_ref, pos_ref, gf_ref, eo_ref, out_ref, wcat_scr, acc_scr,
                  k, rows, cap):
    b = pl.program_id(0)
    n_e = pos_ref.shape[1]
    pos = pos_ref[0, :, rows]
    expert_row = lax.broadcasted_iota(jnp.int32, (n_e, 1), 0)

    s0 = [starts_ref[(b * n_e + e) * LANES + k] for e in range(n_e)]
    s1 = [starts_ref[(b * n_e + e) * LANES + k + 1] for e in range(n_e)]

    def window_starts(n_rows):
        starts = [jnp.minimum(s // WIN_ALIGN, (cap - n_rows) // WIN_ALIGN) * WIN_ALIGN for s in s0]
        vec = jnp.zeros((n_e, 1), jnp.int32)
        for e in range(n_e):
            vec = jnp.where(expert_row == e, starts[e], vec)
        return starts, vec

    def onehot(rel, first, count, width):
        n_lanes = count * width
        lane = lax.broadcasted_iota(jnp.int32, (n_e, n_lanes), 1)
        sub = lax.broadcasted_iota(jnp.int32, (n_e, n_lanes), 0)
        expand = jnp.where(sub == first + lane // width, 1.0, 0.0).astype(BF16)
        spread = lax.dot_general(jnp.clip(rel, -1, width).astype(F32).astype(BF16), expand,
                                 (((0,), (0,)), ((), ())), preferred_element_type=F32)
        want = (lax.broadcasted_iota(jnp.int32, (TOKEN_TILE, n_lanes), 1) % width).astype(F32)
        return jnp.where(spread == want, 1.0, 0.0).astype(BF16)

    def finish(acc):
        out_ref[0, rows, :] = _rms_scale(acc) * gf_ref[...]

    narrow, narrow_vec = window_starts(NARROW_WIN)
    fits = None
    for e in range(n_e):
        ok = s1[e] - narrow[e] <= NARROW_WIN
        fits = ok if fits is None else fits & ok

    @pl.when(fits)
    def _():
        for e in range(n_e):
            src = pl.ds(pl.multiple_of(narrow[e], WIN_ALIGN), NARROW_WIN)
            wcat_scr[pl.ds(e * NARROW_WIN, NARROW_WIN), :] = eo_ref[0, e, src, :]
        hit = onehot(pos - narrow_vec, 0, n_e, NARROW_WIN)
        finish(x1_ref[0, rows, :] + jnp.dot(hit, wcat_scr[pl.ds(0, n_e * NARROW_WIN), :],
                                            preferred_element_type=F32))

    @pl.when(jnp.logical_not(fits))
    def _():
        wide, wide_vec = window_starts(WIN_ROWS)
        rel = pos - wide_vec
        group = wcat_scr.shape[0] // SLOT_BLOCK
        acc_scr[...] = x1_ref[0, rows, :]
        for g in range(n_e // group):
            for i in range(group):
                e = g * group + i
                src = pl.ds(pl.multiple_of(wide[e], WIN_ALIGN), SLOT_BLOCK)
                wcat_scr[pl.ds(i * SLOT_BLOCK, SLOT_BLOCK), :] = eo_ref[0, e, src, :]
            acc_scr[...] += jnp.dot(onehot(rel, g * group, group, SLOT_BLOCK), wcat_scr[...],
                                    preferred_element_type=F32)
        for e in range(n_e):
            src = pl.ds(pl.multiple_of(wide[e] + SLOT_BLOCK, WIN_ALIGN), WIN_ALIGN)
            wcat_scr[pl.ds(e * WIN_ALIGN, WIN_ALIGN), :] = eo_ref[0, e, src, :]
        tail = onehot(rel - SLOT_BLOCK, 0, n_e, WIN_ALIGN)
        finish(acc_scr[...] + jnp.dot(tail, wcat_scr[pl.ds(0, n_e * WIN_ALIGN), :],
                                      preferred_element_type=F32))


def _combine(starts_flat, x1, pos_r, eo, gf):
    bsz, s, d = x1.shape
    n_e = pos_r.shape[1]
    cap = eo.shape[2]
    step_rows = COMBINE_TILES * TOKEN_TILE
    wide_group = 4
    assert n_e % wide_group == 0 and n_e * NARROW_WIN <= wide_group * SLOT_BLOCK
    assert s % step_rows == 0
    grid_spec = pltpu.PrefetchScalarGridSpec(
        num_scalar_prefetch=1,
        grid=(bsz, s // step_rows),
        in_specs=[
            pl.BlockSpec((1, step_rows, d), lambda b, k, st: (b, k, 0)),
            pl.BlockSpec((1, n_e, step_rows), lambda b, k, st: (b, 0, k)),
            pl.BlockSpec((1, d), lambda b, k, st: (0, 0)),
            pl.BlockSpec((1, n_e, cap, d), lambda b, k, st: (b, 0, 0, 0),
                         pipeline_mode=pl.Buffered(1)),
        ],
        out_specs=pl.BlockSpec((1, step_rows, d), lambda b, k, st: (b, k, 0)),
        scratch_shapes=[pltpu.VMEM((wide_group * SLOT_BLOCK, d), BF16),
                        pltpu.VMEM((TOKEN_TILE, d), F32)],
    )
    return pl.pallas_call(
        functools.partial(_combine_kernel, cap=cap),
        grid_spec=grid_spec,
        out_shape=jax.ShapeDtypeStruct((bsz, s, d), F32),
        compiler_params=_cparams(("parallel", "parallel")),
        name="combine",
    )(starts_flat, x1, pos_r, gf, eo)


def _block_diag(w):
    h, hd, _ = w.shape
    eye = jnp.eye(h, dtype=w.dtype)
    return (eye[:, None, :, None] * w[:, :, None, :]).reshape(h * hd, h * hd)


def _lru_params(wa_f, wx_f, wa_b, wx_b, ba_f, bx_f, ba_b, bx_b, lam_f, lam_b):
    d_lru = ba_f.shape[0]
    n_blk = d_lru // LANES
    mats = [0.5 * _block_diag(w) for w in (wa_f, wx_f, wa_b, wx_b)]
    ba_f, bx_f, ba_b, bx_b = (0.5 * v for v in (ba_f, bx_f, ba_b, bx_b))
    w_cat = jnp.stack([
        jnp.concatenate([m[c * LANES:(c + 1) * LANES, c * LANES:(c + 1) * LANES] for m in mats], axis=1)
        for c in range(n_blk)]).astype(BF16)
    b_cat = jnp.stack([
        jnp.concatenate([v[c * LANES:(c + 1) * LANES] for v in (ba_f, bx_f, ba_b, bx_b)])
        for c in range(n_blk)])[:, None, :]
    lam_cat = jnp.stack([
        jnp.concatenate([v[c * LANES:(c + 1) * LANES] for v in (lam_f, lam_b)])
        for c in range(n_blk)])[:, None, :]
    return w_cat, b_cat, lam_cat


def kernel(x, norm1_g, w_in, conv_w, conv_b, lru_wa_f, lru_ba_f, lru_wx_f, lru_bx_f, lru_lam_f,
           lru_wa_b, lru_ba_b, lru_wx_b, lru_bx_b, lru_lam_b, w_out, norm2_g, w_router,
           w_gate, w_up, w_down, normf_g):
    bsz, s, d = x.shape
    d_lru = conv_b.shape[0]
    d_four = w_in.shape[1] - 2 * d_lru
    n_e = w_router.shape[1]
    cap = CAPACITY_FACTOR * s // n_e
    assert s % LRU_CHUNK == 0 and s % (SCAN_GROUPS * SUBLANES * SUBLANES) == 0
    assert s % FOURIER_N2 == 0 and cap % SLOT_BLOCK == 0 and s // TOKEN_TILE < LANES
    assert cap >= WIN_ROWS and w_gate.shape[2] % FF_CHUNK == 0

    lx, lg, fo = _inproj(x.reshape(bsz * s, d), norm1_g[None, :], w_in.astype(BF16), d_lru, d_four)
    lx = lx.reshape(bsz, s, d_lru)
    lg = lg.reshape(bsz, s, d_lru)
    fo = fo.reshape(bsz, s, d_four)

    w_cat, b_cat, lam_cat = _lru_params(lru_wa_f, lru_wx_f, lru_wa_b, lru_wx_b,
                                        lru_ba_f, lru_bx_f, lru_ba_b, lru_bx_b,
                                        lru_lam_f, lru_lam_b)
    y_lru = _lru(lx, lg, conv_w, conv_b[None, :], w_cat, b_cat, lam_cat)
    y_four = _fourier(fo, FOURIER_GROUPS)

    w_out_bf = w_out.astype(BF16)
    x1, h2_tiles, aff_t = _outproj(x, y_lru, y_four, w_out_bf[:d_lru], w_out_bf[d_lru:],
                                   norm2_g[None, :], w_router.T)

    pos_r, starts = _select(aff_t, cap)
    starts_flat = starts.reshape(-1)
    idx, gs = _compact(starts_flat, pos_r, aff_t, cap)
    eo = _moe(idx.reshape(bsz * n_e, 1, cap), gs.reshape((bsz * n_e,) + gs.shape[2:]), h2_tiles,
              w_gate, w_up, w_down, bsz, s)
    return _combine(starts_flat, x1, pos_r, eo, normf_g[None, :])
```
